```python
import math
import numpy as np
import jax
import jax.numpy as jnp
from jax import lax


D_MODEL = 2048
BATCH = 2
SEQ = 8192
DEPTH = 1

CHUNK = 64
ROPE_THETA = 10000.0
NORM_EPS = 1e-6

GDN_K_HEADS = 16
GDN_V_HEADS = 32
GDN_HEAD_DIM = 128
GDN_CONV = 4
GDN_QK_WIDTH = GDN_K_HEADS * GDN_HEAD_DIM
GDN_V_WIDTH = GDN_V_HEADS * GDN_HEAD_DIM

MLA_HEADS = 16
MLA_NOPE_DIM = 128
MLA_ROPE_DIM = 64
MLA_V_DIM = 128
MLA_KV_RANK = 512
MLA_Q_WIDTH = MLA_HEADS * (MLA_NOPE_DIM + MLA_ROPE_DIM)
MLA_O_WIDTH = MLA_HEADS * MLA_V_DIM
IDX_HEADS = 16
IDX_DIM = 64
IDX_ROPE_DIM = 32
TOPK_MAX = 256
Q_BLOCK = 128

N_EXPERTS = 32
TOP_K = 4
D_EXPERT = 2048
SWIGLU_LIMIT = 7.0
SWIGLU_ALPHA = 1.702
EXPERT_BLOCK = 256

IN_SIZES = (GDN_QK_WIDTH, GDN_QK_WIDTH, GDN_V_WIDTH, GDN_V_WIDTH, GDN_V_HEADS, GDN_V_HEADS,
            MLA_Q_WIDTH, MLA_KV_RANK, MLA_ROPE_DIM, IDX_HEADS * IDX_DIM, IDX_DIM, IDX_HEADS,
            D_MODEL, D_MODEL)

kernel_name = 'hybrid_gdn_dsa_moe_block'


def _split(x, sizes):
    cuts = np.cumsum(sizes)[:-1].tolist()
    return jnp.split(x, cuts, axis=-1)


def rms_norm(x, g):
    xf = x.astype(jnp.float32)
    y = xf * lax.rsqrt(jnp.mean(xf * xf, axis=-1, keepdims=True) + NORM_EPS)
    return (y * g.astype(jnp.float32)).astype(x.dtype)


def layer_norm(x, g, b):
    xf = x.astype(jnp.float32)
    mu = jnp.mean(xf, axis=-1, keepdims=True)
    xc = xf - mu
    y = xc * lax.rsqrt(jnp.mean(xc * xc, axis=-1, keepdims=True) + NORM_EPS)
    return (y * g.astype(jnp.float32) + b.astype(jnp.float32)).astype(x.dtype)


def l2_norm(x):
    return x * lax.rsqrt(jnp.sum(x * x, axis=-1, keepdims=True) + NORM_EPS)


def rope_tables(n, dim):
    inv = ROPE_THETA ** (-jnp.arange(0, dim, 2, dtype=jnp.float32) / dim)
    ang = jnp.arange(n, dtype=jnp.float32)[:, None] * inv[None, :]
    return jnp.cos(ang), jnp.sin(ang)


def apply_rope(x, cos, sin):
    xf = x.astype(jnp.float32)
    half = x.shape[-1] // 2
    x1, x2 = xf[..., :half], xf[..., half:]
    c = cos[None, :, None, :]
    s = sin[None, :, None, :]
    return jnp.concatenate([x1 * c - x2 * s, x2 * c + x1 * s], axis=-1).astype(x.dtype)


def causal_depthwise_conv(x, w):
    width, ch = w.shape
    return lax.conv_general_dilated(x, w[:, None, :].astype(x.dtype), window_strides=(1,),
                                    padding=[(width - 1, 0)],
                                    dimension_numbers=('NWC', 'WIO', 'NWC'),
                                    feature_group_count=ch)


def gated_delta_rule(q, k, v, g, beta):
    bsz, t, h, dk = q.shape
    dv = v.shape[-1]
    n = t // CHUNK

    def to_chunks(a):
        return jnp.moveaxis(a.reshape((bsz, n, CHUNK, h) + a.shape[3:]), 3, 1)

    qc, kc, vc, gc, bc = (to_chunks(a) for a in (q, k, v, g, beta))
    G = jnp.cumsum(gc, axis=-1)
    pos = jnp.arange(CHUNK)
    incl = pos[:, None] >= pos[None, :]
    strict = pos[:, None] > pos[None, :]
    decay = jnp.exp(jnp.where(incl, G[..., :, None] - G[..., None, :], -jnp.inf))
    kk = jnp.einsum('bhncd,bhnsd->bhncs', kc, kc)
    a_mat = jnp.where(strict, bc[..., :, None] * kk * decay, 0.0)
    eye = jnp.eye(CHUNK, dtype=jnp.float32)
    t_mat = lax.linalg.triangular_solve(eye + a_mat, jnp.broadcast_to(eye, a_mat.shape),
                                        left_side=True, lower=True, unit_diagonal=True)
    u = jnp.einsum('bhncs,bhnsd->bhncd', t_mat, vc * bc[..., None])
    w = jnp.einsum('bhncs,bhnsd->bhncd', t_mat, kc * (bc * jnp.exp(G))[..., None])
    qk = jnp.einsum('bhncd,bhnsd->bhncs', qc, kc) * decay
    q_dec = qc * jnp.exp(G)[..., None]
    k_dec = kc * jnp.exp(G[..., -1:] - G)[..., None]
    chunk_decay = jnp.exp(G[..., -1])

    def step(state, inp):
        u_n, w_n, qk_n, qd_n, kd_n, cd_n = inp
        v_new = u_n - jnp.einsum('bhcd,bhde->bhce', w_n, state)
        o_n = jnp.einsum('bhcd,bhde->bhce', qd_n, state) + jnp.einsum('bhcs,bhse->bhce', qk_n, v_new)
        state = state * cd_n[..., None, None] + jnp.einsum('bhcd,bhce->bhde', kd_n, v_new)
        return state, o_n

    xs = tuple(jnp.moveaxis(a, 2, 0) for a in (u, w, qk, q_dec, k_dec, chunk_decay))
    s0 = jnp.zeros((bsz, h, dk, dv), jnp.float32)
    _, o = lax.scan(step, s0, xs)
    o = jnp.moveaxis(o, 0, 2)
    return jnp.moveaxis(o, 1, 3).reshape(bsz, t, h, dv)


def gdn_branch(q, k, v, z, a, b, conv_w, a_log, dt_bias, norm_g):
    dtype = q.dtype
    bsz, t = q.shape[:2]
    qkv = jax.nn.silu(causal_depthwise_conv(jnp.concatenate([q, k, v], axis=-1), conv_w))
    q, k, v = _split(qkv, (GDN_QK_WIDTH, GDN_QK_WIDTH, GDN_V_WIDTH))
    q = l2_norm(q.reshape(bsz, t, GDN_K_HEADS, GDN_HEAD_DIM).astype(jnp.float32)) * (GDN_HEAD_DIM ** -0.5)
    k = l2_norm(k.reshape(bsz, t, GDN_K_HEADS, GDN_HEAD_DIM).astype(jnp.float32))
    rep = GDN_V_HEADS // GDN_K_HEADS
    q = jnp.repeat(q, rep, axis=2)
    k = jnp.repeat(k, rep, axis=2)
    v = v.reshape(bsz, t, GDN_V_HEADS, GDN_HEAD_DIM).astype(jnp.float32)
    beta = jax.nn.sigmoid(b.astype(jnp.float32))
    g = -jnp.exp(a_log.astype(jnp.float32)) * jax.nn.softplus(a.astype(jnp.float32) + dt_bias.astype(jnp.float32))
    o = gated_delta_rule(q, k, v, g, beta)
    zf = z.reshape(bsz, t, GDN_V_HEADS, GDN_HEAD_DIM).astype(jnp.float32)
    o = rms_norm(o, norm_g) * jax.nn.silu(zf)
    return o.reshape(bsz, t, GDN_V_WIDTH).astype(dtype)


def dsa_branch(q, ckv, kpe, iq, ik, iw, kv_norm_g, w_uk, w_uv, idx_norm_g, idx_norm_b):
    bsz, t = q.shape[:2]
    n_keep = min(TOPK_MAX, t // 4)
    cos_m, sin_m = rope_tables(t, MLA_ROPE_DIM)
    cos_i, sin_i = rope_tables(t, IDX_ROPE_DIM)
    q = q.reshape(bsz, t, MLA_HEADS, MLA_NOPE_DIM + MLA_ROPE_DIM)
    q_nope = q[..., :MLA_NOPE_DIM]
    q_pe = apply_rope(q[..., MLA_NOPE_DIM:], cos_m, sin_m)
    ckv = rms_norm(ckv, kv_norm_g)
    kpe = apply_rope(kpe[:, :, None, :], cos_m, sin_m)[:, :, 0]
    iq = iq.reshape(bsz, t, IDX_HEADS, IDX_DIM)
    iq = jnp.concatenate([apply_rope(iq[..., :IDX_ROPE_DIM], cos_i, sin_i), iq[..., IDX_ROPE_DIM:]], axis=-1)
    ik = layer_norm(ik, idx_norm_g, idx_norm_b)
    ik = jnp.concatenate([apply_rope(ik[:, :, None, :IDX_ROPE_DIM], cos_i, sin_i)[:, :, 0],
                          ik[..., IDX_ROPE_DIM:]], axis=-1)
    ik_f = ik.astype(jnp.float32)
    iw = iw.astype(jnp.float32) * (IDX_HEADS ** -0.5 * IDX_DIM ** -0.5)
    w_uk3 = w_uk.reshape(MLA_KV_RANK, MLA_HEADS, MLA_NOPE_DIM)
    w_uv3 = w_uv.reshape(MLA_KV_RANK, MLA_HEADS, MLA_V_DIM)
    scale = (MLA_NOPE_DIM + MLA_ROPE_DIM) ** -0.5
    n_blk = t // Q_BLOCK
    key_chunk = jnp.arange(t) // CHUNK
    gather = jax.vmap(lambda src, ids: src[ids])

    def blocks(a):
        return jnp.moveaxis(a.reshape((bsz, n_blk, Q_BLOCK) + a.shape[2:]), 1, 0)

    def attend(inp):
        blk, qn, qp, iqb, iwb = inp
        q_chunk = (blk * Q_BLOCK + jnp.arange(Q_BLOCK)) // CHUNK
        admissible = key_chunk[None, :] <= q_chunk[:, None]
        s_idx = jnp.einsum('bqhd,bsd->bqhs', iqb.astype(jnp.float32), ik_f)
        score = jnp.einsum('bqh,bqhs->bqs', iwb, jax.nn.relu(s_idx))
        score = jnp.where(admissible[None], score, -jnp.inf)
        _, sel = lax.top_k(score, n_keep)
        valid = key_chunk[sel] <= q_chunk[None, :, None]
        ckv_sel = gather(ckv, sel)
        kpe_sel = gather(kpe, sel)
        q_lat = jnp.einsum('bqhd,rhd->bqhr', qn, w_uk3)
        logits = (jnp.einsum('bqhr,bqkr->bqhk', q_lat, ckv_sel)
                  + jnp.einsum('bqhe,bqke->bqhk', qp, kpe_sel)).astype(jnp.float32) * scale
        logits = jnp.where(valid[:, :, None, :], logits, -jnp.inf)
        p = jax.nn.softmax(logits, axis=-1).astype(ckv.dtype)
        o_lat = jnp.einsum('bqhk,bqkr->bqhr', p, ckv_sel)
        return jnp.einsum('bqhr,rhd->bqhd', o_lat, w_uv3)

    o = lax.map(attend, (jnp.arange(n_blk), blocks(q_nope), blocks(q_pe), blocks(iq), blocks(iw)))
    return jnp.moveaxis(o, 0, 1).reshape(bsz, t, MLA_O_WIDTH)


def hybrid_mixer(h, w_in, conv_w, a_log, dt_bias, gdn_norm_g, kv_norm_g, w_uk, w_uv,
                 idx_norm_g, idx_norm_b, w_branch_gdn, w_branch_dsa, w_out):
    proj = h @ w_in
    (gq, gk, gv, gz, ga, gb, mq, mckv, mkpe, iq, ik, iw, gate_a, gate_b) = _split(proj, IN_SIZES)
    o_a = gdn_branch(gq, gk, gv, gz, ga, gb, conv_w, a_log, dt_bias, gdn_norm_g)
    o_b = dsa_branch(mq, mckv, mkpe, iq, ik, iw, kv_norm_g, w_uk, w_uv, idx_norm_g, idx_norm_b)
    merged = jax.nn.sigmoid(gate_a) * (o_a @ w_branch_gdn) + jax.nn.sigmoid(gate_b) * (o_b @ w_branch_dsa)
    return merged @ w_out


def moe_ffn(h, router_w, router_b, w_gate, b_gate, w_up, b_up, w_down, b_down):
    bsz, t, d = h.shape
    tok = h.reshape(-1, d)
    n_tok = tok.shape[0]
    n_assign = n_tok * TOP_K
    logits = tok.astype(jnp.float32) @ router_w.astype(jnp.float32) + router_b.astype(jnp.float32)
    top_val, top_idx = lax.top_k(logits, TOP_K)
    gates = jax.nn.softmax(top_val, axis=-1)
    flat_e = top_idx.reshape(-1)
    flat_tok = jnp.repeat(jnp.arange(n_tok, dtype=jnp.int32), TOP_K)
    order = jnp.argsort(flat_e)
    sorted_e = flat_e[order]
    counts = jnp.bincount(flat_e, length=N_EXPERTS)
    padded = (counts + EXPERT_BLOCK - 1) // EXPERT_BLOCK * EXPERT_BLOCK
    pad_end = jnp.cumsum(padded)
    pad_start = pad_end - padded
    start = jnp.cumsum(counts) - counts
    slot_sorted = jnp.arange(n_assign, dtype=jnp.int32) - start[sorted_e] + pad_start[sorted_e]
    slot = jnp.zeros((n_assign,), jnp.int32).at[order].set(slot_sorted.astype(jnp.int32))
    n_blocks = -(-n_assign // EXPERT_BLOCK) + N_EXPERTS
    slot_tok = jnp.zeros((n_blocks * EXPERT_BLOCK,), jnp.int32).at[slot].set(flat_tok)
    block_expert = jnp.minimum(jnp.searchsorted(pad_end, jnp.arange(n_blocks) * EXPERT_BLOCK, side='right'),
                               N_EXPERTS - 1)
    xs = tok[slot_tok].reshape(n_blocks, EXPERT_BLOCK, d)

    def expert(inp):
        xb, e = inp
        gate = xb @ w_gate[e] + b_gate[e]
        up = xb @ w_up[e] + b_up[e]
        gate = jnp.minimum(gate, SWIGLU_LIMIT)
        up = jnp.clip(up, -SWIGLU_LIMIT, SWIGLU_LIMIT)
        act = (up + 1.0) * (gate * jax.nn.sigmoid(gate * SWIGLU_ALPHA))
        return act @ w_down[e] + b_down[e]

    ys = lax.map(expert, (xs, block_expert)).reshape(-1, d)
    y = ys[slot].reshape(n_tok, TOP_K, d)
    out = jnp.einsum('tk,tkd->td', gates.astype(y.dtype), y)
    return out.reshape(bsz, t, d)


def setup_inputs(seed: int = 0) -> dict:
    key = jax.random.key(seed)
    ks = jax.random.split(key, 32)
    f32 = jnp.float32
    L = DEPTH
    d_in = sum(IN_SIZES)

    def nrm(k, shape, scale):
        return jax.random.normal(k, shape, f32) * scale

    def gain(k, shape):
        return 1.0 + 0.01 * jax.random.normal(k, shape, f32)

    a_log = jnp.log(jax.random.uniform(ks[4], (L, GDN_V_HEADS), f32, 1.0, 16.0))
    dt = jnp.exp(jax.random.uniform(ks[5], (L, GDN_V_HEADS), f32, math.log(1e-3), math.log(1e-1)))
    dt_bias = dt + jnp.log(-jnp.expm1(-dt))
    return {
        'x': nrm(ks[0], (BATCH, SEQ, D_MODEL), 1.0),
        'norm_mix_g': gain(ks[1], (L, D_MODEL)),
        'w_in': nrm(ks[2], (L, D_MODEL, d_in), D_MODEL ** -0.5),
        'gdn_conv_w': nrm(ks[3], (L, GDN_CONV, 2 * GDN_QK_WIDTH + GDN_V_WIDTH), GDN_CONV ** -0.5),
        'gdn_a_log': a_log,
        'gdn_dt_bias': dt_bias,
        'gdn_norm_g': gain(ks[6], (L, GDN_HEAD_DIM)),
        'mla_kv_norm_g': gain(ks[7], (L, MLA_KV_RANK)),
        'mla_w_uk': nrm(ks[8], (L, MLA_KV_RANK, MLA_HEADS * MLA_NOPE_DIM), MLA_KV_RANK ** -0.5),
        'mla_w_uv': nrm(ks[9], (L, MLA_KV_RANK, MLA_O_WIDTH), MLA_KV_RANK ** -0.5),
        'idx_k_norm_g': gain(ks[10], (L, IDX_DIM)),
        'idx_k_norm_b': nrm(ks[11], (L, IDX_DIM), 0.01),
        'w_branch_gdn': nrm(ks[12], (L, GDN_V_WIDTH, D_MODEL), GDN_V_WIDTH ** -0.5),
        'w_branch_dsa': nrm(ks[13], (L, MLA_O_WIDTH, D_MODEL), MLA_O_WIDTH ** -0.5),
        'w_out': nrm(ks[14], (L, D_MODEL, D_MODEL), D_MODEL ** -0.5),
        'norm_ffn_g': gain(ks[15], (L, D_MODEL)),
        'router_w': nrm(ks[16], (L, D_MODEL, N_EXPERTS), D_MODEL ** -0.5),
        'router_b': nrm(ks[17], (L, N_EXPERTS), 0.01),
        'exp_w_gate': nrm(ks[18], (L, N_EXPERTS, D_MODEL, D_EXPERT), D_MODEL ** -0.5),
        'exp_b_gate': nrm(ks[19], (L, N_EXPERTS, D_EXPERT), 0.01),
        'exp_w_up': nrm(ks[20], (L, N_EXPERTS, D_MODEL, D_EXPERT), D_MODEL ** -0.5),
        'exp_b_up': nrm(ks[21], (L, N_EXPERTS, D_EXPERT), 0.01),
        'exp_w_down': nrm(ks[22], (L, N_EXPERTS, D_EXPERT, D_MODEL), D_EXPERT ** -0.5),
        'exp_b_down': nrm(ks[23], (L, N_EXPERTS, D_MODEL), 0.01),
        'norm_final_g': gain(ks[24], (D_MODEL,)),
    }


def reference(x, norm_mix_g, w_in, gdn_conv_w, gdn_a_log, gdn_dt_bias, gdn_norm_g, mla_kv_norm_g,
              mla_w_uk, mla_w_uv, idx_k_norm_g, idx_k_norm_b, w_branch_gdn, w_branch_dsa, w_out,
              norm_ffn_g, router_w, router_b, exp_w_gate, exp_b_gate, exp_w_up, exp_b_up,
              exp_w_down, exp_b_down, norm_final_g):
    for l in range(DEPTH):
        x = x + hybrid_mixer(rms_norm(x, norm_mix_g[l]), w_in[l], gdn_conv_w[l], gdn_a_log[l], gdn_dt_bias[l],
                             gdn_norm_g[l], mla_kv_norm_g[l], mla_w_uk[l], mla_w_uv[l], idx_k_norm_g[l],
                             idx_k_norm_b[l], w_branch_gdn[l], w_branch_dsa[l], w_out[l])
        x = x + moe_ffn(rms_norm(x, norm_ffn_g[l]), router_w[l], router_b[l], exp_w_gate[l], exp_b_gate[l],
                        exp_w_up[l], exp_b_up[l], exp_w_down[l], exp_b_down[l])
    return rms_norm(x, norm_final_g)
```

```python
import functools
import math
import numpy as np
import jax
import jax.numpy as jnp
from jax import lax
from jax.experimental import pallas as pl
from jax.experimental.pallas import tpu as pltpu

D_MODEL = 2048
CHUNK = 64
ROPE_THETA = 10000.0
NORM_EPS = 1e-6
GDN_K_HEADS = 16
GDN_V_HEADS = 32
GDN_HEAD_DIM = 128
GDN_QK_WIDTH = GDN_K_HEADS * GDN_HEAD_DIM
GDN_V_WIDTH = GDN_V_HEADS * GDN_HEAD_DIM
MLA_HEADS = 16
MLA_NOPE_DIM = 128
MLA_ROPE_DIM = 64
MLA_V_DIM = 128
MLA_KV_RANK = 512
MLA_Q_WIDTH = MLA_HEADS * (MLA_NOPE_DIM + MLA_ROPE_DIM)
MLA_O_WIDTH = MLA_HEADS * MLA_V_DIM
IDX_HEADS = 16
IDX_DIM = 64
IDX_ROPE_DIM = 32
TOPK_MAX = 256
Q_BLOCK = 128
N_EXPERTS = 32
TOP_K = 4
D_EXPERT = 2048
SWIGLU_LIMIT = 7.0
SWIGLU_ALPHA = 1.702
EXPERT_BLOCK = 256
IN_SIZES = (GDN_QK_WIDTH, GDN_QK_WIDTH, GDN_V_WIDTH, GDN_V_WIDTH, GDN_V_HEADS, GDN_V_HEADS,
            MLA_Q_WIDTH, MLA_KV_RANK, MLA_ROPE_DIM, IDX_HEADS * IDX_DIM, IDX_DIM, IDX_HEADS,
            D_MODEL, D_MODEL)


def _mm_kernel(a_ref, b_ref, o_ref):
    o_ref[...] = jnp.dot(a_ref[...], b_ref[...], preferred_element_type=jnp.float32)


def _matmul(a, b, tm=512, tn=1024):
    m, k = a.shape
    n = b.shape[1]
    n_pad = -(-n // tn) * tn
    a = a.astype(jnp.bfloat16)
    b = b.astype(jnp.bfloat16)
    if n_pad != n:
        b = jnp.pad(b, ((0, 0), (0, n_pad - n)))
    out = pl.pallas_call(
        _mm_kernel,
        grid=(n_pad // tn, m // tm),
        in_specs=[pl.BlockSpec((tm, k), lambda j, i: (i, 0)),
                  pl.BlockSpec((k, tn), lambda j, i: (0, j))],
        out_specs=pl.BlockSpec((tm, tn), lambda j, i: (i, j)),
        out_shape=jax.ShapeDtypeStruct((m, n_pad), jnp.float32),
        compiler_params=pltpu.CompilerParams(
            dimension_semantics=("arbitrary", "arbitrary"),
            vmem_limit_bytes=56 * 1024 * 1024),
        name="dense_matmul",
    )(a, b)
    return out[:, :n] if n_pad != n else out


def _split(x, sizes):
    cuts = np.cumsum(sizes)[:-1].tolist()
    return jnp.split(x, cuts, axis=-1)


def rms_norm(x, g):
    xf = x.astype(jnp.float32)
    y = xf * lax.rsqrt(jnp.mean(xf * xf, axis=-1, keepdims=True) + NORM_EPS)
    return (y * g.astype(jnp.float32)).astype(x.dtype)


def layer_norm(x, g, b):
    xf = x.astype(jnp.float32)
    mu = jnp.mean(xf, axis=-1, keepdims=True)
    xc = xf - mu
    y = xc * lax.rsqrt(jnp.mean(xc * xc, axis=-1, keepdims=True) + NORM_EPS)
    return (y * g.astype(jnp.float32) + b.astype(jnp.float32)).astype(x.dtype)


def l2_norm(x):
    return x * lax.rsqrt(jnp.sum(x * x, axis=-1, keepdims=True) + NORM_EPS)


def rope_tables(n, dim):
    inv = ROPE_THETA ** (-jnp.arange(0, dim, 2, dtype=jnp.float32) / dim)
    ang = jnp.arange(n, dtype=jnp.float32)[:, None] * inv[None, :]
    return jnp.cos(ang), jnp.sin(ang)


def apply_rope(x, cos, sin):
    xf = x.astype(jnp.float32)
    half = x.shape[-1] // 2
    x1, x2 = xf[..., :half], xf[..., half:]
    c = cos[None, :, None, :]
    s = sin[None, :, None, :]
    return jnp.concatenate([x1 * c - x2 * s, x2 * c + x1 * s], axis=-1).astype(x.dtype)


def causal_depthwise_conv(x, w):
    width, ch = w.shape
    return lax.conv_general_dilated(x, w[:, None, :].astype(x.dtype), window_strides=(1,),
                                    padding=[(width - 1, 0)],
                                    dimension_numbers=('NWC', 'WIO', 'NWC'),
                                    feature_group_count=ch)


def gated_delta_rule(q, k, v, g, beta):
    bsz, t, h, dk = q.shape
    dv = v.shape[-1]
    n = t // CHUNK

    def to_chunks(a):
        return jnp.moveaxis(a.reshape((bsz, n, CHUNK, h) + a.shape[3:]), 3, 1)

    qc, kc, vc, gc, bc = (to_chunks(a) for a in (q, k, v, g, beta))
    G = jnp.cumsum(gc, axis=-1)
    pos = jnp.arange(CHUNK)
    incl = pos[:, None] >= pos[None, :]
    strict = pos[:, None] > pos[None, :]
    decay = jnp.exp(jnp.where(incl, G[..., :, None] - G[..., None, :], -jnp.inf))
    kk = jnp.einsum('bhncd,bhnsd->bhncs', kc, kc)
    a_mat = jnp.where(strict, bc[..., :, None] * kk * decay, 0.0)
    eye = jnp.eye(CHUNK, dtype=jnp.float32)
    t_mat = lax.linalg.triangular_solve(eye + a_mat, jnp.broadcast_to(eye, a_mat.shape),
                                        left_side=True, lower=True, unit_diagonal=True)
    u = jnp.einsum('bhncs,bhnsd->bhncd', t_mat, vc * bc[..., None])
    w = jnp.einsum('bhncs,bhnsd->bhncd', t_mat, kc * (bc * jnp.exp(G))[..., None])
    qk = jnp.einsum('bhncd,bhnsd->bhncs', qc, kc) * decay
    q_dec = qc * jnp.exp(G)[..., None]
    k_dec = kc * jnp.exp(G[..., -1:] - G)[..., None]
    chunk_decay = jnp.exp(G[..., -1])

    def step(state, inp):
        u_n, w_n, qk_n, qd_n, kd_n, cd_n = inp
        v_new = u_n - jnp.einsum('bhcd,bhde->bhce', w_n, state)
        o_n = jnp.einsum('bhcd,bhde->bhce', qd_n, state) + jnp.einsum('bhcs,bhse->bhce', qk_n, v_new)
        state = state * cd_n[..., None, None] + jnp.einsum('bhcd,bhce->bhde', kd_n, v_new)
        return state, o_n

    xs = tuple(jnp.moveaxis(a, 2, 0) for a in (u, w, qk, q_dec, k_dec, chunk_decay))
    s0 = jnp.zeros((bsz, h, dk, dv), jnp.float32)
    _, o = lax.scan(step, s0, xs)
    o = jnp.moveaxis(o, 0, 2)
    return jnp.moveaxis(o, 1, 3).reshape(bsz, t, h, dv)


def gdn_branch(q, k, v, z, a, b, conv_w, a_log, dt_bias, norm_g):
    dtype = q.dtype
    bsz, t = q.shape[:2]
    qkv = jax.nn.silu(causal_depthwise_conv(jnp.concatenate([q, k, v], axis=-1), conv_w))
    q, k, v = _split(qkv, (GDN_QK_WIDTH, GDN_QK_WIDTH, GDN_V_WIDTH))
    q = l2_norm(q.reshape(bsz, t, GDN_K_HEADS, GDN_HEAD_DIM).astype(jnp.float32)) * (GDN_HEAD_DIM ** -0.5)
    k = l2_norm(k.reshape(bsz, t, GDN_K_HEADS, GDN_HEAD_DIM).astype(jnp.float32))
    rep = GDN_V_HEADS // GDN_K_HEADS
    q = jnp.repeat(q, rep, axis=2)
    k = jnp.repeat(k, rep, axis=2)
    v = v.reshape(bsz, t, GDN_V_HEADS, GDN_HEAD_DIM).astype(jnp.float32)
    beta = jax.nn.sigmoid(b.astype(jnp.float32))
    g = -jnp.exp(a_log.astype(jnp.float32)) * jax.nn.softplus(a.astype(jnp.float32) + dt_bias.astype(jnp.float32))
    o = gated_delta_rule(q, k, v, g, beta)
    zf = z.reshape(bsz, t, GDN_V_HEADS, GDN_HEAD_DIM).astype(jnp.float32)
    o = rms_norm(o, norm_g) * jax.nn.silu(zf)
    return o.reshape(bsz, t, GDN_V_WIDTH).astype(dtype)


def dsa_branch(q, ckv, kpe, iq, ik, iw, kv_norm_g, w_uk, w_uv, idx_norm_g, idx_norm_b):
    bsz, t = q.shape[:2]
    n_keep = min(TOPK_MAX, t // 4)
    cos_m, sin_m = rope_tables(t, MLA_ROPE_DIM)
    cos_i, sin_i = rope_tables(t, IDX_ROPE_DIM)
    q = q.reshape(bsz, t, MLA_HEADS, MLA_NOPE_DIM + MLA_ROPE_DIM)
    q_nope = q[..., :MLA_NOPE_DIM]
    q_pe = apply_rope(q[..., MLA_NOPE_DIM:], cos_m, sin_m)
    ckv = rms_norm(ckv, kv_norm_g)
    kpe = apply_rope(kpe[:, :, None, :], cos_m, sin_m)[:, :, 0]
    iq = iq.reshape(bsz, t, IDX_HEADS, IDX_DIM)
    iq = jnp.concatenate([apply_rope(iq[..., :IDX_ROPE_DIM], cos_i, sin_i), iq[..., IDX_ROPE_DIM:]], axis=-1)
    ik = layer_norm(ik, idx_norm_g, idx_norm_b)
    ik = jnp.concatenate([apply_rope(ik[:, :, None, :IDX_ROPE_DIM], cos_i, sin_i)[:, :, 0],
                          ik[..., IDX_ROPE_DIM:]], axis=-1)
    ik_f = ik.astype(jnp.float32)
    iw = iw.astype(jnp.float32) * (IDX_HEADS ** -0.5 * IDX_DIM ** -0.5)
    w_uk3 = w_uk.reshape(MLA_KV_RANK, MLA_HEADS, MLA_NOPE_DIM)
    w_uv3 = w_uv.reshape(MLA_KV_RANK, MLA_HEADS, MLA_V_DIM)
    scale = (MLA_NOPE_DIM + MLA_ROPE_DIM) ** -0.5
    n_blk = t // Q_BLOCK
    key_chunk = jnp.arange(t) // CHUNK
    gather = jax.vmap(lambda src, ids: src[ids])

    def blocks(a):
        return jnp.moveaxis(a.reshape((bsz, n_blk, Q_BLOCK) + a.shape[2:]), 1, 0)

    def attend(inp):
        blk, qn, qp, iqb, iwb = inp
        q_chunk = (blk * Q_BLOCK + jnp.arange(Q_BLOCK)) // CHUNK
        admissible = key_chunk[None, :] <= q_chunk[:, None]
        s_idx = jnp.einsum('bqhd,bsd->bqhs', iqb.astype(jnp.float32), ik_f)
        score = jnp.einsum('bqh,bqhs->bqs', iwb, jax.nn.relu(s_idx))
        score = jnp.where(admissible[None], score, -jnp.inf)
        _, sel = lax.top_k(score, n_keep)
        valid = key_chunk[sel] <= q_chunk[None, :, None]
        ckv_sel = gather(ckv, sel)
        kpe_sel = gather(kpe, sel)
        q_lat = jnp.einsum('bqhd,rhd->bqhr', qn, w_uk3)
        logits = (jnp.einsum('bqhr,bqkr->bqhk', q_lat, ckv_sel)
                  + jnp.einsum('bqhe,bqke->bqhk', qp, kpe_sel)).astype(jnp.float32) * scale
        logits = jnp.where(valid[:, :, None, :], logits, -jnp.inf)
        p = jax.nn.softmax(logits, axis=-1).astype(ckv.dtype)
        o_lat = jnp.einsum('bqhk,bqkr->bqhr', p, ckv_sel)
        return jnp.einsum('bqhr,rhd->bqhd', o_lat, w_uv3)

    o = lax.map(attend, (jnp.arange(n_blk), blocks(q_nope), blocks(q_pe), blocks(iq), blocks(iw)))
    return jnp.moveaxis(o, 0, 1).reshape(bsz, t, MLA_O_WIDTH)


def hybrid_mixer(h, w_in, conv_w, a_log, dt_bias, gdn_norm_g, kv_norm_g, w_uk, w_uv,
                 idx_norm_g, idx_norm_b, w_branch_gdn, w_branch_dsa, w_out):
    bsz, t, d = h.shape
    proj = _matmul(h.reshape(bsz * t, d), w_in).reshape(bsz, t, -1)
    (gq, gk, gv, gz, ga, gb, mq, mckv, mkpe, iq, ik, iw, gate_a, gate_b) = _split(proj, IN_SIZES)
    o_a = gdn_branch(gq, gk, gv, gz, ga, gb, conv_w, a_log, dt_bias, gdn_norm_g)
    o_b = dsa_branch(mq, mckv, mkpe, iq, ik, iw, kv_norm_g, w_uk, w_uv, idx_norm_g, idx_norm_b)
    u_a = _matmul(o_a.reshape(bsz * t, -1), w_branch_gdn).reshape(bsz, t, d)
    u_b = _matmul(o_b.reshape(bsz * t, -1), w_branch_dsa).reshape(bsz, t, d)
    merged = jax.nn.sigmoid(gate_a) * u_a + jax.nn.sigmoid(gate_b) * u_b
    return _matmul(merged.reshape(bsz * t, d), w_out).reshape(bsz, t, d)


def moe_ffn(h, router_w, router_b, w_gate, b_gate, w_up, b_up, w_down, b_down):
    bsz, t, d = h.shape
    tok = h.reshape(-1, d)
    n_tok = tok.shape[0]
    n_assign = n_tok * TOP_K
    logits = tok.astype(jnp.float32) @ router_w.astype(jnp.float32) + router_b.astype(jnp.float32)
    top_val, top_idx = lax.top_k(logits, TOP_K)
    gates = jax.nn.softmax(top_val, axis=-1)
    flat_e = top_idx.reshape(-1)
    flat_tok = jnp.repeat(jnp.arange(n_tok, dtype=jnp.int32), TOP_K)
    order = jnp.argsort(flat_e)
    sorted_e = flat_e[order]
    counts = jnp.bincount(flat_e, length=N_EXPERTS)
    padded = (counts + EXPERT_BLOCK - 1) // EXPERT_BLOCK * EXPERT_BLOCK
    pad_end = jnp.cumsum(padded)
    pad_start = pad_end - padded
    start = jnp.cumsum(counts) - counts
    slot_sorted = jnp.arange(n_assign, dtype=jnp.int32) - start[sorted_e] + pad_start[sorted_e]
    slot = jnp.zeros((n_assign,), jnp.int32).at[order].set(slot_sorted.astype(jnp.int32))
    n_blocks = -(-n_assign // EXPERT_BLOCK) + N_EXPERTS
    slot_tok = jnp.zeros((n_blocks * EXPERT_BLOCK,), jnp.int32).at[slot].set(flat_tok)
    block_expert = jnp.minimum(jnp.searchsorted(pad_end, jnp.arange(n_blocks) * EXPERT_BLOCK, side='right'),
                               N_EXPERTS - 1)
    xs = tok[slot_tok].reshape(n_blocks, EXPERT_BLOCK, d)

    def expert(inp):
        xb, e = inp
        gate = xb @ w_gate[e] + b_gate[e]
        up = xb @ w_up[e] + b_up[e]
        gate = jnp.minimum(gate, SWIGLU_LIMIT)
        up = jnp.clip(up, -SWIGLU_LIMIT, SWIGLU_LIMIT)
        act = (up + 1.0) * (gate * jax.nn.sigmoid(gate * SWIGLU_ALPHA))
        return act @ w_down[e] + b_down[e]

    ys = lax.map(expert, (xs, block_expert)).reshape(-1, d)
    y = ys[slot].reshape(n_tok, TOP_K, d)
    out = jnp.einsum('tk,tkd->td', gates.astype(y.dtype), y)
    return out.reshape(bsz, t, d)


def kernel(x, norm_mix_g, w_in, gdn_conv_w, gdn_a_log, gdn_dt_bias, gdn_norm_g, mla_kv_norm_g,
           mla_w_uk, mla_w_uv, idx_k_norm_g, idx_k_norm_b, w_branch_gdn, w_branch_dsa, w_out,
           norm_ffn_g, router_w, router_b, exp_w_gate, exp_b_gate, exp_w_up, exp_b_up,
           exp_w_down, exp_b_down, norm_final_g):
    l = 0
    x = x + hybrid_mixer(rms_norm(x, norm_mix_g[l]), w_in[l], gdn_conv_w[l], gdn_a_log[l], gdn_dt_bias[l],
                         gdn_norm_g[l], mla_kv_norm_g[l], mla_w_uk[l], mla_w_uv[l], idx_k_norm_g[l],
                         idx_k_norm_b[l], w_branch_gdn[l], w_branch_dsa[l], w_out[l])
    x = x + moe_ffn(rms_norm(x, norm_ffn_g[l]), router_w[l], router_b[l], exp_w_gate[l], exp_b_gate[l],
                    exp_w_up[l], exp_b_up[l], exp_w_down[l], exp_b_down[l])
    return rms_norm(x, norm_final_g)
```

```python
import functools
import math
import numpy as np
import jax
import jax.numpy as jnp
from jax import lax
from jax.experimental import pallas as pl
from jax.experimental.pallas import tpu as pltpu

D_MODEL = 2048
CHUNK = 64
ROPE_THETA = 10000.0
NORM_EPS = 1e-6
GDN_K_HEADS = 16
GDN_V_HEADS = 32
GDN_HEAD_DIM = 128
GDN_QK_WIDTH = GDN_K_HEADS * GDN_HEAD_DIM
GDN_V_WIDTH = GDN_V_HEADS * GDN_HEAD_DIM
MLA_HEADS = 16
MLA_NOPE_DIM = 128
MLA_ROPE_DIM = 64
MLA_V_DIM = 128
MLA_KV_RANK = 512
MLA_Q_WIDTH = MLA_HEADS * (MLA_NOPE_DIM + MLA_ROPE_DIM)
MLA_O_WIDTH = MLA_HEADS * MLA_V_DIM
IDX_HEADS = 16
IDX_DIM = 64
IDX_ROPE_DIM = 32
TOPK_MAX = 256
Q_BLOCK = 128
N_EXPERTS = 32
TOP_K = 4
D_EXPERT = 2048
SWIGLU_LIMIT = 7.0
SWIGLU_ALPHA = 1.702
EXPERT_BLOCK = 256
IN_SIZES = (GDN_QK_WIDTH, GDN_QK_WIDTH, GDN_V_WIDTH, GDN_V_WIDTH, GDN_V_HEADS, GDN_V_HEADS,
            MLA_Q_WIDTH, MLA_KV_RANK, MLA_ROPE_DIM, IDX_HEADS * IDX_DIM, IDX_DIM, IDX_HEADS,
            D_MODEL, D_MODEL)


def _mm_kernel(a_ref, b_ref, o_ref):
    o_ref[...] = jnp.dot(a_ref[...], b_ref[...], preferred_element_type=jnp.float32)


def _matmul(a, b, tm=512, tn=1024):
    m, k = a.shape
    n = b.shape[1]
    n_pad = -(-n // tn) * tn
    a = a.astype(jnp.bfloat16)
    b = b.astype(jnp.bfloat16)
    if n_pad != n:
        b = jnp.pad(b, ((0, 0), (0, n_pad - n)))
    out = pl.pallas_call(
        _mm_kernel,
        grid=(n_pad // tn, m // tm),
        in_specs=[pl.BlockSpec((tm, k), lambda j, i: (i, 0)),
                  pl.BlockSpec((k, tn), lambda j, i: (0, j))],
        out_specs=pl.BlockSpec((tm, tn), lambda j, i: (i, j)),
        out_shape=jax.ShapeDtypeStruct((m, n_pad), jnp.float32),
        compiler_params=pltpu.CompilerParams(
            dimension_semantics=("arbitrary", "arbitrary"),
            vmem_limit_bytes=56 * 1024 * 1024),
        name="dense_matmul",
    )(a, b)
    return out[:, :n] if n_pad != n else out


def _split(x, sizes):
    cuts = np.cumsum(sizes)[:-1].tolist()
    return jnp.split(x, cuts, axis=-1)


def rms_norm(x, g):
    xf = x.astype(jnp.float32)
    y = xf * lax.rsqrt(jnp.mean(xf * xf, axis=-1, keepdims=True) + NORM_EPS)
    return (y * g.astype(jnp.float32)).astype(x.dtype)


def layer_norm(x, g, b):
    xf = x.astype(jnp.float32)
    mu = jnp.mean(xf, axis=-1, keepdims=True)
    xc = xf - mu
    y = xc * lax.rsqrt(jnp.mean(xc * xc, axis=-1, keepdims=True) + NORM_EPS)
    return (y * g.astype(jnp.float32) + b.astype(jnp.float32)).astype(x.dtype)


def l2_norm(x):
    return x * lax.rsqrt(jnp.sum(x * x, axis=-1, keepdims=True) + NORM_EPS)


def rope_tables(n, dim):
    inv = ROPE_THETA ** (-jnp.arange(0, dim, 2, dtype=jnp.float32) / dim)
    ang = jnp.arange(n, dtype=jnp.float32)[:, None] * inv[None, :]
    return jnp.cos(ang), jnp.sin(ang)


def apply_rope(x, cos, sin):
    xf = x.astype(jnp.float32)
    half = x.shape[-1] // 2
    x1, x2 = xf[..., :half], xf[..., half:]
    c = cos[None, :, None, :]
    s = sin[None, :, None, :]
    return jnp.concatenate([x1 * c - x2 * s, x2 * c + x1 * s], axis=-1).astype(x.dtype)


def causal_depthwise_conv(x, w):
    width, ch = w.shape
    return lax.conv_general_dilated(x, w[:, None, :].astype(x.dtype), window_strides=(1,),
                                    padding=[(width - 1, 0)],
                                    dimension_numbers=('NWC', 'WIO', 'NWC'),
                                    feature_group_count=ch)


def gated_delta_rule(q, k, v, g, beta):
    bsz, t, h, dk = q.shape
    dv = v.shape[-1]
    n = t // CHUNK

    def to_chunks(a):
        return jnp.moveaxis(a.reshape((bsz, n, CHUNK, h) + a.shape[3:]), 3, 1)

    qc, kc, vc, gc, bc = (to_chunks(a) for a in (q, k, v, g, beta))
    G = jnp.cumsum(gc, axis=-1)
    pos = jnp.arange(CHUNK)
    incl = pos[:, None] >= pos[None, :]
    strict = pos[:, None] > pos[None, :]
    decay = jnp.exp(jnp.where(incl, G[..., :, None] - G[..., None, :], -jnp.inf))
    kk = jnp.einsum('bhncd,bhnsd->bhncs', kc, kc)
    a_mat = jnp.where(strict, bc[..., :, None] * kk * decay, 0.0)
    eye = jnp.eye(CHUNK, dtype=jnp.float32)
    t_mat = lax.linalg.triangular_solve(eye + a_mat, jnp.broadcast_to(eye, a_mat.shape),
                                        left_side=True, lower=True, unit_diagonal=True)
    u = jnp.einsum('bhncs,bhnsd->bhncd', t_mat, vc * bc[..., None])
    w = jnp.einsum('bhncs,bhnsd->bhncd', t_mat, kc * (bc * jnp.exp(G))[..., None])
    qk = jnp.einsum('bhncd,bhnsd->bhncs', qc, kc) * decay
    q_dec = qc * jnp.exp(G)[..., None]
    k_dec = kc * jnp.exp(G[..., -1:] - G)[..., None]
    chunk_decay = jnp.exp(G[..., -1])

    def step(state, inp):
        u_n, w_n, qk_n, qd_n, kd_n, cd_n = inp
        v_new = u_n - jnp.einsum('bhcd,bhde->bhce', w_n, state)
        o_n = jnp.einsum('bhcd,bhde->bhce', qd_n, state) + jnp.einsum('bhcs,bhse->bhce', qk_n, v_new)
        state = state * cd_n[..., None, None] + jnp.einsum('bhcd,bhce->bhde', kd_n, v_new)
        return state, o_n

    xs = tuple(jnp.moveaxis(a, 2, 0) for a in (u, w, qk, q_dec, k_dec, chunk_decay))
    s0 = jnp.zeros((bsz, h, dk, dv), jnp.float32)
    _, o = lax.scan(step, s0, xs)
    o = jnp.moveaxis(o, 0, 2)
    return jnp.moveaxis(o, 1, 3).reshape(bsz, t, h, dv)


def gdn_branch(q, k, v, z, a, b, conv_w, a_log, dt_bias, norm_g):
    dtype = q.dtype
    bsz, t = q.shape[:2]
    qkv = jax.nn.silu(causal_depthwise_conv(jnp.concatenate([q, k, v], axis=-1), conv_w))
    q, k, v = _split(qkv, (GDN_QK_WIDTH, GDN_QK_WIDTH, GDN_V_WIDTH))
    q = l2_norm(q.reshape(bsz, t, GDN_K_HEADS, GDN_HEAD_DIM).astype(jnp.float32)) * (GDN_HEAD_DIM ** -0.5)
    k = l2_norm(k.reshape(bsz, t, GDN_K_HEADS, GDN_HEAD_DIM).astype(jnp.float32))
    rep = GDN_V_HEADS // GDN_K_HEADS
    q = jnp.repeat(q, rep, axis=2)
    k = jnp.repeat(k, rep, axis=2)
    v = v.reshape(bsz, t, GDN_V_HEADS, GDN_HEAD_DIM).astype(jnp.float32)
    beta = jax.nn.sigmoid(b.astype(jnp.float32))
    g = -jnp.exp(a_log.astype(jnp.float32)) * jax.nn.softplus(a.astype(jnp.float32) + dt_bias.astype(jnp.float32))
    o = gated_delta_rule(q, k, v, g, beta)
    zf = z.reshape(bsz, t, GDN_V_HEADS, GDN_HEAD_DIM).astype(jnp.float32)
    o = rms_norm(o, norm_g) * jax.nn.silu(zf)
    return o.reshape(bsz, t, GDN_V_WIDTH).astype(dtype)


def dsa_branch(q, ckv, kpe, iq, ik, iw, kv_norm_g, w_uk, w_uv, idx_norm_g, idx_norm_b):
    bsz, t = q.shape[:2]
    n_keep = min(TOPK_MAX, t // 4)
    cos_m, sin_m = rope_tables(t, MLA_ROPE_DIM)
    cos_i, sin_i = rope_tables(t, IDX_ROPE_DIM)
    q = q.reshape(bsz, t, MLA_HEADS, MLA_NOPE_DIM + MLA_ROPE_DIM)
    q_nope = q[..., :MLA_NOPE_DIM]
    q_pe = apply_rope(q[..., MLA_NOPE_DIM:], cos_m, sin_m)
    ckv = rms_norm(ckv, kv_norm_g)
    kpe = apply_rope(kpe[:, :, None, :], cos_m, sin_m)[:, :, 0]
    iq = iq.reshape(bsz, t, IDX_HEADS, IDX_DIM)
    iq = jnp.concatenate([apply_rope(iq[..., :IDX_ROPE_DIM], cos_i, sin_i), iq[..., IDX_ROPE_DIM:]], axis=-1)
    ik = layer_norm(ik, idx_norm_g, idx_norm_b)
    ik = jnp.concatenate([apply_rope(ik[:, :, None, :IDX_ROPE_DIM], cos_i, sin_i)[:, :, 0],
                          ik[..., IDX_ROPE_DIM:]], axis=-1)
    ik_f = ik.astype(jnp.float32)
    iw = iw.astype(jnp.float32) * (IDX_HEADS ** -0.5 * IDX_DIM ** -0.5)
    w_uk3 = w_uk.reshape(MLA_KV_RANK, MLA_HEADS, MLA_NOPE_DIM)
    w_uv3 = w_uv.reshape(MLA_KV_RANK, MLA_HEADS, MLA_V_DIM)
    scale = (MLA_NOPE_DIM + MLA_ROPE_DIM) ** -0.5
    n_blk = t // Q_BLOCK
    key_chunk = jnp.arange(t) // CHUNK
    gather = jax.vmap(lambda src, ids: src[ids])

    def blocks(a):
        return jnp.moveaxis(a.reshape((bsz, n_blk, Q_BLOCK) + a.shape[2:]), 1, 0)

    def attend(inp):
        blk, qn, qp, iqb, iwb = inp
        q_chunk = (blk * Q_BLOCK + jnp.arange(Q_BLOCK)) // CHUNK
        admissible = key_chunk[None, :] <= q_chunk[:, None]
        s_idx = jnp.einsum('bqhd,bsd->bqhs', iqb.astype(jnp.float32), ik_f)
        score = jnp.einsum('bqh,bqhs->bqs', iwb, jax.nn.relu(s_idx))
        score = jnp.where(admissible[None], score, -jnp.inf)
        _, sel = lax.top_k(score, n_keep)
        valid = key_chunk[sel] <= q_chunk[None, :, None]
        ckv_sel = gather(ckv, sel)
        kpe_sel = gather(kpe, sel)
        q_lat = jnp.einsum('bqhd,rhd->bqhr', qn, w_uk3)
        logits = (jnp.einsum('bqhr,bqkr->bqhk', q_lat, ckv_sel)
                  + jnp.einsum('bqhe,bqke->bqhk', qp, kpe_sel)).astype(jnp.float32) * scale
        logits = jnp.where(valid[:, :, None, :], logits, -jnp.inf)
        p = jax.nn.softmax(logits, axis=-1).astype(ckv.dtype)
        o_lat = jnp.einsum('bqhk,bqkr->bqhr', p, ckv_sel)
        return jnp.einsum('bqhr,rhd->bqhd', o_lat, w_uv3)

    o = lax.map(attend, (jnp.arange(n_blk), blocks(q_nope), blocks(q_pe), blocks(iq), blocks(iw)))
    return jnp.moveaxis(o, 0, 1).reshape(bsz, t, MLA_O_WIDTH)


LANES = 128
PROJ_TN = 1024
_IN_OFF = np.concatenate([[0], np.cumsum(IN_SIZES)]).tolist()
(_O_GQ, _O_GK, _O_GV, _O_GZ, _O_GA, _O_GB, _O_MQ, _O_CKV, _O_KPE, _O_IQ, _O_IK, _O_IW,
 _O_GATE_A, _O_GATE_B) = _IN_OFF[:-1]
P_QKV = 0
P_GZ = 8192
P_QN = 12288
P_GATE_A = 14336
P_GATE_B = 16384
P_QPE = 18432
P_IQ = 19456
P_CKV = 20480
P_KPE_IK = 20992
P_GAW = 21120
P_WIDTH = 21504


def _proj_column_order():
    cols = []
    cols += list(range(_O_GQ, _O_GZ))
    cols += list(range(_O_GZ, _O_GA))
    hd = MLA_NOPE_DIM + MLA_ROPE_DIM
    cols += [_O_MQ + h * hd + i for h in range(MLA_HEADS) for i in range(MLA_NOPE_DIM)]
    cols += list(range(_O_GATE_A, _O_GATE_A + 2 * D_MODEL))
    cols += [_O_MQ + h * hd + MLA_NOPE_DIM + i for h in range(MLA_HEADS) for i in range(MLA_ROPE_DIM)]
    cols += list(range(_O_IQ, _O_IK))
    cols += list(range(_O_CKV, _O_KPE))
    cols += list(range(_O_KPE, _O_IQ)) + list(range(_O_IK, _O_IW))
    cols += list(range(_O_GA, _O_MQ)) + list(range(_O_IW, _O_GATE_A))
    cols += [-1] * (P_WIDTH - len(cols))
    assert len(cols) == P_WIDTH
    return np.asarray(cols, np.int32)


_PROJ_COLS = _proj_column_order()


def _relayout_w_in(w_in):
    cols = jnp.asarray(np.maximum(_PROJ_COLS, 0))
    w = jnp.take(w_in, cols, axis=1)
    w = jnp.where(jnp.asarray(_PROJ_COLS >= 0)[None, :], w, 0.0)
    return w.astype(jnp.bfloat16)


NEG_BIG = -1e30
INT_MIN = -2 ** 31
PREP_TM = 256
IDX_TQ = 128
ATT_TQ = 256
ATT_TK = 512
QK_PAD = 256


def _rope_tables_lane(t):
    cos_m, sin_m = rope_tables(t, MLA_ROPE_DIM)
    cos_i, sin_i = rope_tables(t, IDX_ROPE_DIM)
    one = jnp.ones((t, IDX_DIM - IDX_ROPE_DIM), jnp.float32)
    zero = jnp.zeros_like(one)
    cq = jnp.concatenate([cos_m, cos_m] * 2, axis=1)
    sq = jnp.concatenate([-sin_m, sin_m] * 2, axis=1)
    ci = jnp.concatenate([cos_i, cos_i, one] * 2, axis=1)
    si = jnp.concatenate([-sin_i, sin_i, zero] * 2, axis=1)
    cs = jnp.concatenate([cos_m, cos_m, cos_i, cos_i, one], axis=1)
    ss = jnp.concatenate([-sin_m, sin_m, -sin_i, sin_i, zero], axis=1)
    return jnp.concatenate([cq, sq, ci, si, cs, ss], axis=1)


def _dsa_prep_kernel(qn_ref, qpe_ref, iq_ref, ckv_ref, sm_ref, gw_ref, tab_ref,
                     kvg_ref, lng_ref, lnb_ref, wuk_ref, wuv_ref,
                     q_out, k_out, v_out, iq_out, ik_out, iw_out):
    f32, bf16 = jnp.float32, jnp.bfloat16
    tm = qn_ref.shape[0]
    lane = lax.broadcasted_iota(jnp.int32, (tm, LANES), 1)
    lo64 = lane < 64

    def swap32(x):
        return jnp.where((lane & 32) == 0, pltpu.roll(x, 96, 1), pltpu.roll(x, 32, 1))

    def swap16(x):
        return jnp.where((lane & 16) == 0, pltpu.roll(x, 112, 1), pltpu.roll(x, 16, 1))

    cq, sq = tab_ref[:, 0:128], tab_ref[:, 128:256]
    ci, si = tab_ref[:, 256:384], tab_ref[:, 384:512]
    cs, ss = tab_ref[:, 512:640], tab_ref[:, 640:768]
    scale = (MLA_NOPE_DIM + MLA_ROPE_DIM) ** -0.5

    for j in range(MLA_HEADS // 2):
        t = qpe_ref[:, LANES * j:LANES * (j + 1)]
        r = (t * cq + swap32(t) * sq) * scale
        q_out[0, 2 * j, :, 128:256] = jnp.where(lo64, r, 0.0).astype(bf16)
        q_out[0, 2 * j + 1, :, 128:256] = jnp.where(lo64, pltpu.roll(r, 64, 1), 0.0).astype(bf16)
        t = iq_ref[:, LANES * j:LANES * (j + 1)]
        r = t * ci + swap16(t) * si
        iq_out[0, 2 * j] = jnp.where(lo64, r, 0.0).astype(bf16)
        iq_out[0, 2 * j + 1] = jnp.where(lo64, pltpu.roll(r, 64, 1), 0.0).astype(bf16)
    for h in range(MLA_HEADS):
        q_out[0, h, :, 0:128] = (qn_ref[:, LANES * h:LANES * (h + 1)] * scale).astype(bf16)

    c = ckv_ref[...]
    cn = c * lax.rsqrt(jnp.mean(c * c, axis=-1, keepdims=True) + NORM_EPS) * kvg_ref[...]
    cnb = cn.astype(bf16)
    kn = jnp.dot(cnb, wuk_ref[...], preferred_element_type=f32)
    vv = jnp.dot(cnb, wuv_ref[...], preferred_element_type=f32)

    s = sm_ref[...]
    hi64 = jnp.logical_not(lo64)
    mu = jnp.sum(jnp.where(hi64, s, 0.0), axis=-1, keepdims=True) * (1.0 / IDX_DIM)
    xc = jnp.where(hi64, s - mu, 0.0)
    var = jnp.sum(xc * xc, axis=-1, keepdims=True) * (1.0 / IDX_DIM)
    ln = xc * lax.rsqrt(var + NORM_EPS) * lng_ref[...] + lnb_ref[...]
    y = jnp.where(hi64, ln, s)
    r = y * cs + jnp.where(lo64, swap32(y), swap16(y)) * ss
    kpe_tile = jnp.where(lo64, r, 0.0).astype(bf16)
    for h in range(MLA_HEADS):
        k_out[0, h, :, 0:128] = kn[:, LANES * h:LANES * (h + 1)].astype(bf16)
        k_out[0, h, :, 128:256] = kpe_tile
        v_out[0, h] = vv[:, LANES * h:LANES * (h + 1)].astype(bf16)
    ik_out[0] = jnp.where(lo64, pltpu.roll(r, 64, 1), 0.0).astype(bf16)
    iw_out[...] = gw_ref[...] * (IDX_HEADS ** -0.5 * IDX_DIM ** -0.5)


def _dsa_prep(proj, tab, kv_norm_g, idx_norm_g, idx_norm_b, w_uk, w_uv, bsz, t):
    tm = PREP_TM
    nt = t // tm
    zeros64 = jnp.zeros((64,), jnp.float32)
    lng = jnp.concatenate([zeros64, idx_norm_g.astype(jnp.float32)])[None, :]
    lnb = jnp.concatenate([zeros64, idx_norm_b.astype(jnp.float32)])[None, :]
    kvg = kv_norm_g.astype(jnp.float32)[None, :]

    def rows(w, off):
        return pl.BlockSpec((tm, w), lambda b, i: (b * nt + i, off // w))

    def const(shape):
        return pl.BlockSpec(shape, lambda b, i: (0,) * len(shape))

    def heads(w):
        return pl.BlockSpec((1, MLA_HEADS, tm, w), lambda b, i: (b, 0, i, 0))

    bf16 = jnp.bfloat16
    return pl.pallas_call(
        _dsa_prep_kernel,
        grid=(bsz, nt),
        in_specs=[rows(2048, P_QN), rows(1024, P_QPE), rows(1024, P_IQ), rows(512, P_CKV),
                  rows(LANES, P_KPE_IK), rows(LANES, P_GAW),
                  pl.BlockSpec((tm, 768), lambda b, i: (i, 0)),
                  const((1, MLA_KV_RANK)), const((1, LANES)), const((1, LANES)),
                  const((MLA_KV_RANK, MLA_HEADS * MLA_NOPE_DIM)), const((MLA_KV_RANK, MLA_O_WIDTH))],
        out_specs=[heads(QK_PAD), heads(QK_PAD), heads(MLA_V_DIM), heads(LANES),
                   pl.BlockSpec((1, tm, LANES), lambda b, i: (b, i, 0)),
                   pl.BlockSpec((tm, LANES), lambda b, i: (b * nt + i, 0))],
        out_shape=[jax.ShapeDtypeStruct((bsz, MLA_HEADS, t, QK_PAD), bf16),
                   jax.ShapeDtypeStruct((bsz, MLA_HEADS, t, QK_PAD), bf16),
                   jax.ShapeDtypeStruct((bsz, MLA_HEADS, t, MLA_V_DIM), bf16),
                   jax.ShapeDtypeStruct((bsz, IDX_HEADS, t, LANES), bf16),
                   jax.ShapeDtypeStruct((bsz, t, LANES), bf16),
                   jax.ShapeDtypeStruct((bsz * t, LANES), jnp.float32)],
        compiler_params=pltpu.CompilerParams(
            dimension_semantics=("arbitrary", "arbitrary"), vmem_limit_bytes=56 * 1024 * 1024),
        name="dsa_prep",
    )(proj, proj, proj, proj, proj, proj, tab, kvg, lng, lnb,
      w_uk.astype(bf16), w_uv.astype(bf16))


def _indexer_kernel(iq_ref, ik_ref, iw_ref, bias_ref, key_scr, tie_scr, *, n_keep):
    f32, i32 = jnp.float32, jnp.int32
    tq = iq_ref.shape[2]
    kc = key_scr.shape[2]
    nch_total = key_scr.shape[0]
    qi = pl.program_id(1)
    nch = ((qi + 1) * tq + kc - 1) // kc
    q_chunk = (lax.broadcasted_iota(i32, (tq, kc), 0) + qi * tq) >> 6
    col0 = lax.broadcasted_iota(i32, (tq, kc), 1)
    n_sub = kc // LANES

    def score_chunk(c, carry):
        ik = ik_ref[0, pl.ds(pl.multiple_of(c * kc, kc), kc), :]
        acc = jnp.zeros((tq, kc), f32)
        for h in range(IDX_HEADS):
            s = lax.dot_general(iq_ref[0, h], ik, (((1,), (1,)), ((), ())), preferred_element_type=f32)
            acc = acc + iw_ref[:, 64 + h:65 + h] * jnp.maximum(s, 0.0)
        bits = pltpu.bitcast(acc, i32)
        key = bits ^ ((bits >> 31) & 0x7FFFFFFF)
        adm = ((col0 + c * kc) >> 6) <= q_chunk
        key_scr[c] = jnp.where(adm, key, INT_MIN)
        return carry

    lax.fori_loop(0, nch, score_chunk, 0)

    def lane_fold(x):
        part = x[:, 0:LANES]
        for j in range(1, n_sub):
            part = part + x[:, LANES * j:LANES * (j + 1)]
        return part

    def rep(x):
        return jnp.concatenate([x] * n_sub, axis=1)

    def count(pred_fn):
        def body(c, acc):
            return acc + lane_fold(jnp.where(pred_fn(key_scr[c], c), 1.0, 0.0))
        acc = lax.fori_loop(0, nch, body, jnp.zeros((tq, LANES), f32))
        return jnp.broadcast_to(jnp.sum(acc, axis=1, keepdims=True), (tq, LANES))

    keep = float(n_keep)
    cnt = count(lambda k, c: k >= 0)
    thr = jnp.where(cnt >= keep, 0, INT_MIN).astype(i32)

    def bit_pass(i, thr):
        cand = thr | jnp.left_shift(jnp.int32(1), 30 - i)
        cand_w = rep(cand)
        cnt = count(lambda k, c: k >= cand_w)
        return jnp.where(cnt >= keep, cand, thr)

    thr = lax.fori_loop(0, 31, bit_pass, thr)
    thr_w = rep(thr)
    cnt_ge = count(lambda k, c: k >= thr_w)
    tie_scr[...] = jnp.full((tq, LANES), 2 ** 30, i32)

    @pl.when(jnp.max(cnt_ge) > keep)
    def _():
        cnt_gt = count(lambda k, c: k > thr_w)
        need = keep - cnt_gt

        def idx_pass(i, x):
            cand = x | jnp.left_shift(jnp.int32(1), idx_bits - 1 - i)
            cand_w = rep(cand)
            cnt = count(lambda k, c: (k == thr_w) & ((col0 + c * kc) < cand_w))
            return jnp.where(cnt < need, cand, x)

        idx_bits = (nch_total * kc - 1).bit_length()
        x = lax.fori_loop(0, idx_bits, idx_pass, jnp.zeros((tq, LANES), i32))
        tie_scr[...] = jnp.where(cnt_ge > keep, x, 2 ** 30)

    tie_w = rep(tie_scr[...])

    def write_chunk(c, carry):
        k = key_scr[c]
        sel = (k > thr_w) | ((k == thr_w) & ((col0 + c * kc) <= tie_w))
        sel = sel & (k != INT_MIN)
        bias_ref[0, c] = jnp.where(sel, 0.0, NEG_BIG)
        return carry

    lax.fori_loop(0, nch, write_chunk, 0)

    def fill_chunk(c, carry):
        bias_ref[0, c] = jnp.full((tq, kc), NEG_BIG, f32)
        return carry

    lax.fori_loop(nch, nch_total, fill_chunk, 0)


def _indexer(iq, ik, iw, bsz, t, n_keep):
    tq, kc = IDX_TQ, ATT_TK
    nq, nkc = t // tq, t // kc
    return pl.pallas_call(
        functools.partial(_indexer_kernel, n_keep=n_keep),
        grid=(bsz, nq),
        in_specs=[pl.BlockSpec((1, IDX_HEADS, tq, LANES), lambda b, i: (b, 0, i, 0)),
                  pl.BlockSpec((1, t, LANES), lambda b, i: (b, 0, 0)),
                  pl.BlockSpec((tq, LANES), lambda b, i: (b * nq + i, 0))],
        out_specs=pl.BlockSpec((1, nkc, tq, kc), lambda b, i: (b, 0, i, 0)),
        out_shape=jax.ShapeDtypeStruct((bsz, nkc, t, kc), jnp.float32),
        scratch_shapes=[pltpu.VMEM((nkc, tq, kc), jnp.int32), pltpu.VMEM((tq, LANES), jnp.int32)],
        compiler_params=pltpu.CompilerParams(
            dimension_semantics=("arbitrary", "arbitrary"), vmem_limit_bytes=56 * 1024 * 1024),
        name="dsa_indexer",
    )(iq, ik, iw)


def _attn_kernel(q_ref, k_ref, v_ref, b_ref, o_ref, m_scr, l_scr, acc_scr):
    f32 = jnp.float32
    tq, tk = b_ref.shape[2], b_ref.shape[3]
    qi, ki = pl.program_id(1), pl.program_id(2)
    last = ((qi + 1) * tq - 1) // tk

    @pl.when(ki == 0)
    def _():
        m_scr[...] = jnp.full(m_scr.shape, NEG_BIG, f32)
        l_scr[...] = jnp.zeros(l_scr.shape, f32)
        acc_scr[...] = jnp.zeros(acc_scr.shape, f32)

    @pl.when(ki <= last)
    def _():
        bias = b_ref[0, 0]

        def head(h, carry):
            s = lax.dot_general(q_ref[0, h], k_ref[0, h], (((1,), (1,)), ((), ())),
                                preferred_element_type=f32) + bias
            m_prev = m_scr[h]
            m_new = jnp.maximum(m_prev, jnp.max(s, axis=1, keepdims=True))
            alpha = jnp.exp(m_prev - m_new)
            p = jnp.exp(s - m_new[:, 0:1])
            l_scr[h] = alpha * l_scr[h] + jnp.sum(p, axis=1, keepdims=True)
            acc_scr[h] = alpha * acc_scr[h] + jnp.dot(p.astype(jnp.bfloat16), v_ref[0, h],
                                                      preferred_element_type=f32)
            m_scr[h] = m_new
            return carry

        lax.fori_loop(0, MLA_HEADS, head, 0)

    @pl.when(ki == last)
    def _():
        for h in range(MLA_HEADS):
            o_ref[0, :, LANES * h:LANES * (h + 1)] = (acc_scr[h] / l_scr[h]).astype(o_ref.dtype)


def _masked_attention(q, k, v, bias, bsz, t):
    tq, tk = ATT_TQ, ATT_TK
    nq, nk = t // tq, t // tk

    def kmap(b, i, j):
        return (b, 0, jnp.minimum(j, ((i + 1) * tq - 1) // tk), 0)

    return pl.pallas_call(
        _attn_kernel,
        grid=(bsz, nq, nk),
        in_specs=[pl.BlockSpec((1, MLA_HEADS, tq, QK_PAD), lambda b, i, j: (b, 0, i, 0)),
                  pl.BlockSpec((1, MLA_HEADS, tk, QK_PAD), kmap),
                  pl.BlockSpec((1, MLA_HEADS, tk, MLA_V_DIM), kmap),
                  pl.BlockSpec((1, 1, tq, tk),
                               lambda b, i, j: (b, jnp.minimum(j, ((i + 1) * tq - 1) // tk), i, 0))],
        out_specs=pl.BlockSpec((1, tq, MLA_O_WIDTH), lambda b, i, j: (b, i, 0)),
        out_shape=jax.ShapeDtypeStruct((bsz, t, MLA_O_WIDTH), jnp.bfloat16),
        scratch_shapes=[pltpu.VMEM((MLA_HEADS, tq, LANES), jnp.float32),
                        pltpu.VMEM((MLA_HEADS, tq, LANES), jnp.float32),
                        pltpu.VMEM((MLA_HEADS, tq, MLA_V_DIM), jnp.float32)],
        compiler_params=pltpu.CompilerParams(
            dimension_semantics=("arbitrary", "arbitrary", "arbitrary"),
            vmem_limit_bytes=56 * 1024 * 1024),
        name="dsa_attention",
    )(q, k, v, bias)


def dsa_branch_pallas(proj, kv_norm_g, w_uk, w_uv, idx_norm_g, idx_norm_b, bsz, t):
    n_keep = min(TOPK_MAX, t // 4)
    tab = _rope_tables_lane(t)
    q, k, v, iq, ik, iw = _dsa_prep(proj, tab, kv_norm_g, idx_norm_g, idx_norm_b, w_uk, w_uv, bsz, t)
    bias = _indexer(iq, ik, iw, bsz, t, n_keep)
    o = _masked_attention(q, k, v, bias, bsz, t)
    return o.reshape(bsz * t, MLA_O_WIDTH)


def hybrid_mixer(h, w_in, conv_w, a_log, dt_bias, gdn_norm_g, kv_norm_g, w_uk, w_uv,
                 idx_norm_g, idx_norm_b, w_branch_gdn, w_branch_dsa, w_out):
    bsz, t, d = h.shape
    n = bsz * t
    proj = _matmul(h.reshape(n, d), _relayout_w_in(w_in), tn=PROJ_TN)

    def cols(off, width):
        return proj[:, off:off + width].reshape(bsz, t, width)

    gq, gk, gv = cols(P_QKV, 2048), cols(P_QKV + 2048, 2048), cols(P_QKV + 4096, 4096)
    gz = cols(P_GZ, 4096)
    ga, gb = cols(P_GAW, 32), cols(P_GAW + 32, 32)
    gate_a, gate_b = cols(P_GATE_A, 2048), cols(P_GATE_B, 2048)
    o_a = gdn_branch(gq, gk, gv, gz, ga, gb, conv_w, a_log, dt_bias, gdn_norm_g)
    o_b = dsa_branch_pallas(proj, kv_norm_g, w_uk, w_uv, idx_norm_g, idx_norm_b, bsz, t)
    u_a = _matmul(o_a.reshape(n, -1), w_branch_gdn).reshape(bsz, t, d)
    u_b = _matmul(o_b, w_branch_dsa).reshape(bsz, t, d)
    merged = jax.nn.sigmoid(gate_a) * u_a + jax.nn.sigmoid(gate_b) * u_b
    return _matmul(merged.reshape(n, d), w_out).reshape(bsz, t, d)


def moe_ffn(h, router_w, router_b, w_gate, b_gate, w_up, b_up, w_down, b_down):
    bsz, t, d = h.shape
    tok = h.reshape(-1, d)
    n_tok = tok.shape[0]
    n_assign = n_tok * TOP_K
    logits = tok.astype(jnp.float32) @ router_w.astype(jnp.float32) + router_b.astype(jnp.float32)
    top_val, top_idx = lax.top_k(logits, TOP_K)
    gates = jax.nn.softmax(top_val, axis=-1)
    flat_e = top_idx.reshape(-1)
    flat_tok = jnp.repeat(jnp.arange(n_tok, dtype=jnp.int32), TOP_K)
    order = jnp.argsort(flat_e)
    sorted_e = flat_e[order]
    counts = jnp.bincount(flat_e, length=N_EXPERTS)
    padded = (counts + EXPERT_BLOCK - 1) // EXPERT_BLOCK * EXPERT_BLOCK
    pad_end = jnp.cumsum(padded)
    pad_start = pad_end - padded
    start = jnp.cumsum(counts) - counts
    slot_sorted = jnp.arange(n_assign, dtype=jnp.int32) - start[sorted_e] + pad_start[sorted_e]
    slot = jnp.zeros((n_assign,), jnp.int32).at[order].set(slot_sorted.astype(jnp.int32))
    n_blocks = -(-n_assign // EXPERT_BLOCK) + N_EXPERTS
    slot_tok = jnp.zeros((n_blocks * EXPERT_BLOCK,), jnp.int32).at[slot].set(flat_tok)
    block_expert = jnp.minimum(jnp.searchsorted(pad_end, jnp.arange(n_blocks) * EXPERT_BLOCK, side='right'),
                               N_EXPERTS - 1)
    xs = tok[slot_tok].reshape(n_blocks, EXPERT_BLOCK, d)

    def expert(inp):
        xb, e = inp
        gate = xb @ w_gate[e] + b_gate[e]
        up = xb @ w_up[e] + b_up[e]
        gate = jnp.minimum(gate, SWIGLU_LIMIT)
        up = jnp.clip(up, -SWIGLU_LIMIT, SWIGLU_LIMIT)
        act = (up + 1.0) * (gate * jax.nn.sigmoid(gate * SWIGLU_ALPHA))
        return act @ w_down[e] + b_down[e]

    ys = lax.map(expert, (xs, block_expert)).reshape(-1, d)
    y = ys[slot].reshape(n_tok, TOP_K, d)
    out = jnp.einsum('tk,tkd->td', gates.astype(y.dtype), y)
    return out.reshape(bsz, t, d)


def kernel(x, norm_mix_g, w_in, gdn_conv_w, gdn_a_log, gdn_dt_bias, gdn_norm_g, mla_kv_norm_g,
           mla_w_uk, mla_w_uv, idx_k_norm_g, idx_k_norm_b, w_branch_gdn, w_branch_dsa, w_out,
           norm_ffn_g, router_w, router_b, exp_w_gate, exp_b_gate, exp_w_up, exp_b_up,
           exp_w_down, exp_b_down, norm_final_g):
    l = 0
    x = x + hybrid_mixer(rms_norm(x, norm_mix_g[l]), w_in[l], gdn_conv_w[l], gdn_a_log[l], gdn_dt_bias[l],
                         gdn_norm_g[l], mla_kv_norm_g[l], mla_w_uk[l], mla_w_uv[l], idx_k_norm_g[l],
                         idx_k_norm_b[l], w_branch_gdn[l], w_branch_dsa[l], w_out[l])
    x = x + moe_ffn(rms_norm(x, norm_ffn_g[l]), router_w[l], router_b[l], exp_w_gate[l], exp_b_gate[l],
                    exp_w_up[l], exp_b_up[l], exp_w_down[l], exp_b_down[l])
    return rms_norm(x, norm_final_g)
```

```python
import functools
import math
import numpy as np
import jax
import jax.numpy as jnp
from jax import lax
from jax.experimental import pallas as pl
from jax.experimental.pallas import tpu as pltpu

D_MODEL = 2048
CHUNK = 64
CHUNK_SHIFT = CHUNK.bit_length() - 1
ROPE_THETA = 10000.0
NORM_EPS = 1e-6
GDN_K_HEADS = 16
GDN_V_HEADS = 32
GDN_HEAD_DIM = 128
GDN_QK_WIDTH = GDN_K_HEADS * GDN_HEAD_DIM
GDN_V_WIDTH = GDN_V_HEADS * GDN_HEAD_DIM
MLA_HEADS = 16
MLA_NOPE_DIM = 128
MLA_ROPE_DIM = 64
MLA_V_DIM = 128
MLA_KV_RANK = 512
MLA_Q_WIDTH = MLA_HEADS * (MLA_NOPE_DIM + MLA_ROPE_DIM)
MLA_O_WIDTH = MLA_HEADS * MLA_V_DIM
IDX_HEADS = 16
IDX_DIM = 64
IDX_ROPE_DIM = 32
TOPK_MAX = 256
Q_BLOCK = 128
N_EXPERTS = 32
TOP_K = 4
D_EXPERT = 2048
SWIGLU_LIMIT = 7.0
SWIGLU_ALPHA = 1.702
EXPERT_BLOCK = 256
IN_SIZES = (GDN_QK_WIDTH, GDN_QK_WIDTH, GDN_V_WIDTH, GDN_V_WIDTH, GDN_V_HEADS, GDN_V_HEADS,
            MLA_Q_WIDTH, MLA_KV_RANK, MLA_ROPE_DIM, IDX_HEADS * IDX_DIM, IDX_DIM, IDX_HEADS,
            D_MODEL, D_MODEL)


def _mm_kernel(a_ref, b_ref, o_ref):
    o_ref[...] = jnp.dot(a_ref[...], b_ref[...], preferred_element_type=jnp.float32)


def _matmul(a, b, tm=512, tn=1024):
    m, k = a.shape
    n = b.shape[1]
    n_pad = -(-n // tn) * tn
    a = a.astype(jnp.bfloat16)
    b = b.astype(jnp.bfloat16)
    if n_pad != n:
        b = jnp.pad(b, ((0, 0), (0, n_pad - n)))
    out = pl.pallas_call(
        _mm_kernel,
        grid=(n_pad // tn, m // tm),
        in_specs=[pl.BlockSpec((tm, k), lambda j, i: (i, 0)),
                  pl.BlockSpec((k, tn), lambda j, i: (0, j))],
        out_specs=pl.BlockSpec((tm, tn), lambda j, i: (i, j)),
        out_shape=jax.ShapeDtypeStruct((m, n_pad), jnp.float32),
        compiler_params=pltpu.CompilerParams(
            dimension_semantics=("arbitrary", "arbitrary"),
            vmem_limit_bytes=56 * 1024 * 1024),
        name="dense_matmul",
    )(a, b)
    return out[:, :n] if n_pad != n else out


def _split(x, sizes):
    cuts = np.cumsum(sizes)[:-1].tolist()
    return jnp.split(x, cuts, axis=-1)


def rms_norm(x, g):
    xf = x.astype(jnp.float32)
    y = xf * lax.rsqrt(jnp.mean(xf * xf, axis=-1, keepdims=True) + NORM_EPS)
    return (y * g.astype(jnp.float32)).astype(x.dtype)


def layer_norm(x, g, b):
    xf = x.astype(jnp.float32)
    mu = jnp.mean(xf, axis=-1, keepdims=True)
    xc = xf - mu
    y = xc * lax.rsqrt(jnp.mean(xc * xc, axis=-1, keepdims=True) + NORM_EPS)
    return (y * g.astype(jnp.float32) + b.astype(jnp.float32)).astype(x.dtype)


def l2_norm(x):
    return x * lax.rsqrt(jnp.sum(x * x, axis=-1, keepdims=True) + NORM_EPS)


def rope_tables(n, dim):
    inv = ROPE_THETA ** (-jnp.arange(0, dim, 2, dtype=jnp.float32) / dim)
    ang = jnp.arange(n, dtype=jnp.float32)[:, None] * inv[None, :]
    return jnp.cos(ang), jnp.sin(ang)


def apply_rope(x, cos, sin):
    xf = x.astype(jnp.float32)
    half = x.shape[-1] // 2
    x1, x2 = xf[..., :half], xf[..., half:]
    c = cos[None, :, None, :]
    s = sin[None, :, None, :]
    return jnp.concatenate([x1 * c - x2 * s, x2 * c + x1 * s], axis=-1).astype(x.dtype)


def causal_depthwise_conv(x, w):
    width, ch = w.shape
    return lax.conv_general_dilated(x, w[:, None, :].astype(x.dtype), window_strides=(1,),
                                    padding=[(width - 1, 0)],
                                    dimension_numbers=('NWC', 'WIO', 'NWC'),
                                    feature_group_count=ch)


def gated_delta_rule(q, k, v, g, beta):
    bsz, t, h, dk = q.shape
    dv = v.shape[-1]
    n = t // CHUNK

    def to_chunks(a):
        return jnp.moveaxis(a.reshape((bsz, n, CHUNK, h) + a.shape[3:]), 3, 1)

    qc, kc, vc, gc, bc = (to_chunks(a) for a in (q, k, v, g, beta))
    G = jnp.cumsum(gc, axis=-1)
    pos = jnp.arange(CHUNK)
    incl = pos[:, None] >= pos[None, :]
    strict = pos[:, None] > pos[None, :]
    decay = jnp.exp(jnp.where(incl, G[..., :, None] - G[..., None, :], -jnp.inf))
    kk = jnp.einsum('bhncd,bhnsd->bhncs', kc, kc)
    a_mat = jnp.where(strict, bc[..., :, None] * kk * decay, 0.0)
    eye = jnp.eye(CHUNK, dtype=jnp.float32)
    t_mat = lax.linalg.triangular_solve(eye + a_mat, jnp.broadcast_to(eye, a_mat.shape),
                                        left_side=True, lower=True, unit_diagonal=True)
    u = jnp.einsum('bhncs,bhnsd->bhncd', t_mat, vc * bc[..., None])
    w = jnp.einsum('bhncs,bhnsd->bhncd', t_mat, kc * (bc * jnp.exp(G))[..., None])
    qk = jnp.einsum('bhncd,bhnsd->bhncs', qc, kc) * decay
    q_dec = qc * jnp.exp(G)[..., None]
    k_dec = kc * jnp.exp(G[..., -1:] - G)[..., None]
    chunk_decay = jnp.exp(G[..., -1])

    def step(state, inp):
        u_n, w_n, qk_n, qd_n, kd_n, cd_n = inp
        v_new = u_n - jnp.einsum('bhcd,bhde->bhce', w_n, state)
        o_n = jnp.einsum('bhcd,bhde->bhce', qd_n, state) + jnp.einsum('bhcs,bhse->bhce', qk_n, v_new)
        state = state * cd_n[..., None, None] + jnp.einsum('bhcd,bhce->bhde', kd_n, v_new)
        return state, o_n

    xs = tuple(jnp.moveaxis(a, 2, 0) for a in (u, w, qk, q_dec, k_dec, chunk_decay))
    s0 = jnp.zeros((bsz, h, dk, dv), jnp.float32)
    _, o = lax.scan(step, s0, xs)
    o = jnp.moveaxis(o, 0, 2)
    return jnp.moveaxis(o, 1, 3).reshape(bsz, t, h, dv)


def gdn_branch(q, k, v, z, a, b, conv_w, a_log, dt_bias, norm_g):
    dtype = q.dtype
    bsz, t = q.shape[:2]
    qkv = jax.nn.silu(causal_depthwise_conv(jnp.concatenate([q, k, v], axis=-1), conv_w))
    q, k, v = _split(qkv, (GDN_QK_WIDTH, GDN_QK_WIDTH, GDN_V_WIDTH))
    q = l2_norm(q.reshape(bsz, t, GDN_K_HEADS, GDN_HEAD_DIM).astype(jnp.float32)) * (GDN_HEAD_DIM ** -0.5)
    k = l2_norm(k.reshape(bsz, t, GDN_K_HEADS, GDN_HEAD_DIM).astype(jnp.float32))
    rep = GDN_V_HEADS // GDN_K_HEADS
    q = jnp.repeat(q, rep, axis=2)
    k = jnp.repeat(k, rep, axis=2)
    v = v.reshape(bsz, t, GDN_V_HEADS, GDN_HEAD_DIM).astype(jnp.float32)
    beta = jax.nn.sigmoid(b.astype(jnp.float32))
    g = -jnp.exp(a_log.astype(jnp.float32)) * jax.nn.softplus(a.astype(jnp.float32) + dt_bias.astype(jnp.float32))
    o = gated_delta_rule(q, k, v, g, beta)
    zf = z.reshape(bsz, t, GDN_V_HEADS, GDN_HEAD_DIM).astype(jnp.float32)
    o = rms_norm(o, norm_g) * jax.nn.silu(zf)
    return o.reshape(bsz, t, GDN_V_WIDTH).astype(dtype)


def dsa_branch(q, ckv, kpe, iq, ik, iw, kv_norm_g, w_uk, w_uv, idx_norm_g, idx_norm_b):
    bsz, t = q.shape[:2]
    n_keep = min(TOPK_MAX, t // 4)
    cos_m, sin_m = rope_tables(t, MLA_ROPE_DIM)
    cos_i, sin_i = rope_tables(t, IDX_ROPE_DIM)
    q = q.reshape(bsz, t, MLA_HEADS, MLA_NOPE_DIM + MLA_ROPE_DIM)
    q_nope = q[..., :MLA_NOPE_DIM]
    q_pe = apply_rope(q[..., MLA_NOPE_DIM:], cos_m, sin_m)
    ckv = rms_norm(ckv, kv_norm_g)
    kpe = apply_rope(kpe[:, :, None, :], cos_m, sin_m)[:, :, 0]
    iq = iq.reshape(bsz, t, IDX_HEADS, IDX_DIM)
    iq = jnp.concatenate([apply_rope(iq[..., :IDX_ROPE_DIM], cos_i, sin_i), iq[..., IDX_ROPE_DIM:]], axis=-1)
    ik = layer_norm(ik, idx_norm_g, idx_norm_b)
    ik = jnp.concatenate([apply_rope(ik[:, :, None, :IDX_ROPE_DIM], cos_i, sin_i)[:, :, 0],
                          ik[..., IDX_ROPE_DIM:]], axis=-1)
    ik_f = ik.astype(jnp.float32)
    iw = iw.astype(jnp.float32) * (IDX_HEADS ** -0.5 * IDX_DIM ** -0.5)
    w_uk3 = w_uk.reshape(MLA_KV_RANK, MLA_HEADS, MLA_NOPE_DIM)
    w_uv3 = w_uv.reshape(MLA_KV_RANK, MLA_HEADS, MLA_V_DIM)
    scale = (MLA_NOPE_DIM + MLA_ROPE_DIM) ** -0.5
    n_blk = t // Q_BLOCK
    key_chunk = jnp.arange(t) // CHUNK
    gather = jax.vmap(lambda src, ids: src[ids])

    def blocks(a):
        return jnp.moveaxis(a.reshape((bsz, n_blk, Q_BLOCK) + a.shape[2:]), 1, 0)

    def attend(inp):
        blk, qn, qp, iqb, iwb = inp
        q_chunk = (blk * Q_BLOCK + jnp.arange(Q_BLOCK)) // CHUNK
        admissible = key_chunk[None, :] <= q_chunk[:, None]
        s_idx = jnp.einsum('bqhd,bsd->bqhs', iqb.astype(jnp.float32), ik_f)
        score = jnp.einsum('bqh,bqhs->bqs', iwb, jax.nn.relu(s_idx))
        score = jnp.where(admissible[None], score, -jnp.inf)
        _, sel = lax.top_k(score, n_keep)
        valid = key_chunk[sel] <= q_chunk[None, :, None]
        ckv_sel = gather(ckv, sel)
        kpe_sel = gather(kpe, sel)
        q_lat = jnp.einsum('bqhd,rhd->bqhr', qn, w_uk3)
        logits = (jnp.einsum('bqhr,bqkr->bqhk', q_lat, ckv_sel)
                  + jnp.einsum('bqhe,bqke->bqhk', qp, kpe_sel)).astype(jnp.float32) * scale
        logits = jnp.where(valid[:, :, None, :], logits, -jnp.inf)
        p = jax.nn.softmax(logits, axis=-1).astype(ckv.dtype)
        o_lat = jnp.einsum('bqhk,bqkr->bqhr', p, ckv_sel)
        return jnp.einsum('bqhr,rhd->bqhd', o_lat, w_uv3)

    o = lax.map(attend, (jnp.arange(n_blk), blocks(q_nope), blocks(q_pe), blocks(iq), blocks(iw)))
    return jnp.moveaxis(o, 0, 1).reshape(bsz, t, MLA_O_WIDTH)


LANES = 128
PROJ_TN = 1024
_IN_OFF = np.concatenate([[0], np.cumsum(IN_SIZES)]).tolist()
(_O_GQ, _O_GK, _O_GV, _O_GZ, _O_GA, _O_GB, _O_MQ, _O_CKV, _O_KPE, _O_IQ, _O_IK, _O_IW,
 _O_GATE_A, _O_GATE_B) = _IN_OFF[:-1]
P_QKV = 0
P_GZ = 8192
P_QN = 12288
P_GATE_A = 14336
P_GATE_B = 16384
P_QPE = 18432
P_IQ = 19456
P_CKV = 20480
P_KPE_IK = 20992
P_GAW = 21120
P_WIDTH = 21504


def _proj_column_order():
    cols = []
    cols += list(range(_O_GQ, _O_GZ))
    cols += list(range(_O_GZ, _O_GA))
    hd = MLA_NOPE_DIM + MLA_ROPE_DIM
    cols += [_O_MQ + h * hd + i for h in range(MLA_HEADS) for i in range(MLA_NOPE_DIM)]
    cols += list(range(_O_GATE_A, _O_GATE_A + 2 * D_MODEL))
    cols += [_O_MQ + h * hd + MLA_NOPE_DIM + i for h in range(MLA_HEADS) for i in range(MLA_ROPE_DIM)]
    cols += list(range(_O_IQ, _O_IK))
    cols += list(range(_O_CKV, _O_KPE))
    cols += list(range(_O_KPE, _O_IQ)) + list(range(_O_IK, _O_IW))
    cols += list(range(_O_GA, _O_MQ)) + list(range(_O_IW, _O_GATE_A))
    cols += [-1] * (P_WIDTH - len(cols))
    assert len(cols) == P_WIDTH
    return np.asarray(cols, np.int32)


_PROJ_COLS = _proj_column_order()


def _relayout_w_in(w_in):
    cols = jnp.asarray(np.maximum(_PROJ_COLS, 0))
    w = jnp.take(w_in, cols, axis=1)
    w = jnp.where(jnp.asarray(_PROJ_COLS >= 0)[None, :], w, 0.0)
    return w.astype(jnp.bfloat16)


NEG_BIG = -1e30
INT_MIN = -2 ** 31
PREP_TM = 256
IDX_TQ = 128
ATT_TQ = 512
ATT_TK = 512
QK_PAD = 256
ATT_HEAD_UNROLL = 2


def _rope_tables_lane(t):
    cos_m, sin_m = rope_tables(t, MLA_ROPE_DIM)
    cos_i, sin_i = rope_tables(t, IDX_ROPE_DIM)
    one = jnp.ones((t, IDX_DIM - IDX_ROPE_DIM), jnp.float32)
    zero = jnp.zeros_like(one)
    cq = jnp.concatenate([cos_m, cos_m] * 2, axis=1)
    sq = jnp.concatenate([-sin_m, sin_m] * 2, axis=1)
    ci = jnp.concatenate([cos_i, cos_i, one] * 2, axis=1)
    si = jnp.concatenate([-sin_i, sin_i, zero] * 2, axis=1)
    cs = jnp.concatenate([cos_m, cos_m, cos_i, cos_i, one], axis=1)
    ss = jnp.concatenate([-sin_m, sin_m, -sin_i, sin_i, zero], axis=1)
    return jnp.concatenate([cq, sq, ci, si, cs, ss], axis=1)


def _dsa_prep_kernel(qn_ref, qpe_ref, iq_ref, ckv_ref, sm_ref, gw_ref, tab_ref,
                     kvg_ref, lng_ref, lnb_ref, wuk_ref, wuv_ref,
                     q_out, k_out, v_out, iq_out, ik_out, iw_out):
    f32, bf16 = jnp.float32, jnp.bfloat16
    tm = qn_ref.shape[0]
    lane = lax.broadcasted_iota(jnp.int32, (tm, LANES), 1)
    lo64 = lane < 64

    def swap32(x):
        return jnp.where((lane & 32) == 0, pltpu.roll(x, 96, 1), pltpu.roll(x, 32, 1))

    def swap16(x):
        return jnp.where((lane & 16) == 0, pltpu.roll(x, 112, 1), pltpu.roll(x, 16, 1))

    cq, sq = tab_ref[:, 0:128], tab_ref[:, 128:256]
    ci, si = tab_ref[:, 256:384], tab_ref[:, 384:512]
    cs, ss = tab_ref[:, 512:640], tab_ref[:, 640:768]
    scale = (MLA_NOPE_DIM + MLA_ROPE_DIM) ** -0.5 * math.log2(math.e)

    for j in range(MLA_HEADS // 2):
        t = qpe_ref[:, LANES * j:LANES * (j + 1)]
        r = (t * cq + swap32(t) * sq) * scale
        q_out[0, 2 * j, :, 128:256] = jnp.where(lo64, r, 0.0).astype(bf16)
        q_out[0, 2 * j + 1, :, 128:256] = jnp.where(lo64, pltpu.roll(r, 64, 1), 0.0).astype(bf16)
        t = iq_ref[:, LANES * j:LANES * (j + 1)]
        r = t * ci + swap16(t) * si
        iq_out[0, 2 * j] = jnp.where(lo64, r, 0.0).astype(bf16)
        iq_out[0, 2 * j + 1] = jnp.where(lo64, pltpu.roll(r, 64, 1), 0.0).astype(bf16)
    for h in range(MLA_HEADS):
        q_out[0, h, :, 0:128] = (qn_ref[:, LANES * h:LANES * (h + 1)] * scale).astype(bf16)

    c = ckv_ref[...]
    cn = c * lax.rsqrt(jnp.mean(c * c, axis=-1, keepdims=True) + NORM_EPS) * kvg_ref[...]
    cnb = cn.astype(bf16)
    kn = jnp.dot(cnb, wuk_ref[...], preferred_element_type=f32)
    vv = jnp.dot(cnb, wuv_ref[...], preferred_element_type=f32)

    s = sm_ref[...]
    hi64 = jnp.logical_not(lo64)
    mu = jnp.sum(jnp.where(hi64, s, 0.0), axis=-1, keepdims=True) * (1.0 / IDX_DIM)
    xc = jnp.where(hi64, s - mu, 0.0)
    var = jnp.sum(xc * xc, axis=-1, keepdims=True) * (1.0 / IDX_DIM)
    ln = xc * lax.rsqrt(var + NORM_EPS) * lng_ref[...] + lnb_ref[...]
    y = jnp.where(hi64, ln, s)
    r = y * cs + jnp.where(lo64, swap32(y), swap16(y)) * ss
    kpe_t = jnp.where(lo64, r, 0.0).T.astype(bf16)
    for h in range(MLA_HEADS):
        k_out[0, h, 0:128, :] = kn[:, LANES * h:LANES * (h + 1)].T.astype(bf16)
        k_out[0, h, 128:256, :] = kpe_t
        v_out[0, h] = vv[:, LANES * h:LANES * (h + 1)].astype(bf16)
    ik_out[0] = jnp.where(lo64, pltpu.roll(r, 64, 1), 0.0).astype(bf16)
    iw_out[...] = gw_ref[...] * (IDX_HEADS ** -0.5 * IDX_DIM ** -0.5)


def _dsa_prep(proj, tab, kv_norm_g, idx_norm_g, idx_norm_b, w_uk, w_uv, bsz, t):
    tm = PREP_TM
    nt = t // tm
    zeros64 = jnp.zeros((64,), jnp.float32)
    lng = jnp.concatenate([zeros64, idx_norm_g.astype(jnp.float32)])[None, :]
    lnb = jnp.concatenate([zeros64, idx_norm_b.astype(jnp.float32)])[None, :]
    kvg = kv_norm_g.astype(jnp.float32)[None, :]

    def rows(w, off):
        return pl.BlockSpec((tm, w), lambda b, i: (b * nt + i, off // w))

    def const(shape):
        return pl.BlockSpec(shape, lambda b, i: (0,) * len(shape))

    def heads(w):
        return pl.BlockSpec((1, MLA_HEADS, tm, w), lambda b, i: (b, 0, i, 0))

    bf16 = jnp.bfloat16
    return pl.pallas_call(
        _dsa_prep_kernel,
        grid=(bsz, nt),
        in_specs=[rows(2048, P_QN), rows(1024, P_QPE), rows(1024, P_IQ), rows(512, P_CKV),
                  rows(LANES, P_KPE_IK), rows(LANES, P_GAW),
                  pl.BlockSpec((tm, 768), lambda b, i: (i, 0)),
                  const((1, MLA_KV_RANK)), const((1, LANES)), const((1, LANES)),
                  const((MLA_KV_RANK, MLA_HEADS * MLA_NOPE_DIM)), const((MLA_KV_RANK, MLA_O_WIDTH))],
        out_specs=[heads(QK_PAD), pl.BlockSpec((1, MLA_HEADS, QK_PAD, tm), lambda b, i: (b, 0, 0, i)),
                   heads(MLA_V_DIM), heads(LANES),
                   pl.BlockSpec((1, tm, LANES), lambda b, i: (b, i, 0)),
                   pl.BlockSpec((tm, LANES), lambda b, i: (b * nt + i, 0))],
        out_shape=[jax.ShapeDtypeStruct((bsz, MLA_HEADS, t, QK_PAD), bf16),
                   jax.ShapeDtypeStruct((bsz, MLA_HEADS, QK_PAD, t), bf16),
                   jax.ShapeDtypeStruct((bsz, MLA_HEADS, t, MLA_V_DIM), bf16),
                   jax.ShapeDtypeStruct((bsz, IDX_HEADS, t, LANES), bf16),
                   jax.ShapeDtypeStruct((bsz, t, LANES), bf16),
                   jax.ShapeDtypeStruct((bsz * t, LANES), jnp.float32)],
        compiler_params=pltpu.CompilerParams(
            dimension_semantics=("arbitrary", "arbitrary"), vmem_limit_bytes=56 * 1024 * 1024),
        name="dsa_prep",
    )(proj, proj, proj, proj, proj, proj, tab, kvg, lng, lnb,
      w_uk.astype(bf16), w_uv.astype(bf16))


def _indexer_kernel(iq_ref, ik_ref, iw_ref, bias_ref, key_scr, tie_scr, *, n_keep):
    f32, i32 = jnp.float32, jnp.int32
    tq = iq_ref.shape[2]
    kc = key_scr.shape[2]
    nch_total = key_scr.shape[0]
    qi = pl.program_id(1)
    nch = ((qi + 1) * tq + kc - 1) // kc
    q_chunk = (lax.broadcasted_iota(i32, (tq, kc), 0) + qi * tq) >> 6
    col0 = lax.broadcasted_iota(i32, (tq, kc), 1)
    n_sub = kc // LANES

    def score_chunk(c, carry):
        ik = ik_ref[0, pl.ds(pl.multiple_of(c * kc, kc), kc), :]
        acc = jnp.zeros((tq, kc), f32)
        for h in range(IDX_HEADS):
            s = lax.dot_general(iq_ref[0, h], ik, (((1,), (1,)), ((), ())), preferred_element_type=f32)
            acc = acc + iw_ref[:, 64 + h:65 + h] * jnp.maximum(s, 0.0)
        bits = pltpu.bitcast(acc, i32)
        key = bits ^ ((bits >> 31) & 0x7FFFFFFF)
        adm = ((col0 + c * kc) >> CHUNK_SHIFT) <= q_chunk
        key_scr[c] = jnp.where(adm, key, INT_MIN)
        return carry

    lax.fori_loop(0, nch, score_chunk, 0)

    def lane_fold(x):
        part = x[:, 0:LANES]
        for j in range(1, n_sub):
            part = part + x[:, LANES * j:LANES * (j + 1)]
        return part

    def rep(x):
        return jnp.concatenate([x] * n_sub, axis=1)

    def count(pred_fn):
        def body(c, acc):
            return acc + lane_fold(jnp.where(pred_fn(key_scr[c], c), 1.0, 0.0))
        acc = lax.fori_loop(0, nch, body, jnp.zeros((tq, LANES), f32))
        return jnp.broadcast_to(jnp.sum(acc, axis=1, keepdims=True), (tq, LANES))

    keep = float(n_keep)
    cnt = count(lambda k, c: k >= 0)
    thr = jnp.where(cnt >= keep, 0, INT_MIN).astype(i32)

    def bit_pass(i, thr):
        cand = thr | jnp.left_shift(jnp.int32(1), 30 - i)
        cand_w = rep(cand)
        cnt = count(lambda k, c: k >= cand_w)
        return jnp.where(cnt >= keep, cand, thr)

    thr = lax.fori_loop(0, 31, bit_pass, thr)
    thr_w = rep(thr)
    cnt_ge = count(lambda k, c: k >= thr_w)
    tie_scr[...] = jnp.full((tq, LANES), 2 ** 30, i32)

    @pl.when(jnp.max(cnt_ge) > keep)
    def _():
        cnt_gt = count(lambda k, c: k > thr_w)
        need = keep - cnt_gt

        def idx_pass(i, x):
            cand = x | jnp.left_shift(jnp.int32(1), idx_bits - 1 - i)
            cand_w = rep(cand)
            cnt = count(lambda k, c: (k == thr_w) & ((col0 + c * kc) < cand_w))
            return jnp.where(cnt < need, cand, x)

        idx_bits = (nch_total * kc - 1).bit_length()
        x = lax.fori_loop(0, idx_bits, idx_pass, jnp.zeros((tq, LANES), i32))
        tie_scr[...] = jnp.where(cnt_ge > keep, x, 2 ** 30)

    tie_w = rep(tie_scr[...])

    def write_chunk(c, carry):
        k = key_scr[c]
        sel = (k > thr_w) | ((k == thr_w) & ((col0 + c * kc) <= tie_w))
        sel = sel & (k != INT_MIN)
        bias_ref[0, c] = jnp.where(sel, 0.0, NEG_BIG)
        return carry

    lax.fori_loop(0, nch, write_chunk, 0)

    def fill_chunk(c, carry):
        bias_ref[0, c] = jnp.full((tq, kc), NEG_BIG, f32)
        return carry

    lax.fori_loop(nch, nch_total, fill_chunk, 0)


def _indexer(iq, ik, iw, bsz, t, n_keep):
    tq, kc = IDX_TQ, ATT_TK
    nq, nkc = t // tq, t // kc
    return pl.pallas_call(
        functools.partial(_indexer_kernel, n_keep=n_keep),
        grid=(bsz, nq),
        in_specs=[pl.BlockSpec((1, IDX_HEADS, tq, LANES), lambda b, i: (b, 0, i, 0)),
                  pl.BlockSpec((1, t, LANES), lambda b, i: (b, 0, 0)),
                  pl.BlockSpec((tq, LANES), lambda b, i: (b * nq + i, 0))],
        out_specs=pl.BlockSpec((1, nkc, tq, kc), lambda b, i: (b, 0, i, 0)),
        out_shape=jax.ShapeDtypeStruct((bsz, nkc, t, kc), jnp.float32),
        scratch_shapes=[pltpu.VMEM((nkc, tq, kc), jnp.int32), pltpu.VMEM((tq, LANES), jnp.int32)],
        compiler_params=pltpu.CompilerParams(
            dimension_semantics=("arbitrary", "arbitrary"), vmem_limit_bytes=56 * 1024 * 1024),
        name="dsa_indexer",
    )(iq, ik, iw)


def _attn_kernel(q_ref, k_ref, v_ref, b_ref, o_ref, m_scr, l_scr, acc_scr):
    f32 = jnp.float32
    tq, tk = b_ref.shape[2], b_ref.shape[3]
    qi, ki = pl.program_id(1), pl.program_id(2)
    last = ((qi + 1) * tq - 1) // tk

    @pl.when(ki == 0)
    def _():
        m_scr[...] = jnp.full(m_scr.shape, NEG_BIG, f32)
        l_scr[...] = jnp.zeros(l_scr.shape, f32)
        acc_scr[...] = jnp.zeros(acc_scr.shape, f32)

    @pl.when(ki <= last)
    def _():
        bias = b_ref[0, 0]

        def head_group(gi, carry):
            hs = [gi * ATT_HEAD_UNROLL + u for u in range(ATT_HEAD_UNROLL)]
            s = [jnp.dot(q_ref[0, h], k_ref[0, h], preferred_element_type=f32) + bias for h in hs]
            m_prev = [m_scr[h] for h in hs]
            m_new = [jnp.maximum(mp, jnp.max(x, axis=1, keepdims=True)) for mp, x in zip(m_prev, s)]
            p = [jnp.exp2(x - mn[:, 0:1]) for x, mn in zip(s, m_new)]
            pv = [jnp.dot(x.astype(jnp.bfloat16), v_ref[0, h], preferred_element_type=f32)
                  for x, h in zip(p, hs)]
            for u, h in enumerate(hs):
                alpha = jnp.exp2(m_prev[u] - m_new[u])
                l_scr[h] = alpha * l_scr[h] + jnp.sum(p[u], axis=1, keepdims=True)
                acc_scr[h] = alpha * acc_scr[h] + pv[u]
                m_scr[h] = m_new[u]
            return carry

        lax.fori_loop(0, MLA_HEADS // ATT_HEAD_UNROLL, head_group, 0)

    @pl.when(ki == last)
    def _():
        for h in range(MLA_HEADS):
            o_ref[0, :, LANES * h:LANES * (h + 1)] = (acc_scr[h] / l_scr[h]).astype(o_ref.dtype)


def _masked_attention(q, k, v, bias, bsz, t):
    tq, tk = ATT_TQ, ATT_TK
    nq, nk = t // tq, t // tk

    def kmap(b, i, j):
        return (b, 0, jnp.minimum(j, ((i + 1) * tq - 1) // tk), 0)

    return pl.pallas_call(
        _attn_kernel,
        grid=(bsz, nq, nk),
        in_specs=[pl.BlockSpec((1, MLA_HEADS, tq, QK_PAD), lambda b, i, j: (b, 0, i, 0)),
                  pl.BlockSpec((1, MLA_HEADS, QK_PAD, tk),
                               lambda b, i, j: (b, 0, 0, jnp.minimum(j, ((i + 1) * tq - 1) // tk))),
                  pl.BlockSpec((1, MLA_HEADS, tk, MLA_V_DIM), kmap),
                  pl.BlockSpec((1, 1, tq, tk),
                               lambda b, i, j: (b, jnp.minimum(j, ((i + 1) * tq - 1) // tk), i, 0))],
        out_specs=pl.BlockSpec((1, tq, MLA_O_WIDTH), lambda b, i, j: (b, i, 0)),
        out_shape=jax.ShapeDtypeStruct((bsz, t, MLA_O_WIDTH), jnp.bfloat16),
        scratch_shapes=[pltpu.VMEM((MLA_HEADS, tq, LANES), jnp.float32),
                        pltpu.VMEM((MLA_HEADS, tq, LANES), jnp.float32),
                        pltpu.VMEM((MLA_HEADS, tq, MLA_V_DIM), jnp.float32)],
        compiler_params=pltpu.CompilerParams(
            dimension_semantics=("arbitrary", "arbitrary", "arbitrary"),
            vmem_limit_bytes=56 * 1024 * 1024),
        name="dsa_attention",
    )(q, k, v, bias)


def dsa_branch_pallas(proj, kv_norm_g, w_uk, w_uv, idx_norm_g, idx_norm_b, bsz, t):
    n_keep = min(TOPK_MAX, t // 4)
    tab = _rope_tables_lane(t)
    q, k, v, iq, ik, iw = _dsa_prep(proj, tab, kv_norm_g, idx_norm_g, idx_norm_b, w_uk, w_uv, bsz, t)
    bias = _indexer(iq, ik, iw, bsz, t, n_keep)
    o = _masked_attention(q, k, v, bias, bsz, t)
    return o.reshape(bsz * t, MLA_O_WIDTH)


GDN_TB = 256
GDN_HG = 4
GDN_REP = GDN_V_HEADS // GDN_K_HEADS


def _gdn_kernel(q_ref, k_ref, v_ref, z_ref, ab_ref, cwq_ref, cwk_ref, cwv_ref, alog_ref, dtb_ref, ng_ref,
                o_ref, tq_scr, tk_scr, tv_scr, state_scr):
    f32, bf16 = jnp.float32, jnp.bfloat16
    tb = q_ref.shape[0]
    g = pl.program_id(1)
    step = pl.program_id(2)
    n_chunk = tb // CHUNK
    kh_n = GDN_HG // GDN_REP

    @pl.when(step == 0)
    def _():
        tq_scr[...] = jnp.zeros(tq_scr.shape, f32)
        tk_scr[...] = jnp.zeros(tk_scr.shape, f32)
        tv_scr[...] = jnp.zeros(tv_scr.shape, f32)
        state_scr[...] = jnp.zeros(state_scr.shape, f32)

    def conv_silu(x_ref, w_ref, tail_scr):
        x = x_ref[...]
        tail = tail_scr[...]
        w = w_ref[...]
        row8 = lax.broadcasted_iota(jnp.int32, tail.shape, 0)
        y = x * w[3:4, :]
        for s in (1, 2, 3):
            xs = pltpu.roll(x, s, 0)
            top = jnp.where(row8 < s, pltpu.roll(tail, s, 0), xs[0:8])
            xs = jnp.concatenate([top, xs[8:]], axis=0)
            y = y + xs * w[3 - s:4 - s, :]
        tail_scr[...] = x[tb - 8:tb]
        return y * (1.0 / (1.0 + jnp.exp(-y)))

    qc = conv_silu(q_ref, cwq_ref, tq_scr)
    kc = conv_silu(k_ref, cwk_ref, tk_scr)
    vc = conv_silu(v_ref, cwv_ref, tv_scr)

    def l2n(x):
        return x * lax.rsqrt(jnp.sum(x * x, axis=-1, keepdims=True) + NORM_EPS)

    qn = [l2n(qc[:, LANES * j:LANES * (j + 1)]) * (GDN_HEAD_DIM ** -0.5) for j in range(kh_n)]
    kn = [l2n(kc[:, LANES * j:LANES * (j + 1)]) for j in range(kh_n)]

    ab = pltpu.roll(ab_ref[...], (LANES - GDN_HG * g) % LANES, 1)
    xg = ab + dtb_ref[0]
    softplus = jnp.maximum(xg, 0.0) + jnp.log1p(jnp.exp(-jnp.abs(xg)))
    g_all = -jnp.exp(alog_ref[0]) * softplus
    beta_all = 1.0 / (1.0 + jnp.exp(-ab))
    rin = lax.broadcasted_iota(jnp.int32, (tb, LANES), 0) & (CHUNK - 1)
    cum = g_all
    for s in (1, 2, 4, 8, 16, 32):
        cum = cum + jnp.where(rin >= s, pltpu.roll(cum, s, 0), 0.0)

    ii = lax.broadcasted_iota(jnp.int32, (tb, tb), 0)
    jj = lax.broadcasted_iota(jnp.int32, (tb, tb), 1)
    same = (ii >> CHUNK_SHIFT) == (jj >> CHUNK_SHIFT)
    incl = same & (ii >= jj)
    strict = same & (ii > jj)
    eye = jnp.where(ii == jj, 1.0, 0.0)
    nt_dims = (((1,), (1,)), ((), ()))
    tn_dims = (((0,), (0,)), ((), ()))
    ng = ng_ref[...]
    heads = range(GDN_HG)

    cum_t = cum.T
    kb = [x.astype(bf16) for x in kn]
    qb = [x.astype(bf16) for x in qn]
    kk = [lax.dot_general(kb[j], kb[j], nt_dims, preferred_element_type=f32) for j in range(kh_n)]
    qk = [lax.dot_general(qb[j], kb[j], nt_dims, preferred_element_type=f32) for j in range(kh_n)]
    gcol = [cum[:, hh:hh + 1] for hh in heads]
    bcol = [beta_all[:, 32 + hh:33 + hh] for hh in heads]
    dec = [jnp.where(incl, jnp.exp(jnp.where(incl, gcol[hh] - cum_t[hh:hh + 1, :], 0.0)), 0.0) for hh in heads]
    pw = [jnp.where(strict, -(bcol[hh] * kk[hh // GDN_REP] * dec[hh]), 0.0) for hh in heads]
    tmat = [eye + pw[hh] for hh in heads]
    pw = [jnp.dot(pw[hh].astype(bf16), pw[hh].astype(bf16), preferred_element_type=f32) for hh in heads]
    for _ in range(4):
        both = [jnp.dot(jnp.concatenate([tmat[hh], pw[hh]], axis=0).astype(bf16), pw[hh].astype(bf16),
                        preferred_element_type=f32) for hh in heads]
        tmat = [tmat[hh] + both[hh][0:tb] for hh in heads]
        pw = [both[hh][tb:2 * tb] for hh in heads]
    tmat = [tmat[hh] + jnp.dot(tmat[hh].astype(bf16), pw[hh].astype(bf16), preferred_element_type=f32)
            for hh in heads]
    eg = [jnp.exp(gcol[hh]) for hh in heads]
    uw = [jnp.dot(tmat[hh].astype(bf16),
                  jnp.concatenate([bcol[hh] * vc[:, LANES * hh:LANES * (hh + 1)],
                                   (bcol[hh] * eg[hh]) * kn[hh // GDN_REP]], axis=1).astype(bf16),
                  preferred_element_type=f32) for hh in heads]
    qkd = [qk[hh // GDN_REP] * dec[hh] for hh in heads]
    qd = [(qn[hh // GDN_REP] * eg[hh]).astype(bf16) for hh in heads]
    states = [state_scr[hh] for hh in heads]

    for c in range(n_chunk):
        r0 = c * CHUNK
        sb = [states[hh].astype(bf16) for hh in heads]
        ws = [jnp.dot(jnp.concatenate([uw[hh][r0:r0 + CHUNK, LANES:2 * LANES].astype(bf16),
                                       qd[hh][r0:r0 + CHUNK]], axis=0), sb[hh],
                      preferred_element_type=f32) for hh in heads]
        vb = [(uw[hh][r0:r0 + CHUNK, 0:LANES] - ws[hh][0:CHUNK]).astype(bf16) for hh in heads]
        g_last = [cum[r0 + CHUNK - 1:r0 + CHUNK, hh:hh + 1] for hh in heads]
        kd = [(kn[hh // GDN_REP][r0:r0 + CHUNK] * jnp.exp(g_last[hh] - gcol[hh][r0:r0 + CHUNK])).astype(bf16)
              for hh in heads]
        o = [ws[hh][CHUNK:2 * CHUNK] + jnp.dot(qkd[hh][r0:r0 + CHUNK, r0:r0 + CHUNK].astype(bf16), vb[hh],
                                               preferred_element_type=f32) for hh in heads]
        states = [states[hh] * jnp.exp(g_last[hh])
                  + lax.dot_general(kd[hh], vb[hh], tn_dims, preferred_element_type=f32) for hh in heads]
        for hh in heads:
            on = o[hh] * lax.rsqrt(jnp.mean(o[hh] * o[hh], axis=-1, keepdims=True) + NORM_EPS) * ng
            zz = z_ref[r0:r0 + CHUNK, LANES * hh:LANES * (hh + 1)]
            o_ref[r0:r0 + CHUNK, LANES * hh:LANES * (hh + 1)] = (
                on * (zz * (1.0 / (1.0 + jnp.exp(-zz))))).astype(o_ref.dtype)

    for hh in range(GDN_HG):
        state_scr[hh] = states[hh]


def _gdn_branch_pallas(proj, conv_w, a_log, dt_bias, norm_g, bsz, t):
    tb = GDN_TB
    nt = t // tb
    n_groups = GDN_V_HEADS // GDN_HG
    qw = GDN_HG // GDN_REP * GDN_HEAD_DIM
    vw = GDN_HG * GDN_HEAD_DIM
    f32 = jnp.float32

    def per_group(x):
        x = x.astype(f32).reshape(n_groups, 1, GDN_HG)
        return jnp.pad(x, ((0, 0), (0, 0), (0, LANES - GDN_HG)))

    def rows(w, off):
        return pl.BlockSpec((tb, w), lambda b, g, s: (b * nt + s, off // w + g))

    def convw(w, off):
        return pl.BlockSpec((4, w), lambda b, g, s: (0, off // w + g))

    grp = pl.BlockSpec((1, 1, LANES), lambda b, g, s: (g, 0, 0))
    conv_w = conv_w.astype(f32)
    return pl.pallas_call(
        _gdn_kernel,
        grid=(bsz, n_groups, nt),
        in_specs=[rows(qw, P_QKV), rows(qw, P_QKV + GDN_QK_WIDTH), rows(vw, P_QKV + 2 * GDN_QK_WIDTH),
                  rows(vw, P_GZ),
                  pl.BlockSpec((tb, LANES), lambda b, g, s: (b * nt + s, P_GAW // LANES)),
                  convw(qw, 0), convw(qw, GDN_QK_WIDTH), convw(vw, 2 * GDN_QK_WIDTH),
                  grp, grp,
                  pl.BlockSpec((1, LANES), lambda b, g, s: (0, 0))],
        out_specs=pl.BlockSpec((tb, vw), lambda b, g, s: (b * nt + s, g)),
        out_shape=jax.ShapeDtypeStruct((bsz * t, GDN_V_WIDTH), jnp.bfloat16),
        scratch_shapes=[pltpu.VMEM((8, qw), f32), pltpu.VMEM((8, qw), f32), pltpu.VMEM((8, vw), f32),
                        pltpu.VMEM((GDN_HG, GDN_HEAD_DIM, GDN_HEAD_DIM), f32)],
        compiler_params=pltpu.CompilerParams(
            dimension_semantics=("arbitrary", "arbitrary", "arbitrary"),
            vmem_limit_bytes=56 * 1024 * 1024),
        name="gdn_branch",
    )(proj, proj, proj, proj, proj, conv_w, conv_w, conv_w,
      per_group(a_log), per_group(dt_bias), norm_g.astype(f32)[None, :])


def hybrid_mixer(h, w_in, conv_w, a_log, dt_bias, gdn_norm_g, kv_norm_g, w_uk, w_uv,
                 idx_norm_g, idx_norm_b, w_branch_gdn, w_branch_dsa, w_out):
    bsz, t, d = h.shape
    n = bsz * t
    proj = _matmul(h.reshape(n, d), _relayout_w_in(w_in), tn=PROJ_TN)

    def cols(off, width):
        return proj[:, off:off + width].reshape(bsz, t, width)

    gate_a, gate_b = cols(P_GATE_A, 2048), cols(P_GATE_B, 2048)
    o_a = _gdn_branch_pallas(proj, conv_w, a_log, dt_bias, gdn_norm_g, bsz, t)
    o_b = dsa_branch_pallas(proj, kv_norm_g, w_uk, w_uv, idx_norm_g, idx_norm_b, bsz, t)
    u_a = _matmul(o_a, w_branch_gdn).reshape(bsz, t, d)
    u_b = _matmul(o_b, w_branch_dsa).reshape(bsz, t, d)
    merged = jax.nn.sigmoid(gate_a) * u_a + jax.nn.sigmoid(gate_b) * u_b
    return _matmul(merged.reshape(n, d), w_out).reshape(bsz, t, d)


ROUTER_TM = 512
MOE_BLK = 256
MOE_TN = 1024
COMBINE_TM = 128
DISPATCH_WINDOW = 32
VMEM_LIMIT = 56 * 1024 * 1024


def _router_kernel(h_ref, rw_ref, rb_ref, meta_ref, cnt_ref, carry_scr):
    f32 = jnp.float32
    tm = h_ref.shape[0]

    @pl.when(pl.program_id(0) == 0)
    def _():
        carry_scr[...] = jnp.zeros(carry_scr.shape, f32)

    logits = jnp.dot(h_ref[...], rw_ref[...], preferred_element_type=f32) + rb_ref[...]
    lane = lax.broadcasted_iota(jnp.int32, (tm, LANES), 1).astype(f32)
    vals = logits
    hot_sum = jnp.zeros((tm, LANES), f32)
    tops, ids, hots = [], [], []
    for _ in range(TOP_K):
        m = jnp.max(vals, axis=1, keepdims=True)
        idx = jnp.min(jnp.where(vals == m, lane, float(LANES)), axis=1, keepdims=True)
        hot = lane == idx
        tops.append(m)
        ids.append(idx)
        hots.append(hot)
        vals = jnp.where(hot, -jnp.inf, vals)
        hot_sum = hot_sum + jnp.where(hot, 1.0, 0.0)
    exps = [jnp.exp(v - tops[0]) for v in tops]
    inv = 1.0 / (exps[0] + exps[1] + exps[2] + exps[3])
    ri = lax.broadcasted_iota(jnp.int32, (tm, tm), 0)
    ci = lax.broadcasted_iota(jnp.int32, (tm, tm), 1)
    lower = jnp.where(ri > ci, 1.0, 0.0).astype(jnp.bfloat16)
    base = carry_scr[...] + jnp.dot(lower, hot_sum.astype(jnp.bfloat16), preferred_element_type=f32)
    meta = jnp.zeros((tm, LANES), f32)
    for k in range(TOP_K):
        rank = jnp.sum(jnp.where(hots[k], base, 0.0), axis=1, keepdims=True)
        meta = jnp.where(lane == float(k), ids[k], meta)
        meta = jnp.where(lane == float(TOP_K + k), exps[k] * inv, meta)
        meta = jnp.where(lane == float(2 * TOP_K + k), rank, meta)
    meta_ref[...] = meta
    carry_scr[...] = carry_scr[...] + jnp.sum(hot_sum, axis=0, keepdims=True)
    cnt_ref[...] = carry_scr[...]


def _router(h2, router_w, router_b):
    n, d = h2.shape
    tm = ROUTER_TM
    rw = jnp.pad(router_w.astype(jnp.bfloat16), ((0, 0), (0, LANES - N_EXPERTS)))
    rb = jnp.concatenate([router_b.astype(jnp.float32),
                          jnp.full((LANES - N_EXPERTS,), -jnp.inf, jnp.float32)])[None, :]
    return pl.pallas_call(
        _router_kernel,
        grid=(n // tm,),
        in_specs=[pl.BlockSpec((tm, d), lambda i: (i, 0)),
                  pl.BlockSpec((d, LANES), lambda i: (0, 0)),
                  pl.BlockSpec((1, LANES), lambda i: (0, 0))],
        out_specs=[pl.BlockSpec((tm, LANES), lambda i: (i, 0)),
                   pl.BlockSpec((1, LANES), lambda i: (0, 0))],
        out_shape=[jax.ShapeDtypeStruct((n, LANES), jnp.float32),
                   jax.ShapeDtypeStruct((1, LANES), jnp.float32)],
        scratch_shapes=[pltpu.VMEM((1, LANES), jnp.float32)],
        compiler_params=pltpu.CompilerParams(dimension_semantics=("arbitrary",), vmem_limit_bytes=VMEM_LIMIT),
        name="moe_router",
    )(h2, rw, rb)


def _dispatch_kernel(slot_ref, h_ref, xs_in_ref, xs_ref, sem):
    del xs_in_ref
    n_tok = h_ref.shape[0]

    def copy(t, k):
        return pltpu.make_async_copy(h_ref.at[pl.ds(t, 1)], xs_ref.at[pl.ds(slot_ref[TOP_K * t + k], 1)], sem)

    def body(t, carry):
        for k in range(TOP_K):
            copy(t, k).start()

        @pl.when(t >= DISPATCH_WINDOW)
        def _():
            for k in range(TOP_K):
                copy(t - DISPATCH_WINDOW, k).wait()
        return carry

    lax.fori_loop(0, n_tok, body, 0)

    def drain(t, carry):
        for k in range(TOP_K):
            copy(t, k).wait()
        return carry

    lax.fori_loop(n_tok - DISPATCH_WINDOW, n_tok, drain, 0)


def _dispatch(slot, h2, n_rows):
    n, d = h2.shape
    xs0 = jnp.zeros((n_rows, d), h2.dtype)
    return pl.pallas_call(
        _dispatch_kernel,
        in_specs=[pl.BlockSpec(memory_space=pltpu.SMEM),
                  pl.BlockSpec(memory_space=pl.ANY),
                  pl.BlockSpec(memory_space=pl.ANY)],
        out_specs=pl.BlockSpec(memory_space=pl.ANY),
        out_shape=jax.ShapeDtypeStruct((n_rows, d), h2.dtype),
        scratch_shapes=[pltpu.SemaphoreType.DMA(())],
        input_output_aliases={2: 0},
        name="moe_dispatch",
    )(slot, h2, xs0)


def _moe_up_kernel(be_ref, nb_ref, x_ref, wg_ref, wu_ref, bg_ref, bu_ref, o_ref):
    del be_ref
    j = pl.program_id(1)

    @pl.when(j < nb_ref[0])
    def _():
        x = x_ref[...].astype(jnp.bfloat16)
        gate = jnp.dot(x, wg_ref[0], preferred_element_type=jnp.float32) + bg_ref[0]
        up = jnp.dot(x, wu_ref[0], preferred_element_type=jnp.float32) + bu_ref[0]
        gate = jnp.minimum(gate, SWIGLU_LIMIT)
        up = jnp.clip(up, -SWIGLU_LIMIT, SWIGLU_LIMIT)
        act = (up + 1.0) * (gate * (1.0 / (1.0 + jnp.exp(-SWIGLU_ALPHA * gate))))
        o_ref[...] = act.astype(o_ref.dtype)


def _moe_down_kernel(be_ref, nb_ref, a_ref, wd_ref, bd_ref, o_ref):
    del be_ref
    j = pl.program_id(1)

    @pl.when(j < nb_ref[0])
    def _():
        o_ref[...] = jnp.dot(a_ref[...], wd_ref[0], preferred_element_type=jnp.float32) + bd_ref[0]


def _moe_experts(xs, block_expert, n_used, w_gate, b_gate, w_up, b_up, w_down, b_down):
    n_rows, d = xs.shape
    f = w_gate.shape[2]
    blk, tn = MOE_BLK, MOE_TN
    n_blocks = n_rows // blk
    bf16 = jnp.bfloat16

    def wspec(k):
        return pl.BlockSpec((1, k, tn), lambda c, j, be, nb: (be[j], 0, c))

    bspec = pl.BlockSpec((1, 1, tn), lambda c, j, be, nb: (be[j], 0, c))
    rows_in = lambda k: pl.BlockSpec((blk, k), lambda c, j, be, nb: (j, 0))
    rows_out = pl.BlockSpec((blk, tn), lambda c, j, be, nb: (j, c))
    params = pltpu.CompilerParams(dimension_semantics=("arbitrary", "arbitrary"), vmem_limit_bytes=VMEM_LIMIT)
    act = pl.pallas_call(
        _moe_up_kernel,
        grid_spec=pltpu.PrefetchScalarGridSpec(
            num_scalar_prefetch=2, grid=(f // tn, n_blocks),
            in_specs=[rows_in(d), wspec(d), wspec(d), bspec, bspec],
            out_specs=rows_out),
        out_shape=jax.ShapeDtypeStruct((n_rows, f), bf16),
        compiler_params=params,
        name="moe_up",
    )(block_expert, n_used, xs, w_gate.astype(bf16), w_up.astype(bf16),
      b_gate.astype(jnp.float32)[:, None, :], b_up.astype(jnp.float32)[:, None, :])
    return pl.pallas_call(
        _moe_down_kernel,
        grid_spec=pltpu.PrefetchScalarGridSpec(
            num_scalar_prefetch=2, grid=(d // tn, n_blocks),
            in_specs=[rows_in(f), wspec(f), bspec],
            out_specs=rows_out),
        out_shape=jax.ShapeDtypeStruct((n_rows, d), jnp.float32),
        compiler_params=params,
        name="moe_down",
    )(block_expert, n_used, act, w_down.astype(bf16), b_down.astype(jnp.float32)[:, None, :])


def _combine_kernel(slot_ref, x1_ref, meta_ref, g_ref, ys_ref, o_ref, buf, sem):
    i = pl.program_id(0)
    n_steps = pl.num_programs(0)
    tm = x1_ref.shape[0]

    def copy(tile, r, k, half):
        return pltpu.make_async_copy(ys_ref.at[pl.ds(slot_ref[TOP_K * (tile * tm + r) + k], 1)],
                                     buf.at[half, k, pl.ds(r, 1)], sem.at[half])

    def start_tile(tile, half):
        def body(r, carry):
            for k in range(TOP_K):
                copy(tile, r, k, half).start()
            return carry
        lax.fori_loop(0, tm, body, 0)

    @pl.when(i == 0)
    def _():
        start_tile(0, 0)

    @pl.when(i + 1 < n_steps)
    def _():
        start_tile(i + 1, (i + 1) % 2)

    half = i % 2

    def wait_row(r, carry):
        for k in range(TOP_K):
            copy(i, r, k, half).wait()
        return carry

    lax.fori_loop(0, tm, wait_row, 0)
    acc = x1_ref[...]
    for k in range(TOP_K):
        acc = acc + meta_ref[:, TOP_K + k:TOP_K + k + 1] * buf[half, k]
    o_ref[...] = acc * lax.rsqrt(jnp.mean(acc * acc, axis=-1, keepdims=True) + NORM_EPS) * g_ref[...]


def _combine(slot, x1, meta, ys, norm_g):
    n, d = x1.shape
    tm = COMBINE_TM
    return pl.pallas_call(
        _combine_kernel,
        grid_spec=pltpu.PrefetchScalarGridSpec(
            num_scalar_prefetch=1, grid=(n // tm,),
            in_specs=[pl.BlockSpec((tm, d), lambda i, s: (i, 0)),
                      pl.BlockSpec((tm, LANES), lambda i, s: (i, 0)),
                      pl.BlockSpec((1, d), lambda i, s: (0, 0)),
                      pl.BlockSpec(memory_space=pl.ANY)],
            out_specs=pl.BlockSpec((tm, d), lambda i, s: (i, 0)),
            scratch_shapes=[pltpu.VMEM((2, TOP_K, tm, d), jnp.float32),
                            pltpu.SemaphoreType.DMA((2,))]),
        out_shape=jax.ShapeDtypeStruct((n, d), jnp.float32),
        compiler_params=pltpu.CompilerParams(dimension_semantics=("arbitrary",), vmem_limit_bytes=VMEM_LIMIT),
        name="moe_combine",
    )(slot, x1, meta, norm_g.astype(jnp.float32)[None, :], ys)


def moe_final_pallas(x1, norm_ffn_g, router_w, router_b, w_gate, b_gate, w_up, b_up, w_down, b_down,
                     norm_final_g):
    n, d = x1.shape
    h2 = rms_norm(x1, norm_ffn_g)
    meta, cnt = _router(h2.astype(jnp.bfloat16), router_w, router_b)
    counts = cnt[0, :N_EXPERTS].astype(jnp.int32)
    padded = (counts + MOE_BLK - 1) // MOE_BLK * MOE_BLK
    pad_end = jnp.cumsum(padded)
    pad_start = pad_end - padded
    expert = meta[:, 0:TOP_K].astype(jnp.int32)
    rank = meta[:, 2 * TOP_K:3 * TOP_K].astype(jnp.int32)
    slot = (pad_start[expert] + rank).reshape(-1)
    n_blocks = n * TOP_K // MOE_BLK + N_EXPERTS
    block_expert = jnp.minimum(jnp.searchsorted(pad_end, jnp.arange(n_blocks, dtype=jnp.int32) * MOE_BLK,
                                                side='right'), N_EXPERTS - 1).astype(jnp.int32)
    n_used = (pad_end[-1:] // MOE_BLK).astype(jnp.int32)
    xs = _dispatch(slot, h2, n_blocks * MOE_BLK)
    ys = _moe_experts(xs, block_expert, n_used, w_gate, b_gate, w_up, b_up, w_down, b_down)
    return _combine(slot, x1, meta, ys, norm_final_g)


def moe_ffn(h, router_w, router_b, w_gate, b_gate, w_up, b_up, w_down, b_down):
    bsz, t, d = h.shape
    tok = h.reshape(-1, d)
    n_tok = tok.shape[0]
    n_assign = n_tok * TOP_K
    logits = tok.astype(jnp.float32) @ router_w.astype(jnp.float32) + router_b.astype(jnp.float32)
    top_val, top_idx = lax.top_k(logits, TOP_K)
    gates = jax.nn.softmax(top_val, axis=-1)
    flat_e = top_idx.reshape(-1)
    flat_tok = jnp.repeat(jnp.arange(n_tok, dtype=jnp.int32), TOP_K)
    order = jnp.argsort(flat_e)
    sorted_e = flat_e[order]
    counts = jnp.bincount(flat_e, length=N_EXPERTS)
    padded = (counts + EXPERT_BLOCK - 1) // EXPERT_BLOCK * EXPERT_BLOCK
    pad_end = jnp.cumsum(padded)
    pad_start = pad_end - padded
    start = jnp.cumsum(counts) - counts
    slot_sorted = jnp.arange(n_assign, dtype=jnp.int32) - start[sorted_e] + pad_start[sorted_e]
    slot = jnp.zeros((n_assign,), jnp.int32).at[order].set(slot_sorted.astype(jnp.int32))
    n_blocks = -(-n_assign // EXPERT_BLOCK) + N_EXPERTS
    slot_tok = jnp.zeros((n_blocks * EXPERT_BLOCK,), jnp.int32).at[slot].set(flat_tok)
    block_expert = jnp.minimum(jnp.searchsorted(pad_end, jnp.arange(n_blocks) * EXPERT_BLOCK, side='right'),
                               N_EXPERTS - 1)
    xs = tok[slot_tok].reshape(n_blocks, EXPERT_BLOCK, d)

    def expert(inp):
        xb, e = inp
        gate = xb @ w_gate[e] + b_gate[e]
        up = xb @ w_up[e] + b_up[e]
        gate = jnp.minimum(gate, SWIGLU_LIMIT)
        up = jnp.clip(up, -SWIGLU_LIMIT, SWIGLU_LIMIT)
        act = (up + 1.0) * (gate * jax.nn.sigmoid(gate * SWIGLU_ALPHA))
        return act @ w_down[e] + b_down[e]

    ys = lax.map(expert, (xs, block_expert)).reshape(-1, d)
    y = ys[slot].reshape(n_tok, TOP_K, d)
    out = jnp.einsum('tk,tkd->td', gates.astype(y.dtype), y)
    return out.reshape(bsz, t, d)


def kernel(x, norm_mix_g, w_in, gdn_conv_w, gdn_a_log, gdn_dt_bias, gdn_norm_g, mla_kv_norm_g,
           mla_w_uk, mla_w_uv, idx_k_norm_g, idx_k_norm_b, w_branch_gdn, w_branch_dsa, w_out,
           norm_ffn_g, router_w, router_b, exp_w_gate, exp_b_gate, exp_w_up, exp_b_up,
           exp_w_down, exp_b_down, norm_final_g):
    l = 0
    x = x + hybrid_mixer(rms_norm(x, norm_mix_g[l]), w_in[l], gdn_conv_w[l], gdn_a_log[l], gdn_dt_bias[l],
                         gdn_norm_g[l], mla_kv_norm_g[l], mla_w_uk[l], mla_w_uv[l], idx_k_norm_g[l],
                         idx_k_norm_b[l], w_branch_gdn[l], w_branch_dsa[l], w_out[l])
    bsz, t, d = x.shape
    out = moe_final_pallas(x.reshape(bsz * t, d), norm_ffn_g[l], router_w[l], router_b[l], exp_w_gate[l],
                           exp_b_gate[l], exp_w_up[l], exp_b_up[l], exp_w_down[l], exp_b_down[l], norm_final_g)
    return out.reshape(bsz, t, d)
```

```python
import functools
import math
import numpy as np
import jax
import jax.numpy as jnp
from jax import lax
from jax.experimental import pallas as pl
from jax.experimental.pallas import tpu as pltpu

D_MODEL = 2048
CHUNK = 64
CHUNK_SHIFT = CHUNK.bit_length() - 1
ROPE_THETA = 10000.0
NORM_EPS = 1e-6
GDN_K_HEADS = 16
GDN_V_HEADS = 32
GDN_HEAD_DIM = 128
GDN_QK_WIDTH = GDN_K_HEADS * GDN_HEAD_DIM
GDN_V_WIDTH = GDN_V_HEADS * GDN_HEAD_DIM
MLA_HEADS = 16
MLA_NOPE_DIM = 128
MLA_ROPE_DIM = 64
MLA_V_DIM = 128
MLA_KV_RANK = 512
MLA_Q_WIDTH = MLA_HEADS * (MLA_NOPE_DIM + MLA_ROPE_DIM)
MLA_O_WIDTH = MLA_HEADS * MLA_V_DIM
IDX_HEADS = 16
IDX_DIM = 64
IDX_ROPE_DIM = 32
TOPK_MAX = 256
Q_BLOCK = 128
N_EXPERTS = 32
TOP_K = 4
D_EXPERT = 2048
SWIGLU_LIMIT = 7.0
SWIGLU_ALPHA = 1.702
EXPERT_BLOCK = 256
IN_SIZES = (GDN_QK_WIDTH, GDN_QK_WIDTH, GDN_V_WIDTH, GDN_V_WIDTH, GDN_V_HEADS, GDN_V_HEADS,
            MLA_Q_WIDTH, MLA_KV_RANK, MLA_ROPE_DIM, IDX_HEADS * IDX_DIM, IDX_DIM, IDX_HEADS,
            D_MODEL, D_MODEL)


def _mm_kernel(a_ref, b_ref, o_ref):
    o_ref[...] = jnp.dot(a_ref[...], b_ref[...], preferred_element_type=jnp.float32)


def _matmul(a, b, tm=512, tn=1024):
    m, k = a.shape
    n = b.shape[1]
    n_pad = -(-n // tn) * tn
    a = a.astype(jnp.bfloat16)
    b = b.astype(jnp.bfloat16)
    if n_pad != n:
        b = jnp.pad(b, ((0, 0), (0, n_pad - n)))
    out = pl.pallas_call(
        _mm_kernel,
        grid=(n_pad // tn, m // tm),
        in_specs=[pl.BlockSpec((tm, k), lambda j, i: (i, 0)),
                  pl.BlockSpec((k, tn), lambda j, i: (0, j))],
        out_specs=pl.BlockSpec((tm, tn), lambda j, i: (i, j)),
        out_shape=jax.ShapeDtypeStruct((m, n_pad), jnp.float32),
        compiler_params=pltpu.CompilerParams(
            dimension_semantics=("arbitrary", "arbitrary"),
            vmem_limit_bytes=56 * 1024 * 1024),
        name="dense_matmul",
    )(a, b)
    return out[:, :n] if n_pad != n else out


def _split(x, sizes):
    cuts = np.cumsum(sizes)[:-1].tolist()
    return jnp.split(x, cuts, axis=-1)


def rms_norm(x, g):
    xf = x.astype(jnp.float32)
    y = xf * lax.rsqrt(jnp.mean(xf * xf, axis=-1, keepdims=True) + NORM_EPS)
    return (y * g.astype(jnp.float32)).astype(x.dtype)


def layer_norm(x, g, b):
    xf = x.astype(jnp.float32)
    mu = jnp.mean(xf, axis=-1, keepdims=True)
    xc = xf - mu
    y = xc * lax.rsqrt(jnp.mean(xc * xc, axis=-1, keepdims=True) + NORM_EPS)
    return (y * g.astype(jnp.float32) + b.astype(jnp.float32)).astype(x.dtype)


def l2_norm(x):
    return x * lax.rsqrt(jnp.sum(x * x, axis=-1, keepdims=True) + NORM_EPS)


def rope_tables(n, dim):
    inv = ROPE_THETA ** (-jnp.arange(0, dim, 2, dtype=jnp.float32) / dim)
    ang = jnp.arange(n, dtype=jnp.float32)[:, None] * inv[None, :]
    return jnp.cos(ang), jnp.sin(ang)


def apply_rope(x, cos, sin):
    xf = x.astype(jnp.float32)
    half = x.shape[-1] // 2
    x1, x2 = xf[..., :half], xf[..., half:]
    c = cos[None, :, None, :]
    s = sin[None, :, None, :]
    return jnp.concatenate([x1 * c - x2 * s, x2 * c + x1 * s], axis=-1).astype(x.dtype)


def causal_depthwise_conv(x, w):
    width, ch = w.shape
    return lax.conv_general_dilated(x, w[:, None, :].astype(x.dtype), window_strides=(1,),
                                    padding=[(width - 1, 0)],
                                    dimension_numbers=('NWC', 'WIO', 'NWC'),
                                    feature_group_count=ch)


def gated_delta_rule(q, k, v, g, beta):
    bsz, t, h, dk = q.shape
    dv = v.shape[-1]
    n = t // CHUNK

    def to_chunks(a):
        return jnp.moveaxis(a.reshape((bsz, n, CHUNK, h) + a.shape[3:]), 3, 1)

    qc, kc, vc, gc, bc = (to_chunks(a) for a in (q, k, v, g, beta))
    G = jnp.cumsum(gc, axis=-1)
    pos = jnp.arange(CHUNK)
    incl = pos[:, None] >= pos[None, :]
    strict = pos[:, None] > pos[None, :]
    decay = jnp.exp(jnp.where(incl, G[..., :, None] - G[..., None, :], -jnp.inf))
    kk = jnp.einsum('bhncd,bhnsd->bhncs', kc, kc)
    a_mat = jnp.where(strict, bc[..., :, None] * kk * decay, 0.0)
    eye = jnp.eye(CHUNK, dtype=jnp.float32)
    t_mat = lax.linalg.triangular_solve(eye + a_mat, jnp.broadcast_to(eye, a_mat.shape),
                                        left_side=True, lower=True, unit_diagonal=True)
    u = jnp.einsum('bhncs,bhnsd->bhncd', t_mat, vc * bc[..., None])
    w = jnp.einsum('bhncs,bhnsd->bhncd', t_mat, kc * (bc * jnp.exp(G))[..., None])
    qk = jnp.einsum('bhncd,bhnsd->bhncs', qc, kc) * decay
    q_dec = qc * jnp.exp(G)[..., None]
    k_dec = kc * jnp.exp(G[..., -1:] - G)[..., None]
    chunk_decay = jnp.exp(G[..., -1])

    def step(state, inp):
        u_n, w_n, qk_n, qd_n, kd_n, cd_n = inp
        v_new = u_n - jnp.einsum('bhcd,bhde->bhce', w_n, state)
        o_n = jnp.einsum('bhcd,bhde->bhce', qd_n, state) + jnp.einsum('bhcs,bhse->bhce', qk_n, v_new)
        state = state * cd_n[..., None, None] + jnp.einsum('bhcd,bhce->bhde', kd_n, v_new)
        return state, o_n

    xs = tuple(jnp.moveaxis(a, 2, 0) for a in (u, w, qk, q_dec, k_dec, chunk_decay))
    s0 = jnp.zeros((bsz, h, dk, dv), jnp.float32)
    _, o = lax.scan(step, s0, xs)
    o = jnp.moveaxis(o, 0, 2)
    return jnp.moveaxis(o, 1, 3).reshape(bsz, t, h, dv)


def gdn_branch(q, k, v, z, a, b, conv_w, a_log, dt_bias, norm_g):
    dtype = q.dtype
    bsz, t = q.shape[:2]
    qkv = jax.nn.silu(causal_depthwise_conv(jnp.concatenate([q, k, v], axis=-1), conv_w))
    q, k, v = _split(qkv, (GDN_QK_WIDTH, GDN_QK_WIDTH, GDN_V_WIDTH))
    q = l2_norm(q.reshape(bsz, t, GDN_K_HEADS, GDN_HEAD_DIM).astype(jnp.float32)) * (GDN_HEAD_DIM ** -0.5)
    k = l2_norm(k.reshape(bsz, t, GDN_K_HEADS, GDN_HEAD_DIM).astype(jnp.float32))
    rep = GDN_V_HEADS // GDN_K_HEADS
    q = jnp.repeat(q, rep, axis=2)
    k = jnp.repeat(k, rep, axis=2)
    v = v.reshape(bsz, t, GDN_V_HEADS, GDN_HEAD_DIM).astype(jnp.float32)
    beta = jax.nn.sigmoid(b.astype(jnp.float32))
    g = -jnp.exp(a_log.astype(jnp.float32)) * jax.nn.softplus(a.astype(jnp.float32) + dt_bias.astype(jnp.float32))
    o = gated_delta_rule(q, k, v, g, beta)
    zf = z.reshape(bsz, t, GDN_V_HEADS, GDN_HEAD_DIM).astype(jnp.float32)
    o = rms_norm(o, norm_g) * jax.nn.silu(zf)
    return o.reshape(bsz, t, GDN_V_WIDTH).astype(dtype)


def dsa_branch(q, ckv, kpe, iq, ik, iw, kv_norm_g, w_uk, w_uv, idx_norm_g, idx_norm_b):
    bsz, t = q.shape[:2]
    n_keep = min(TOPK_MAX, t // 4)
    cos_m, sin_m = rope_tables(t, MLA_ROPE_DIM)
    cos_i, sin_i = rope_tables(t, IDX_ROPE_DIM)
    q = q.reshape(bsz, t, MLA_HEADS, MLA_NOPE_DIM + MLA_ROPE_DIM)
    q_nope = q[..., :MLA_NOPE_DIM]
    q_pe = apply_rope(q[..., MLA_NOPE_DIM:], cos_m, sin_m)
    ckv = rms_norm(ckv, kv_norm_g)
    kpe = apply_rope(kpe[:, :, None, :], cos_m, sin_m)[:, :, 0]
    iq = iq.reshape(bsz, t, IDX_HEADS, IDX_DIM)
    iq = jnp.concatenate([apply_rope(iq[..., :IDX_ROPE_DIM], cos_i, sin_i), iq[..., IDX_ROPE_DIM:]], axis=-1)
    ik = layer_norm(ik, idx_norm_g, idx_norm_b)
    ik = jnp.concatenate([apply_rope(ik[:, :, None, :IDX_ROPE_DIM], cos_i, sin_i)[:, :, 0],
                          ik[..., IDX_ROPE_DIM:]], axis=-1)
    ik_f = ik.astype(jnp.float32)
    iw = iw.astype(jnp.float32) * (IDX_HEADS ** -0.5 * IDX_DIM ** -0.5)
    w_uk3 = w_uk.reshape(MLA_KV_RANK, MLA_HEADS, MLA_NOPE_DIM)
    w_uv3 = w_uv.reshape(MLA_KV_RANK, MLA_HEADS, MLA_V_DIM)
    scale = (MLA_NOPE_DIM + MLA_ROPE_DIM) ** -0.5
    n_blk = t // Q_BLOCK
    key_chunk = jnp.arange(t) // CHUNK
    gather = jax.vmap(lambda src, ids: src[ids])

    def blocks(a):
        return jnp.moveaxis(a.reshape((bsz, n_blk, Q_BLOCK) + a.shape[2:]), 1, 0)

    def attend(inp):
        blk, qn, qp, iqb, iwb = inp
        q_chunk = (blk * Q_BLOCK + jnp.arange(Q_BLOCK)) // CHUNK
        admissible = key_chunk[None, :] <= q_chunk[:, None]
        s_idx = jnp.einsum('bqhd,bsd->bqhs', iqb.astype(jnp.float32), ik_f)
        score = jnp.einsum('bqh,bqhs->bqs', iwb, jax.nn.relu(s_idx))
        score = jnp.where(admissible[None], score, -jnp.inf)
        _, sel = lax.top_k(score, n_keep)
        valid = key_chunk[sel] <= q_chunk[None, :, None]
        ckv_sel = gather(ckv, sel)
        kpe_sel = gather(kpe, sel)
        q_lat = jnp.einsum('bqhd,rhd->bqhr', qn, w_uk3)
        logits = (jnp.einsum('bqhr,bqkr->bqhk', q_lat, ckv_sel)
                  + jnp.einsum('bqhe,bqke->bqhk', qp, kpe_sel)).astype(jnp.float32) * scale
        logits = jnp.where(valid[:, :, None, :], logits, -jnp.inf)
        p = jax.nn.softmax(logits, axis=-1).astype(ckv.dtype)
        o_lat = jnp.einsum('bqhk,bqkr->bqhr', p, ckv_sel)
        return jnp.einsum('bqhr,rhd->bqhd', o_lat, w_uv3)

    o = lax.map(attend, (jnp.arange(n_blk), blocks(q_nope), blocks(q_pe), blocks(iq), blocks(iw)))
    return jnp.moveaxis(o, 0, 1).reshape(bsz, t, MLA_O_WIDTH)


LANES = 128
PROJ_TN = 1024
_IN_OFF = np.concatenate([[0], np.cumsum(IN_SIZES)]).tolist()
(_O_GQ, _O_GK, _O_GV, _O_GZ, _O_GA, _O_GB, _O_MQ, _O_CKV, _O_KPE, _O_IQ, _O_IK, _O_IW,
 _O_GATE_A, _O_GATE_B) = _IN_OFF[:-1]
P_QKV = 0
P_GZ = 8192
P_QN = 12288
P_GATE_A = 14336
P_GATE_B = 16384
P_QPE = 18432
P_IQ = 19456
P_CKV = 20480
P_KPE_IK = 20992
P_GAW = 21120
P_WIDTH = 21504


def _proj_column_order():
    cols = []
    cols += list(range(_O_GQ, _O_GZ))
    cols += list(range(_O_GZ, _O_GA))
    hd = MLA_NOPE_DIM + MLA_ROPE_DIM
    cols += [_O_MQ + h * hd + i for h in range(MLA_HEADS) for i in range(MLA_NOPE_DIM)]
    cols += list(range(_O_GATE_A, _O_GATE_A + 2 * D_MODEL))
    cols += [_O_MQ + h * hd + MLA_NOPE_DIM + i for h in range(MLA_HEADS) for i in range(MLA_ROPE_DIM)]
    cols += list(range(_O_IQ, _O_IK))
    cols += list(range(_O_CKV, _O_KPE))
    cols += list(range(_O_KPE, _O_IQ)) + list(range(_O_IK, _O_IW))
    cols += list(range(_O_GA, _O_MQ)) + list(range(_O_IW, _O_GATE_A))
    cols += [-1] * (P_WIDTH - len(cols))
    assert len(cols) == P_WIDTH
    return np.asarray(cols, np.int32)


_PROJ_COLS = _proj_column_order()


def _relayout_w_in(w_in):
    d = w_in.shape[0]
    w = w_in.astype(jnp.bfloat16)
    mq = w[:, _O_MQ:_O_CKV].reshape(d, MLA_HEADS, MLA_NOPE_DIM + MLA_ROPE_DIM)
    parts = [w[:, _O_GQ:_O_GA],
             mq[:, :, :MLA_NOPE_DIM].reshape(d, -1),
             w[:, _O_GATE_A:_O_GATE_A + 2 * D_MODEL],
             mq[:, :, MLA_NOPE_DIM:].reshape(d, -1),
             w[:, _O_IQ:_O_IK], w[:, _O_CKV:_O_KPE], w[:, _O_KPE:_O_IQ], w[:, _O_IK:_O_IW],
             w[:, _O_GA:_O_MQ], w[:, _O_IW:_O_GATE_A]]
    used = sum(p.shape[1] for p in parts)
    return jnp.concatenate(parts + [jnp.zeros((d, P_WIDTH - used), jnp.bfloat16)], axis=1)


NEG_BIG = -1e30
INT_MIN = -2 ** 31
PREP_TM = 256
IDX_TQ = 128
ATT_TQ = 512
ATT_TK = 512
QK_PAD = 256
ATT_HEAD_UNROLL = 2


def _rope_tables_lane(t):
    cos_m, sin_m = rope_tables(t, MLA_ROPE_DIM)
    cos_i, sin_i = rope_tables(t, IDX_ROPE_DIM)
    one = jnp.ones((t, IDX_DIM - IDX_ROPE_DIM), jnp.float32)
    zero = jnp.zeros_like(one)
    cq = jnp.concatenate([cos_m, cos_m] * 2, axis=1)
    sq = jnp.concatenate([-sin_m, sin_m] * 2, axis=1)
    ci = jnp.concatenate([cos_i, cos_i, one] * 2, axis=1)
    si = jnp.concatenate([-sin_i, sin_i, zero] * 2, axis=1)
    cs = jnp.concatenate([cos_m, cos_m, cos_i, cos_i, one], axis=1)
    ss = jnp.concatenate([-sin_m, sin_m, -sin_i, sin_i, zero], axis=1)
    return jnp.concatenate([cq, sq, ci, si, cs, ss], axis=1)


def _dsa_prep_kernel(qn_ref, qpe_ref, iq_ref, ckv_ref, sm_ref, gw_ref, tab_ref,
                     kvg_ref, lng_ref, lnb_ref, wuk_ref, wuv_ref,
                     q_out, k_out, v_out, iq_out, ik_out, iw_out):
    f32, bf16 = jnp.float32, jnp.bfloat16
    tm = qn_ref.shape[0]
    lane = lax.broadcasted_iota(jnp.int32, (tm, LANES), 1)
    lo64 = lane < 64

    def swap32(x):
        return jnp.where((lane & 32) == 0, pltpu.roll(x, 96, 1), pltpu.roll(x, 32, 1))

    def swap16(x):
        return jnp.where((lane & 16) == 0, pltpu.roll(x, 112, 1), pltpu.roll(x, 16, 1))

    cq, sq = tab_ref[:, 0:128], tab_ref[:, 128:256]
    ci, si = tab_ref[:, 256:384], tab_ref[:, 384:512]
    cs, ss = tab_ref[:, 512:640], tab_ref[:, 640:768]
    scale = (MLA_NOPE_DIM + MLA_ROPE_DIM) ** -0.5 * math.log2(math.e)

    for j in range(MLA_HEADS // 2):
        t = qpe_ref[:, LANES * j:LANES * (j + 1)]
        r = (t * cq + swap32(t) * sq) * scale
        q_out[0, 2 * j, :, 128:256] = jnp.where(lo64, r, 0.0).astype(bf16)
        q_out[0, 2 * j + 1, :, 128:256] = jnp.where(lo64, pltpu.roll(r, 64, 1), 0.0).astype(bf16)
        t = iq_ref[:, LANES * j:LANES * (j + 1)]
        r = t * ci + swap16(t) * si
        iq_out[0, 2 * j] = jnp.where(lo64, r, 0.0).astype(bf16)
        iq_out[0, 2 * j + 1] = jnp.where(lo64, pltpu.roll(r, 64, 1), 0.0).astype(bf16)
    for h in range(MLA_HEADS):
        q_out[0, h, :, 0:128] = (qn_ref[:, LANES * h:LANES * (h + 1)] * scale).astype(bf16)

    c = ckv_ref[...]
    cn = c * lax.rsqrt(jnp.mean(c * c, axis=-1, keepdims=True) + NORM_EPS) * kvg_ref[...]
    cnb = cn.astype(bf16)
    kn = jnp.dot(cnb, wuk_ref[...], preferred_element_type=f32)
    vv = jnp.dot(cnb, wuv_ref[...], preferred_element_type=f32)

    s = sm_ref[...]
    hi64 = jnp.logical_not(lo64)
    mu = jnp.sum(jnp.where(hi64, s, 0.0), axis=-1, keepdims=True) * (1.0 / IDX_DIM)
    xc = jnp.where(hi64, s - mu, 0.0)
    var = jnp.sum(xc * xc, axis=-1, keepdims=True) * (1.0 / IDX_DIM)
    ln = xc * lax.rsqrt(var + NORM_EPS) * lng_ref[...] + lnb_ref[...]
    y = jnp.where(hi64, ln, s)
    r = y * cs + jnp.where(lo64, swap32(y), swap16(y)) * ss
    kpe_t = jnp.where(lo64, r, 0.0).T.astype(bf16)
    for h in range(MLA_HEADS):
        k_out[0, h, 0:128, :] = kn[:, LANES * h:LANES * (h + 1)].T.astype(bf16)
        k_out[0, h, 128:256, :] = kpe_t
        v_out[0, h] = vv[:, LANES * h:LANES * (h + 1)].astype(bf16)
    ik_out[0] = jnp.where(lo64, pltpu.roll(r, 64, 1), 0.0).astype(bf16)
    iw_out[...] = gw_ref[...] * (IDX_HEADS ** -0.5 * IDX_DIM ** -0.5)


def _dsa_prep(proj, tab, kv_norm_g, idx_norm_g, idx_norm_b, w_uk, w_uv, bsz, t):
    tm = PREP_TM
    nt = t // tm
    zeros64 = jnp.zeros((64,), jnp.float32)
    lng = jnp.concatenate([zeros64, idx_norm_g.astype(jnp.float32)])[None, :]
    lnb = jnp.concatenate([zeros64, idx_norm_b.astype(jnp.float32)])[None, :]
    kvg = kv_norm_g.astype(jnp.float32)[None, :]

    def rows(w, off):
        return pl.BlockSpec((tm, w), lambda b, i: (b * nt + i, off // w))

    def const(shape):
        return pl.BlockSpec(shape, lambda b, i: (0,) * len(shape))

    def heads(w):
        return pl.BlockSpec((1, MLA_HEADS, tm, w), lambda b, i: (b, 0, i, 0))

    bf16 = jnp.bfloat16
    return pl.pallas_call(
        _dsa_prep_kernel,
        grid=(bsz, nt),
        in_specs=[rows(2048, P_QN), rows(1024, P_QPE), rows(1024, P_IQ), rows(512, P_CKV),
                  rows(LANES, P_KPE_IK), rows(LANES, P_GAW),
                  pl.BlockSpec((tm, 768), lambda b, i: (i, 0)),
                  const((1, MLA_KV_RANK)), const((1, LANES)), const((1, LANES)),
                  const((MLA_KV_RANK, MLA_HEADS * MLA_NOPE_DIM)), const((MLA_KV_RANK, MLA_O_WIDTH))],
        out_specs=[heads(QK_PAD), pl.BlockSpec((1, MLA_HEADS, QK_PAD, tm), lambda b, i: (b, 0, 0, i)),
                   heads(MLA_V_DIM), heads(LANES),
                   pl.BlockSpec((1, tm, LANES), lambda b, i: (b, i, 0)),
                   pl.BlockSpec((tm, LANES), lambda b, i: (b * nt + i, 0))],
        out_shape=[jax.ShapeDtypeStruct((bsz, MLA_HEADS, t, QK_PAD), bf16),
                   jax.ShapeDtypeStruct((bsz, MLA_HEADS, QK_PAD, t), bf16),
                   jax.ShapeDtypeStruct((bsz, MLA_HEADS, t, MLA_V_DIM), bf16),
                   jax.ShapeDtypeStruct((bsz, IDX_HEADS, t, LANES), bf16),
                   jax.ShapeDtypeStruct((bsz, t, LANES), bf16),
                   jax.ShapeDtypeStruct((bsz * t, LANES), jnp.float32)],
        compiler_params=pltpu.CompilerParams(
            dimension_semantics=("arbitrary", "arbitrary"), vmem_limit_bytes=56 * 1024 * 1024),
        name="dsa_prep",
    )(proj, proj, proj, proj, proj, proj, tab, kvg, lng, lnb,
      w_uk.astype(bf16), w_uv.astype(bf16))


def _indexer_kernel(iq_ref, ik_ref, iw_ref, bias_ref, key_scr, tie_scr, *, n_keep):
    f32, i32 = jnp.float32, jnp.int32
    tq = iq_ref.shape[2]
    kc = key_scr.shape[2]
    nch_total = key_scr.shape[0]
    qi = pl.program_id(1)
    nch = ((qi + 1) * tq + kc - 1) // kc
    q_chunk = (lax.broadcasted_iota(i32, (tq, kc), 0) + qi * tq) >> 6
    col0 = lax.broadcasted_iota(i32, (tq, kc), 1)
    n_sub = kc // LANES

    def score_chunk(c, carry):
        ik = ik_ref[0, pl.ds(pl.multiple_of(c * kc, kc), kc), :]
        acc = jnp.zeros((tq, kc), f32)
        for h in range(IDX_HEADS):
            s = lax.dot_general(iq_ref[0, h], ik, (((1,), (1,)), ((), ())), preferred_element_type=f32)
            acc = acc + iw_ref[:, 64 + h:65 + h] * jnp.maximum(s, 0.0)
        bits = pltpu.bitcast(acc, i32)
        key = bits ^ ((bits >> 31) & 0x7FFFFFFF)
        adm = ((col0 + c * kc) >> CHUNK_SHIFT) <= q_chunk
        key_scr[c] = jnp.where(adm, key, INT_MIN)
        return carry

    lax.fori_loop(0, nch, score_chunk, 0)

    def lane_fold(x):
        part = x[:, 0:LANES]
        for j in range(1, n_sub):
            part = part + x[:, LANES * j:LANES * (j + 1)]
        return part

    def rep(x):
        return jnp.concatenate([x] * n_sub, axis=1)

    def count(pred_fn):
        def body(c, acc):
            return acc + lane_fold(jnp.where(pred_fn(key_scr[c], c), 1.0, 0.0))
        acc = lax.fori_loop(0, nch, body, jnp.zeros((tq, LANES), f32))
        return jnp.broadcast_to(jnp.sum(acc, axis=1, keepdims=True), (tq, LANES))

    keep = float(n_keep)
    cnt = count(lambda k, c: k >= 0)
    thr = jnp.where(cnt >= keep, 0, INT_MIN).astype(i32)

    def bit_pass(i, thr):
        cand = thr | jnp.left_shift(jnp.int32(1), 30 - i)
        cand_w = rep(cand)
        cnt = count(lambda k, c: k >= cand_w)
        return jnp.where(cnt >= keep, cand, thr)

    thr = lax.fori_loop(0, 31, bit_pass, thr)
    thr_w = rep(thr)
    cnt_ge = count(lambda k, c: k >= thr_w)
    tie_scr[...] = jnp.full((tq, LANES), 2 ** 30, i32)

    @pl.when(jnp.max(cnt_ge) > keep)
    def _():
        cnt_gt = count(lambda k, c: k > thr_w)
        need = keep - cnt_gt

        def idx_pass(i, x):
            cand = x | jnp.left_shift(jnp.int32(1), idx_bits - 1 - i)
            cand_w = rep(cand)
            cnt = count(lambda k, c: (k == thr_w) & ((col0 + c * kc) < cand_w))
            return jnp.where(cnt < need, cand, x)

        idx_bits = (nch_total * kc - 1).bit_length()
        x = lax.fori_loop(0, idx_bits, idx_pass, jnp.zeros((tq, LANES), i32))
        tie_scr[...] = jnp.where(cnt_ge > keep, x, 2 ** 30)

    tie_w = rep(tie_scr[...])

    def write_chunk(c, carry):
        k = key_scr[c]
        sel = (k > thr_w) | ((k == thr_w) & ((col0 + c * kc) <= tie_w))
        sel = sel & (k != INT_MIN)
        bias_ref[0, c] = jnp.where(sel, 0.0, NEG_BIG)
        return carry

    lax.fori_loop(0, nch, write_chunk, 0)

    def fill_chunk(c, carry):
        bias_ref[0, c] = jnp.full((tq, kc), NEG_BIG, f32)
        return carry

    lax.fori_loop(nch, nch_total, fill_chunk, 0)


def _indexer(iq, ik, iw, bsz, t, n_keep):
    tq, kc = IDX_TQ, ATT_TK
    nq, nkc = t // tq, t // kc
    return pl.pallas_call(
        functools.partial(_indexer_kernel, n_keep=n_keep),
        grid=(bsz, nq),
        in_specs=[pl.BlockSpec((1, IDX_HEADS, tq, LANES), lambda b, i: (b, 0, i, 0)),
                  pl.BlockSpec((1, t, LANES), lambda b, i: (b, 0, 0)),
                  pl.BlockSpec((tq, LANES), lambda b, i: (b * nq + i, 0))],
        out_specs=pl.BlockSpec((1, nkc, tq, kc), lambda b, i: (b, 0, i, 0)),
        out_shape=jax.ShapeDtypeStruct((bsz, nkc, t, kc), jnp.float32),
        scratch_shapes=[pltpu.VMEM((nkc, tq, kc), jnp.int32), pltpu.VMEM((tq, LANES), jnp.int32)],
        compiler_params=pltpu.CompilerParams(
            dimension_semantics=("arbitrary", "arbitrary"), vmem_limit_bytes=56 * 1024 * 1024),
        name="dsa_indexer",
    )(iq, ik, iw)


def _attn_kernel(q_ref, k_ref, v_ref, b_ref, o_ref, m_scr, l_scr, acc_scr):
    f32 = jnp.float32
    tq, tk = b_ref.shape[2], b_ref.shape[3]
    qi, ki = pl.program_id(1), pl.program_id(2)
    last = ((qi + 1) * tq - 1) // tk

    @pl.when(ki == 0)
    def _():
        m_scr[...] = jnp.full(m_scr.shape, NEG_BIG, f32)
        l_scr[...] = jnp.zeros(l_scr.shape, f32)
        acc_scr[...] = jnp.zeros(acc_scr.shape, f32)

    @pl.when(ki <= last)
    def _():
        bias = b_ref[0, 0]

        def head_group(gi, carry):
            hs = [gi * ATT_HEAD_UNROLL + u for u in range(ATT_HEAD_UNROLL)]
            s = [jnp.dot(q_ref[0, h], k_ref[0, h], preferred_element_type=f32) + bias for h in hs]
            m_prev = [m_scr[h] for h in hs]
            m_new = [jnp.maximum(mp, jnp.max(x, axis=1, keepdims=True)) for mp, x in zip(m_prev, s)]
            p = [jnp.exp2(x - mn[:, 0:1]) for x, mn in zip(s, m_new)]
            pv = [jnp.dot(x.astype(jnp.bfloat16), v_ref[0, h], preferred_element_type=f32)
                  for x, h in zip(p, hs)]
            for u, h in enumerate(hs):
                alpha = jnp.exp2(m_prev[u] - m_new[u])
                l_scr[h] = alpha * l_scr[h] + jnp.sum(p[u], axis=1, keepdims=True)
                acc_scr[h] = alpha * acc_scr[h] + pv[u]
                m_scr[h] = m_new[u]
            return carry

        lax.fori_loop(0, MLA_HEADS // ATT_HEAD_UNROLL, head_group, 0)

    @pl.when(ki == last)
    def _():
        for h in range(MLA_HEADS):
            o_ref[0, :, LANES * h:LANES * (h + 1)] = (acc_scr[h] / l_scr[h]).astype(o_ref.dtype)


def _masked_attention(q, k, v, bias, bsz, t):
    tq, tk = ATT_TQ, ATT_TK
    nq, nk = t // tq, t // tk

    def kmap(b, i, j):
        return (b, 0, jnp.minimum(j, ((i + 1) * tq - 1) // tk), 0)

    return pl.pallas_call(
        _attn_kernel,
        grid=(bsz, nq, nk),
        in_specs=[pl.BlockSpec((1, MLA_HEADS, tq, QK_PAD), lambda b, i, j: (b, 0, i, 0)),
                  pl.BlockSpec((1, MLA_HEADS, QK_PAD, tk),
                               lambda b, i, j: (b, 0, 0, jnp.minimum(j, ((i + 1) * tq - 1) // tk))),
                  pl.BlockSpec((1, MLA_HEADS, tk, MLA_V_DIM), kmap),
                  pl.BlockSpec((1, 1, tq, tk),
                               lambda b, i, j: (b, jnp.minimum(j, ((i + 1) * tq - 1) // tk), i, 0))],
        out_specs=pl.BlockSpec((1, tq, MLA_O_WIDTH), lambda b, i, j: (b, i, 0)),
        out_shape=jax.ShapeDtypeStruct((bsz, t, MLA_O_WIDTH), jnp.bfloat16),
        scratch_shapes=[pltpu.VMEM((MLA_HEADS, tq, LANES), jnp.float32),
                        pltpu.VMEM((MLA_HEADS, tq, LANES), jnp.float32),
                        pltpu.VMEM((MLA_HEADS, tq, MLA_V_DIM), jnp.float32)],
        compiler_params=pltpu.CompilerParams(
            dimension_semantics=("arbitrary", "arbitrary", "arbitrary"),
            vmem_limit_bytes=56 * 1024 * 1024),
        name="dsa_attention",
    )(q, k, v, bias)


def dsa_branch_pallas(proj, kv_norm_g, w_uk, w_uv, idx_norm_g, idx_norm_b, bsz, t):
    n_keep = min(TOPK_MAX, t // 4)
    tab = _rope_tables_lane(t)
    q, k, v, iq, ik, iw = _dsa_prep(proj, tab, kv_norm_g, idx_norm_g, idx_norm_b, w_uk, w_uv, bsz, t)
    bias = _indexer(iq, ik, iw, bsz, t, n_keep)
    o = _masked_attention(q, k, v, bias, bsz, t)
    return o.reshape(bsz * t, MLA_O_WIDTH)


GDN_TB = 256
GDN_HG = 4
GDN_REP = GDN_V_HEADS // GDN_K_HEADS


def _gdn_kernel(q_ref, k_ref, v_ref, z_ref, ab_ref, cwq_ref, cwk_ref, cwv_ref, alog_ref, dtb_ref, ng_ref,
                o_ref, tq_scr, tk_scr, tv_scr, state_scr):
    f32, bf16 = jnp.float32, jnp.bfloat16
    tb = q_ref.shape[0]
    g = pl.program_id(1)
    step = pl.program_id(2)
    n_chunk = tb // CHUNK
    kh_n = GDN_HG // GDN_REP

    @pl.when(step == 0)
    def _():
        tq_scr[...] = jnp.zeros(tq_scr.shape, f32)
        tk_scr[...] = jnp.zeros(tk_scr.shape, f32)
        tv_scr[...] = jnp.zeros(tv_scr.shape, f32)
        state_scr[...] = jnp.zeros(state_scr.shape, f32)

    def conv_silu(x_ref, w_ref, tail_scr):
        x = x_ref[...]
        tail = tail_scr[...]
        w = w_ref[...]
        row8 = lax.broadcasted_iota(jnp.int32, tail.shape, 0)
        y = x * w[3:4, :]
        for s in (1, 2, 3):
            xs = pltpu.roll(x, s, 0)
            top = jnp.where(row8 < s, pltpu.roll(tail, s, 0), xs[0:8])
            xs = jnp.concatenate([top, xs[8:]], axis=0)
            y = y + xs * w[3 - s:4 - s, :]
        tail_scr[...] = x[tb - 8:tb]
        return y * (1.0 / (1.0 + jnp.exp(-y)))

    qc = conv_silu(q_ref, cwq_ref, tq_scr)
    kc = conv_silu(k_ref, cwk_ref, tk_scr)
    vc = conv_silu(v_ref, cwv_ref, tv_scr)

    def l2n(x):
        return x * lax.rsqrt(jnp.sum(x * x, axis=-1, keepdims=True) + NORM_EPS)

    qn = [l2n(qc[:, LANES * j:LANES * (j + 1)]) * (GDN_HEAD_DIM ** -0.5) for j in range(kh_n)]
    kn = [l2n(kc[:, LANES * j:LANES * (j + 1)]) for j in range(kh_n)]

    ab = pltpu.roll(ab_ref[...], (LANES - GDN_HG * g) % LANES, 1)
    xg = ab + dtb_ref[0]
    softplus = jnp.maximum(xg, 0.0) + jnp.log1p(jnp.exp(-jnp.abs(xg)))
    g_all = -jnp.exp(alog_ref[0]) * softplus
    beta_all = 1.0 / (1.0 + jnp.exp(-ab))
    rin = lax.broadcasted_iota(jnp.int32, (tb, LANES), 0) & (CHUNK - 1)
    cum = g_all
    for s in (1, 2, 4, 8, 16, 32):
        cum = cum + jnp.where(rin >= s, pltpu.roll(cum, s, 0), 0.0)

    ii = lax.broadcasted_iota(jnp.int32, (tb, tb), 0)
    jj = lax.broadcasted_iota(jnp.int32, (tb, tb), 1)
    same = (ii >> CHUNK_SHIFT) == (jj >> CHUNK_SHIFT)
    incl = same & (ii >= jj)
    strict = same & (ii > jj)
    eye = jnp.where(ii == jj, 1.0, 0.0)
    nt_dims = (((1,), (1,)), ((), ()))
    tn_dims = (((0,), (0,)), ((), ()))
    ng = ng_ref[...]
    heads = range(GDN_HG)

    cum_t = cum.T
    kb = [x.astype(bf16) for x in kn]
    qb = [x.astype(bf16) for x in qn]
    kk = [lax.dot_general(kb[j], kb[j], nt_dims, preferred_element_type=f32) for j in range(kh_n)]
    qk = [lax.dot_general(qb[j], kb[j], nt_dims, preferred_element_type=f32) for j in range(kh_n)]
    gcol = [cum[:, hh:hh + 1] for hh in heads]
    bcol = [beta_all[:, 32 + hh:33 + hh] for hh in heads]
    dec = [jnp.where(incl, jnp.exp(jnp.where(incl, gcol[hh] - cum_t[hh:hh + 1, :], 0.0)), 0.0) for hh in heads]
    pw = [jnp.where(strict, -(bcol[hh] * kk[hh // GDN_REP] * dec[hh]), 0.0) for hh in heads]
    tmat = [eye + pw[hh] for hh in heads]
    pw = [jnp.dot(pw[hh].astype(bf16), pw[hh].astype(bf16), preferred_element_type=f32) for hh in heads]
    for _ in range(4):
        both = [jnp.dot(jnp.concatenate([tmat[hh], pw[hh]], axis=0).astype(bf16), pw[hh].astype(bf16),
                        preferred_element_type=f32) for hh in heads]
        tmat = [tmat[hh] + both[hh][0:tb] for hh in heads]
        pw = [both[hh][tb:2 * tb] for hh in heads]
    tmat = [tmat[hh] + jnp.dot(tmat[hh].astype(bf16), pw[hh].astype(bf16), preferred_element_type=f32)
            for hh in heads]
    eg = [jnp.exp(gcol[hh]) for hh in heads]
    uw = [jnp.dot(tmat[hh].astype(bf16),
                  jnp.concatenate([bcol[hh] * vc[:, LANES * hh:LANES * (hh + 1)],
                                   (bcol[hh] * eg[hh]) * kn[hh // GDN_REP]], axis=1).astype(bf16),
                  preferred_element_type=f32) for hh in heads]
    qkd = [qk[hh // GDN_REP] * dec[hh] for hh in heads]
    qd = [(qn[hh // GDN_REP] * eg[hh]).astype(bf16) for hh in heads]
    states = [state_scr[hh] for hh in heads]

    for c in range(n_chunk):
        r0 = c * CHUNK
        sb = [states[hh].astype(bf16) for hh in heads]
        ws = [jnp.dot(jnp.concatenate([uw[hh][r0:r0 + CHUNK, LANES:2 * LANES].astype(bf16),
                                       qd[hh][r0:r0 + CHUNK]], axis=0), sb[hh],
                      preferred_element_type=f32) for hh in heads]
        vb = [(uw[hh][r0:r0 + CHUNK, 0:LANES] - ws[hh][0:CHUNK]).astype(bf16) for hh in heads]
        g_last = [cum[r0 + CHUNK - 1:r0 + CHUNK, hh:hh + 1] for hh in heads]
        kd = [(kn[hh // GDN_REP][r0:r0 + CHUNK] * jnp.exp(g_last[hh] - gcol[hh][r0:r0 + CHUNK])).astype(bf16)
              for hh in heads]
        o = [ws[hh][CHUNK:2 * CHUNK] + jnp.dot(qkd[hh][r0:r0 + CHUNK, r0:r0 + CHUNK].astype(bf16), vb[hh],
                                               preferred_element_type=f32) for hh in heads]
        states = [states[hh] * jnp.exp(g_last[hh])
                  + lax.dot_general(kd[hh], vb[hh], tn_dims, preferred_element_type=f32) for hh in heads]
        for hh in heads:
            on = o[hh] * lax.rsqrt(jnp.mean(o[hh] * o[hh], axis=-1, keepdims=True) + NORM_EPS) * ng
            zz = z_ref[r0:r0 + CHUNK, LANES * hh:LANES * (hh + 1)]
            o_ref[r0:r0 + CHUNK, LANES * hh:LANES * (hh + 1)] = (
                on * (zz * (1.0 / (1.0 + jnp.exp(-zz))))).astype(o_ref.dtype)

    for hh in range(GDN_HG):
        state_scr[hh] = states[hh]


def _gdn_branch_pallas(proj, conv_w, a_log, dt_bias, norm_g, bsz, t):
    tb = GDN_TB
    nt = t // tb
    n_groups = GDN_V_HEADS // GDN_HG
    qw = GDN_HG // GDN_REP * GDN_HEAD_DIM
    vw = GDN_HG * GDN_HEAD_DIM
    f32 = jnp.float32

    def per_group(x):
        x = x.astype(f32).reshape(n_groups, 1, GDN_HG)
        return jnp.pad(x, ((0, 0), (0, 0), (0, LANES - GDN_HG)))

    def rows(w, off):
        return pl.BlockSpec((tb, w), lambda b, g, s: (b * nt + s, off // w + g))

    def convw(w, off):
        return pl.BlockSpec((4, w), lambda b, g, s: (0, off // w + g))

    grp = pl.BlockSpec((1, 1, LANES), lambda b, g, s: (g, 0, 0))
    conv_w = conv_w.astype(f32)
    return pl.pallas_call(
        _gdn_kernel,
        grid=(bsz, n_groups, nt),
        in_specs=[rows(qw, P_QKV), rows(qw, P_QKV + GDN_QK_WIDTH), rows(vw, P_QKV + 2 * GDN_QK_WIDTH),
                  rows(vw, P_GZ),
                  pl.BlockSpec((tb, LANES), lambda b, g, s: (b * nt + s, P_GAW // LANES)),
                  convw(qw, 0), convw(qw, GDN_QK_WIDTH), convw(vw, 2 * GDN_QK_WIDTH),
                  grp, grp,
                  pl.BlockSpec((1, LANES), lambda b, g, s: (0, 0))],
        out_specs=pl.BlockSpec((tb, vw), lambda b, g, s: (b * nt + s, g)),
        out_shape=jax.ShapeDtypeStruct((bsz * t, GDN_V_WIDTH), jnp.bfloat16),
        scratch_shapes=[pltpu.VMEM((8, qw), f32), pltpu.VMEM((8, qw), f32), pltpu.VMEM((8, vw), f32),
                        pltpu.VMEM((GDN_HG, GDN_HEAD_DIM, GDN_HEAD_DIM), f32)],
        compiler_params=pltpu.CompilerParams(
            dimension_semantics=("arbitrary", "arbitrary", "arbitrary"),
            vmem_limit_bytes=56 * 1024 * 1024),
        name="gdn_branch",
    )(proj, proj, proj, proj, proj, conv_w, conv_w, conv_w,
      per_group(a_log), per_group(dt_bias), norm_g.astype(f32)[None, :])


def hybrid_mixer(h, w_in, conv_w, a_log, dt_bias, gdn_norm_g, kv_norm_g, w_uk, w_uv,
                 idx_norm_g, idx_norm_b, w_branch_gdn, w_branch_dsa, w_out):
    bsz, t, d = h.shape
    n = bsz * t
    proj = _matmul(h.reshape(n, d), _relayout_w_in(w_in), tn=PROJ_TN)

    def cols(off, width):
        return proj[:, off:off + width].reshape(bsz, t, width)

    gate_a, gate_b = cols(P_GATE_A, 2048), cols(P_GATE_B, 2048)
    o_a = _gdn_branch_pallas(proj, conv_w, a_log, dt_bias, gdn_norm_g, bsz, t)
    o_b = dsa_branch_pallas(proj, kv_norm_g, w_uk, w_uv, idx_norm_g, idx_norm_b, bsz, t)
    u_a = _matmul(o_a, w_branch_gdn).reshape(bsz, t, d)
    u_b = _matmul(o_b, w_branch_dsa).reshape(bsz, t, d)
    merged = jax.nn.sigmoid(gate_a) * u_a + jax.nn.sigmoid(gate_b) * u_b
    return _matmul(merged.reshape(n, d), w_out).reshape(bsz, t, d)


ROUTER_TM = 512
MOE_BLK = 256
MOE_TN = 1024
COMBINE_TM = 128
DISPATCH_TM = 256
VMEM_LIMIT = 56 * 1024 * 1024


def _router_kernel(x_ref, g_ref, rw_ref, rb_ref, meta_ref, cnt_ref, carry_scr):
    f32 = jnp.float32
    tm = x_ref.shape[0]

    @pl.when(pl.program_id(0) == 0)
    def _():
        carry_scr[...] = jnp.zeros(carry_scr.shape, f32)

    x = x_ref[...]
    h = (x * lax.rsqrt(jnp.mean(x * x, axis=-1, keepdims=True) + NORM_EPS) * g_ref[...]).astype(jnp.bfloat16)
    logits = jnp.dot(h, rw_ref[...], preferred_element_type=f32) + rb_ref[...]
    lane = lax.broadcasted_iota(jnp.int32, (tm, LANES), 1).astype(f32)
    vals = logits
    hot_sum = jnp.zeros((tm, LANES), f32)
    tops, ids, hots = [], [], []
    for _ in range(TOP_K):
        m = jnp.max(vals, axis=1, keepdims=True)
        idx = jnp.min(jnp.where(vals == m, lane, float(LANES)), axis=1, keepdims=True)
        hot = lane == idx
        tops.append(m)
        ids.append(idx)
        hots.append(hot)
        vals = jnp.where(hot, -jnp.inf, vals)
        hot_sum = hot_sum + jnp.where(hot, 1.0, 0.0)
    exps = [jnp.exp(v - tops[0]) for v in tops]
    inv = 1.0 / (exps[0] + exps[1] + exps[2] + exps[3])
    ri = lax.broadcasted_iota(jnp.int32, (tm, tm), 0)
    ci = lax.broadcasted_iota(jnp.int32, (tm, tm), 1)
    lower = jnp.where(ri > ci, 1.0, 0.0).astype(jnp.bfloat16)
    base = carry_scr[...] + jnp.dot(lower, hot_sum.astype(jnp.bfloat16), preferred_element_type=f32)
    meta = jnp.zeros((tm, LANES), f32)
    for k in range(TOP_K):
        rank = jnp.sum(jnp.where(hots[k], base, 0.0), axis=1, keepdims=True)
        meta = jnp.where(lane == float(k), ids[k], meta)
        meta = jnp.where(lane == float(TOP_K + k), exps[k] * inv, meta)
        meta = jnp.where(lane == float(2 * TOP_K + k), rank, meta)
    meta_ref[...] = meta
    carry_scr[...] = carry_scr[...] + jnp.sum(hot_sum, axis=0, keepdims=True)
    cnt_ref[...] = carry_scr[...]


def _router(x1, norm_g, router_w, router_b):
    n, d = x1.shape
    tm = ROUTER_TM
    rw = jnp.pad(router_w.astype(jnp.bfloat16), ((0, 0), (0, LANES - N_EXPERTS)))
    rb = jnp.concatenate([router_b.astype(jnp.float32),
                          jnp.full((LANES - N_EXPERTS,), -jnp.inf, jnp.float32)])[None, :]
    return pl.pallas_call(
        _router_kernel,
        grid=(n // tm,),
        in_specs=[pl.BlockSpec((tm, d), lambda i: (i, 0)),
                  pl.BlockSpec((1, d), lambda i: (0, 0)),
                  pl.BlockSpec((d, LANES), lambda i: (0, 0)),
                  pl.BlockSpec((1, LANES), lambda i: (0, 0))],
        out_specs=[pl.BlockSpec((tm, LANES), lambda i: (i, 0)),
                   pl.BlockSpec((1, LANES), lambda i: (0, 0))],
        out_shape=[jax.ShapeDtypeStruct((n, LANES), jnp.float32),
                   jax.ShapeDtypeStruct((1, LANES), jnp.float32)],
        scratch_shapes=[pltpu.VMEM((1, LANES), jnp.float32)],
        compiler_params=pltpu.CompilerParams(dimension_semantics=("arbitrary",), vmem_limit_bytes=VMEM_LIMIT),
        name="moe_router",
    )(x1, norm_g.astype(jnp.float32)[None, :], rw, rb)


def _dispatch_kernel(slot_ref, x_ref, g_ref, xs_in_ref, xs_ref, h_scr, sem):
    del xs_in_ref
    tm = x_ref.shape[0]
    x = x_ref[...]
    h_scr[...] = x * lax.rsqrt(jnp.mean(x * x, axis=-1, keepdims=True) + NORM_EPS) * g_ref[...]

    def copy(r, k):
        return pltpu.make_async_copy(h_scr.at[pl.ds(r, 1)], xs_ref.at[pl.ds(slot_ref[TOP_K * r + k], 1)], sem)

    def start_row(r, carry):
        for k in range(TOP_K):
            copy(r, k).start()
        return carry

    def wait_row(r, carry):
        for k in range(TOP_K):
            copy(r, k).wait()
        return carry

    lax.fori_loop(0, tm, start_row, 0)
    lax.fori_loop(0, tm, wait_row, 0)


def _dispatch(slot, x1, norm_g, n_rows):
    n, d = x1.shape
    tm = DISPATCH_TM
    xs0 = jnp.zeros((n_rows, d), jnp.float32)
    return pl.pallas_call(
        _dispatch_kernel,
        grid=(n // tm,),
        in_specs=[pl.BlockSpec((tm * TOP_K,), lambda i: (i,), memory_space=pltpu.SMEM),
                  pl.BlockSpec((tm, d), lambda i: (i, 0)),
                  pl.BlockSpec((1, d), lambda i: (0, 0)),
                  pl.BlockSpec(memory_space=pl.ANY)],
        out_specs=pl.BlockSpec(memory_space=pl.ANY),
        out_shape=jax.ShapeDtypeStruct((n_rows, d), jnp.float32),
        scratch_shapes=[pltpu.VMEM((tm, d), jnp.float32), pltpu.SemaphoreType.DMA(())],
        input_output_aliases={3: 0},
        compiler_params=pltpu.CompilerParams(dimension_semantics=("arbitrary",), vmem_limit_bytes=VMEM_LIMIT),
        name="moe_dispatch",
    )(slot, x1, norm_g.astype(jnp.float32)[None, :], xs0)


def _moe_up_kernel(be_ref, nb_ref, x_ref, wg_ref, wu_ref, bg_ref, bu_ref, o_ref):
    del be_ref
    j = pl.program_id(1)

    @pl.when(j < nb_ref[0])
    def _():
        x = x_ref[...].astype(jnp.bfloat16)
        gate = jnp.dot(x, wg_ref[0], preferred_element_type=jnp.float32) + bg_ref[0]
        up = jnp.dot(x, wu_ref[0], preferred_element_type=jnp.float32) + bu_ref[0]
        gate = jnp.minimum(gate, SWIGLU_LIMIT)
        up = jnp.clip(up, -SWIGLU_LIMIT, SWIGLU_LIMIT)
        act = (up + 1.0) * (gate * (1.0 / (1.0 + jnp.exp(-SWIGLU_ALPHA * gate))))
        o_ref[...] = act.astype(o_ref.dtype)


def _moe_down_kernel(be_ref, nb_ref, a_ref, wd_ref, bd_ref, o_ref):
    del be_ref
    j = pl.program_id(1)

    @pl.when(j < nb_ref[0])
    def _():
        o_ref[...] = jnp.dot(a_ref[...], wd_ref[0], preferred_element_type=jnp.float32) + bd_ref[0]


def _moe_experts(xs, block_expert, n_used, w_gate, b_gate, w_up, b_up, w_down, b_down):
    n_rows, d = xs.shape
    f = w_gate.shape[2]
    blk, tn = MOE_BLK, MOE_TN
    n_blocks = n_rows // blk
    bf16 = jnp.bfloat16

    def wspec(k):
        return pl.BlockSpec((1, k, tn), lambda c, j, be, nb: (be[j], 0, c))

    bspec = pl.BlockSpec((1, 1, tn), lambda c, j, be, nb: (be[j], 0, c))
    rows_in = lambda k: pl.BlockSpec((blk, k), lambda c, j, be, nb: (j, 0))
    rows_out = pl.BlockSpec((blk, tn), lambda c, j, be, nb: (j, c))
    params = pltpu.CompilerParams(dimension_semantics=("arbitrary", "arbitrary"), vmem_limit_bytes=VMEM_LIMIT)
    act = pl.pallas_call(
        _moe_up_kernel,
        grid_spec=pltpu.PrefetchScalarGridSpec(
            num_scalar_prefetch=2, grid=(f // tn, n_blocks),
            in_specs=[rows_in(d), wspec(d), wspec(d), bspec, bspec],
            out_specs=rows_out),
        out_shape=jax.ShapeDtypeStruct((n_rows, f), bf16),
        compiler_params=params,
        name="moe_up",
    )(block_expert, n_used, xs, w_gate.astype(bf16), w_up.astype(bf16),
      b_gate.astype(jnp.float32)[:, None, :], b_up.astype(jnp.float32)[:, None, :])
    return pl.pallas_call(
        _moe_down_kernel,
        grid_spec=pltpu.PrefetchScalarGridSpec(
            num_scalar_prefetch=2, grid=(d // tn, n_blocks),
            in_specs=[rows_in(f), wspec(f), bspec],
            out_specs=rows_out),
        out_shape=jax.ShapeDtypeStruct((n_rows, d), jnp.float32),
        compiler_params=params,
        name="moe_down",
    )(block_expert, n_used, act, w_down.astype(bf16), b_down.astype(jnp.float32)[:, None, :])


def _combine_kernel(slot_ref, slot_next_ref, x1_ref, meta_ref, g_ref, ys_ref, o_ref, buf, sem):
    i = pl.program_id(0)
    n_steps = pl.num_programs(0)
    tm = x1_ref.shape[0]

    def copy(slots, r, k, half):
        return pltpu.make_async_copy(ys_ref.at[pl.ds(slots[TOP_K * r + k], 1)],
                                     buf.at[half, k, pl.ds(r, 1)], sem.at[half])

    def start_tile(slots, half):
        def body(r, carry):
            for k in range(TOP_K):
                copy(slots, r, k, half).start()
            return carry
        lax.fori_loop(0, tm, body, 0)

    @pl.when(i == 0)
    def _():
        start_tile(slot_ref, 0)

    @pl.when(i + 1 < n_steps)
    def _():
        start_tile(slot_next_ref, (i + 1) % 2)

    half = i % 2

    def wait_row(r, carry):
        for k in range(TOP_K):
            copy(slot_ref, r, k, half).wait()
        return carry

    lax.fori_loop(0, tm, wait_row, 0)
    acc = x1_ref[...]
    for k in range(TOP_K):
        acc = acc + meta_ref[:, TOP_K + k:TOP_K + k + 1] * buf[half, k]
    o_ref[...] = acc * lax.rsqrt(jnp.mean(acc * acc, axis=-1, keepdims=True) + NORM_EPS) * g_ref[...]


def _combine(slot, x1, meta, ys, norm_g):
    n, d = x1.shape
    tm = COMBINE_TM
    n_steps = n // tm
    return pl.pallas_call(
        _combine_kernel,
        grid=(n_steps,),
        in_specs=[pl.BlockSpec((tm * TOP_K,), lambda i: (i,), memory_space=pltpu.SMEM),
                  pl.BlockSpec((tm * TOP_K,), lambda i: (jnp.minimum(i + 1, n_steps - 1),),
                               memory_space=pltpu.SMEM),
                  pl.BlockSpec((tm, d), lambda i: (i, 0)),
                  pl.BlockSpec((tm, LANES), lambda i: (i, 0)),
                  pl.BlockSpec((1, d), lambda i: (0, 0)),
                  pl.BlockSpec(memory_space=pl.ANY)],
        out_specs=pl.BlockSpec((tm, d), lambda i: (i, 0)),
        out_shape=jax.ShapeDtypeStruct((n, d), jnp.float32),
        scratch_shapes=[pltpu.VMEM((2, TOP_K, tm, d), jnp.float32),
                        pltpu.SemaphoreType.DMA((2,))],
        compiler_params=pltpu.CompilerParams(dimension_semantics=("arbitrary",), vmem_limit_bytes=VMEM_LIMIT),
        name="moe_combine",
    )(slot, slot, x1, meta, norm_g.astype(jnp.float32)[None, :], ys)


def moe_final_pallas(x1, norm_ffn_g, router_w, router_b, w_gate, b_gate, w_up, b_up, w_down, b_down,
                     norm_final_g):
    n, d = x1.shape
    meta, cnt = _router(x1, norm_ffn_g, router_w, router_b)
    counts = cnt[0, :N_EXPERTS].astype(jnp.int32)
    padded = (counts + MOE_BLK - 1) // MOE_BLK * MOE_BLK
    pad_end = jnp.cumsum(padded)
    pad_start = pad_end - padded
    expert = meta[:, 0:TOP_K].astype(jnp.int32)
    rank = meta[:, 2 * TOP_K:3 * TOP_K].astype(jnp.int32)
    slot = (pad_start[expert] + rank).reshape(-1)
    n_blocks = n * TOP_K // MOE_BLK + N_EXPERTS
    block_expert = jnp.minimum(jnp.searchsorted(pad_end, jnp.arange(n_blocks, dtype=jnp.int32) * MOE_BLK,
                                                side='right'), N_EXPERTS - 1).astype(jnp.int32)
    n_used = (pad_end[-1:] // MOE_BLK).astype(jnp.int32)
    xs = _dispatch(slot, x1, norm_ffn_g, n_blocks * MOE_BLK)
    ys = _moe_experts(xs, block_expert, n_used, w_gate, b_gate, w_up, b_up, w_down, b_down)
    return _combine(slot, x1, meta, ys, norm_final_g)


def moe_ffn(h, router_w, router_b, w_gate, b_gate, w_up, b_up, w_down, b_down):
    bsz, t, d = h.shape
    tok = h.reshape(-1, d)
    n_tok = tok.shape[0]
    n_assign = n_tok * TOP_K
    logits = tok.astype(jnp.float32) @ router_w.astype(jnp.float32) + router_b.astype(jnp.float32)
    top_val, top_idx = lax.top_k(logits, TOP_K)
    gates = jax.nn.softmax(top_val, axis=-1)
    flat_e = top_idx.reshape(-1)
    flat_tok = jnp.repeat(jnp.arange(n_tok, dtype=jnp.int32), TOP_K)
    order = jnp.argsort(flat_e)
    sorted_e = flat_e[order]
    counts = jnp.bincount(flat_e, length=N_EXPERTS)
    padded = (counts + EXPERT_BLOCK - 1) // EXPERT_BLOCK * EXPERT_BLOCK
    pad_end = jnp.cumsum(padded)
    pad_start = pad_end - padded
    start = jnp.cumsum(counts) - counts
    slot_sorted = jnp.arange(n_assign, dtype=jnp.int32) - start[sorted_e] + pad_start[sorted_e]
    slot = jnp.zeros((n_assign,), jnp.int32).at[order].set(slot_sorted.astype(jnp.int32))
    n_blocks = -(-n_assign // EXPERT_BLOCK) + N_EXPERTS
    slot_tok = jnp.zeros((n_blocks * EXPERT_BLOCK,), jnp.int32).at[slot].set(flat_tok)
    block_expert = jnp.minimum(jnp.searchsorted(pad_end, jnp.arange(n_blocks) * EXPERT_BLOCK, side='right'),
                               N_EXPERTS - 1)
    xs = tok[slot_tok].reshape(n_blocks, EXPERT_BLOCK, d)

    def expert(inp):
        xb, e = inp
        gate = xb @ w_gate[e] + b_gate[e]
        up = xb @ w_up[e] + b_up[e]
        gate = jnp.minimum(gate, SWIGLU_LIMIT)
        up = jnp.clip(up, -SWIGLU_LIMIT, SWIGLU_LIMIT)
        act = (up + 1.0) * (gate * jax.nn.sigmoid(gate * SWIGLU_ALPHA))
        return act @ w_down[e] + b_down[e]

    ys = lax.map(expert, (xs, block_expert)).reshape(-1, d)
    y = ys[slot].reshape(n_tok, TOP_K, d)
    out = jnp.einsum('tk,tkd->td', gates.astype(y.dtype), y)
    return out.reshape(bsz, t, d)


def kernel(x, norm_mix_g, w_in, gdn_conv_w, gdn_a_log, gdn_dt_bias, gdn_norm_g, mla_kv_norm_g,
           mla_w_uk, mla_w_uv, idx_k_norm_g, idx_k_norm_b, w_branch_gdn, w_branch_dsa, w_out,
           norm_ffn_g, router_w, router_b, exp_w_gate, exp_b_gate, exp_w_up, exp_b_up,
           exp_w_down, exp_b_down, norm_final_g):
    l = 0
    x = x + hybrid_mixer(rms_norm(x, norm_mix_g[l]), w_in[l], gdn_conv_w[l], gdn_a_log[l], gdn_dt_bias[l],
                         gdn_norm_g[l], mla_kv_norm_g[l], mla_w_uk[l], mla_w_uv[l], idx_k_norm_g[l],
                         idx_k_norm_b[l], w_branch_gdn[l], w_branch_dsa[l], w_out[l])
    bsz, t, d = x.shape
    out = moe_final_pallas(x.reshape(bsz * t, d), norm_ffn_g[l], router_w[l], router_b[l], exp_w_gate[l],
                           exp_b_gate[l], exp_w_up[l], exp_b_up[l], exp_w_down[l], exp_b_down[l], norm_final_g)
    return out.reshape(bsz, t, d)
```

```python
import functools
import math
import numpy as np
import jax
import jax.numpy as jnp
from jax import lax
from jax.experimental import pallas as pl
from jax.experimental.pallas import tpu as pltpu

D_MODEL = 2048
CHUNK = 64
CHUNK_SHIFT = CHUNK.bit_length() - 1
ROPE_THETA = 10000.0
NORM_EPS = 1e-6
GDN_K_HEADS = 16
GDN_V_HEADS = 32
GDN_HEAD_DIM = 128
GDN_QK_WIDTH = GDN_K_HEADS * GDN_HEAD_DIM
GDN_V_WIDTH = GDN_V_HEADS * GDN_HEAD_DIM
MLA_HEADS = 16
MLA_NOPE_DIM = 128
MLA_ROPE_DIM = 64
MLA_V_DIM = 128
MLA_KV_RANK = 512
MLA_Q_WIDTH = MLA_HEADS * (MLA_NOPE_DIM + MLA_ROPE_DIM)
MLA_O_WIDTH = MLA_HEADS * MLA_V_DIM
IDX_HEADS = 16
IDX_DIM = 64
IDX_ROPE_DIM = 32
TOPK_MAX = 256
Q_BLOCK = 128
N_EXPERTS = 32
TOP_K = 4
D_EXPERT = 2048
SWIGLU_LIMIT = 7.0
SWIGLU_ALPHA = 1.702
EXPERT_BLOCK = 256
LANES = 128
VMEM_LIMIT = 56 * 1024 * 1024
IN_SIZES = (GDN_QK_WIDTH, GDN_QK_WIDTH, GDN_V_WIDTH, GDN_V_WIDTH, GDN_V_HEADS, GDN_V_HEADS,
            MLA_Q_WIDTH, MLA_KV_RANK, MLA_ROPE_DIM, IDX_HEADS * IDX_DIM, IDX_DIM, IDX_HEADS,
            D_MODEL, D_MODEL)


def _mm_kernel(a_ref, b_ref, o_ref):
    o_ref[...] = jnp.dot(a_ref[...], b_ref[...], preferred_element_type=jnp.float32)


def _matmul(a, b, tm=512, tn=1024):
    m, k = a.shape
    n = b.shape[1]
    n_pad = -(-n // tn) * tn
    a = a.astype(jnp.bfloat16)
    b = b.astype(jnp.bfloat16)
    if n_pad != n:
        b = jnp.pad(b, ((0, 0), (0, n_pad - n)))
    out = pl.pallas_call(
        _mm_kernel,
        grid=(n_pad // tn, m // tm),
        in_specs=[pl.BlockSpec((tm, k), lambda j, i: (i, 0)),
                  pl.BlockSpec((k, tn), lambda j, i: (0, j))],
        out_specs=pl.BlockSpec((tm, tn), lambda j, i: (i, j)),
        out_shape=jax.ShapeDtypeStruct((m, n_pad), jnp.float32),
        compiler_params=pltpu.CompilerParams(
            dimension_semantics=("arbitrary", "arbitrary"),
            vmem_limit_bytes=VMEM_LIMIT),
        name="dense_matmul",
    )(a, b)
    return out[:, :n] if n_pad != n else out


def rms_norm(x, g):
    xf = x.astype(jnp.float32)
    y = xf * lax.rsqrt(jnp.mean(xf * xf, axis=-1, keepdims=True) + NORM_EPS)
    return (y * g.astype(jnp.float32)).astype(x.dtype)


def rope_tables(n, dim):
    inv = ROPE_THETA ** (-jnp.arange(0, dim, 2, dtype=jnp.float32) / dim)
    ang = jnp.arange(n, dtype=jnp.float32)[:, None] * inv[None, :]
    return jnp.cos(ang), jnp.sin(ang)


PROJ_TN = 1024
_IN_OFF = np.concatenate([[0], np.cumsum(IN_SIZES)]).tolist()
(_O_GQ, _O_GK, _O_GV, _O_GZ, _O_GA, _O_GB, _O_MQ, _O_CKV, _O_KPE, _O_IQ, _O_IK, _O_IW,
 _O_GATE_A, _O_GATE_B) = _IN_OFF[:-1]
P_QKV = 0
P_GZ = 8192
P_QN = 12288
P_GATE_A = 14336
P_GATE_B = 16384
P_QPE = 18432
P_IQ = 19456
P_CKV = 20480
P_KPE_IK = 20992
P_GAW = 21120
P_WIDTH = 21504


def _proj_column_order():
    cols = []
    cols += list(range(_O_GQ, _O_GZ))
    cols += list(range(_O_GZ, _O_GA))
    hd = MLA_NOPE_DIM + MLA_ROPE_DIM
    cols += [_O_MQ + h * hd + i for h in range(MLA_HEADS) for i in range(MLA_NOPE_DIM)]
    cols += list(range(_O_GATE_A, _O_GATE_A + 2 * D_MODEL))
    cols += [_O_MQ + h * hd + MLA_NOPE_DIM + i for h in range(MLA_HEADS) for i in range(MLA_ROPE_DIM)]
    cols += list(range(_O_IQ, _O_IK))
    cols += list(range(_O_CKV, _O_KPE))
    cols += list(range(_O_KPE, _O_IQ)) + list(range(_O_IK, _O_IW))
    cols += list(range(_O_GA, _O_MQ)) + list(range(_O_IW, _O_GATE_A))
    cols += [-1] * (P_WIDTH - len(cols))
    assert len(cols) == P_WIDTH
    return np.asarray(cols, np.int32)


_PROJ_COLS = _proj_column_order()


def _relayout_w_in(w_in):
    d = w_in.shape[0]
    w = w_in.astype(jnp.bfloat16)
    mq = w[:, _O_MQ:_O_CKV].reshape(d, MLA_HEADS, MLA_NOPE_DIM + MLA_ROPE_DIM)
    parts = [w[:, _O_GQ:_O_GA],
             mq[:, :, :MLA_NOPE_DIM].reshape(d, -1),
             w[:, _O_GATE_A:_O_GATE_A + 2 * D_MODEL],
             mq[:, :, MLA_NOPE_DIM:].reshape(d, -1),
             w[:, _O_IQ:_O_IK], w[:, _O_CKV:_O_KPE], w[:, _O_KPE:_O_IQ], w[:, _O_IK:_O_IW],
             w[:, _O_GA:_O_MQ], w[:, _O_IW:_O_GATE_A]]
    used = sum(p.shape[1] for p in parts)
    return jnp.concatenate(parts + [jnp.zeros((d, P_WIDTH - used), jnp.bfloat16)], axis=1)


NEG_BIG = -1e30
INT_MIN = -2 ** 31
PREP_TM = 256
IDX_TQ = 128
ATT_TQ = 512
ATT_TK = 512
QK_PAD = 256
ATT_HEAD_UNROLL = 2
V_PAD = 256


def _rope_tables_lane(t):
    cos_m, sin_m = rope_tables(t, MLA_ROPE_DIM)
    cos_i, sin_i = rope_tables(t, IDX_ROPE_DIM)
    one = jnp.ones((t, IDX_DIM - IDX_ROPE_DIM), jnp.float32)
    zero = jnp.zeros_like(one)
    cq = jnp.concatenate([cos_m, cos_m] * 2, axis=1)
    sq = jnp.concatenate([-sin_m, sin_m] * 2, axis=1)
    ci = jnp.concatenate([cos_i, cos_i, one] * 2, axis=1)
    si = jnp.concatenate([-sin_i, sin_i, zero] * 2, axis=1)
    cs = jnp.concatenate([cos_m, cos_m, cos_i, cos_i, one], axis=1)
    ss = jnp.concatenate([-sin_m, sin_m, -sin_i, sin_i, zero], axis=1)
    return jnp.concatenate([cq, sq, ci, si, cs, ss], axis=1)


def _dsa_prep_kernel(qn_ref, qpe_ref, iq_ref, ckv_ref, sm_ref, gw_ref, tab_ref,
                     kvg_ref, lng_ref, lnb_ref, wuk_ref, wuv_ref,
                     q_out, k_out, v_out, iq_out, ik_out, iw_out):
    f32, bf16 = jnp.float32, jnp.bfloat16
    tm = qn_ref.shape[0]
    lane = lax.broadcasted_iota(jnp.int32, (tm, LANES), 1)
    lo64 = lane < 64

    def swap32(x):
        return jnp.where((lane & 32) == 0, pltpu.roll(x, 96, 1), pltpu.roll(x, 32, 1))

    def swap16(x):
        return jnp.where((lane & 16) == 0, pltpu.roll(x, 112, 1), pltpu.roll(x, 16, 1))

    cq, sq = tab_ref[:, 0:128], tab_ref[:, 128:256]
    ci, si = tab_ref[:, 256:384], tab_ref[:, 384:512]
    cs, ss = tab_ref[:, 512:640], tab_ref[:, 640:768]
    scale = (MLA_NOPE_DIM + MLA_ROPE_DIM) ** -0.5 * math.log2(math.e)

    for j in range(MLA_HEADS // 2):
        t = qpe_ref[:, LANES * j:LANES * (j + 1)]
        r = (t * cq + swap32(t) * sq) * scale
        q_out[0, 2 * j, :, 128:256] = jnp.where(lo64, r, 0.0).astype(bf16)
        q_out[0, 2 * j + 1, :, 128:256] = jnp.where(lo64, pltpu.roll(r, 64, 1), 0.0).astype(bf16)
        t = iq_ref[:, LANES * j:LANES * (j + 1)]
        r = t * ci + swap16(t) * si
        iq_out[0, 2 * j] = jnp.where(lo64, r, 0.0).astype(bf16)
        iq_out[0, 2 * j + 1] = jnp.where(lo64, pltpu.roll(r, 64, 1), 0.0).astype(bf16)
    for h in range(MLA_HEADS):
        q_out[0, h, :, 0:128] = (qn_ref[:, LANES * h:LANES * (h + 1)] * scale).astype(bf16)

    c = ckv_ref[...]
    cn = c * lax.rsqrt(jnp.mean(c * c, axis=-1, keepdims=True) + NORM_EPS) * kvg_ref[...]
    cnb = cn.astype(bf16)
    kn = jnp.dot(cnb, wuk_ref[...], preferred_element_type=f32)
    vv = jnp.dot(cnb, wuv_ref[...], preferred_element_type=f32)

    s = sm_ref[...]
    hi64 = jnp.logical_not(lo64)
    mu = jnp.sum(jnp.where(hi64, s, 0.0), axis=-1, keepdims=True) * (1.0 / IDX_DIM)
    xc = jnp.where(hi64, s - mu, 0.0)
    var = jnp.sum(xc * xc, axis=-1, keepdims=True) * (1.0 / IDX_DIM)
    ln = xc * lax.rsqrt(var + NORM_EPS) * lng_ref[...] + lnb_ref[...]
    y = jnp.where(hi64, ln, s)
    r = y * cs + jnp.where(lo64, swap32(y), swap16(y)) * ss
    kpe_t = jnp.where(lo64, r, 0.0).T.astype(bf16)
    for h in range(MLA_HEADS):
        k_out[0, h, 0:128, :] = kn[:, LANES * h:LANES * (h + 1)].T.astype(bf16)
        k_out[0, h, 128:256, :] = kpe_t
        v_out[0, h, :, 0:128] = vv[:, LANES * h:LANES * (h + 1)].astype(bf16)
        v_out[0, h, :, 128:256] = jnp.ones((tm, LANES), bf16)
    ik_out[0] = jnp.where(lo64, pltpu.roll(r, 64, 1), 0.0).astype(bf16)
    iw_out[...] = gw_ref[...] * (IDX_HEADS ** -0.5 * IDX_DIM ** -0.5)


def _dsa_prep(proj, tab, kv_norm_g, idx_norm_g, idx_norm_b, w_uk, w_uv, bsz, t):
    tm = PREP_TM
    nt = t // tm
    zeros64 = jnp.zeros((64,), jnp.float32)
    lng = jnp.concatenate([zeros64, idx_norm_g.astype(jnp.float32)])[None, :]
    lnb = jnp.concatenate([zeros64, idx_norm_b.astype(jnp.float32)])[None, :]
    kvg = kv_norm_g.astype(jnp.float32)[None, :]

    def rows(w, off):
        return pl.BlockSpec((tm, w), lambda b, i: (b * nt + i, off // w))

    def const(shape):
        return pl.BlockSpec(shape, lambda b, i: (0,) * len(shape))

    def heads(w):
        return pl.BlockSpec((1, MLA_HEADS, tm, w), lambda b, i: (b, 0, i, 0))

    bf16 = jnp.bfloat16
    return pl.pallas_call(
        _dsa_prep_kernel,
        grid=(bsz, nt),
        in_specs=[rows(2048, P_QN), rows(1024, P_QPE), rows(1024, P_IQ), rows(512, P_CKV),
                  rows(LANES, P_KPE_IK), rows(LANES, P_GAW),
                  pl.BlockSpec((tm, 768), lambda b, i: (i, 0)),
                  const((1, MLA_KV_RANK)), const((1, LANES)), const((1, LANES)),
                  const((MLA_KV_RANK, MLA_HEADS * MLA_NOPE_DIM)), const((MLA_KV_RANK, MLA_O_WIDTH))],
        out_specs=[heads(QK_PAD), pl.BlockSpec((1, MLA_HEADS, QK_PAD, tm), lambda b, i: (b, 0, 0, i)),
                   heads(V_PAD), heads(LANES),
                   pl.BlockSpec((1, tm, LANES), lambda b, i: (b, i, 0)),
                   pl.BlockSpec((tm, LANES), lambda b, i: (b * nt + i, 0))],
        out_shape=[jax.ShapeDtypeStruct((bsz, MLA_HEADS, t, QK_PAD), bf16),
                   jax.ShapeDtypeStruct((bsz, MLA_HEADS, QK_PAD, t), bf16),
                   jax.ShapeDtypeStruct((bsz, MLA_HEADS, t, V_PAD), bf16),
                   jax.ShapeDtypeStruct((bsz, IDX_HEADS, t, LANES), bf16),
                   jax.ShapeDtypeStruct((bsz, t, LANES), bf16),
                   jax.ShapeDtypeStruct((bsz * t, LANES), jnp.float32)],
        compiler_params=pltpu.CompilerParams(
            dimension_semantics=("arbitrary", "arbitrary"), vmem_limit_bytes=VMEM_LIMIT),
        name="dsa_prep",
    )(proj, proj, proj, proj, proj, proj, tab, kvg, lng, lnb,
      w_uk.astype(bf16), w_uv.astype(bf16))


def _indexer_kernel(iq_ref, ik_ref, iw_ref, bias_ref, key_scr, tie_scr, *, n_keep):
    f32, i32 = jnp.float32, jnp.int32
    tq = iq_ref.shape[2]
    kc = key_scr.shape[2]
    nch_total = key_scr.shape[0]
    qi = pl.program_id(1)
    nch = ((qi + 1) * tq + kc - 1) // kc
    q_chunk = (lax.broadcasted_iota(i32, (tq, kc), 0) + qi * tq) >> 6
    col0 = lax.broadcasted_iota(i32, (tq, kc), 1)
    n_sub = kc // LANES

    def score_chunk(c, carry):
        ik = ik_ref[0, pl.ds(pl.multiple_of(c * kc, kc), kc), :]
        acc = jnp.zeros((tq, kc), f32)
        for h in range(IDX_HEADS):
            s = lax.dot_general(iq_ref[0, h], ik, (((1,), (1,)), ((), ())), preferred_element_type=f32)
            acc = acc + iw_ref[:, 64 + h:65 + h] * jnp.maximum(s, 0.0)
        bits = pltpu.bitcast(acc, i32)
        key = bits ^ ((bits >> 31) & 0x7FFFFFFF)
        adm = ((col0 + c * kc) >> CHUNK_SHIFT) <= q_chunk
        key_scr[c] = jnp.where(adm, key, INT_MIN)
        return carry

    lax.fori_loop(0, nch, score_chunk, 0)

    def lane_fold(x):
        part = x[:, 0:LANES]
        for j in range(1, n_sub):
            part = part + x[:, LANES * j:LANES * (j + 1)]
        return part

    def rep(x):
        return jnp.concatenate([x] * n_sub, axis=1)

    def count(pred_fn):
        def body(c, acc):
            return acc + lane_fold(jnp.where(pred_fn(key_scr[c], c), 1.0, 0.0))
        acc = lax.fori_loop(0, nch, body, jnp.zeros((tq, LANES), f32))
        return jnp.broadcast_to(jnp.sum(acc, axis=1, keepdims=True), (tq, LANES))

    keep = float(n_keep)
    cnt = count(lambda k, c: k >= 0)
    thr = jnp.where(cnt >= keep, 0, INT_MIN).astype(i32)
    cnt_thr = jnp.where(cnt >= keep, cnt, (nch * kc).astype(f32))

    def bit_cond(state):
        i, _, cnt_thr = state
        return jnp.logical_and(i < 31, jnp.max(jnp.abs(cnt_thr - keep)) > 0.0)

    def bit_pass(state):
        i, thr, cnt_thr = state
        cand = thr | jnp.left_shift(jnp.int32(1), 30 - i)
        cand_w = rep(cand)
        cnt = count(lambda k, c: k >= cand_w)
        take = cnt >= keep
        return i + 1, jnp.where(take, cand, thr), jnp.where(take, cnt, cnt_thr)

    _, thr, cnt_ge = lax.while_loop(bit_cond, bit_pass, (jnp.int32(0), thr, cnt_thr))
    thr_w = rep(thr)
    tie_scr[...] = jnp.full((tq, LANES), 2 ** 30, i32)

    @pl.when(jnp.max(cnt_ge) > keep)
    def _():
        cnt_gt = count(lambda k, c: k > thr_w)
        need = keep - cnt_gt

        def idx_pass(i, x):
            cand = x | jnp.left_shift(jnp.int32(1), idx_bits - 1 - i)
            cand_w = rep(cand)
            cnt = count(lambda k, c: (k == thr_w) & ((col0 + c * kc) < cand_w))
            return jnp.where(cnt < need, cand, x)

        idx_bits = (nch_total * kc - 1).bit_length()
        x = lax.fori_loop(0, idx_bits, idx_pass, jnp.zeros((tq, LANES), i32))
        tie_scr[...] = jnp.where(cnt_ge > keep, x, 2 ** 30)

    tie_w = rep(tie_scr[...])

    def write_chunk(c, carry):
        k = key_scr[c]
        sel = (k > thr_w) | ((k == thr_w) & ((col0 + c * kc) <= tie_w))
        sel = sel & (k != INT_MIN)
        bias_ref[0, c] = jnp.where(sel, 0.0, NEG_BIG)
        return carry

    lax.fori_loop(0, nch, write_chunk, 0)

    def fill_chunk(c, carry):
        bias_ref[0, c] = jnp.full((tq, kc), NEG_BIG, f32)
        return carry

    lax.fori_loop(nch, nch_total, fill_chunk, 0)


def _indexer(iq, ik, iw, bsz, t, n_keep):
    tq, kc = IDX_TQ, ATT_TK
    nq, nkc = t // tq, t // kc
    return pl.pallas_call(
        functools.partial(_indexer_kernel, n_keep=n_keep),
        grid=(bsz, nq),
        in_specs=[pl.BlockSpec((1, IDX_HEADS, tq, LANES), lambda b, i: (b, 0, i, 0)),
                  pl.BlockSpec((1, t, LANES), lambda b, i: (b, 0, 0)),
                  pl.BlockSpec((tq, LANES), lambda b, i: (b * nq + i, 0))],
        out_specs=pl.BlockSpec((1, nkc, tq, kc), lambda b, i: (b, 0, i, 0)),
        out_shape=jax.ShapeDtypeStruct((bsz, nkc, t, kc), jnp.float32),
        scratch_shapes=[pltpu.VMEM((nkc, tq, kc), jnp.int32), pltpu.VMEM((tq, LANES), jnp.int32)],
        compiler_params=pltpu.CompilerParams(
            dimension_semantics=("arbitrary", "arbitrary"), vmem_limit_bytes=VMEM_LIMIT),
        name="dsa_indexer",
    )(iq, ik, iw)


def _attn_kernel(q_ref, k_ref, v_ref, b_ref, o_ref, m_scr, acc_scr):
    f32 = jnp.float32
    tq, tk = b_ref.shape[2], b_ref.shape[3]
    qi, ki = pl.program_id(1), pl.program_id(2)
    last = ((qi + 1) * tq - 1) // tk

    @pl.when(ki == 0)
    def _():
        m_scr[...] = jnp.full(m_scr.shape, NEG_BIG, f32)
        acc_scr[...] = jnp.zeros(acc_scr.shape, f32)

    @pl.when(ki <= last)
    def _():
        bias = b_ref[0, 0]

        def head_group(gi, carry):
            hs = [gi * ATT_HEAD_UNROLL + u for u in range(ATT_HEAD_UNROLL)]
            s = [jnp.dot(q_ref[0, h], k_ref[0, h], preferred_element_type=f32) + bias for h in hs]
            m_prev = [m_scr[h] for h in hs]
            m_new = [jnp.maximum(mp, jnp.max(x, axis=1, keepdims=True)) for mp, x in zip(m_prev, s)]
            p = [jnp.exp2(x - mn[:, 0:1]) for x, mn in zip(s, m_new)]
            pv = [jnp.dot(x.astype(jnp.bfloat16), v_ref[0, h], preferred_element_type=f32)
                  for x, h in zip(p, hs)]
            for u, h in enumerate(hs):
                alpha = jnp.exp2(m_prev[u] - m_new[u])
                acc_scr[h, :, 0:128] = alpha * acc_scr[h, :, 0:128] + pv[u][:, 0:128]
                acc_scr[h, :, 128:256] = alpha * acc_scr[h, :, 128:256] + pv[u][:, 128:256]
                m_scr[h] = m_new[u]
            return carry

        lax.fori_loop(0, MLA_HEADS // ATT_HEAD_UNROLL, head_group, 0)

    @pl.when(ki == last)
    def _():
        for h in range(MLA_HEADS):
            o_ref[0, :, LANES * h:LANES * (h + 1)] = (
                acc_scr[h, :, 0:128] / acc_scr[h, :, 128:256]).astype(o_ref.dtype)


def _masked_attention(q, k, v, bias, bsz, t):
    tq, tk = ATT_TQ, ATT_TK
    nq, nk = t // tq, t // tk

    def kmap(b, i, j):
        return (b, 0, jnp.minimum(j, ((i + 1) * tq - 1) // tk), 0)

    return pl.pallas_call(
        _attn_kernel,
        grid=(bsz, nq, nk),
        in_specs=[pl.BlockSpec((1, MLA_HEADS, tq, QK_PAD), lambda b, i, j: (b, 0, i, 0)),
                  pl.BlockSpec((1, MLA_HEADS, QK_PAD, tk),
                               lambda b, i, j: (b, 0, 0, jnp.minimum(j, ((i + 1) * tq - 1) // tk))),
                  pl.BlockSpec((1, MLA_HEADS, tk, V_PAD), kmap),
                  pl.BlockSpec((1, 1, tq, tk),
                               lambda b, i, j: (b, jnp.minimum(j, ((i + 1) * tq - 1) // tk), i, 0))],
        out_specs=pl.BlockSpec((1, tq, MLA_O_WIDTH), lambda b, i, j: (b, i, 0)),
        out_shape=jax.ShapeDtypeStruct((bsz, t, MLA_O_WIDTH), jnp.bfloat16),
        scratch_shapes=[pltpu.VMEM((MLA_HEADS, tq, LANES), jnp.float32),
                        pltpu.VMEM((MLA_HEADS, tq, V_PAD), jnp.float32)],
        compiler_params=pltpu.CompilerParams(
            dimension_semantics=("arbitrary", "arbitrary", "arbitrary"),
            vmem_limit_bytes=VMEM_LIMIT),
        name="dsa_attention",
    )(q, k, v, bias)


def dsa_branch_pallas(proj, kv_norm_g, w_uk, w_uv, idx_norm_g, idx_norm_b, bsz, t):
    n_keep = min(TOPK_MAX, t // 4)
    tab = _rope_tables_lane(t)
    q, k, v, iq, ik, iw = _dsa_prep(proj, tab, kv_norm_g, idx_norm_g, idx_norm_b, w_uk, w_uv, bsz, t)
    bias = _indexer(iq, ik, iw, bsz, t, n_keep)
    o = _masked_attention(q, k, v, bias, bsz, t)
    return o.reshape(bsz * t, MLA_O_WIDTH)


GDN_TB = 256
GDN_HG = 4
GDN_REP = GDN_V_HEADS // GDN_K_HEADS


def _gdn_kernel(q_ref, k_ref, v_ref, z_ref, ab_ref, cwq_ref, cwk_ref, cwv_ref, alog_ref, dtb_ref, ng_ref,
                o_ref, tq_scr, tk_scr, tv_scr, state_scr):
    f32, bf16 = jnp.float32, jnp.bfloat16
    tb = q_ref.shape[0]
    g = pl.program_id(1)
    step = pl.program_id(2)
    n_chunk = tb // CHUNK
    kh_n = GDN_HG // GDN_REP

    @pl.when(step == 0)
    def _():
        tq_scr[...] = jnp.zeros(tq_scr.shape, f32)
        tk_scr[...] = jnp.zeros(tk_scr.shape, f32)
        tv_scr[...] = jnp.zeros(tv_scr.shape, f32)
        state_scr[...] = jnp.zeros(state_scr.shape, f32)

    def conv_silu(x_ref, w_ref, tail_scr):
        x = x_ref[...]
        tail = tail_scr[...]
        w = w_ref[...]
        row8 = lax.broadcasted_iota(jnp.int32, tail.shape, 0)
        y = x * w[3:4, :]
        for s in (1, 2, 3):
            xs = pltpu.roll(x, s, 0)
            top = jnp.where(row8 < s, pltpu.roll(tail, s, 0), xs[0:8])
            xs = jnp.concatenate([top, xs[8:]], axis=0)
            y = y + xs * w[3 - s:4 - s, :]
        tail_scr[...] = x[tb - 8:tb]
        return y * (1.0 / (1.0 + jnp.exp(-y)))

    qc = conv_silu(q_ref, cwq_ref, tq_scr)
    kc = conv_silu(k_ref, cwk_ref, tk_scr)
    vc = conv_silu(v_ref, cwv_ref, tv_scr)

    def l2n(x):
        return x * lax.rsqrt(jnp.sum(x * x, axis=-1, keepdims=True) + NORM_EPS)

    qn = [l2n(qc[:, LANES * j:LANES * (j + 1)]) * (GDN_HEAD_DIM ** -0.5) for j in range(kh_n)]
    kn = [l2n(kc[:, LANES * j:LANES * (j + 1)]) for j in range(kh_n)]

    ab = pltpu.roll(ab_ref[...], (LANES - GDN_HG * g) % LANES, 1)
    xg = ab + dtb_ref[0]
    softplus = jnp.maximum(xg, 0.0) + jnp.log1p(jnp.exp(-jnp.abs(xg)))
    g_all = -jnp.exp(alog_ref[0]) * softplus
    beta_all = 1.0 / (1.0 + jnp.exp(-ab))
    rin = lax.broadcasted_iota(jnp.int32, (tb, LANES), 0) & (CHUNK - 1)
    cum = g_all
    for s in (1, 2, 4, 8, 16, 32):
        cum = cum + jnp.where(rin >= s, pltpu.roll(cum, s, 0), 0.0)

    ii = lax.broadcasted_iota(jnp.int32, (tb, tb), 0)
    jj = lax.broadcasted_iota(jnp.int32, (tb, tb), 1)
    same = (ii >> CHUNK_SHIFT) == (jj >> CHUNK_SHIFT)
    incl = same & (ii >= jj)
    strict = same & (ii > jj)
    eye = jnp.where(ii == jj, 1.0, 0.0)
    nt_dims = (((1,), (1,)), ((), ()))
    tn_dims = (((0,), (0,)), ((), ()))
    ng = ng_ref[...]
    heads = range(GDN_HG)

    cum_t = cum.T
    kb = [x.astype(bf16) for x in kn]
    qb = [x.astype(bf16) for x in qn]
    kk = [lax.dot_general(kb[j], kb[j], nt_dims, preferred_element_type=f32) for j in range(kh_n)]
    qk = [lax.dot_general(qb[j], kb[j], nt_dims, preferred_element_type=f32) for j in range(kh_n)]
    gcol = [cum[:, hh:hh + 1] for hh in heads]
    bcol = [beta_all[:, 32 + hh:33 + hh] for hh in heads]
    dec = [jnp.where(incl, jnp.exp(jnp.where(incl, gcol[hh] - cum_t[hh:hh + 1, :], 0.0)), 0.0) for hh in heads]
    pw = [jnp.where(strict, -(bcol[hh] * kk[hh // GDN_REP] * dec[hh]), 0.0) for hh in heads]
    tmat = [eye + pw[hh] for hh in heads]
    pw = [jnp.dot(pw[hh].astype(bf16), pw[hh].astype(bf16), preferred_element_type=f32) for hh in heads]
    for _ in range(4):
        both = [jnp.dot(jnp.concatenate([tmat[hh], pw[hh]], axis=0).astype(bf16), pw[hh].astype(bf16),
                        preferred_element_type=f32) for hh in heads]
        tmat = [tmat[hh] + both[hh][0:tb] for hh in heads]
        pw = [both[hh][tb:2 * tb] for hh in heads]
    tmat = [tmat[hh] + jnp.dot(tmat[hh].astype(bf16), pw[hh].astype(bf16), preferred_element_type=f32)
            for hh in heads]
    eg = [jnp.exp(gcol[hh]) for hh in heads]
    uw = [jnp.dot(tmat[hh].astype(bf16),
                  jnp.concatenate([bcol[hh] * vc[:, LANES * hh:LANES * (hh + 1)],
                                   (bcol[hh] * eg[hh]) * kn[hh // GDN_REP]], axis=1).astype(bf16),
                  preferred_element_type=f32) for hh in heads]
    qkd = [qk[hh // GDN_REP] * dec[hh] for hh in heads]
    qd = [(qn[hh // GDN_REP] * eg[hh]).astype(bf16) for hh in heads]
    states = [state_scr[hh] for hh in heads]

    for c in range(n_chunk):
        r0 = c * CHUNK
        sb = [states[hh].astype(bf16) for hh in heads]
        ws = [jnp.dot(jnp.concatenate([uw[hh][r0:r0 + CHUNK, LANES:2 * LANES].astype(bf16),
                                       qd[hh][r0:r0 + CHUNK]], axis=0), sb[hh],
                      preferred_element_type=f32) for hh in heads]
        vb = [(uw[hh][r0:r0 + CHUNK, 0:LANES] - ws[hh][0:CHUNK]).astype(bf16) for hh in heads]
        g_last = [cum[r0 + CHUNK - 1:r0 + CHUNK, hh:hh + 1] for hh in heads]
        kd = [(kn[hh // GDN_REP][r0:r0 + CHUNK] * jnp.exp(g_last[hh] - gcol[hh][r0:r0 + CHUNK])).astype(bf16)
              for hh in heads]
        o = [ws[hh][CHUNK:2 * CHUNK] + jnp.dot(qkd[hh][r0:r0 + CHUNK, r0:r0 + CHUNK].astype(bf16), vb[hh],
                                               preferred_element_type=f32) for hh in heads]
        states = [states[hh] * jnp.exp(g_last[hh])
                  + lax.dot_general(kd[hh], vb[hh], tn_dims, preferred_element_type=f32) for hh in heads]
        for hh in heads:
            on = o[hh] * lax.rsqrt(jnp.mean(o[hh] * o[hh], axis=-1, keepdims=True) + NORM_EPS) * ng
            zz = z_ref[r0:r0 + CHUNK, LANES * hh:LANES * (hh + 1)]
            o_ref[r0:r0 + CHUNK, LANES * hh:LANES * (hh + 1)] = (
                on * (zz * (1.0 / (1.0 + jnp.exp(-zz))))).astype(o_ref.dtype)

    for hh in range(GDN_HG):
        state_scr[hh] = states[hh]


def _gdn_branch_pallas(proj, conv_w, a_log, dt_bias, norm_g, bsz, t):
    tb = GDN_TB
    nt = t // tb
    n_groups = GDN_V_HEADS // GDN_HG
    qw = GDN_HG // GDN_REP * GDN_HEAD_DIM
    vw = GDN_HG * GDN_HEAD_DIM
    f32 = jnp.float32

    def per_group(x):
        x = x.astype(f32).reshape(n_groups, 1, GDN_HG)
        return jnp.pad(x, ((0, 0), (0, 0), (0, LANES - GDN_HG)))

    def rows(w, off):
        return pl.BlockSpec((tb, w), lambda b, g, s: (b * nt + s, off // w + g))

    def convw(w, off):
        return pl.BlockSpec((4, w), lambda b, g, s: (0, off // w + g))

    grp = pl.BlockSpec((1, 1, LANES), lambda b, g, s: (g, 0, 0))
    conv_w = conv_w.astype(f32)
    return pl.pallas_call(
        _gdn_kernel,
        grid=(bsz, n_groups, nt),
        in_specs=[rows(qw, P_QKV), rows(qw, P_QKV + GDN_QK_WIDTH), rows(vw, P_QKV + 2 * GDN_QK_WIDTH),
                  rows(vw, P_GZ),
                  pl.BlockSpec((tb, LANES), lambda b, g, s: (b * nt + s, P_GAW // LANES)),
                  convw(qw, 0), convw(qw, GDN_QK_WIDTH), convw(vw, 2 * GDN_QK_WIDTH),
                  grp, grp,
                  pl.BlockSpec((1, LANES), lambda b, g, s: (0, 0))],
        out_specs=pl.BlockSpec((tb, vw), lambda b, g, s: (b * nt + s, g)),
        out_shape=jax.ShapeDtypeStruct((bsz * t, GDN_V_WIDTH), jnp.bfloat16),
        scratch_shapes=[pltpu.VMEM((8, qw), f32), pltpu.VMEM((8, qw), f32), pltpu.VMEM((8, vw), f32),
                        pltpu.VMEM((GDN_HG, GDN_HEAD_DIM, GDN_HEAD_DIM), f32)],
        compiler_params=pltpu.CompilerParams(
            dimension_semantics=("arbitrary", "arbitrary", "arbitrary"),
            vmem_limit_bytes=VMEM_LIMIT),
        name="gdn_branch",
    )(proj, proj, proj, proj, proj, conv_w, conv_w, conv_w,
      per_group(a_log), per_group(dt_bias), norm_g.astype(f32)[None, :])


MERGE_TM = 512
MERGE_TN = 512


def _merge_kernel(oa_ref, ob_ref, wa_ref, wb_ref, ga_ref, gb_ref, o_ref):
    f32 = jnp.float32
    ua = jnp.dot(oa_ref[...], wa_ref[...], preferred_element_type=f32)
    ub = jnp.dot(ob_ref[...], wb_ref[...], preferred_element_type=f32)
    sa = 1.0 / (1.0 + jnp.exp(-ga_ref[...]))
    sb = 1.0 / (1.0 + jnp.exp(-gb_ref[...]))
    o_ref[...] = (sa * ua + sb * ub).astype(o_ref.dtype)


def _outproj_kernel(m_ref, w_ref, x_ref, o_ref):
    o_ref[...] = x_ref[...] + jnp.dot(m_ref[...], w_ref[...], preferred_element_type=jnp.float32)


def _merge_and_project(x, proj, o_a, o_b, w_branch_gdn, w_branch_dsa, w_out):
    n, d = x.shape
    tm, tn = MERGE_TM, MERGE_TN
    bf16 = jnp.bfloat16
    params = pltpu.CompilerParams(dimension_semantics=("arbitrary", "arbitrary"), vmem_limit_bytes=VMEM_LIMIT)
    merged = pl.pallas_call(
        _merge_kernel,
        grid=(d // tn, n // tm),
        in_specs=[pl.BlockSpec((tm, o_a.shape[1]), lambda j, i: (i, 0)),
                  pl.BlockSpec((tm, o_b.shape[1]), lambda j, i: (i, 0)),
                  pl.BlockSpec((o_a.shape[1], tn), lambda j, i: (0, j)),
                  pl.BlockSpec((o_b.shape[1], tn), lambda j, i: (0, j)),
                  pl.BlockSpec((tm, tn), lambda j, i: (i, P_GATE_A // tn + j)),
                  pl.BlockSpec((tm, tn), lambda j, i: (i, P_GATE_B // tn + j))],
        out_specs=pl.BlockSpec((tm, tn), lambda j, i: (i, j)),
        out_shape=jax.ShapeDtypeStruct((n, d), bf16),
        compiler_params=params,
        name="branch_merge",
    )(o_a, o_b, w_branch_gdn.astype(bf16), w_branch_dsa.astype(bf16), proj, proj)
    return pl.pallas_call(
        _outproj_kernel,
        grid=(n // tm,),
        in_specs=[pl.BlockSpec((tm, d), lambda i: (i, 0)),
                  pl.BlockSpec((d, d), lambda i: (0, 0)),
                  pl.BlockSpec((tm, d), lambda i: (i, 0))],
        out_specs=pl.BlockSpec((tm, d), lambda i: (i, 0)),
        out_shape=jax.ShapeDtypeStruct((n, d), jnp.float32),
        compiler_params=pltpu.CompilerParams(dimension_semantics=("arbitrary",), vmem_limit_bytes=VMEM_LIMIT),
        name="out_projection",
    )(merged, w_out.astype(bf16), x)


def hybrid_mixer(x, norm_g, w_in, conv_w, a_log, dt_bias, gdn_norm_g, kv_norm_g, w_uk, w_uv,
                 idx_norm_g, idx_norm_b, w_branch_gdn, w_branch_dsa, w_out):
    bsz, t, d = x.shape
    n = bsz * t
    h = rms_norm(x, norm_g).reshape(n, d)
    proj = _matmul(h, _relayout_w_in(w_in), tn=PROJ_TN)
    o_a = _gdn_branch_pallas(proj, conv_w, a_log, dt_bias, gdn_norm_g, bsz, t)
    o_b = dsa_branch_pallas(proj, kv_norm_g, w_uk, w_uv, idx_norm_g, idx_norm_b, bsz, t)
    return _merge_and_project(x.reshape(n, d), proj, o_a, o_b, w_branch_gdn, w_branch_dsa, w_out)


ROUTER_TM = 512
MOE_BLK = 256
MOE_TN = 512
COMBINE_TM = 128
DISPATCH_TM = 256


def _router_kernel(x_ref, g_ref, rw_ref, rb_ref, meta_ref, cnt_ref, carry_scr):
    f32 = jnp.float32
    tm = x_ref.shape[0]

    @pl.when(pl.program_id(0) == 0)
    def _():
        carry_scr[...] = jnp.zeros(carry_scr.shape, f32)

    x = x_ref[...]
    h = (x * lax.rsqrt(jnp.mean(x * x, axis=-1, keepdims=True) + NORM_EPS) * g_ref[...]).astype(jnp.bfloat16)
    logits = jnp.dot(h, rw_ref[...], preferred_element_type=f32) + rb_ref[...]
    lane = lax.broadcasted_iota(jnp.int32, (tm, LANES), 1).astype(f32)
    vals = logits
    hot_sum = jnp.zeros((tm, LANES), f32)
    tops, ids, hots = [], [], []
    for _ in range(TOP_K):
        m = jnp.max(vals, axis=1, keepdims=True)
        idx = jnp.min(jnp.where(vals == m, lane, float(LANES)), axis=1, keepdims=True)
        hot = lane == idx
        tops.append(m)
        ids.append(idx)
        hots.append(hot)
        vals = jnp.where(hot, -jnp.inf, vals)
        hot_sum = hot_sum + jnp.where(hot, 1.0, 0.0)
    exps = [jnp.exp(v - tops[0]) for v in tops]
    inv = 1.0 / (exps[0] + exps[1] + exps[2] + exps[3])
    ri = lax.broadcasted_iota(jnp.int32, (tm, tm), 0)
    ci = lax.broadcasted_iota(jnp.int32, (tm, tm), 1)
    lower = jnp.where(ri > ci, 1.0, 0.0).astype(jnp.bfloat16)
    base = carry_scr[...] + jnp.dot(lower, hot_sum.astype(jnp.bfloat16), preferred_element_type=f32)
    meta = jnp.zeros((tm, LANES), f32)
    for k in range(TOP_K):
        rank = jnp.sum(jnp.where(hots[k], base, 0.0), axis=1, keepdims=True)
        meta = jnp.where(lane == float(k), ids[k], meta)
        meta = jnp.where(lane == float(TOP_K + k), exps[k] * inv, meta)
        meta = jnp.where(lane == float(2 * TOP_K + k), rank, meta)
    meta_ref[...] = meta
    carry_scr[...] = carry_scr[...] + jnp.sum(hot_sum, axis=0, keepdims=True)
    cnt_ref[...] = carry_scr[...]


def _router(x1, norm_g, router_w, router_b):
    n, d = x1.shape
    tm = ROUTER_TM
    rw = jnp.pad(router_w.astype(jnp.bfloat16), ((0, 0), (0, LANES - N_EXPERTS)))
    rb = jnp.concatenate([router_b.astype(jnp.float32),
                          jnp.full((LANES - N_EXPERTS,), -jnp.inf, jnp.float32)])[None, :]
    return pl.pallas_call(
        _router_kernel,
        grid=(n // tm,),
        in_specs=[pl.BlockSpec((tm, d), lambda i: (i, 0)),
                  pl.BlockSpec((1, d), lambda i: (0, 0)),
                  pl.BlockSpec((d, LANES), lambda i: (0, 0)),
                  pl.BlockSpec((1, LANES), lambda i: (0, 0))],
        out_specs=[pl.BlockSpec((tm, LANES), lambda i: (i, 0)),
                   pl.BlockSpec((1, LANES), lambda i: (0, 0))],
        out_shape=[jax.ShapeDtypeStruct((n, LANES), jnp.float32),
                   jax.ShapeDtypeStruct((1, LANES), jnp.float32)],
        scratch_shapes=[pltpu.VMEM((1, LANES), jnp.float32)],
        compiler_params=pltpu.CompilerParams(dimension_semantics=("arbitrary",), vmem_limit_bytes=VMEM_LIMIT),
        name="moe_router",
    )(x1, norm_g.astype(jnp.float32)[None, :], rw, rb)


def _dispatch_kernel(slot_ref, x_ref, g_ref, xs_in_ref, xs_ref, h_scr, sem):
    del xs_in_ref
    tm = x_ref.shape[0]
    x = x_ref[...]
    h_scr[...] = x * lax.rsqrt(jnp.mean(x * x, axis=-1, keepdims=True) + NORM_EPS) * g_ref[...]

    def copy(r, k):
        return pltpu.make_async_copy(h_scr.at[pl.ds(r, 1)], xs_ref.at[pl.ds(slot_ref[TOP_K * r + k], 1)], sem)

    def start_row(r, carry):
        for k in range(TOP_K):
            copy(r, k).start()
        return carry

    def wait_row(r, carry):
        for k in range(TOP_K):
            copy(r, k).wait()
        return carry

    lax.fori_loop(0, tm, start_row, 0)
    lax.fori_loop(0, tm, wait_row, 0)


def _dispatch(slot, x1, norm_g, n_rows):
    n, d = x1.shape
    tm = DISPATCH_TM
    xs0 = jnp.zeros((n_rows, d), jnp.float32)
    return pl.pallas_call(
        _dispatch_kernel,
        grid=(n // tm,),
        in_specs=[pl.BlockSpec((tm * TOP_K,), lambda i: (i,), memory_space=pltpu.SMEM),
                  pl.BlockSpec((tm, d), lambda i: (i, 0)),
                  pl.BlockSpec((1, d), lambda i: (0, 0)),
                  pl.BlockSpec(memory_space=pl.ANY)],
        out_specs=pl.BlockSpec(memory_space=pl.ANY),
        out_shape=jax.ShapeDtypeStruct((n_rows, d), jnp.float32),
        scratch_shapes=[pltpu.VMEM((tm, d), jnp.float32), pltpu.SemaphoreType.DMA(())],
        input_output_aliases={3: 0},
        compiler_params=pltpu.CompilerParams(dimension_semantics=("arbitrary",), vmem_limit_bytes=VMEM_LIMIT),
        name="moe_dispatch",
    )(slot, x1, norm_g.astype(jnp.float32)[None, :], xs0)


def _expert_changed(be_ref, j):
    return jnp.logical_or(j == 0, be_ref[j] != be_ref[jnp.maximum(j - 1, 0)])


def _moe_up_kernel(be_ref, nb_ref, x_ref, wg_ref, wu_ref, bg_ref, bu_ref, o_ref, wg_scr, wu_scr):
    j = pl.program_id(1)

    @pl.when(jnp.logical_and(j < nb_ref[0], _expert_changed(be_ref, j)))
    def _():
        wg_scr[...] = wg_ref[0].astype(jnp.bfloat16)
        wu_scr[...] = wu_ref[0].astype(jnp.bfloat16)

    @pl.when(j < nb_ref[0])
    def _():
        x = x_ref[...].astype(jnp.bfloat16)
        gate = jnp.dot(x, wg_scr[...], preferred_element_type=jnp.float32) + bg_ref[0]
        up = jnp.dot(x, wu_scr[...], preferred_element_type=jnp.float32) + bu_ref[0]
        gate = jnp.minimum(gate, SWIGLU_LIMIT)
        up = jnp.clip(up, -SWIGLU_LIMIT, SWIGLU_LIMIT)
        act = (up + 1.0) * (gate * (1.0 / (1.0 + jnp.exp(-SWIGLU_ALPHA * gate))))
        o_ref[...] = act.astype(o_ref.dtype)


def _moe_down_kernel(be_ref, nb_ref, a_ref, wd_ref, bd_ref, o_ref, wd_scr):
    j = pl.program_id(1)

    @pl.when(jnp.logical_and(j < nb_ref[0], _expert_changed(be_ref, j)))
    def _():
        wd_scr[...] = wd_ref[0].astype(jnp.bfloat16)

    @pl.when(j < nb_ref[0])
    def _():
        o_ref[...] = jnp.dot(a_ref[...], wd_scr[...], preferred_element_type=jnp.float32) + bd_ref[0]


def _moe_experts(xs, block_expert, n_used, w_gate, b_gate, w_up, b_up, w_down, b_down):
    n_rows, d = xs.shape
    f = w_gate.shape[2]
    blk, tn = MOE_BLK, MOE_TN
    n_blocks = n_rows // blk
    bf16 = jnp.bfloat16

    def wspec(k):
        return pl.BlockSpec((1, k, tn), lambda c, j, be, nb: (be[j], 0, c))

    bspec = pl.BlockSpec((1, 1, tn), lambda c, j, be, nb: (be[j], 0, c))
    rows_in = lambda k: pl.BlockSpec((blk, k), lambda c, j, be, nb: (j, 0))
    rows_out = pl.BlockSpec((blk, tn), lambda c, j, be, nb: (j, c))
    params = pltpu.CompilerParams(dimension_semantics=("arbitrary", "arbitrary"), vmem_limit_bytes=VMEM_LIMIT)
    act = pl.pallas_call(
        _moe_up_kernel,
        grid_spec=pltpu.PrefetchScalarGridSpec(
            num_scalar_prefetch=2, grid=(f // tn, n_blocks),
            in_specs=[rows_in(d), wspec(d), wspec(d), bspec, bspec],
            out_specs=rows_out,
            scratch_shapes=[pltpu.VMEM((d, tn), bf16), pltpu.VMEM((d, tn), bf16)]),
        out_shape=jax.ShapeDtypeStruct((n_rows, f), bf16),
        compiler_params=params,
        name="moe_up",
    )(block_expert, n_used, xs, w_gate.astype(jnp.float32), w_up.astype(jnp.float32),
      b_gate.astype(jnp.float32)[:, None, :], b_up.astype(jnp.float32)[:, None, :])
    return pl.pallas_call(
        _moe_down_kernel,
        grid_spec=pltpu.PrefetchScalarGridSpec(
            num_scalar_prefetch=2, grid=(d // tn, n_blocks),
            in_specs=[rows_in(f), wspec(f), bspec],
            out_specs=rows_out,
            scratch_shapes=[pltpu.VMEM((f, tn), bf16)]),
        out_shape=jax.ShapeDtypeStruct((n_rows, d), jnp.float32),
        compiler_params=params,
        name="moe_down",
    )(block_expert, n_used, act, w_down.astype(jnp.float32), b_down.astype(jnp.float32)[:, None, :])


def _combine_kernel(slot_ref, slot_next_ref, x1_ref, meta_ref, g_ref, ys_ref, o_ref, buf, sem):
    i = pl.program_id(0)
    n_steps = pl.num_programs(0)
    tm = x1_ref.shape[0]

    def copy(slots, r, k, half):
        return pltpu.make_async_copy(ys_ref.at[pl.ds(slots[TOP_K * r + k], 1)],
                                     buf.at[half, k, pl.ds(r, 1)], sem.at[half])

    def start_tile(slots, half):
        def body(r, carry):
            for k in range(TOP_K):
                copy(slots, r, k, half).start()
            return carry
        lax.fori_loop(0, tm, body, 0)

    @pl.when(i == 0)
    def _():
        start_tile(slot_ref, 0)

    @pl.when(i + 1 < n_steps)
    def _():
        start_tile(slot_next_ref, (i + 1) % 2)

    half = i % 2

    def wait_row(r, carry):
        for k in range(TOP_K):
            copy(slot_ref, r, k, half).wait()
        return carry

    lax.fori_loop(0, tm, wait_row, 0)
    acc = x1_ref[...]
    for k in range(TOP_K):
        acc = acc + meta_ref[:, TOP_K + k:TOP_K + k + 1] * buf[half, k]
    o_ref[...] = acc * lax.rsqrt(jnp.mean(acc * acc, axis=-1, keepdims=True) + NORM_EPS) * g_ref[...]


def _combine(slot, x1, meta, ys, norm_g):
    n, d = x1.shape
    tm = COMBINE_TM
    n_steps = n // tm
    return pl.pallas_call(
        _combine_kernel,
        grid=(n_steps,),
        in_specs=[pl.BlockSpec((tm * TOP_K,), lambda i: (i,), memory_space=pltpu.SMEM),
                  pl.BlockSpec((tm * TOP_K,), lambda i: (jnp.minimum(i + 1, n_steps - 1),),
                               memory_space=pltpu.SMEM),
                  pl.BlockSpec((tm, d), lambda i: (i, 0)),
                  pl.BlockSpec((tm, LANES), lambda i: (i, 0)),
                  pl.BlockSpec((1, d), lambda i: (0, 0)),
                  pl.BlockSpec(memory_space=pl.ANY)],
        out_specs=pl.BlockSpec((tm, d), lambda i: (i, 0)),
        out_shape=jax.ShapeDtypeStruct((n, d), jnp.float32),
        scratch_shapes=[pltpu.VMEM((2, TOP_K, tm, d), jnp.float32),
                        pltpu.SemaphoreType.DMA((2,))],
        compiler_params=pltpu.CompilerParams(dimension_semantics=("arbitrary",), vmem_limit_bytes=VMEM_LIMIT),
        name="moe_combine",
    )(slot, slot, x1, meta, norm_g.astype(jnp.float32)[None, :], ys)


def moe_final_pallas(x1, norm_ffn_g, router_w, router_b, w_gate, b_gate, w_up, b_up, w_down, b_down,
                     norm_final_g):
    n, d = x1.shape
    meta, cnt = _router(x1, norm_ffn_g, router_w, router_b)
    counts = cnt[0, :N_EXPERTS].astype(jnp.int32)
    padded = (counts + MOE_BLK - 1) // MOE_BLK * MOE_BLK
    pad_end = jnp.cumsum(padded)
    pad_start = pad_end - padded
    expert = meta[:, 0:TOP_K].astype(jnp.int32)
    rank = meta[:, 2 * TOP_K:3 * TOP_K].astype(jnp.int32)
    slot = (pad_start[expert] + rank).reshape(-1)
    n_blocks = n * TOP_K // MOE_BLK + N_EXPERTS
    block_expert = jnp.minimum(jnp.searchsorted(pad_end, jnp.arange(n_blocks, dtype=jnp.int32) * MOE_BLK,
                                                side='right'), N_EXPERTS - 1).astype(jnp.int32)
    n_used = (pad_end[-1:] // MOE_BLK).astype(jnp.int32)
    xs = _dispatch(slot, x1, norm_ffn_g, n_blocks * MOE_BLK)
    ys = _moe_experts(xs, block_expert, n_used, w_gate, b_gate, w_up, b_up, w_down, b_down)
    return _combine(slot, x1, meta, ys, norm_final_g)


def kernel(x, norm_mix_g, w_in, gdn_conv_w, gdn_a_log, gdn_dt_bias, gdn_norm_g, mla_kv_norm_g,
           mla_w_uk, mla_w_uv, idx_k_norm_g, idx_k_norm_b, w_branch_gdn, w_branch_dsa, w_out,
           norm_ffn_g, router_w, router_b, exp_w_gate, exp_b_gate, exp_w_up, exp_b_up,
           exp_w_down, exp_b_down, norm_final_g):
    l = 0
    bsz, t, d = x.shape
    x1 = hybrid_mixer(x, norm_mix_g[l], w_in[l], gdn_conv_w[l], gdn_a_log[l], gdn_dt_bias[l],
                      gdn_norm_g[l], mla_kv_norm_g[l], mla_w_uk[l], mla_w_uv[l], idx_k_norm_g[l],
                      idx_k_norm_b[l], w_branch_gdn[l], w_branch_dsa[l], w_out[l])
    out = moe_final_pallas(x1, norm_ffn_g[l], router_w[l], router_b[l], exp_w_gate[l],
                           exp_b_gate[l], exp_w_up[l], exp_b_up[l], exp_w_down[l], exp_b_down[l], norm_final_g)
    return out.reshape(bsz, t, d)
```

```python
import functools
import math
import numpy as np
import jax
import jax.numpy as jnp
from jax import lax
from jax.experimental import pallas as pl
from jax.experimental.pallas import tpu as pltpu

D_MODEL = 2048
CHUNK = 64
CHUNK_SHIFT = CHUNK.bit_length() - 1
ROPE_THETA = 10000.0
NORM_EPS = 1e-6
GDN_K_HEADS = 16
GDN_V_HEADS = 32
GDN_HEAD_DIM = 128
GDN_QK_WIDTH = GDN_K_HEADS * GDN_HEAD_DIM
GDN_V_WIDTH = GDN_V_HEADS * GDN_HEAD_DIM
MLA_HEADS = 16
MLA_NOPE_DIM = 128
MLA_ROPE_DIM = 64
MLA_V_DIM = 128
MLA_KV_RANK = 512
MLA_Q_WIDTH = MLA_HEADS * (MLA_NOPE_DIM + MLA_ROPE_DIM)
MLA_O_WIDTH = MLA_HEADS * MLA_V_DIM
IDX_HEADS = 16
IDX_DIM = 64
IDX_ROPE_DIM = 32
TOPK_MAX = 256
Q_BLOCK = 128
N_EXPERTS = 32
TOP_K = 4
D_EXPERT = 2048
SWIGLU_LIMIT = 7.0
SWIGLU_ALPHA = 1.702
EXPERT_BLOCK = 256
LANES = 128
VMEM_LIMIT = 56 * 1024 * 1024
IN_SIZES = (GDN_QK_WIDTH, GDN_QK_WIDTH, GDN_V_WIDTH, GDN_V_WIDTH, GDN_V_HEADS, GDN_V_HEADS,
            MLA_Q_WIDTH, MLA_KV_RANK, MLA_ROPE_DIM, IDX_HEADS * IDX_DIM, IDX_DIM, IDX_HEADS,
            D_MODEL, D_MODEL)


def _mm_kernel(a_ref, b_ref, o_ref):
    o_ref[...] = jnp.dot(a_ref[...], b_ref[...], preferred_element_type=jnp.float32)


def _matmul(a, b, tm=512, tn=1024):
    m, k = a.shape
    n = b.shape[1]
    n_pad = -(-n // tn) * tn
    a = a.astype(jnp.bfloat16)
    b = b.astype(jnp.bfloat16)
    if n_pad != n:
        b = jnp.pad(b, ((0, 0), (0, n_pad - n)))
    out = pl.pallas_call(
        _mm_kernel,
        grid=(n_pad // tn, m // tm),
        in_specs=[pl.BlockSpec((tm, k), lambda j, i: (i, 0)),
                  pl.BlockSpec((k, tn), lambda j, i: (0, j))],
        out_specs=pl.BlockSpec((tm, tn), lambda j, i: (i, j)),
        out_shape=jax.ShapeDtypeStruct((m, n_pad), jnp.float32),
        compiler_params=pltpu.CompilerParams(
            dimension_semantics=("arbitrary", "arbitrary"),
            vmem_limit_bytes=VMEM_LIMIT),
        name="dense_matmul",
    )(a, b)
    return out[:, :n] if n_pad != n else out


def rms_norm(x, g):
    xf = x.astype(jnp.float32)
    y = xf * lax.rsqrt(jnp.mean(xf * xf, axis=-1, keepdims=True) + NORM_EPS)
    return (y * g.astype(jnp.float32)).astype(x.dtype)


def rope_tables(n, dim):
    inv = ROPE_THETA ** (-jnp.arange(0, dim, 2, dtype=jnp.float32) / dim)
    ang = jnp.arange(n, dtype=jnp.float32)[:, None] * inv[None, :]
    return jnp.cos(ang), jnp.sin(ang)


PROJ_TN = 1024
_IN_OFF = np.concatenate([[0], np.cumsum(IN_SIZES)]).tolist()
(_O_GQ, _O_GK, _O_GV, _O_GZ, _O_GA, _O_GB, _O_MQ, _O_CKV, _O_KPE, _O_IQ, _O_IK, _O_IW,
 _O_GATE_A, _O_GATE_B) = _IN_OFF[:-1]
P_QKV = 0
P_GZ = 8192
P_QN = 12288
P_GATE_A = 14336
P_GATE_B = 16384
P_QPE = 18432
P_IQ = 19456
P_CKV = 20480
P_KPE_IK = 20992
P_GAW = 21120
P_WIDTH = 21504


def _proj_column_order():
    cols = []
    cols += list(range(_O_GQ, _O_GZ))
    cols += list(range(_O_GZ, _O_GA))
    hd = MLA_NOPE_DIM + MLA_ROPE_DIM
    cols += [_O_MQ + h * hd + i for h in range(MLA_HEADS) for i in range(MLA_NOPE_DIM)]
    cols += list(range(_O_GATE_A, _O_GATE_A + 2 * D_MODEL))
    cols += [_O_MQ + h * hd + MLA_NOPE_DIM + i for h in range(MLA_HEADS) for i in range(MLA_ROPE_DIM)]
    cols += list(range(_O_IQ, _O_IK))
    cols += list(range(_O_CKV, _O_KPE))
    cols += list(range(_O_KPE, _O_IQ)) + list(range(_O_IK, _O_IW))
    cols += list(range(_O_GA, _O_MQ)) + list(range(_O_IW, _O_GATE_A))
    cols += [-1] * (P_WIDTH - len(cols))
    assert len(cols) == P_WIDTH
    return np.asarray(cols, np.int32)


_PROJ_COLS = _proj_column_order()


def _relayout_w_in(w_in):
    d = w_in.shape[0]
    w = w_in.astype(jnp.bfloat16)
    mq = w[:, _O_MQ:_O_CKV].reshape(d, MLA_HEADS, MLA_NOPE_DIM + MLA_ROPE_DIM)
    parts = [w[:, _O_GQ:_O_GA],
             mq[:, :, :MLA_NOPE_DIM].reshape(d, -1),
             w[:, _O_GATE_A:_O_GATE_A + 2 * D_MODEL],
             mq[:, :, MLA_NOPE_DIM:].reshape(d, -1),
             w[:, _O_IQ:_O_IK], w[:, _O_CKV:_O_KPE], w[:, _O_KPE:_O_IQ], w[:, _O_IK:_O_IW],
             w[:, _O_GA:_O_MQ], w[:, _O_IW:_O_GATE_A]]
    used = sum(p.shape[1] for p in parts)
    return jnp.concatenate(parts + [jnp.zeros((d, P_WIDTH - used), jnp.bfloat16)], axis=1)


NEG_BIG = -1e30
INT_MIN = -2 ** 31
PREP_TM = 256
IDX_TQ = 128
ATT_TQ = 512
ATT_TK = 512
QK_PAD = 256
ATT_HEAD_UNROLL = 2
V_PAD = 256


def _rope_tables_lane(t):
    cos_m, sin_m = rope_tables(t, MLA_ROPE_DIM)
    cos_i, sin_i = rope_tables(t, IDX_ROPE_DIM)
    one = jnp.ones((t, IDX_DIM - IDX_ROPE_DIM), jnp.float32)
    zero = jnp.zeros_like(one)
    cq = jnp.concatenate([cos_m, cos_m] * 2, axis=1)
    sq = jnp.concatenate([-sin_m, sin_m] * 2, axis=1)
    ci = jnp.concatenate([cos_i, cos_i, one] * 2, axis=1)
    si = jnp.concatenate([-sin_i, sin_i, zero] * 2, axis=1)
    cs = jnp.concatenate([cos_m, cos_m, cos_i, cos_i, one], axis=1)
    ss = jnp.concatenate([-sin_m, sin_m, -sin_i, sin_i, zero], axis=1)
    return jnp.concatenate([cq, sq, ci, si, cs, ss], axis=1)


def _dsa_prep_kernel(qn_ref, qpe_ref, iq_ref, ckv_ref, sm_ref, gw_ref, tab_ref,
                     kvg_ref, lng_ref, lnb_ref, wuk_ref, wuv_ref,
                     q_out, k_out, v_out, iq_out, ik_out, iw_out):
    f32, bf16 = jnp.float32, jnp.bfloat16
    tm = qn_ref.shape[0]
    lane = lax.broadcasted_iota(jnp.int32, (tm, LANES), 1)
    lo64 = lane < 64

    def swap32(x):
        return jnp.where((lane & 32) == 0, pltpu.roll(x, 96, 1), pltpu.roll(x, 32, 1))

    def swap16(x):
        return jnp.where((lane & 16) == 0, pltpu.roll(x, 112, 1), pltpu.roll(x, 16, 1))

    cq, sq = tab_ref[:, 0:128], tab_ref[:, 128:256]
    ci, si = tab_ref[:, 256:384], tab_ref[:, 384:512]
    cs, ss = tab_ref[:, 512:640], tab_ref[:, 640:768]
    scale = (MLA_NOPE_DIM + MLA_ROPE_DIM) ** -0.5 * math.log2(math.e)

    for j in range(MLA_HEADS // 2):
        t = qpe_ref[:, LANES * j:LANES * (j + 1)]
        r = (t * cq + swap32(t) * sq) * scale
        q_out[0, 2 * j, :, 128:256] = jnp.where(lo64, r, 0.0).astype(bf16)
        q_out[0, 2 * j + 1, :, 128:256] = jnp.where(lo64, pltpu.roll(r, 64, 1), 0.0).astype(bf16)
        t = iq_ref[:, LANES * j:LANES * (j + 1)]
        r = t * ci + swap16(t) * si
        iq_out[0, 2 * j] = jnp.where(lo64, r, 0.0).astype(bf16)
        iq_out[0, 2 * j + 1] = jnp.where(lo64, pltpu.roll(r, 64, 1), 0.0).astype(bf16)
    for h in range(MLA_HEADS):
        q_out[0, h, :, 0:128] = (qn_ref[:, LANES * h:LANES * (h + 1)] * scale).astype(bf16)

    c = ckv_ref[...]
    cn = c * lax.rsqrt(jnp.mean(c * c, axis=-1, keepdims=True) + NORM_EPS) * kvg_ref[...]
    cnb = cn.astype(bf16)
    kn = jnp.dot(cnb, wuk_ref[...], preferred_element_type=f32)
    vv = jnp.dot(cnb, wuv_ref[...], preferred_element_type=f32)

    s = sm_ref[...]
    hi64 = jnp.logical_not(lo64)
    mu = jnp.sum(jnp.where(hi64, s, 0.0), axis=-1, keepdims=True) * (1.0 / IDX_DIM)
    xc = jnp.where(hi64, s - mu, 0.0)
    var = jnp.sum(xc * xc, axis=-1, keepdims=True) * (1.0 / IDX_DIM)
    ln = xc * lax.rsqrt(var + NORM_EPS) * lng_ref[...] + lnb_ref[...]
    y = jnp.where(hi64, ln, s)
    r = y * cs + jnp.where(lo64, swap32(y), swap16(y)) * ss
    kpe_t = jnp.where(lo64, r, 0.0).T.astype(bf16)
    for h in range(MLA_HEADS):
        k_out[0, h, 0:128, :] = kn[:, LANES * h:LANES * (h + 1)].T.astype(bf16)
        k_out[0, h, 128:256, :] = kpe_t
        v_out[0, h, :, 0:128] = vv[:, LANES * h:LANES * (h + 1)].astype(bf16)
        v_out[0, h, :, 128:256] = jnp.ones((tm, LANES), bf16)
    ik_out[0] = jnp.where(lo64, pltpu.roll(r, 64, 1), 0.0).astype(bf16)
    iw_out[...] = gw_ref[...] * (IDX_HEADS ** -0.5 * IDX_DIM ** -0.5)


def _dsa_prep(proj, tab, kv_norm_g, idx_norm_g, idx_norm_b, w_uk, w_uv, bsz, t):
    tm = PREP_TM
    nt = t // tm
    zeros64 = jnp.zeros((64,), jnp.float32)
    lng = jnp.concatenate([zeros64, idx_norm_g.astype(jnp.float32)])[None, :]
    lnb = jnp.concatenate([zeros64, idx_norm_b.astype(jnp.float32)])[None, :]
    kvg = kv_norm_g.astype(jnp.float32)[None, :]

    def rows(w, off):
        return pl.BlockSpec((tm, w), lambda b, i: (b * nt + i, off // w))

    def const(shape):
        return pl.BlockSpec(shape, lambda b, i: (0,) * len(shape))

    def heads(w):
        return pl.BlockSpec((1, MLA_HEADS, tm, w), lambda b, i: (b, 0, i, 0))

    bf16 = jnp.bfloat16
    return pl.pallas_call(
        _dsa_prep_kernel,
        grid=(bsz, nt),
        in_specs=[rows(2048, P_QN), rows(1024, P_QPE), rows(1024, P_IQ), rows(512, P_CKV),
                  rows(LANES, P_KPE_IK), rows(LANES, P_GAW),
                  pl.BlockSpec((tm, 768), lambda b, i: (i, 0)),
                  const((1, MLA_KV_RANK)), const((1, LANES)), const((1, LANES)),
                  const((MLA_KV_RANK, MLA_HEADS * MLA_NOPE_DIM)), const((MLA_KV_RANK, MLA_O_WIDTH))],
        out_specs=[heads(QK_PAD), pl.BlockSpec((1, MLA_HEADS, QK_PAD, tm), lambda b, i: (b, 0, 0, i)),
                   heads(V_PAD), heads(LANES),
                   pl.BlockSpec((1, tm, LANES), lambda b, i: (b, i, 0)),
                   pl.BlockSpec((tm, LANES), lambda b, i: (b * nt + i, 0))],
        out_shape=[jax.ShapeDtypeStruct((bsz, MLA_HEADS, t, QK_PAD), bf16),
                   jax.ShapeDtypeStruct((bsz, MLA_HEADS, QK_PAD, t), bf16),
                   jax.ShapeDtypeStruct((bsz, MLA_HEADS, t, V_PAD), bf16),
                   jax.ShapeDtypeStruct((bsz, IDX_HEADS, t, LANES), bf16),
                   jax.ShapeDtypeStruct((bsz, t, LANES), bf16),
                   jax.ShapeDtypeStruct((bsz * t, LANES), jnp.float32)],
        compiler_params=pltpu.CompilerParams(
            dimension_semantics=("arbitrary", "arbitrary"), vmem_limit_bytes=VMEM_LIMIT),
        name="dsa_prep",
    )(proj, proj, proj, proj, proj, proj, tab, kvg, lng, lnb,
      w_uk.astype(bf16), w_uv.astype(bf16))


def _indexer_kernel(iq_ref, ik_ref, iw_ref, bias_ref, key_scr, tie_scr, *, n_keep):
    f32, i32 = jnp.float32, jnp.int32
    tq = iq_ref.shape[2]
    kc = key_scr.shape[2]
    nch_total = key_scr.shape[0]
    qi = pl.program_id(1)
    nch = ((qi + 1) * tq + kc - 1) // kc
    q_chunk = (lax.broadcasted_iota(i32, (tq, kc), 0) + qi * tq) >> 6
    col0 = lax.broadcasted_iota(i32, (tq, kc), 1)
    n_sub = kc // LANES

    def score_chunk(c, carry):
        ik = ik_ref[0, pl.ds(pl.multiple_of(c * kc, kc), kc), :]
        acc = jnp.zeros((tq, kc), f32)
        for h in range(IDX_HEADS):
            s = lax.dot_general(iq_ref[0, h], ik, (((1,), (1,)), ((), ())), preferred_element_type=f32)
            acc = acc + iw_ref[:, 64 + h:65 + h] * jnp.maximum(s, 0.0)
        bits = pltpu.bitcast(acc, i32)
        key = bits ^ ((bits >> 31) & 0x7FFFFFFF)
        adm = ((col0 + c * kc) >> CHUNK_SHIFT) <= q_chunk
        key_scr[c] = jnp.where(adm, key, INT_MIN)
        return carry

    lax.fori_loop(0, nch, score_chunk, 0)

    def lane_fold(x):
        part = x[:, 0:LANES]
        for j in range(1, n_sub):
            part = part + x[:, LANES * j:LANES * (j + 1)]
        return part

    def rep(x):
        return jnp.concatenate([x] * n_sub, axis=1)

    def count(pred_fn):
        def body(c, acc):
            return acc + lane_fold(jnp.where(pred_fn(key_scr[c], c), 1.0, 0.0))
        acc = lax.fori_loop(0, nch, body, jnp.zeros((tq, LANES), f32))
        return jnp.broadcast_to(jnp.sum(acc, axis=1, keepdims=True), (tq, LANES))

    keep = float(n_keep)
    cnt = count(lambda k, c: k >= 0)
    thr = jnp.where(cnt >= keep, 0, INT_MIN).astype(i32)
    cnt_thr = jnp.where(cnt >= keep, cnt, (nch * kc).astype(f32))

    def bit_cond(state):
        i, _, cnt_thr = state
        return jnp.logical_and(i < 31, jnp.max(jnp.abs(cnt_thr - keep)) > 0.0)

    def bit_pass(state):
        i, thr, cnt_thr = state
        cand = thr | jnp.left_shift(jnp.int32(1), 30 - i)
        cand_w = rep(cand)
        cnt = count(lambda k, c: k >= cand_w)
        take = cnt >= keep
        return i + 1, jnp.where(take, cand, thr), jnp.where(take, cnt, cnt_thr)

    _, thr, cnt_ge = lax.while_loop(bit_cond, bit_pass, (jnp.int32(0), thr, cnt_thr))
    thr_w = rep(thr)
    tie_scr[...] = jnp.full((tq, LANES), 2 ** 30, i32)

    @pl.when(jnp.max(cnt_ge) > keep)
    def _():
        cnt_gt = count(lambda k, c: k > thr_w)
        need = keep - cnt_gt

        def idx_pass(i, x):
            cand = x | jnp.left_shift(jnp.int32(1), idx_bits - 1 - i)
            cand_w = rep(cand)
            cnt = count(lambda k, c: (k == thr_w) & ((col0 + c * kc) < cand_w))
            return jnp.where(cnt < need, cand, x)

        idx_bits = (nch_total * kc - 1).bit_length()
        x = lax.fori_loop(0, idx_bits, idx_pass, jnp.zeros((tq, LANES), i32))
        tie_scr[...] = jnp.where(cnt_ge > keep, x, 2 ** 30)

    tie_w = rep(tie_scr[...])

    def write_chunk(c, carry):
        k = key_scr[c]
        sel = (k > thr_w) | ((k == thr_w) & ((col0 + c * kc) <= tie_w))
        sel = sel & (k != INT_MIN)
        bias_ref[0, c] = jnp.where(sel, 0.0, NEG_BIG)
        return carry

    lax.fori_loop(0, nch, write_chunk, 0)

    def fill_chunk(c, carry):
        bias_ref[0, c] = jnp.full((tq, kc), NEG_BIG, f32)
        return carry

    lax.fori_loop(nch, nch_total, fill_chunk, 0)


def _indexer(iq, ik, iw, bsz, t, n_keep):
    tq, kc = IDX_TQ, ATT_TK
    nq, nkc = t // tq, t // kc
    return pl.pallas_call(
        functools.partial(_indexer_kernel, n_keep=n_keep),
        grid=(bsz, nq),
        in_specs=[pl.BlockSpec((1, IDX_HEADS, tq, LANES), lambda b, i: (b, 0, i, 0)),
                  pl.BlockSpec((1, t, LANES), lambda b, i: (b, 0, 0)),
                  pl.BlockSpec((tq, LANES), lambda b, i: (b * nq + i, 0))],
        out_specs=pl.BlockSpec((1, nkc, tq, kc), lambda b, i: (b, 0, i, 0)),
        out_shape=jax.ShapeDtypeStruct((bsz, nkc, t, kc), jnp.float32),
        scratch_shapes=[pltpu.VMEM((nkc, tq, kc), jnp.int32), pltpu.VMEM((tq, LANES), jnp.int32)],
        compiler_params=pltpu.CompilerParams(
            dimension_semantics=("arbitrary", "arbitrary"), vmem_limit_bytes=VMEM_LIMIT),
        name="dsa_indexer",
    )(iq, ik, iw)


def _attn_kernel(q_ref, k_ref, v_ref, b_ref, o_ref, m_scr, acc_scr):
    f32 = jnp.float32
    tq, tk = b_ref.shape[2], b_ref.shape[3]
    qi, ki = pl.program_id(1), pl.program_id(2)
    last = ((qi + 1) * tq - 1) // tk

    @pl.when(ki == 0)
    def _():
        m_scr[...] = jnp.full(m_scr.shape, NEG_BIG, f32)
        acc_scr[...] = jnp.zeros(acc_scr.shape, f32)

    @pl.when(ki <= last)
    def _():
        bias = b_ref[0, 0]

        def head_group(gi, carry):
            hs = [gi * ATT_HEAD_UNROLL + u for u in range(ATT_HEAD_UNROLL)]
            s = [jnp.dot(q_ref[0, h], k_ref[0, h], preferred_element_type=f32) + bias for h in hs]
            m_prev = [m_scr[h] for h in hs]
            m_new = [jnp.maximum(mp, jnp.max(x, axis=1, keepdims=True)) for mp, x in zip(m_prev, s)]
            p = [jnp.exp2((x - mn[:, 0:1]).astype(jnp.bfloat16)) for x, mn in zip(s, m_new)]
            pv = [jnp.dot(x, v_ref[0, h], preferred_element_type=f32) for x, h in zip(p, hs)]
            for u, h in enumerate(hs):
                alpha = jnp.exp2(m_prev[u] - m_new[u])
                acc_scr[h, :, 0:128] = alpha * acc_scr[h, :, 0:128] + pv[u][:, 0:128]
                acc_scr[h, :, 128:256] = alpha * acc_scr[h, :, 128:256] + pv[u][:, 128:256]
                m_scr[h] = m_new[u]
            return carry

        lax.fori_loop(0, MLA_HEADS // ATT_HEAD_UNROLL, head_group, 0)

    @pl.when(ki == last)
    def _():
        for h in range(MLA_HEADS):
            o_ref[0, :, LANES * h:LANES * (h + 1)] = (
                acc_scr[h, :, 0:128] / acc_scr[h, :, 128:256]).astype(o_ref.dtype)


def _masked_attention(q, k, v, bias, bsz, t):
    tq, tk = ATT_TQ, ATT_TK
    nq, nk = t // tq, t // tk

    def kmap(b, i, j):
        return (b, 0, jnp.minimum(j, ((i + 1) * tq - 1) // tk), 0)

    return pl.pallas_call(
        _attn_kernel,
        grid=(bsz, nq, nk),
        in_specs=[pl.BlockSpec((1, MLA_HEADS, tq, QK_PAD), lambda b, i, j: (b, 0, i, 0)),
                  pl.BlockSpec((1, MLA_HEADS, QK_PAD, tk),
                               lambda b, i, j: (b, 0, 0, jnp.minimum(j, ((i + 1) * tq - 1) // tk))),
                  pl.BlockSpec((1, MLA_HEADS, tk, V_PAD), kmap),
                  pl.BlockSpec((1, 1, tq, tk),
                               lambda b, i, j: (b, jnp.minimum(j, ((i + 1) * tq - 1) // tk), i, 0))],
        out_specs=pl.BlockSpec((1, tq, MLA_O_WIDTH), lambda b, i, j: (b, i, 0)),
        out_shape=jax.ShapeDtypeStruct((bsz, t, MLA_O_WIDTH), jnp.bfloat16),
        scratch_shapes=[pltpu.VMEM((MLA_HEADS, tq, LANES), jnp.float32),
                        pltpu.VMEM((MLA_HEADS, tq, V_PAD), jnp.float32)],
        compiler_params=pltpu.CompilerParams(
            dimension_semantics=("arbitrary", "arbitrary", "arbitrary"),
            vmem_limit_bytes=VMEM_LIMIT),
        name="dsa_attention",
    )(q, k, v, bias)


def dsa_branch_pallas(proj, kv_norm_g, w_uk, w_uv, idx_norm_g, idx_norm_b, bsz, t):
    n_keep = min(TOPK_MAX, t // 4)
    tab = _rope_tables_lane(t)
    q, k, v, iq, ik, iw = _dsa_prep(proj, tab, kv_norm_g, idx_norm_g, idx_norm_b, w_uk, w_uv, bsz, t)
    bias = _indexer(iq, ik, iw, bsz, t, n_keep)
    o = _masked_attention(q, k, v, bias, bsz, t)
    return o.reshape(bsz * t, MLA_O_WIDTH)


GDN_TB = 256
GDN_HG = 4
GDN_REP = GDN_V_HEADS // GDN_K_HEADS


def _gdn_kernel(q_ref, k_ref, v_ref, z_ref, ab_ref, cwq_ref, cwk_ref, cwv_ref, alog_ref, dtb_ref, ng_ref,
                o_ref, tq_scr, tk_scr, tv_scr, state_scr):
    f32, bf16 = jnp.float32, jnp.bfloat16
    tb = q_ref.shape[0]
    g = pl.program_id(1)
    step = pl.program_id(2)
    n_chunk = tb // CHUNK
    kh_n = GDN_HG // GDN_REP

    @pl.when(step == 0)
    def _():
        tq_scr[...] = jnp.zeros(tq_scr.shape, f32)
        tk_scr[...] = jnp.zeros(tk_scr.shape, f32)
        tv_scr[...] = jnp.zeros(tv_scr.shape, f32)
        state_scr[...] = jnp.zeros(state_scr.shape, f32)

    def conv_silu(x_ref, w_ref, buf):
        x = x_ref[...]
        buf[8:8 + tb, :] = x
        w = w_ref[...]
        y = x * w[3:4, :]
        for s in (1, 2, 3):
            y = y + buf[8 - s:8 - s + tb, :] * w[3 - s:4 - s, :]
        buf[0:8, :] = x[tb - 8:tb]
        return y * (1.0 / (1.0 + jnp.exp(-y)))

    qc = conv_silu(q_ref, cwq_ref, tq_scr)
    kc = conv_silu(k_ref, cwk_ref, tk_scr)
    vc = conv_silu(v_ref, cwv_ref, tv_scr)

    def l2n(x):
        return x * lax.rsqrt(jnp.sum(x * x, axis=-1, keepdims=True) + NORM_EPS)

    qn = [l2n(qc[:, LANES * j:LANES * (j + 1)]) * (GDN_HEAD_DIM ** -0.5) for j in range(kh_n)]
    kn = [l2n(kc[:, LANES * j:LANES * (j + 1)]) for j in range(kh_n)]

    ab = pltpu.roll(ab_ref[...], (LANES - GDN_HG * g) % LANES, 1)
    xg = ab + dtb_ref[0]
    softplus = jnp.maximum(xg, 0.0) + jnp.log1p(jnp.exp(-jnp.abs(xg)))
    g_all = -jnp.exp(alog_ref[0]) * softplus
    beta_all = 1.0 / (1.0 + jnp.exp(-ab))
    rin = lax.broadcasted_iota(jnp.int32, (tb, LANES), 0) & (CHUNK - 1)
    cum = g_all
    for s in (1, 2, 4, 8, 16, 32):
        cum = cum + jnp.where(rin >= s, pltpu.roll(cum, s, 0), 0.0)

    ii = lax.broadcasted_iota(jnp.int32, (tb, tb), 0)
    jj = lax.broadcasted_iota(jnp.int32, (tb, tb), 1)
    same = (ii >> CHUNK_SHIFT) == (jj >> CHUNK_SHIFT)
    incl = same & (ii >= jj)
    strict = same & (ii > jj)
    eye = jnp.where(ii == jj, 1.0, 0.0)
    nt_dims = (((1,), (1,)), ((), ()))
    tn_dims = (((0,), (0,)), ((), ()))
    ng = ng_ref[...]
    heads = range(GDN_HG)

    cum_t = cum.T
    kb = [x.astype(bf16) for x in kn]
    qb = [x.astype(bf16) for x in qn]
    kk = [lax.dot_general(kb[j], kb[j], nt_dims, preferred_element_type=f32) for j in range(kh_n)]
    qk = [lax.dot_general(qb[j], kb[j], nt_dims, preferred_element_type=f32) for j in range(kh_n)]
    gcol = [cum[:, hh:hh + 1] for hh in heads]
    bcol = [beta_all[:, 32 + hh:33 + hh] for hh in heads]
    dec = [jnp.where(incl, jnp.exp(jnp.where(incl, gcol[hh] - cum_t[hh:hh + 1, :], 0.0)), 0.0) for hh in heads]
    pw = [jnp.where(strict, -(bcol[hh] * kk[hh // GDN_REP] * dec[hh]), 0.0) for hh in heads]
    tmat = [eye + pw[hh] for hh in heads]
    pw = [jnp.dot(pw[hh].astype(bf16), pw[hh].astype(bf16), preferred_element_type=f32) for hh in heads]
    for _ in range(4):
        both = [jnp.dot(jnp.concatenate([tmat[hh], pw[hh]], axis=0).astype(bf16), pw[hh].astype(bf16),
                        preferred_element_type=f32) for hh in heads]
        tmat = [tmat[hh] + both[hh][0:tb] for hh in heads]
        pw = [both[hh][tb:2 * tb] for hh in heads]
    tmat = [tmat[hh] + jnp.dot(tmat[hh].astype(bf16), pw[hh].astype(bf16), preferred_element_type=f32)
            for hh in heads]
    eg = [jnp.exp(gcol[hh]) for hh in heads]
    uw = [jnp.dot(tmat[hh].astype(bf16),
                  jnp.concatenate([bcol[hh] * vc[:, LANES * hh:LANES * (hh + 1)],
                                   (bcol[hh] * eg[hh]) * kn[hh // GDN_REP]], axis=1).astype(bf16),
                  preferred_element_type=f32) for hh in heads]
    qkd = [qk[hh // GDN_REP] * dec[hh] for hh in heads]
    qd = [(qn[hh // GDN_REP] * eg[hh]).astype(bf16) for hh in heads]
    states = [state_scr[hh] for hh in heads]

    for c in range(n_chunk):
        r0 = c * CHUNK
        sb = [states[hh].astype(bf16) for hh in heads]
        ws = [jnp.dot(jnp.concatenate([uw[hh][r0:r0 + CHUNK, LANES:2 * LANES].astype(bf16),
                                       qd[hh][r0:r0 + CHUNK]], axis=0), sb[hh],
                      preferred_element_type=f32) for hh in heads]
        vb = [(uw[hh][r0:r0 + CHUNK, 0:LANES] - ws[hh][0:CHUNK]).astype(bf16) for hh in heads]
        g_last = [cum[r0 + CHUNK - 1:r0 + CHUNK, hh:hh + 1] for hh in heads]
        kd = [(kn[hh // GDN_REP][r0:r0 + CHUNK] * jnp.exp(g_last[hh] - gcol[hh][r0:r0 + CHUNK])).astype(bf16)
              for hh in heads]
        o = [ws[hh][CHUNK:2 * CHUNK] + jnp.dot(qkd[hh][r0:r0 + CHUNK, r0:r0 + CHUNK].astype(bf16), vb[hh],
                                               preferred_element_type=f32) for hh in heads]
        states = [states[hh] * jnp.exp(g_last[hh])
                  + lax.dot_general(kd[hh], vb[hh], tn_dims, preferred_element_type=f32) for hh in heads]
        for hh in heads:
            on = o[hh] * lax.rsqrt(jnp.mean(o[hh] * o[hh], axis=-1, keepdims=True) + NORM_EPS) * ng
            zz = z_ref[r0:r0 + CHUNK, LANES * hh:LANES * (hh + 1)]
            o_ref[r0:r0 + CHUNK, LANES * hh:LANES * (hh + 1)] = (
                on * (zz * (1.0 / (1.0 + jnp.exp(-zz))))).astype(o_ref.dtype)

    for hh in range(GDN_HG):
        state_scr[hh] = states[hh]


def _gdn_branch_pallas(proj, conv_w, a_log, dt_bias, norm_g, bsz, t):
    tb = GDN_TB
    nt = t // tb
    n_groups = GDN_V_HEADS // GDN_HG
    qw = GDN_HG // GDN_REP * GDN_HEAD_DIM
    vw = GDN_HG * GDN_HEAD_DIM
    f32 = jnp.float32

    def per_group(x):
        x = x.astype(f32).reshape(n_groups, 1, GDN_HG)
        return jnp.pad(x, ((0, 0), (0, 0), (0, LANES - GDN_HG)))

    def rows(w, off):
        return pl.BlockSpec((tb, w), lambda b, g, s: (b * nt + s, off // w + g))

    def convw(w, off):
        return pl.BlockSpec((4, w), lambda b, g, s: (0, off // w + g))

    grp = pl.BlockSpec((1, 1, LANES), lambda b, g, s: (g, 0, 0))
    conv_w = conv_w.astype(f32)
    return pl.pallas_call(
        _gdn_kernel,
        grid=(bsz, n_groups, nt),
        in_specs=[rows(qw, P_QKV), rows(qw, P_QKV + GDN_QK_WIDTH), rows(vw, P_QKV + 2 * GDN_QK_WIDTH),
                  rows(vw, P_GZ),
                  pl.BlockSpec((tb, LANES), lambda b, g, s: (b * nt + s, P_GAW // LANES)),
                  convw(qw, 0), convw(qw, GDN_QK_WIDTH), convw(vw, 2 * GDN_QK_WIDTH),
                  grp, grp,
                  pl.BlockSpec((1, LANES), lambda b, g, s: (0, 0))],
        out_specs=pl.BlockSpec((tb, vw), lambda b, g, s: (b * nt + s, g)),
        out_shape=jax.ShapeDtypeStruct((bsz * t, GDN_V_WIDTH), jnp.bfloat16),
        scratch_shapes=[pltpu.VMEM((tb + 8, qw), f32), pltpu.VMEM((tb + 8, qw), f32), pltpu.VMEM((tb + 8, vw), f32),
                        pltpu.VMEM((GDN_HG, GDN_HEAD_DIM, GDN_HEAD_DIM), f32)],
        compiler_params=pltpu.CompilerParams(
            dimension_semantics=("arbitrary", "arbitrary", "arbitrary"),
            vmem_limit_bytes=VMEM_LIMIT),
        name="gdn_branch",
    )(proj, proj, proj, proj, proj, conv_w, conv_w, conv_w,
      per_group(a_log), per_group(dt_bias), norm_g.astype(f32)[None, :])


MERGE_TM = 512
MERGE_TN = 512


def _merge_kernel(oa_ref, ob_ref, wa_ref, wb_ref, ga_ref, gb_ref, o_ref):
    f32 = jnp.float32
    ua = jnp.dot(oa_ref[...], wa_ref[...], preferred_element_type=f32)
    ub = jnp.dot(ob_ref[...], wb_ref[...], preferred_element_type=f32)
    sa = 1.0 / (1.0 + jnp.exp(-ga_ref[...]))
    sb = 1.0 / (1.0 + jnp.exp(-gb_ref[...]))
    o_ref[...] = (sa * ua + sb * ub).astype(o_ref.dtype)


def _outproj_kernel(m_ref, w_ref, x_ref, o_ref):
    o_ref[...] = x_ref[...] + jnp.dot(m_ref[...], w_ref[...], preferred_element_type=jnp.float32)


def _merge_and_project(x, proj, o_a, o_b, w_branch_gdn, w_branch_dsa, w_out):
    n, d = x.shape
    tm, tn = MERGE_TM, MERGE_TN
    bf16 = jnp.bfloat16
    params = pltpu.CompilerParams(dimension_semantics=("arbitrary", "arbitrary"), vmem_limit_bytes=VMEM_LIMIT)
    merged = pl.pallas_call(
        _merge_kernel,
        grid=(d // tn, n // tm),
        in_specs=[pl.BlockSpec((tm, o_a.shape[1]), lambda j, i: (i, 0)),
                  pl.BlockSpec((tm, o_b.shape[1]), lambda j, i: (i, 0)),
                  pl.BlockSpec((o_a.shape[1], tn), lambda j, i: (0, j)),
                  pl.BlockSpec((o_b.shape[1], tn), lambda j, i: (0, j)),
                  pl.BlockSpec((tm, tn), lambda j, i: (i, P_GATE_A // tn + j)),
                  pl.BlockSpec((tm, tn), lambda j, i: (i, P_GATE_B // tn + j))],
        out_specs=pl.BlockSpec((tm, tn), lambda j, i: (i, j)),
        out_shape=jax.ShapeDtypeStruct((n, d), bf16),
        compiler_params=params,
        name="branch_merge",
    )(o_a, o_b, w_branch_gdn.astype(bf16), w_branch_dsa.astype(bf16), proj, proj)
    return pl.pallas_call(
        _outproj_kernel,
        grid=(n // tm,),
        in_specs=[pl.BlockSpec((tm, d), lambda i: (i, 0)),
                  pl.BlockSpec((d, d), lambda i: (0, 0)),
                  pl.BlockSpec((tm, d), lambda i: (i, 0))],
        out_specs=pl.BlockSpec((tm, d), lambda i: (i, 0)),
        out_shape=jax.ShapeDtypeStruct((n, d), jnp.float32),
        compiler_params=pltpu.CompilerParams(dimension_semantics=("arbitrary",), vmem_limit_bytes=VMEM_LIMIT),
        name="out_projection",
    )(merged, w_out.astype(bf16), x)


def hybrid_mixer(x, norm_g, w_in, conv_w, a_log, dt_bias, gdn_norm_g, kv_norm_g, w_uk, w_uv,
                 idx_norm_g, idx_norm_b, w_branch_gdn, w_branch_dsa, w_out):
    bsz, t, d = x.shape
    n = bsz * t
    h = rms_norm(x, norm_g).reshape(n, d)
    proj = _matmul(h, _relayout_w_in(w_in), tn=PROJ_TN)
    o_a = _gdn_branch_pallas(proj, conv_w, a_log, dt_bias, gdn_norm_g, bsz, t)
    o_b = dsa_branch_pallas(proj, kv_norm_g, w_uk, w_uv, idx_norm_g, idx_norm_b, bsz, t)
    return _merge_and_project(x.reshape(n, d), proj, o_a, o_b, w_branch_gdn, w_branch_dsa, w_out)


ROUTER_TM = 512
MOE_BLK = 256
MOE_TN = 1024
COMBINE_TM = 128
DISPATCH_TM = 256


def _router_kernel(x_ref, g_ref, rw_ref, rb_ref, meta_ref, cnt_ref, carry_scr):
    f32 = jnp.float32
    tm = x_ref.shape[0]

    @pl.when(pl.program_id(0) == 0)
    def _():
        carry_scr[...] = jnp.zeros(carry_scr.shape, f32)

    x = x_ref[...]
    h = (x * lax.rsqrt(jnp.mean(x * x, axis=-1, keepdims=True) + NORM_EPS) * g_ref[...]).astype(jnp.bfloat16)
    logits = jnp.dot(h, rw_ref[...], preferred_element_type=f32) + rb_ref[...]
    lane = lax.broadcasted_iota(jnp.int32, (tm, LANES), 1).astype(f32)
    vals = logits
    hot_sum = jnp.zeros((tm, LANES), f32)
    tops, ids, hots = [], [], []
    for _ in range(TOP_K):
        m = jnp.max(vals, axis=1, keepdims=True)
        idx = jnp.min(jnp.where(vals == m, lane, float(LANES)), axis=1, keepdims=True)
        hot = lane == idx
        tops.append(m)
        ids.append(idx)
        hots.append(hot)
        vals = jnp.where(hot, -jnp.inf, vals)
        hot_sum = hot_sum + jnp.where(hot, 1.0, 0.0)
    exps = [jnp.exp(v - tops[0]) for v in tops]
    inv = 1.0 / (exps[0] + exps[1] + exps[2] + exps[3])
    ri = lax.broadcasted_iota(jnp.int32, (tm, tm), 0)
    ci = lax.broadcasted_iota(jnp.int32, (tm, tm), 1)
    lower = jnp.where(ri > ci, 1.0, 0.0).astype(jnp.bfloat16)
    base = carry_scr[...] + jnp.dot(lower, hot_sum.astype(jnp.bfloat16), preferred_element_type=f32)
    meta = jnp.zeros((tm, LANES), f32)
    for k in range(TOP_K):
        rank = jnp.sum(jnp.where(hots[k], base, 0.0), axis=1, keepdims=True)
        meta = jnp.where(lane == float(k), ids[k], meta)
        meta = jnp.where(lane == float(TOP_K + k), exps[k] * inv, meta)
        meta = jnp.where(lane == float(2 * TOP_K + k), rank, meta)
    meta_ref[...] = meta
    carry_scr[...] = carry_scr[...] + jnp.sum(hot_sum, axis=0, keepdims=True)
    cnt_ref[...] = carry_scr[...]


def _router(x1, norm_g, router_w, router_b):
    n, d = x1.shape
    tm = ROUTER_TM
    rw = jnp.pad(router_w.astype(jnp.bfloat16), ((0, 0), (0, LANES - N_EXPERTS)))
    rb = jnp.concatenate([router_b.astype(jnp.float32),
                          jnp.full((LANES - N_EXPERTS,), -jnp.inf, jnp.float32)])[None, :]
    return pl.pallas_call(
        _router_kernel,
        grid=(n // tm,),
        in_specs=[pl.BlockSpec((tm, d), lambda i: (i, 0)),
                  pl.BlockSpec((1, d), lambda i: (0, 0)),
                  pl.BlockSpec((d, LANES), lambda i: (0, 0)),
                  pl.BlockSpec((1, LANES), lambda i: (0, 0))],
        out_specs=[pl.BlockSpec((tm, LANES), lambda i: (i, 0)),
                   pl.BlockSpec((1, LANES), lambda i: (0, 0))],
        out_shape=[jax.ShapeDtypeStruct((n, LANES), jnp.float32),
                   jax.ShapeDtypeStruct((1, LANES), jnp.float32)],
        scratch_shapes=[pltpu.VMEM((1, LANES), jnp.float32)],
        compiler_params=pltpu.CompilerParams(dimension_semantics=("arbitrary",), vmem_limit_bytes=VMEM_LIMIT),
        name="moe_router",
    )(x1, norm_g.astype(jnp.float32)[None, :], rw, rb)


def _dispatch_kernel(slot_ref, x_ref, g_ref, xs_in_ref, xs_ref, h_scr, sem):
    del xs_in_ref
    tm = x_ref.shape[0]
    x = x_ref[...]
    h_scr[...] = x * lax.rsqrt(jnp.mean(x * x, axis=-1, keepdims=True) + NORM_EPS) * g_ref[...]

    def copy(r, k):
        return pltpu.make_async_copy(h_scr.at[pl.ds(r, 1)], xs_ref.at[pl.ds(slot_ref[TOP_K * r + k], 1)], sem)

    def start_row(r, carry):
        for k in range(TOP_K):
            copy(r, k).start()
        return carry

    def wait_row(r, carry):
        for k in range(TOP_K):
            copy(r, k).wait()
        return carry

    lax.fori_loop(0, tm, start_row, 0)
    lax.fori_loop(0, tm, wait_row, 0)


def _dispatch(slot, x1, norm_g, n_rows):
    n, d = x1.shape
    tm = DISPATCH_TM
    xs0 = jnp.zeros((n_rows, d), jnp.float32)
    return pl.pallas_call(
        _dispatch_kernel,
        grid=(n // tm,),
        in_specs=[pl.BlockSpec((tm * TOP_K,), lambda i: (i,), memory_space=pltpu.SMEM),
                  pl.BlockSpec((tm, d), lambda i: (i, 0)),
                  pl.BlockSpec((1, d), lambda i: (0, 0)),
                  pl.BlockSpec(memory_space=pl.ANY)],
        out_specs=pl.BlockSpec(memory_space=pl.ANY),
        out_shape=jax.ShapeDtypeStruct((n_rows, d), jnp.float32),
        scratch_shapes=[pltpu.VMEM((tm, d), jnp.float32), pltpu.SemaphoreType.DMA(())],
        input_output_aliases={3: 0},
        compiler_params=pltpu.CompilerParams(dimension_semantics=("arbitrary",), vmem_limit_bytes=VMEM_LIMIT),
        name="moe_dispatch",
    )(slot, x1, norm_g.astype(jnp.float32)[None, :], xs0)


def _moe_up_kernel(be_ref, nb_ref, x_ref, wg_ref, wu_ref, bg_ref, bu_ref, o_ref):
    del be_ref
    j = pl.program_id(1)

    @pl.when(j < nb_ref[0])
    def _():
        x = x_ref[...].astype(jnp.bfloat16)
        gate = jnp.dot(x, wg_ref[0], preferred_element_type=jnp.float32) + bg_ref[0]
        up = jnp.dot(x, wu_ref[0], preferred_element_type=jnp.float32) + bu_ref[0]
        gate = jnp.minimum(gate, SWIGLU_LIMIT)
        up = jnp.clip(up, -SWIGLU_LIMIT, SWIGLU_LIMIT)
        act = (up + 1.0) * (gate * (1.0 / (1.0 + jnp.exp(-SWIGLU_ALPHA * gate))))
        o_ref[...] = act.astype(o_ref.dtype)


def _moe_down_kernel(be_ref, nb_ref, a_ref, wd_ref, bd_ref, o_ref):
    del be_ref
    j = pl.program_id(1)

    @pl.when(j < nb_ref[0])
    def _():
        o_ref[...] = jnp.dot(a_ref[...], wd_ref[0], preferred_element_type=jnp.float32) + bd_ref[0]


def _moe_experts(xs, block_expert, n_used, w_gate, b_gate, w_up, b_up, w_down, b_down):
    n_rows, d = xs.shape
    f = w_gate.shape[2]
    blk, tn = MOE_BLK, MOE_TN
    n_blocks = n_rows // blk
    bf16 = jnp.bfloat16

    def wspec(k):
        return pl.BlockSpec((1, k, tn), lambda c, j, be, nb: (be[j], 0, c))

    bspec = pl.BlockSpec((1, 1, tn), lambda c, j, be, nb: (be[j], 0, c))
    rows_in = lambda k: pl.BlockSpec((blk, k), lambda c, j, be, nb: (j, 0))
    rows_out = pl.BlockSpec((blk, tn), lambda c, j, be, nb: (j, c))
    params = pltpu.CompilerParams(dimension_semantics=("arbitrary", "arbitrary"), vmem_limit_bytes=VMEM_LIMIT)
    act = pl.pallas_call(
        _moe_up_kernel,
        grid_spec=pltpu.PrefetchScalarGridSpec(
            num_scalar_prefetch=2, grid=(f // tn, n_blocks),
            in_specs=[rows_in(d), wspec(d), wspec(d), bspec, bspec],
            out_specs=rows_out),
        out_shape=jax.ShapeDtypeStruct((n_rows, f), bf16),
        compiler_params=params,
        name="moe_up",
    )(block_expert, n_used, xs, w_gate.astype(bf16), w_up.astype(bf16),
      b_gate.astype(jnp.float32)[:, None, :], b_up.astype(jnp.float32)[:, None, :])
    return pl.pallas_call(
        _moe_down_kernel,
        grid_spec=pltpu.PrefetchScalarGridSpec(
            num_scalar_prefetch=2, grid=(d // tn, n_blocks),
            in_specs=[rows_in(f), wspec(f), bspec],
            out_specs=rows_out),
        out_shape=jax.ShapeDtypeStruct((n_rows, d), jnp.float32),
        compiler_params=params,
        name="moe_down",
    )(block_expert, n_used, act, w_down.astype(bf16), b_down.astype(jnp.float32)[:, None, :])


def _combine_kernel(slot_ref, slot_next_ref, x1_ref, meta_ref, g_ref, ys_ref, o_ref, buf, sem):
    i = pl.program_id(0)
    n_steps = pl.num_programs(0)
    tm = x1_ref.shape[0]

    def copy(slots, r, k, half):
        return pltpu.make_async_copy(ys_ref.at[pl.ds(slots[TOP_K * r + k], 1)],
                                     buf.at[half, k, pl.ds(r, 1)], sem.at[half])

    def start_tile(slots, half):
        def body(r, carry):
            for k in range(TOP_K):
                copy(slots, r, k, half).start()
            return carry
        lax.fori_loop(0, tm, body, 0)

    @pl.when(i == 0)
    def _():
        start_tile(slot_ref, 0)

    @pl.when(i + 1 < n_steps)
    def _():
        start_tile(slot_next_ref, (i + 1) % 2)

    half = i % 2

    def wait_row(r, carry):
        for k in range(TOP_K):
            copy(slot_ref, r, k, half).wait()
        return carry

    lax.fori_loop(0, tm, wait_row, 0)
    acc = x1_ref[...]
    for k in range(TOP_K):
        acc = acc + meta_ref[:, TOP_K + k:TOP_K + k + 1] * buf[half, k]
    o_ref[...] = acc * lax.rsqrt(jnp.mean(acc * acc, axis=-1, keepdims=True) + NORM_EPS) * g_ref[...]


def _combine(slot, x1, meta, ys, norm_g):
    n, d = x1.shape
    tm = COMBINE_TM
    n_steps = n // tm
    return pl.pallas_call(
        _combine_kernel,
        grid=(n_steps,),
        in_specs=[pl.BlockSpec((tm * TOP_K,), lambda i: (i,), memory_space=pltpu.SMEM),
                  pl.BlockSpec((tm * TOP_K,), lambda i: (jnp.minimum(i + 1, n_steps - 1),),
                               memory_space=pltpu.SMEM),
                  pl.BlockSpec((tm, d), lambda i: (i, 0)),
                  pl.BlockSpec((tm, LANES), lambda i: (i, 0)),
                  pl.BlockSpec((1, d), lambda i: (0, 0)),
                  pl.BlockSpec(memory_space=pl.ANY)],
        out_specs=pl.BlockSpec((tm, d), lambda i: (i, 0)),
        out_shape=jax.ShapeDtypeStruct((n, d), jnp.float32),
        scratch_shapes=[pltpu.VMEM((2, TOP_K, tm, d), jnp.float32),
                        pltpu.SemaphoreType.DMA((2,))],
        compiler_params=pltpu.CompilerParams(dimension_semantics=("arbitrary",), vmem_limit_bytes=VMEM_LIMIT),
        name="moe_combine",
    )(slot, slot, x1, meta, norm_g.astype(jnp.float32)[None, :], ys)


def moe_final_pallas(x1, norm_ffn_g, router_w, router_b, w_gate, b_gate, w_up, b_up, w_down, b_down,
                     norm_final_g):
    n, d = x1.shape
    meta, cnt = _router(x1, norm_ffn_g, router_w, router_b)
    counts = cnt[0, :N_EXPERTS].astype(jnp.int32)
    padded = (counts + MOE_BLK - 1) // MOE_BLK * MOE_BLK
    pad_end = jnp.cumsum(padded)
    pad_start = pad_end - padded
    expert = meta[:, 0:TOP_K].astype(jnp.int32)
    rank = meta[:, 2 * TOP_K:3 * TOP_K].astype(jnp.int32)
    seg_start = jnp.sum(jnp.where(expert[..., None] == jnp.arange(N_EXPERTS, dtype=jnp.int32), pad_start, 0),
                        axis=-1)
    slot = (seg_start + rank).reshape(-1)
    n_blocks = n * TOP_K // MOE_BLK + N_EXPERTS
    block_start = jnp.arange(n_blocks, dtype=jnp.int32) * MOE_BLK
    block_expert = jnp.minimum(jnp.sum((pad_end[None, :] <= block_start[:, None]).astype(jnp.int32), axis=1),
                               N_EXPERTS - 1)
    n_used = (pad_end[-1:] // MOE_BLK).astype(jnp.int32)
    xs = _dispatch(slot, x1, norm_ffn_g, n_blocks * MOE_BLK)
    ys = _moe_experts(xs, block_expert, n_used, w_gate, b_gate, w_up, b_up, w_down, b_down)
    return _combine(slot, x1, meta, ys, norm_final_g)


def kernel(x, norm_mix_g, w_in, gdn_conv_w, gdn_a_log, gdn_dt_bias, gdn_norm_g, mla_kv_norm_g,
           mla_w_uk, mla_w_uv, idx_k_norm_g, idx_k_norm_b, w_branch_gdn, w_branch_dsa, w_out,
           norm_ffn_g, router_w, router_b, exp_w_gate, exp_b_gate, exp_w_up, exp_b_up,
           exp_w_down, exp_b_down, norm_final_g):
    l = 0
    bsz, t, d = x.shape
    x1 = hybrid_mixer(x, norm_mix_g[l], w_in[l], gdn_conv_w[l], gdn_a_log[l], gdn_dt_bias[l],
                      gdn_norm_g[l], mla_kv_norm_g[l], mla_w_uk[l], mla_w_uv[l], idx_k_norm_g[l],
                      idx_k_norm_b[l], w_branch_gdn[l], w_branch_dsa[l], w_out[l])
    out = moe_final_pallas(x1, norm_ffn_g[l], router_w[l], router_b[l], exp_w_gate[l],
                           exp_b_gate[l], exp_w_up[l], exp_b_up[l], exp_w_down[l], exp_b_down[l], norm_final_g)
    return out.reshape(bsz, t, d)
```

```python
import functools
import math
import numpy as np
import jax
import jax.numpy as jnp
from jax import lax
from jax.experimental import pallas as pl
from jax.experimental.pallas import tpu as pltpu

D_MODEL = 2048
CHUNK = 64
CHUNK_SHIFT = CHUNK.bit_length() - 1
ROPE_THETA = 10000.0
NORM_EPS = 1e-6
GDN_K_HEADS = 16
GDN_V_HEADS = 32
GDN_HEAD_DIM = 128
GDN_QK_WIDTH = GDN_K_HEADS * GDN_HEAD_DIM
GDN_V_WIDTH = GDN_V_HEADS * GDN_HEAD_DIM
MLA_HEADS = 16
MLA_NOPE_DIM = 128
MLA_ROPE_DIM = 64
MLA_V_DIM = 128
MLA_KV_RANK = 512
MLA_Q_WIDTH = MLA_HEADS * (MLA_NOPE_DIM + MLA_ROPE_DIM)
MLA_O_WIDTH = MLA_HEADS * MLA_V_DIM
IDX_HEADS = 16
IDX_DIM = 64
IDX_ROPE_DIM = 32
TOPK_MAX = 256
Q_BLOCK = 128
N_EXPERTS = 32
TOP_K = 4
D_EXPERT = 2048
SWIGLU_LIMIT = 7.0
SWIGLU_ALPHA = 1.702
EXPERT_BLOCK = 256
LANES = 128
VMEM_LIMIT = 56 * 1024 * 1024
IN_SIZES = (GDN_QK_WIDTH, GDN_QK_WIDTH, GDN_V_WIDTH, GDN_V_WIDTH, GDN_V_HEADS, GDN_V_HEADS,
            MLA_Q_WIDTH, MLA_KV_RANK, MLA_ROPE_DIM, IDX_HEADS * IDX_DIM, IDX_DIM, IDX_HEADS,
            D_MODEL, D_MODEL)


def _mm_kernel(a_ref, b_ref, o_ref):
    o_ref[...] = jnp.dot(a_ref[...], b_ref[...], preferred_element_type=jnp.float32)


def _matmul(a, b, tm=512, tn=1024):
    m, k = a.shape
    n = b.shape[1]
    n_pad = -(-n // tn) * tn
    a = a.astype(jnp.bfloat16)
    b = b.astype(jnp.bfloat16)
    if n_pad != n:
        b = jnp.pad(b, ((0, 0), (0, n_pad - n)))
    out = pl.pallas_call(
        _mm_kernel,
        grid=(n_pad // tn, m // tm),
        in_specs=[pl.BlockSpec((tm, k), lambda j, i: (i, 0)),
                  pl.BlockSpec((k, tn), lambda j, i: (0, j))],
        out_specs=pl.BlockSpec((tm, tn), lambda j, i: (i, j)),
        out_shape=jax.ShapeDtypeStruct((m, n_pad), jnp.float32),
        compiler_params=pltpu.CompilerParams(
            dimension_semantics=("arbitrary", "arbitrary"),
            vmem_limit_bytes=VMEM_LIMIT),
        name="dense_matmul",
    )(a, b)
    return out[:, :n] if n_pad != n else out


def rms_norm(x, g):
    xf = x.astype(jnp.float32)
    y = xf * lax.rsqrt(jnp.mean(xf * xf, axis=-1, keepdims=True) + NORM_EPS)
    return (y * g.astype(jnp.float32)).astype(x.dtype)


def rope_tables(n, dim):
    inv = ROPE_THETA ** (-jnp.arange(0, dim, 2, dtype=jnp.float32) / dim)
    ang = jnp.arange(n, dtype=jnp.float32)[:, None] * inv[None, :]
    return jnp.cos(ang), jnp.sin(ang)


PROJ_TN = 1024
_IN_OFF = np.concatenate([[0], np.cumsum(IN_SIZES)]).tolist()
(_O_GQ, _O_GK, _O_GV, _O_GZ, _O_GA, _O_GB, _O_MQ, _O_CKV, _O_KPE, _O_IQ, _O_IK, _O_IW,
 _O_GATE_A, _O_GATE_B) = _IN_OFF[:-1]
P_QKV = 0
P_GZ = 8192
P_QN = 12288
P_GATE_A = 14336
P_GATE_B = 16384
P_QPE = 18432
P_IQ = 19456
P_CKV = 20480
P_KPE_IK = 20992
P_GAW = 21120
P_WIDTH = 21504


def _proj_column_order():
    cols = []
    cols += list(range(_O_GQ, _O_GZ))
    cols += list(range(_O_GZ, _O_GA))
    hd = MLA_NOPE_DIM + MLA_ROPE_DIM
    cols += [_O_MQ + h * hd + i for h in range(MLA_HEADS) for i in range(MLA_NOPE_DIM)]
    cols += list(range(_O_GATE_A, _O_GATE_A + 2 * D_MODEL))
    cols += [_O_MQ + h * hd + MLA_NOPE_DIM + i for h in range(MLA_HEADS) for i in range(MLA_ROPE_DIM)]
    cols += list(range(_O_IQ, _O_IK))
    cols += list(range(_O_CKV, _O_KPE))
    cols += list(range(_O_KPE, _O_IQ)) + list(range(_O_IK, _O_IW))
    cols += list(range(_O_GA, _O_MQ)) + list(range(_O_IW, _O_GATE_A))
    cols += [-1] * (P_WIDTH - len(cols))
    assert len(cols) == P_WIDTH
    return np.asarray(cols, np.int32)


_PROJ_COLS = _proj_column_order()


def _relayout_w_in(w_in):
    d = w_in.shape[0]
    w = w_in.astype(jnp.bfloat16)
    mq = w[:, _O_MQ:_O_CKV].reshape(d, MLA_HEADS, MLA_NOPE_DIM + MLA_ROPE_DIM)
    parts = [w[:, _O_GQ:_O_GA],
             mq[:, :, :MLA_NOPE_DIM].reshape(d, -1),
             w[:, _O_GATE_A:_O_GATE_A + 2 * D_MODEL],
             mq[:, :, MLA_NOPE_DIM:].reshape(d, -1),
             w[:, _O_IQ:_O_IK], w[:, _O_CKV:_O_KPE], w[:, _O_KPE:_O_IQ], w[:, _O_IK:_O_IW],
             w[:, _O_GA:_O_MQ], w[:, _O_IW:_O_GATE_A]]
    used = sum(p.shape[1] for p in parts)
    return jnp.concatenate(parts + [jnp.zeros((d, P_WIDTH - used), jnp.bfloat16)], axis=1)


NEG_BIG = -1e30
INT_MIN = -2 ** 31
PREP_TM = 256
IDX_TQ = 256
ATT_TQ = 512
ATT_TK = 512
QK_PAD = 256
ATT_HEAD_UNROLL = 2
V_PAD = 256


def _rope_tables_lane(t):
    cos_m, sin_m = rope_tables(t, MLA_ROPE_DIM)
    cos_i, sin_i = rope_tables(t, IDX_ROPE_DIM)
    one = jnp.ones((t, IDX_DIM - IDX_ROPE_DIM), jnp.float32)
    zero = jnp.zeros_like(one)
    cq = jnp.concatenate([cos_m, cos_m] * 2, axis=1)
    sq = jnp.concatenate([-sin_m, sin_m] * 2, axis=1)
    ci = jnp.concatenate([cos_i, cos_i, one] * 2, axis=1)
    si = jnp.concatenate([-sin_i, sin_i, zero] * 2, axis=1)
    cs = jnp.concatenate([cos_m, cos_m, cos_i, cos_i, one], axis=1)
    ss = jnp.concatenate([-sin_m, sin_m, -sin_i, sin_i, zero], axis=1)
    return jnp.concatenate([cq, sq, ci, si, cs, ss], axis=1)


def _dsa_prep_kernel(qn_ref, qpe_ref, iq_ref, ckv_ref, sm_ref, gw_ref, tab_ref,
                     kvg_ref, lng_ref, lnb_ref, wuk_ref, wuv_ref,
                     q_out, k_out, v_out, iq_out, ik_out, iw_out):
    f32, bf16 = jnp.float32, jnp.bfloat16
    tm = qn_ref.shape[0]
    lane = lax.broadcasted_iota(jnp.int32, (tm, LANES), 1)
    lo64 = lane < 64

    def swap32(x):
        return jnp.where((lane & 32) == 0, pltpu.roll(x, 96, 1), pltpu.roll(x, 32, 1))

    def swap16(x):
        return jnp.where((lane & 16) == 0, pltpu.roll(x, 112, 1), pltpu.roll(x, 16, 1))

    cq, sq = tab_ref[:, 0:128], tab_ref[:, 128:256]
    ci, si = tab_ref[:, 256:384], tab_ref[:, 384:512]
    cs, ss = tab_ref[:, 512:640], tab_ref[:, 640:768]
    scale = (MLA_NOPE_DIM + MLA_ROPE_DIM) ** -0.5 * math.log2(math.e)

    for j in range(MLA_HEADS // 2):
        t = qpe_ref[:, LANES * j:LANES * (j + 1)]
        r = (t * cq + swap32(t) * sq) * scale
        q_out[0, 2 * j, :, 128:256] = jnp.where(lo64, r, 0.0).astype(bf16)
        q_out[0, 2 * j + 1, :, 128:256] = jnp.where(lo64, pltpu.roll(r, 64, 1), 0.0).astype(bf16)
        t = iq_ref[:, LANES * j:LANES * (j + 1)]
        r = t * ci + swap16(t) * si
        iq_out[0, 2 * j] = jnp.where(lo64, r, 0.0).astype(bf16)
        iq_out[0, 2 * j + 1] = jnp.where(lo64, pltpu.roll(r, 64, 1), 0.0).astype(bf16)
    for h in range(MLA_HEADS):
        q_out[0, h, :, 0:128] = (qn_ref[:, LANES * h:LANES * (h + 1)] * scale).astype(bf16)

    c = ckv_ref[...]
    cn = c * lax.rsqrt(jnp.mean(c * c, axis=-1, keepdims=True) + NORM_EPS) * kvg_ref[...]
    cnb = cn.astype(bf16)
    kn = jnp.dot(cnb, wuk_ref[...], preferred_element_type=f32)
    vv = jnp.dot(cnb, wuv_ref[...], preferred_element_type=f32)

    s = sm_ref[...]
    hi64 = jnp.logical_not(lo64)
    mu = jnp.sum(jnp.where(hi64, s, 0.0), axis=-1, keepdims=True) * (1.0 / IDX_DIM)
    xc = jnp.where(hi64, s - mu, 0.0)
    var = jnp.sum(xc * xc, axis=-1, keepdims=True) * (1.0 / IDX_DIM)
    ln = xc * lax.rsqrt(var + NORM_EPS) * lng_ref[...] + lnb_ref[...]
    y = jnp.where(hi64, ln, s)
    r = y * cs + jnp.where(lo64, swap32(y), swap16(y)) * ss
    kpe_t = jnp.where(lo64, r, 0.0).T.astype(bf16)
    for h in range(MLA_HEADS):
        k_out[0, h, 0:128, :] = kn[:, LANES * h:LANES * (h + 1)].T.astype(bf16)
        k_out[0, h, 128:256, :] = kpe_t
        v_out[0, h, :, 0:128] = vv[:, LANES * h:LANES * (h + 1)].astype(bf16)
        v_out[0, h, :, 128:256] = jnp.ones((tm, LANES), bf16)
    ik_out[0] = jnp.where(lo64, pltpu.roll(r, 64, 1), 0.0).astype(bf16)
    iw_out[...] = gw_ref[...] * (IDX_HEADS ** -0.5 * IDX_DIM ** -0.5)


def _dsa_prep(proj, tab, kv_norm_g, idx_norm_g, idx_norm_b, w_uk, w_uv, bsz, t):
    tm = PREP_TM
    nt = t // tm
    zeros64 = jnp.zeros((64,), jnp.float32)
    lng = jnp.concatenate([zeros64, idx_norm_g.astype(jnp.float32)])[None, :]
    lnb = jnp.concatenate([zeros64, idx_norm_b.astype(jnp.float32)])[None, :]
    kvg = kv_norm_g.astype(jnp.float32)[None, :]

    def rows(w, off):
        return pl.BlockSpec((tm, w), lambda b, i: (b * nt + i, off // w))

    def const(shape):
        return pl.BlockSpec(shape, lambda b, i: (0,) * len(shape))

    def heads(w):
        return pl.BlockSpec((1, MLA_HEADS, tm, w), lambda b, i: (b, 0, i, 0))

    bf16 = jnp.bfloat16
    return pl.pallas_call(
        _dsa_prep_kernel,
        grid=(bsz, nt),
        in_specs=[rows(2048, P_QN), rows(1024, P_QPE), rows(1024, P_IQ), rows(512, P_CKV),
                  rows(LANES, P_KPE_IK), rows(LANES, P_GAW),
                  pl.BlockSpec((tm, 768), lambda b, i: (i, 0)),
                  const((1, MLA_KV_RANK)), const((1, LANES)), const((1, LANES)),
                  const((MLA_KV_RANK, MLA_HEADS * MLA_NOPE_DIM)), const((MLA_KV_RANK, MLA_O_WIDTH))],
        out_specs=[heads(QK_PAD), pl.BlockSpec((1, MLA_HEADS, QK_PAD, tm), lambda b, i: (b, 0, 0, i)),
                   heads(V_PAD), heads(LANES),
                   pl.BlockSpec((1, tm, LANES), lambda b, i: (b, i, 0)),
                   pl.BlockSpec((tm, LANES), lambda b, i: (b * nt + i, 0))],
        out_shape=[jax.ShapeDtypeStruct((bsz, MLA_HEADS, t, QK_PAD), bf16),
                   jax.ShapeDtypeStruct((bsz, MLA_HEADS, QK_PAD, t), bf16),
                   jax.ShapeDtypeStruct((bsz, MLA_HEADS, t, V_PAD), bf16),
                   jax.ShapeDtypeStruct((bsz, IDX_HEADS, t, LANES), bf16),
                   jax.ShapeDtypeStruct((bsz, t, LANES), bf16),
                   jax.ShapeDtypeStruct((bsz * t, LANES), jnp.float32)],
        compiler_params=pltpu.CompilerParams(
            dimension_semantics=("arbitrary", "arbitrary"), vmem_limit_bytes=VMEM_LIMIT),
        name="dsa_prep",
    )(proj, proj, proj, proj, proj, proj, tab, kvg, lng, lnb,
      w_uk.astype(bf16), w_uv.astype(bf16))


def _indexer_kernel(iq_ref, ik_ref, iw_ref, bias_ref, key_scr, tie_scr, *, n_keep):
    f32, i32 = jnp.float32, jnp.int32
    tq = iq_ref.shape[2]
    kc = key_scr.shape[2]
    nch_total = key_scr.shape[0]
    qi = pl.program_id(1)
    nch = ((qi + 1) * tq + kc - 1) // kc
    q_chunk = (lax.broadcasted_iota(i32, (tq, kc), 0) + qi * tq) >> 6
    col0 = lax.broadcasted_iota(i32, (tq, kc), 1)
    n_sub = kc // LANES

    def score_chunk(c, carry):
        ik = ik_ref[0, pl.ds(pl.multiple_of(c * kc, kc), kc), :]
        acc = jnp.zeros((tq, kc), f32)
        for h in range(IDX_HEADS):
            s = lax.dot_general(iq_ref[0, h], ik, (((1,), (1,)), ((), ())), preferred_element_type=f32)
            acc = acc + iw_ref[:, 64 + h:65 + h] * jnp.maximum(s, 0.0)
        bits = pltpu.bitcast(acc, i32)
        key = bits ^ ((bits >> 31) & 0x7FFFFFFF)
        adm = ((col0 + c * kc) >> CHUNK_SHIFT) <= q_chunk
        key_scr[c] = jnp.where(adm, key, INT_MIN)
        return carry

    lax.fori_loop(0, nch, score_chunk, 0)

    def lane_fold(x):
        part = x[:, 0:LANES]
        for j in range(1, n_sub):
            part = part + x[:, LANES * j:LANES * (j + 1)]
        return part

    def rep(x):
        return jnp.concatenate([x] * n_sub, axis=1)

    def count(pred_fn):
        def body(c, acc):
            return acc + lane_fold(jnp.where(pred_fn(key_scr[c], c), 1.0, 0.0))
        acc = lax.fori_loop(0, nch, body, jnp.zeros((tq, LANES), f32))
        return jnp.broadcast_to(jnp.sum(acc, axis=1, keepdims=True), (tq, LANES))

    keep = float(n_keep)
    cnt = count(lambda k, c: k >= 0)
    thr = jnp.where(cnt >= keep, 0, INT_MIN).astype(i32)
    cnt_thr = jnp.where(cnt >= keep, cnt, (nch * kc).astype(f32))

    def bit_cond(state):
        i, _, cnt_thr = state
        return jnp.logical_and(i < 31, jnp.max(jnp.abs(cnt_thr - keep)) > 0.0)

    def bit_pass(state):
        i, thr, cnt_thr = state
        cand = thr | jnp.left_shift(jnp.int32(1), 30 - i)
        cand_w = rep(cand)
        cnt = count(lambda k, c: k >= cand_w)
        take = cnt >= keep
        return i + 1, jnp.where(take, cand, thr), jnp.where(take, cnt, cnt_thr)

    _, thr, cnt_ge = lax.while_loop(bit_cond, bit_pass, (jnp.int32(0), thr, cnt_thr))
    thr_w = rep(thr)
    tie_scr[...] = jnp.full((tq, LANES), 2 ** 30, i32)

    @pl.when(jnp.max(cnt_ge) > keep)
    def _():
        cnt_gt = count(lambda k, c: k > thr_w)
        need = keep - cnt_gt

        def idx_pass(i, x):
            cand = x | jnp.left_shift(jnp.int32(1), idx_bits - 1 - i)
            cand_w = rep(cand)
            cnt = count(lambda k, c: (k == thr_w) & ((col0 + c * kc) < cand_w))
            return jnp.where(cnt < need, cand, x)

        idx_bits = (nch_total * kc - 1).bit_length()
        x = lax.fori_loop(0, idx_bits, idx_pass, jnp.zeros((tq, LANES), i32))
        tie_scr[...] = jnp.where(cnt_ge > keep, x, 2 ** 30)

    tie_w = rep(tie_scr[...])

    def write_chunk(c, carry):
        k = key_scr[c]
        sel = (k > thr_w) | ((k == thr_w) & ((col0 + c * kc) <= tie_w))
        sel = sel & (k != INT_MIN)
        bias_ref[0, c] = jnp.where(sel, 0.0, NEG_BIG)
        return carry

    lax.fori_loop(0, nch, write_chunk, 0)

    def fill_chunk(c, carry):
        bias_ref[0, c] = jnp.full((tq, kc), NEG_BIG, f32)
        return carry

    lax.fori_loop(nch, nch_total, fill_chunk, 0)


def _indexer(iq, ik, iw, bsz, t, n_keep):
    tq, kc = IDX_TQ, ATT_TK
    nq, nkc = t // tq, t // kc
    return pl.pallas_call(
        functools.partial(_indexer_kernel, n_keep=n_keep),
        grid=(bsz, nq),
        in_specs=[pl.BlockSpec((1, IDX_HEADS, tq, LANES), lambda b, i: (b, 0, i, 0)),
                  pl.BlockSpec((1, t, LANES), lambda b, i: (b, 0, 0)),
                  pl.BlockSpec((tq, LANES), lambda b, i: (b * nq + i, 0))],
        out_specs=pl.BlockSpec((1, nkc, tq, kc), lambda b, i: (b, 0, i, 0)),
        out_shape=jax.ShapeDtypeStruct((bsz, nkc, t, kc), jnp.float32),
        scratch_shapes=[pltpu.VMEM((nkc, tq, kc), jnp.int32), pltpu.VMEM((tq, LANES), jnp.int32)],
        compiler_params=pltpu.CompilerParams(
            dimension_semantics=("arbitrary", "arbitrary"), vmem_limit_bytes=VMEM_LIMIT),
        name="dsa_indexer",
    )(iq, ik, iw)


def _attn_kernel(q_ref, k_ref, v_ref, b_ref, o_ref, m_scr, acc_scr):
    f32 = jnp.float32
    tq, tk = b_ref.shape[2], b_ref.shape[3]
    qi, ki = pl.program_id(1), pl.program_id(2)
    last = ((qi + 1) * tq - 1) // tk

    @pl.when(ki == 0)
    def _():
        m_scr[...] = jnp.full(m_scr.shape, NEG_BIG, f32)
        acc_scr[...] = jnp.zeros(acc_scr.shape, f32)

    @pl.when(ki <= last)
    def _():
        bias = b_ref[0, 0]

        def head_group(gi, carry):
            hs = [gi * ATT_HEAD_UNROLL + u for u in range(ATT_HEAD_UNROLL)]
            s = [jnp.dot(q_ref[0, h], k_ref[0, h], preferred_element_type=f32) + bias for h in hs]
            m_prev = [m_scr[h] for h in hs]
            m_new = [jnp.maximum(mp, jnp.max(x, axis=1, keepdims=True)) for mp, x in zip(m_prev, s)]
            p = [jnp.exp2((x - mn[:, 0:1]).astype(jnp.bfloat16)) for x, mn in zip(s, m_new)]
            pv = [jnp.dot(x, v_ref[0, h], preferred_element_type=f32) for x, h in zip(p, hs)]
            for u, h in enumerate(hs):
                alpha = jnp.exp2(m_prev[u] - m_new[u])
                acc_scr[h, :, 0:128] = alpha * acc_scr[h, :, 0:128] + pv[u][:, 0:128]
                acc_scr[h, :, 128:256] = alpha * acc_scr[h, :, 128:256] + pv[u][:, 128:256]
                m_scr[h] = m_new[u]
            return carry

        lax.fori_loop(0, MLA_HEADS // ATT_HEAD_UNROLL, head_group, 0)

    @pl.when(ki == last)
    def _():
        for h in range(MLA_HEADS):
            o_ref[0, :, LANES * h:LANES * (h + 1)] = (
                acc_scr[h, :, 0:128] / acc_scr[h, :, 128:256]).astype(o_ref.dtype)


def _masked_attention(q, k, v, bias, bsz, t):
    tq, tk = ATT_TQ, ATT_TK
    nq, nk = t // tq, t // tk

    def kmap(b, i, j):
        return (b, 0, jnp.minimum(j, ((i + 1) * tq - 1) // tk), 0)

    return pl.pallas_call(
        _attn_kernel,
        grid=(bsz, nq, nk),
        in_specs=[pl.BlockSpec((1, MLA_HEADS, tq, QK_PAD), lambda b, i, j: (b, 0, i, 0)),
                  pl.BlockSpec((1, MLA_HEADS, QK_PAD, tk),
                               lambda b, i, j: (b, 0, 0, jnp.minimum(j, ((i + 1) * tq - 1) // tk))),
                  pl.BlockSpec((1, MLA_HEADS, tk, V_PAD), kmap),
                  pl.BlockSpec((1, 1, tq, tk),
                               lambda b, i, j: (b, jnp.minimum(j, ((i + 1) * tq - 1) // tk), i, 0))],
        out_specs=pl.BlockSpec((1, tq, MLA_O_WIDTH), lambda b, i, j: (b, i, 0)),
        out_shape=jax.ShapeDtypeStruct((bsz, t, MLA_O_WIDTH), jnp.bfloat16),
        scratch_shapes=[pltpu.VMEM((MLA_HEADS, tq, LANES), jnp.float32),
                        pltpu.VMEM((MLA_HEADS, tq, V_PAD), jnp.float32)],
        compiler_params=pltpu.CompilerParams(
            dimension_semantics=("arbitrary", "arbitrary", "arbitrary"),
            vmem_limit_bytes=VMEM_LIMIT),
        name="dsa_attention",
    )(q, k, v, bias)


def dsa_branch_pallas(proj, kv_norm_g, w_uk, w_uv, idx_norm_g, idx_norm_b, bsz, t):
    n_keep = min(TOPK_MAX, t // 4)
    tab = _rope_tables_lane(t)
    q, k, v, iq, ik, iw = _dsa_prep(proj, tab, kv_norm_g, idx_norm_g, idx_norm_b, w_uk, w_uv, bsz, t)
    bias = _indexer(iq, ik, iw, bsz, t, n_keep)
    o = _masked_attention(q, k, v, bias, bsz, t)
    return o.reshape(bsz * t, MLA_O_WIDTH)


GDN_TB = 256
GDN_HG = 8
GDN_REP = GDN_V_HEADS // GDN_K_HEADS


def _gdn_kernel(q_ref, k_ref, v_ref, z_ref, ab_ref, cwq_ref, cwk_ref, cwv_ref, alog_ref, dtb_ref, ng_ref,
                o_ref, tq_scr, tk_scr, tv_scr, state_scr):
    f32, bf16 = jnp.float32, jnp.bfloat16
    tb = q_ref.shape[0]
    g = pl.program_id(1)
    step = pl.program_id(2)
    n_chunk = tb // CHUNK
    kh_n = GDN_HG // GDN_REP

    @pl.when(step == 0)
    def _():
        tq_scr[...] = jnp.zeros(tq_scr.shape, f32)
        tk_scr[...] = jnp.zeros(tk_scr.shape, f32)
        tv_scr[...] = jnp.zeros(tv_scr.shape, f32)
        state_scr[...] = jnp.zeros(state_scr.shape, f32)

    def conv_silu(x_ref, w_ref, buf):
        x = x_ref[...]
        buf[8:8 + tb, :] = x
        w = w_ref[...]
        y = x * w[3:4, :]
        for s in (1, 2, 3):
            y = y + buf[8 - s:8 - s + tb, :] * w[3 - s:4 - s, :]
        buf[0:8, :] = x[tb - 8:tb]
        return y * (1.0 / (1.0 + jnp.exp(-y)))

    qc = conv_silu(q_ref, cwq_ref, tq_scr)
    kc = conv_silu(k_ref, cwk_ref, tk_scr)
    vc = conv_silu(v_ref, cwv_ref, tv_scr)

    def l2n(x):
        return x * lax.rsqrt(jnp.sum(x * x, axis=-1, keepdims=True) + NORM_EPS)

    qn = [l2n(qc[:, LANES * j:LANES * (j + 1)]) * (GDN_HEAD_DIM ** -0.5) for j in range(kh_n)]
    kn = [l2n(kc[:, LANES * j:LANES * (j + 1)]) for j in range(kh_n)]

    ab = pltpu.roll(ab_ref[...], (LANES - GDN_HG * g) % LANES, 1)
    xg = ab + dtb_ref[0]
    softplus = jnp.maximum(xg, 0.0) + jnp.log1p(jnp.exp(-jnp.abs(xg)))
    g_all = -jnp.exp(alog_ref[0]) * softplus
    beta_all = 1.0 / (1.0 + jnp.exp(-ab))
    rin = lax.broadcasted_iota(jnp.int32, (tb, LANES), 0) & (CHUNK - 1)
    cum = g_all
    for s in (1, 2, 4, 8, 16, 32):
        cum = cum + jnp.where(rin >= s, pltpu.roll(cum, s, 0), 0.0)

    ii = lax.broadcasted_iota(jnp.int32, (tb, tb), 0)
    jj = lax.broadcasted_iota(jnp.int32, (tb, tb), 1)
    same = (ii >> CHUNK_SHIFT) == (jj >> CHUNK_SHIFT)
    incl = same & (ii >= jj)
    strict = same & (ii > jj)
    eye = jnp.where(ii == jj, 1.0, 0.0)
    nt_dims = (((1,), (1,)), ((), ()))
    tn_dims = (((0,), (0,)), ((), ()))
    ng = ng_ref[...]
    heads = range(GDN_HG)

    cum_t = cum.T
    kb = [x.astype(bf16) for x in kn]
    qb = [x.astype(bf16) for x in qn]
    kk = [lax.dot_general(kb[j], kb[j], nt_dims, preferred_element_type=f32) for j in range(kh_n)]
    qk = [lax.dot_general(qb[j], kb[j], nt_dims, preferred_element_type=f32) for j in range(kh_n)]
    gcol = [cum[:, hh:hh + 1] for hh in heads]
    bcol = [beta_all[:, 32 + hh:33 + hh] for hh in heads]
    dec = [jnp.where(incl, jnp.exp(jnp.where(incl, gcol[hh] - cum_t[hh:hh + 1, :], 0.0)), 0.0) for hh in heads]
    pw = [jnp.where(strict, -(bcol[hh] * kk[hh // GDN_REP] * dec[hh]), 0.0) for hh in heads]
    tmat = [eye + pw[hh] for hh in heads]
    pw = [jnp.dot(pw[hh].astype(bf16), pw[hh].astype(bf16), preferred_element_type=f32) for hh in heads]
    for _ in range(4):
        both = [jnp.dot(jnp.concatenate([tmat[hh], pw[hh]], axis=0).astype(bf16), pw[hh].astype(bf16),
                        preferred_element_type=f32) for hh in heads]
        tmat = [tmat[hh] + both[hh][0:tb] for hh in heads]
        pw = [both[hh][tb:2 * tb] for hh in heads]
    tmat = [tmat[hh] + jnp.dot(tmat[hh].astype(bf16), pw[hh].astype(bf16), preferred_element_type=f32)
            for hh in heads]
    eg = [jnp.exp(gcol[hh]) for hh in heads]
    uw = [jnp.dot(tmat[hh].astype(bf16),
                  jnp.concatenate([bcol[hh] * vc[:, LANES * hh:LANES * (hh + 1)],
                                   (bcol[hh] * eg[hh]) * kn[hh // GDN_REP]], axis=1).astype(bf16),
                  preferred_element_type=f32) for hh in heads]
    qkd = [qk[hh // GDN_REP] * dec[hh] for hh in heads]
    qd = [(qn[hh // GDN_REP] * eg[hh]).astype(bf16) for hh in heads]
    states = [state_scr[hh] for hh in heads]

    for c in range(n_chunk):
        r0 = c * CHUNK
        sb = [states[hh].astype(bf16) for hh in heads]
        ws = [jnp.dot(jnp.concatenate([uw[hh][r0:r0 + CHUNK, LANES:2 * LANES].astype(bf16),
                                       qd[hh][r0:r0 + CHUNK]], axis=0), sb[hh],
                      preferred_element_type=f32) for hh in heads]
        vb = [(uw[hh][r0:r0 + CHUNK, 0:LANES] - ws[hh][0:CHUNK]).astype(bf16) for hh in heads]
        g_last = [cum[r0 + CHUNK - 1:r0 + CHUNK, hh:hh + 1] for hh in heads]
        kd = [(kn[hh // GDN_REP][r0:r0 + CHUNK] * jnp.exp(g_last[hh] - gcol[hh][r0:r0 + CHUNK])).astype(bf16)
              for hh in heads]
        o = [ws[hh][CHUNK:2 * CHUNK] + jnp.dot(qkd[hh][r0:r0 + CHUNK, r0:r0 + CHUNK].astype(bf16), vb[hh],
                                               preferred_element_type=f32) for hh in heads]
        states = [states[hh] * jnp.exp(g_last[hh])
                  + lax.dot_general(kd[hh], vb[hh], tn_dims, preferred_element_type=f32) for hh in heads]
        for hh in heads:
            on = o[hh] * lax.rsqrt(jnp.mean(o[hh] * o[hh], axis=-1, keepdims=True) + NORM_EPS) * ng
            zz = z_ref[r0:r0 + CHUNK, LANES * hh:LANES * (hh + 1)]
            o_ref[r0:r0 + CHUNK, LANES * hh:LANES * (hh + 1)] = (
                on * (zz * (1.0 / (1.0 + jnp.exp(-zz))))).astype(o_ref.dtype)

    for hh in range(GDN_HG):
        state_scr[hh] = states[hh]


def _gdn_branch_pallas(proj, conv_w, a_log, dt_bias, norm_g, bsz, t):
    tb = GDN_TB
    nt = t // tb
    n_groups = GDN_V_HEADS // GDN_HG
    qw = GDN_HG // GDN_REP * GDN_HEAD_DIM
    vw = GDN_HG * GDN_HEAD_DIM
    f32 = jnp.float32

    def per_group(x):
        x = x.astype(f32).reshape(n_groups, 1, GDN_HG)
        return jnp.pad(x, ((0, 0), (0, 0), (0, LANES - GDN_HG)))

    def rows(w, off):
        return pl.BlockSpec((tb, w), lambda b, g, s: (b * nt + s, off // w + g))

    def convw(w, off):
        return pl.BlockSpec((4, w), lambda b, g, s: (0, off // w + g))

    grp = pl.BlockSpec((1, 1, LANES), lambda b, g, s: (g, 0, 0))
    conv_w = conv_w.astype(f32)
    return pl.pallas_call(
        _gdn_kernel,
        grid=(bsz, n_groups, nt),
        in_specs=[rows(qw, P_QKV), rows(qw, P_QKV + GDN_QK_WIDTH), rows(vw, P_QKV + 2 * GDN_QK_WIDTH),
                  rows(vw, P_GZ),
                  pl.BlockSpec((tb, LANES), lambda b, g, s: (b * nt + s, P_GAW // LANES)),
                  convw(qw, 0), convw(qw, GDN_QK_WIDTH), convw(vw, 2 * GDN_QK_WIDTH),
                  grp, grp,
                  pl.BlockSpec((1, LANES), lambda b, g, s: (0, 0))],
        out_specs=pl.BlockSpec((tb, vw), lambda b, g, s: (b * nt + s, g)),
        out_shape=jax.ShapeDtypeStruct((bsz * t, GDN_V_WIDTH), jnp.bfloat16),
        scratch_shapes=[pltpu.VMEM((tb + 8, qw), f32), pltpu.VMEM((tb + 8, qw), f32), pltpu.VMEM((tb + 8, vw), f32),
                        pltpu.VMEM((GDN_HG, GDN_HEAD_DIM, GDN_HEAD_DIM), f32)],
        compiler_params=pltpu.CompilerParams(
            dimension_semantics=("arbitrary", "arbitrary", "arbitrary"),
            vmem_limit_bytes=VMEM_LIMIT),
        name="gdn_branch",
    )(proj, proj, proj, proj, proj, conv_w, conv_w, conv_w,
      per_group(a_log), per_group(dt_bias), norm_g.astype(f32)[None, :])


MERGE_TM = 512
MERGE_TN = 512


def _merge_kernel(oa_ref, ob_ref, wa_ref, wb_ref, ga_ref, gb_ref, o_ref):
    f32 = jnp.float32
    ua = jnp.dot(oa_ref[...], wa_ref[...], preferred_element_type=f32)
    ub = jnp.dot(ob_ref[...], wb_ref[...], preferred_element_type=f32)
    sa = 1.0 / (1.0 + jnp.exp(-ga_ref[...]))
    sb = 1.0 / (1.0 + jnp.exp(-gb_ref[...]))
    o_ref[...] = (sa * ua + sb * ub).astype(o_ref.dtype)


def _outproj_kernel(m_ref, w_ref, x_ref, o_ref):
    o_ref[...] = x_ref[...] + jnp.dot(m_ref[...], w_ref[...], preferred_element_type=jnp.float32)


def _merge_and_project(x, proj, o_a, o_b, w_branch_gdn, w_branch_dsa, w_out):
    n, d = x.shape
    tm, tn = MERGE_TM, MERGE_TN
    bf16 = jnp.bfloat16
    params = pltpu.CompilerParams(dimension_semantics=("arbitrary", "arbitrary"), vmem_limit_bytes=VMEM_LIMIT)
    merged = pl.pallas_call(
        _merge_kernel,
        grid=(d // tn, n // tm),
        in_specs=[pl.BlockSpec((tm, o_a.shape[1]), lambda j, i: (i, 0)),
                  pl.BlockSpec((tm, o_b.shape[1]), lambda j, i: (i, 0)),
                  pl.BlockSpec((o_a.shape[1], tn), lambda j, i: (0, j)),
                  pl.BlockSpec((o_b.shape[1], tn), lambda j, i: (0, j)),
                  pl.BlockSpec((tm, tn), lambda j, i: (i, P_GATE_A // tn + j)),
                  pl.BlockSpec((tm, tn), lambda j, i: (i, P_GATE_B // tn + j))],
        out_specs=pl.BlockSpec((tm, tn), lambda j, i: (i, j)),
        out_shape=jax.ShapeDtypeStruct((n, d), bf16),
        compiler_params=params,
        name="branch_merge",
    )(o_a, o_b, w_branch_gdn.astype(bf16), w_branch_dsa.astype(bf16), proj, proj)
    return pl.pallas_call(
        _outproj_kernel,
        grid=(n // tm,),
        in_specs=[pl.BlockSpec((tm, d), lambda i: (i, 0)),
                  pl.BlockSpec((d, d), lambda i: (0, 0)),
                  pl.BlockSpec((tm, d), lambda i: (i, 0))],
        out_specs=pl.BlockSpec((tm, d), lambda i: (i, 0)),
        out_shape=jax.ShapeDtypeStruct((n, d), jnp.float32),
        compiler_params=pltpu.CompilerParams(dimension_semantics=("arbitrary",), vmem_limit_bytes=VMEM_LIMIT),
        name="out_projection",
    )(merged, w_out.astype(bf16), x)


def hybrid_mixer(x, norm_g, w_in, conv_w, a_log, dt_bias, gdn_norm_g, kv_norm_g, w_uk, w_uv,
                 idx_norm_g, idx_norm_b, w_branch_gdn, w_branch_dsa, w_out):
    bsz, t, d = x.shape
    n = bsz * t
    h = rms_norm(x, norm_g).reshape(n, d)
    proj = _matmul(h, _relayout_w_in(w_in), tn=PROJ_TN)
    o_a = _gdn_branch_pallas(proj, conv_w, a_log, dt_bias, gdn_norm_g, bsz, t)
    o_b = dsa_branch_pallas(proj, kv_norm_g, w_uk, w_uv, idx_norm_g, idx_norm_b, bsz, t)
    return _merge_and_project(x.reshape(n, d), proj, o_a, o_b, w_branch_gdn, w_branch_dsa, w_out)


ROUTER_TM = 512
MOE_BLK = 256
MOE_TN = 1024
COMBINE_TM = 128
DISPATCH_TM = 256


def _router_kernel(x_ref, g_ref, rw_ref, rb_ref, meta_ref, cnt_ref, carry_scr):
    f32 = jnp.float32
    tm = x_ref.shape[0]

    @pl.when(pl.program_id(0) == 0)
    def _():
        carry_scr[...] = jnp.zeros(carry_scr.shape, f32)

    x = x_ref[...]
    h = (x * lax.rsqrt(jnp.mean(x * x, axis=-1, keepdims=True) + NORM_EPS) * g_ref[...]).astype(jnp.bfloat16)
    logits = jnp.dot(h, rw_ref[...], preferred_element_type=f32) + rb_ref[...]
    lane = lax.broadcasted_iota(jnp.int32, (tm, LANES), 1).astype(f32)
    vals = logits
    hot_sum = jnp.zeros((tm, LANES), f32)
    tops, ids, hots = [], [], []
    for _ in range(TOP_K):
        m = jnp.max(vals, axis=1, keepdims=True)
        idx = jnp.min(jnp.where(vals == m, lane, float(LANES)), axis=1, keepdims=True)
        hot = lane == idx
        tops.append(m)
        ids.append(idx)
        hots.append(hot)
        vals = jnp.where(hot, -jnp.inf, vals)
        hot_sum = hot_sum + jnp.where(hot, 1.0, 0.0)
    exps = [jnp.exp(v - tops[0]) for v in tops]
    inv = 1.0 / (exps[0] + exps[1] + exps[2] + exps[3])
    ri = lax.broadcasted_iota(jnp.int32, (tm, tm), 0)
    ci = lax.broadcasted_iota(jnp.int32, (tm, tm), 1)
    lower = jnp.where(ri > ci, 1.0, 0.0).astype(jnp.bfloat16)
    base = carry_scr[...] + jnp.dot(lower, hot_sum.astype(jnp.bfloat16), preferred_element_type=f32)
    meta = jnp.zeros((tm, LANES), f32)
    for k in range(TOP_K):
        rank = jnp.sum(jnp.where(hots[k], base, 0.0), axis=1, keepdims=True)
        meta = jnp.where(lane == float(k), ids[k], meta)
        meta = jnp.where(lane == float(TOP_K + k), exps[k] * inv, meta)
        meta = jnp.where(lane == float(2 * TOP_K + k), rank, meta)
    meta_ref[...] = meta
    carry_scr[...] = carry_scr[...] + jnp.sum(hot_sum, axis=0, keepdims=True)
    cnt_ref[...] = carry_scr[...]


def _router(x1, norm_g, router_w, router_b):
    n, d = x1.shape
    tm = ROUTER_TM
    rw = jnp.pad(router_w.astype(jnp.bfloat16), ((0, 0), (0, LANES - N_EXPERTS)))
    rb = jnp.concatenate([router_b.astype(jnp.float32),
                          jnp.full((LANES - N_EXPERTS,), -jnp.inf, jnp.float32)])[None, :]
    return pl.pallas_call(
        _router_kernel,
        grid=(n // tm,),
        in_specs=[pl.BlockSpec((tm, d), lambda i: (i, 0)),
                  pl.BlockSpec((1, d), lambda i: (0, 0)),
                  pl.BlockSpec((d, LANES), lambda i: (0, 0)),
                  pl.BlockSpec((1, LANES), lambda i: (0, 0))],
        out_specs=[pl.BlockSpec((tm, LANES), lambda i: (i, 0)),
                   pl.BlockSpec((1, LANES), lambda i: (0, 0))],
        out_shape=[jax.ShapeDtypeStruct((n, LANES), jnp.float32),
                   jax.ShapeDtypeStruct((1, LANES), jnp.float32)],
        scratch_shapes=[pltpu.VMEM((1, LANES), jnp.float32)],
        compiler_params=pltpu.CompilerParams(dimension_semantics=("arbitrary",), vmem_limit_bytes=VMEM_LIMIT),
        name="moe_router",
    )(x1, norm_g.astype(jnp.float32)[None, :], rw, rb)


def _dispatch_kernel(slot_ref, x_ref, g_ref, xs_in_ref, xs_ref, h_scr, sem):
    del xs_in_ref
    tm = x_ref.shape[0]
    x = x_ref[...]
    h_scr[...] = x * lax.rsqrt(jnp.mean(x * x, axis=-1, keepdims=True) + NORM_EPS) * g_ref[...]

    def copy(r, k):
        return pltpu.make_async_copy(h_scr.at[pl.ds(r, 1)], xs_ref.at[pl.ds(slot_ref[TOP_K * r + k], 1)], sem)

    def start_row(r, carry):
        for k in range(TOP_K):
            copy(r, k).start()
        return carry

    lax.fori_loop(0, tm, start_row, 0)
    for _ in range(TOP_K):
        pltpu.make_async_copy(h_scr, xs_ref.at[pl.ds(0, tm)], sem).wait()


def _dispatch(slot, x1, norm_g, n_rows):
    n, d = x1.shape
    tm = DISPATCH_TM
    xs0 = jnp.zeros((n_rows, d), jnp.float32)
    return pl.pallas_call(
        _dispatch_kernel,
        grid=(n // tm,),
        in_specs=[pl.BlockSpec((tm * TOP_K,), lambda i: (i,), memory_space=pltpu.SMEM),
                  pl.BlockSpec((tm, d), lambda i: (i, 0)),
                  pl.BlockSpec((1, d), lambda i: (0, 0)),
                  pl.BlockSpec(memory_space=pl.ANY)],
        out_specs=pl.BlockSpec(memory_space=pl.ANY),
        out_shape=jax.ShapeDtypeStruct((n_rows, d), jnp.float32),
        scratch_shapes=[pltpu.VMEM((tm, d), jnp.float32), pltpu.SemaphoreType.DMA(())],
        input_output_aliases={3: 0},
        compiler_params=pltpu.CompilerParams(dimension_semantics=("arbitrary",), vmem_limit_bytes=VMEM_LIMIT),
        name="moe_dispatch",
    )(slot, x1, norm_g.astype(jnp.float32)[None, :], xs0)


def _expert_changed(be_ref, j):
    return jnp.logical_or(j == 0, be_ref[j] != be_ref[jnp.maximum(j - 1, 0)])


def _moe_up_kernel(be_ref, nb_ref, x_ref, wg_ref, wu_ref, bg_ref, bu_ref, o_ref, wg_scr, wu_scr):
    j = pl.program_id(1)

    @pl.when(jnp.logical_and(j < nb_ref[0], _expert_changed(be_ref, j)))
    def _():
        wg_scr[...] = wg_ref[0].astype(jnp.bfloat16)
        wu_scr[...] = wu_ref[0].astype(jnp.bfloat16)

    @pl.when(j < nb_ref[0])
    def _():
        x = x_ref[...].astype(jnp.bfloat16)
        gate = jnp.dot(x, wg_scr[...], preferred_element_type=jnp.float32) + bg_ref[0]
        up = jnp.dot(x, wu_scr[...], preferred_element_type=jnp.float32) + bu_ref[0]
        gate = jnp.minimum(gate, SWIGLU_LIMIT)
        up = jnp.clip(up, -SWIGLU_LIMIT, SWIGLU_LIMIT)
        act = (up + 1.0) * (gate * (1.0 / (1.0 + jnp.exp(-SWIGLU_ALPHA * gate))))
        o_ref[...] = act.astype(o_ref.dtype)

    @pl.when(j >= nb_ref[0])
    def _():
        o_ref[...] = jnp.zeros(o_ref.shape, o_ref.dtype)


def _moe_down_kernel(be_ref, nb_ref, a_ref, wd_ref, bd_ref, o_ref, wd_scr):
    j = pl.program_id(1)

    @pl.when(jnp.logical_and(j < nb_ref[0], _expert_changed(be_ref, j)))
    def _():
        wd_scr[...] = wd_ref[0].astype(jnp.bfloat16)

    @pl.when(j < nb_ref[0])
    def _():
        o_ref[...] = jnp.dot(a_ref[...], wd_scr[...], preferred_element_type=jnp.float32) + bd_ref[0]

    @pl.when(j >= nb_ref[0])
    def _():
        o_ref[...] = jnp.zeros(o_ref.shape, o_ref.dtype)


def _moe_experts(xs, block_expert, n_used, w_gate, b_gate, w_up, b_up, w_down, b_down):
    n_rows, d = xs.shape
    f = w_gate.shape[2]
    blk, tn = MOE_BLK, MOE_TN
    n_blocks = n_rows // blk
    bf16 = jnp.bfloat16

    def wspec(k):
        return pl.BlockSpec((1, k, tn), lambda c, j, be, nb: (be[j], 0, c))

    bspec = pl.BlockSpec((1, 1, tn), lambda c, j, be, nb: (be[j], 0, c))
    rows_in = lambda k: pl.BlockSpec((blk, k), lambda c, j, be, nb: (j, 0))
    rows_out = pl.BlockSpec((blk, tn), lambda c, j, be, nb: (j, c))
    params = pltpu.CompilerParams(dimension_semantics=("arbitrary", "arbitrary"), vmem_limit_bytes=VMEM_LIMIT)
    act = pl.pallas_call(
        _moe_up_kernel,
        grid_spec=pltpu.PrefetchScalarGridSpec(
            num_scalar_prefetch=2, grid=(f // tn, n_blocks),
            in_specs=[rows_in(d), wspec(d), wspec(d), bspec, bspec],
            out_specs=rows_out,
            scratch_shapes=[pltpu.VMEM((d, tn), bf16), pltpu.VMEM((d, tn), bf16)]),
        out_shape=jax.ShapeDtypeStruct((n_rows, f), bf16),
        compiler_params=params,
        name="moe_up",
    )(block_expert, n_used, xs, w_gate.astype(jnp.float32), w_up.astype(jnp.float32),
      b_gate.astype(jnp.float32)[:, None, :], b_up.astype(jnp.float32)[:, None, :])
    return pl.pallas_call(
        _moe_down_kernel,
        grid_spec=pltpu.PrefetchScalarGridSpec(
            num_scalar_prefetch=2, grid=(d // tn, n_blocks),
            in_specs=[rows_in(f), wspec(f), bspec],
            out_specs=rows_out,
            scratch_shapes=[pltpu.VMEM((f, tn), bf16)]),
        out_shape=jax.ShapeDtypeStruct((n_rows, d), jnp.float32),
        compiler_params=params,
        name="moe_down",
    )(block_expert, n_used, act, w_down.astype(jnp.float32), b_down.astype(jnp.float32)[:, None, :])


def _combine_kernel(slot_ref, slot_next_ref, x1_ref, meta_ref, g_ref, ys_ref, o_ref, buf, sem):
    i = pl.program_id(0)
    n_steps = pl.num_programs(0)
    tm = x1_ref.shape[0]

    def copy(slots, r, k, half):
        return pltpu.make_async_copy(ys_ref.at[pl.ds(slots[TOP_K * r + k], 1)],
                                     buf.at[half, k, pl.ds(r, 1)], sem.at[half])

    def start_tile(slots, half):
        def body(r, carry):
            for k in range(TOP_K):
                copy(slots, r, k, half).start()
            return carry
        lax.fori_loop(0, tm, body, 0)

    @pl.when(i == 0)
    def _():
        start_tile(slot_ref, 0)

    @pl.when(i + 1 < n_steps)
    def _():
        start_tile(slot_next_ref, (i + 1) % 2)

    half = i % 2

    for k in range(TOP_K):
        pltpu.make_async_copy(ys_ref.at[pl.ds(0, tm)], buf.at[half, k], sem.at[half]).wait()
    acc = x1_ref[...]
    for k in range(TOP_K):
        acc = acc + meta_ref[:, TOP_K + k:TOP_K + k + 1] * buf[half, k]
    o_ref[...] = acc * lax.rsqrt(jnp.mean(acc * acc, axis=-1, keepdims=True) + NORM_EPS) * g_ref[...]


def _combine(slot, x1, meta, ys, norm_g):
    n, d = x1.shape
    tm = COMBINE_TM
    n_steps = n // tm
    return pl.pallas_call(
        _combine_kernel,
        grid=(n_steps,),
        in_specs=[pl.BlockSpec((tm * TOP_K,), lambda i: (i,), memory_space=pltpu.SMEM),
                  pl.BlockSpec((tm * TOP_K,), lambda i: (jnp.minimum(i + 1, n_steps - 1),),
                               memory_space=pltpu.SMEM),
                  pl.BlockSpec((tm, d), lambda i: (i, 0)),
                  pl.BlockSpec((tm, LANES), lambda i: (i, 0)),
                  pl.BlockSpec((1, d), lambda i: (0, 0)),
                  pl.BlockSpec(memory_space=pl.ANY)],
        out_specs=pl.BlockSpec((tm, d), lambda i: (i, 0)),
        out_shape=jax.ShapeDtypeStruct((n, d), jnp.float32),
        scratch_shapes=[pltpu.VMEM((2, TOP_K, tm, d), jnp.float32),
                        pltpu.SemaphoreType.DMA((2,))],
        compiler_params=pltpu.CompilerParams(dimension_semantics=("arbitrary",), vmem_limit_bytes=VMEM_LIMIT),
        name="moe_combine",
    )(slot, slot, x1, meta, norm_g.astype(jnp.float32)[None, :], ys)


def moe_final_pallas(x1, norm_ffn_g, router_w, router_b, w_gate, b_gate, w_up, b_up, w_down, b_down,
                     norm_final_g):
    n, d = x1.shape
    meta, cnt = _router(x1, norm_ffn_g, router_w, router_b)
    counts = cnt[0, :N_EXPERTS].astype(jnp.int32)
    padded = (counts + MOE_BLK - 1) // MOE_BLK * MOE_BLK
    pad_end = jnp.cumsum(padded)
    pad_start = pad_end - padded
    expert = meta[:, 0:TOP_K].astype(jnp.int32)
    rank = meta[:, 2 * TOP_K:3 * TOP_K].astype(jnp.int32)
    seg_start = jnp.sum(jnp.where(expert[..., None] == jnp.arange(N_EXPERTS, dtype=jnp.int32), pad_start, 0),
                        axis=-1)
    slot = (seg_start + rank).reshape(-1)
    n_blocks = n * TOP_K // MOE_BLK + N_EXPERTS
    block_start = jnp.arange(n_blocks, dtype=jnp.int32) * MOE_BLK
    block_expert = jnp.minimum(jnp.sum((pad_end[None, :] <= block_start[:, None]).astype(jnp.int32), axis=1),
                               N_EXPERTS - 1)
    n_used = (pad_end[-1:] // MOE_BLK).astype(jnp.int32)
    xs = _dispatch(slot, x1, norm_ffn_g, n_blocks * MOE_BLK)
    ys = _moe_experts(xs, block_expert, n_used, w_gate, b_gate, w_up, b_up, w_down, b_down)
    return _combine(slot, x1, meta, ys, norm_final_g)


def kernel(x, norm_mix_g, w_in, gdn_conv_w, gdn_a_log, gdn_dt_bias, gdn_norm_g, mla_kv_norm_g,
           mla_w_uk, mla_w_uv, idx_k_norm_g, idx_k_norm_b, w_branch_gdn, w_branch_dsa, w_out,
           norm_ffn_g, router_w, router_b, exp_w_gate, exp_b_gate, exp_w_up, exp_b_up,
           exp_w_down, exp_b_down, norm_final_g):
    l = 0
    bsz, t, d = x.shape
    x1 = hybrid_mixer(x, norm_mix_g[l], w_in[l], gdn_conv_w[l], gdn_a_log[l], gdn_dt_bias[l],
                      gdn_norm_g[l], mla_kv_norm_g[l], mla_w_uk[l], mla_w_uv[l], idx_k_norm_g[l],
                      idx_k_norm_b[l], w_branch_gdn[l], w_branch_dsa[l], w_out[l])
    out = moe_final_pallas(x1, norm_ffn_g[l], router_w[l], router_b[l], exp_w_gate[l],
                           exp_b_gate[l], exp_w_up[l], exp_b_up[l], exp_w_down[l], exp_b_down[l], norm_final_g)
    return out.reshape(bsz, t, d)
```

```python
import functools
import math
import numpy as np
import jax
import jax.numpy as jnp
from jax import lax
from jax.experimental import pallas as pl
from jax.experimental.pallas import tpu as pltpu

D_MODEL = 2048
CHUNK = 64
CHUNK_SHIFT = CHUNK.bit_length() - 1
ROPE_THETA = 10000.0
NORM_EPS = 1e-6
GDN_K_HEADS = 16
GDN_V_HEADS = 32
GDN_HEAD_DIM = 128
GDN_QK_WIDTH = GDN_K_HEADS * GDN_HEAD_DIM
GDN_V_WIDTH = GDN_V_HEADS * GDN_HEAD_DIM
MLA_HEADS = 16
MLA_NOPE_DIM = 128
MLA_ROPE_DIM = 64
MLA_V_DIM = 128
MLA_KV_RANK = 512
MLA_Q_WIDTH = MLA_HEADS * (MLA_NOPE_DIM + MLA_ROPE_DIM)
MLA_O_WIDTH = MLA_HEADS * MLA_V_DIM
IDX_HEADS = 16
IDX_DIM = 64
IDX_ROPE_DIM = 32
TOPK_MAX = 256
Q_BLOCK = 128
N_EXPERTS = 32
TOP_K = 4
D_EXPERT = 2048
SWIGLU_LIMIT = 7.0
SWIGLU_ALPHA = 1.702
EXPERT_BLOCK = 256
LANES = 128
VMEM_LIMIT = 56 * 1024 * 1024
IN_SIZES = (GDN_QK_WIDTH, GDN_QK_WIDTH, GDN_V_WIDTH, GDN_V_WIDTH, GDN_V_HEADS, GDN_V_HEADS,
            MLA_Q_WIDTH, MLA_KV_RANK, MLA_ROPE_DIM, IDX_HEADS * IDX_DIM, IDX_DIM, IDX_HEADS,
            D_MODEL, D_MODEL)


def _mm_kernel(a_ref, b_ref, o_ref):
    o_ref[...] = jnp.dot(a_ref[...], b_ref[...], preferred_element_type=jnp.float32)


def _matmul(a, b, tm=512, tn=1024):
    m, k = a.shape
    n = b.shape[1]
    n_pad = -(-n // tn) * tn
    a = a.astype(jnp.bfloat16)
    b = b.astype(jnp.bfloat16)
    if n_pad != n:
        b = jnp.pad(b, ((0, 0), (0, n_pad - n)))
    out = pl.pallas_call(
        _mm_kernel,
        grid=(n_pad // tn, m // tm),
        in_specs=[pl.BlockSpec((tm, k), lambda j, i: (i, 0)),
                  pl.BlockSpec((k, tn), lambda j, i: (0, j))],
        out_specs=pl.BlockSpec((tm, tn), lambda j, i: (i, j)),
        out_shape=jax.ShapeDtypeStruct((m, n_pad), jnp.float32),
        compiler_params=pltpu.CompilerParams(
            dimension_semantics=("arbitrary", "arbitrary"),
            vmem_limit_bytes=VMEM_LIMIT),
        name="dense_matmul",
    )(a, b)
    return out[:, :n] if n_pad != n else out


def rms_norm(x, g):
    xf = x.astype(jnp.float32)
    y = xf * lax.rsqrt(jnp.mean(xf * xf, axis=-1, keepdims=True) + NORM_EPS)
    return (y * g.astype(jnp.float32)).astype(x.dtype)


def rope_tables(n, dim):
    inv = ROPE_THETA ** (-jnp.arange(0, dim, 2, dtype=jnp.float32) / dim)
    ang = jnp.arange(n, dtype=jnp.float32)[:, None] * inv[None, :]
    return jnp.cos(ang), jnp.sin(ang)


PROJ_TN = 1024
_IN_OFF = np.concatenate([[0], np.cumsum(IN_SIZES)]).tolist()
(_O_GQ, _O_GK, _O_GV, _O_GZ, _O_GA, _O_GB, _O_MQ, _O_CKV, _O_KPE, _O_IQ, _O_IK, _O_IW,
 _O_GATE_A, _O_GATE_B) = _IN_OFF[:-1]
P_QKV = 0
P_GZ = 8192
P_QN = 12288
P_GATE_A = 14336
P_GATE_B = 16384
P_QPE = 18432
P_IQ = 19456
P_CKV = 20480
P_KPE_IK = 20992
P_GAW = 21120
P_WIDTH = 21504


def _proj_column_order():
    cols = []
    cols += list(range(_O_GQ, _O_GZ))
    cols += list(range(_O_GZ, _O_GA))
    hd = MLA_NOPE_DIM + MLA_ROPE_DIM
    cols += [_O_MQ + h * hd + i for h in range(MLA_HEADS) for i in range(MLA_NOPE_DIM)]
    cols += list(range(_O_GATE_A, _O_GATE_A + 2 * D_MODEL))
    cols += [_O_MQ + h * hd + MLA_NOPE_DIM + i for h in range(MLA_HEADS) for i in range(MLA_ROPE_DIM)]
    cols += list(range(_O_IQ, _O_IK))
    cols += list(range(_O_CKV, _O_KPE))
    cols += list(range(_O_KPE, _O_IQ)) + list(range(_O_IK, _O_IW))
    cols += list(range(_O_GA, _O_MQ)) + list(range(_O_IW, _O_GATE_A))
    cols += [-1] * (P_WIDTH - len(cols))
    assert len(cols) == P_WIDTH
    return np.asarray(cols, np.int32)


_PROJ_COLS = _proj_column_order()


def _relayout_w_in(w_in):
    d = w_in.shape[0]
    w = w_in.astype(jnp.bfloat16)
    mq = w[:, _O_MQ:_O_CKV].reshape(d, MLA_HEADS, MLA_NOPE_DIM + MLA_ROPE_DIM)
    parts = [w[:, _O_GQ:_O_GA],
             mq[:, :, :MLA_NOPE_DIM].reshape(d, -1),
             w[:, _O_GATE_A:_O_GATE_A + 2 * D_MODEL],
             mq[:, :, MLA_NOPE_DIM:].reshape(d, -1),
             w[:, _O_IQ:_O_IK], w[:, _O_CKV:_O_KPE], w[:, _O_KPE:_O_IQ], w[:, _O_IK:_O_IW],
             w[:, _O_GA:_O_MQ], w[:, _O_IW:_O_GATE_A]]
    used = sum(p.shape[1] for p in parts)
    return jnp.concatenate(parts + [jnp.zeros((d, P_WIDTH - used), jnp.bfloat16)], axis=1)


NEG_BIG = -1e30
INT_MIN = -2 ** 31
PREP_TM = 256
IDX_TQ = 256
ATT_HG = 8
ATT_TQ = 1024
ATT_TK = 512
QK_PAD = 256
ATT_HEAD_UNROLL = 2
V_PAD = 256


def _rope_tables_lane(t):
    cos_m, sin_m = rope_tables(t, MLA_ROPE_DIM)
    cos_i, sin_i = rope_tables(t, IDX_ROPE_DIM)
    one = jnp.ones((t, IDX_DIM - IDX_ROPE_DIM), jnp.float32)
    zero = jnp.zeros_like(one)
    cq = jnp.concatenate([cos_m, cos_m] * 2, axis=1)
    sq = jnp.concatenate([-sin_m, sin_m] * 2, axis=1)
    ci = jnp.concatenate([cos_i, cos_i, one] * 2, axis=1)
    si = jnp.concatenate([-sin_i, sin_i, zero] * 2, axis=1)
    cs = jnp.concatenate([cos_m, cos_m, cos_i, cos_i, one], axis=1)
    ss = jnp.concatenate([-sin_m, sin_m, -sin_i, sin_i, zero], axis=1)
    return jnp.concatenate([cq, sq, ci, si, cs, ss], axis=1)


def _dsa_prep_kernel(qn_ref, qpe_ref, iq_ref, ckv_ref, sm_ref, gw_ref, tab_ref,
                     kvg_ref, lng_ref, lnb_ref, wuk_ref, wuv_ref,
                     q_out, k_out, v_out, iq_out, ik_out, iw_out):
    f32, bf16 = jnp.float32, jnp.bfloat16
    tm = qn_ref.shape[0]
    lane = lax.broadcasted_iota(jnp.int32, (tm, LANES), 1)
    lo64 = lane < 64

    def swap32(x):
        return jnp.where((lane & 32) == 0, pltpu.roll(x, 96, 1), pltpu.roll(x, 32, 1))

    def swap16(x):
        return jnp.where((lane & 16) == 0, pltpu.roll(x, 112, 1), pltpu.roll(x, 16, 1))

    cq, sq = tab_ref[:, 0:128], tab_ref[:, 128:256]
    ci, si = tab_ref[:, 256:384], tab_ref[:, 384:512]
    cs, ss = tab_ref[:, 512:640], tab_ref[:, 640:768]
    scale = (MLA_NOPE_DIM + MLA_ROPE_DIM) ** -0.5 * math.log2(math.e)

    for j in range(MLA_HEADS // 2):
        t = qpe_ref[:, LANES * j:LANES * (j + 1)]
        r = (t * cq + swap32(t) * sq) * scale
        q_out[0, 2 * j, :, 128:256] = jnp.where(lo64, r, 0.0).astype(bf16)
        q_out[0, 2 * j + 1, :, 128:256] = jnp.where(lo64, pltpu.roll(r, 64, 1), 0.0).astype(bf16)
        t = iq_ref[:, LANES * j:LANES * (j + 1)]
        r = t * ci + swap16(t) * si
        iq_out[0, 2 * j] = jnp.where(lo64, r, 0.0).astype(bf16)
        iq_out[0, 2 * j + 1] = jnp.where(lo64, pltpu.roll(r, 64, 1), 0.0).astype(bf16)
    for h in range(MLA_HEADS):
        q_out[0, h, :, 0:128] = (qn_ref[:, LANES * h:LANES * (h + 1)] * scale).astype(bf16)

    c = ckv_ref[...]
    cn = c * lax.rsqrt(jnp.mean(c * c, axis=-1, keepdims=True) + NORM_EPS) * kvg_ref[...]
    cnb = cn.astype(bf16)
    kn = jnp.dot(cnb, wuk_ref[...], preferred_element_type=f32)
    vv = jnp.dot(cnb, wuv_ref[...], preferred_element_type=f32)

    s = sm_ref[...]
    hi64 = jnp.logical_not(lo64)
    mu = jnp.sum(jnp.where(hi64, s, 0.0), axis=-1, keepdims=True) * (1.0 / IDX_DIM)
    xc = jnp.where(hi64, s - mu, 0.0)
    var = jnp.sum(xc * xc, axis=-1, keepdims=True) * (1.0 / IDX_DIM)
    ln = xc * lax.rsqrt(var + NORM_EPS) * lng_ref[...] + lnb_ref[...]
    y = jnp.where(hi64, ln, s)
    r = y * cs + jnp.where(lo64, swap32(y), swap16(y)) * ss
    kpe_t = jnp.where(lo64, r, 0.0).T.astype(bf16)
    for h in range(MLA_HEADS):
        k_out[0, h, 0:128, :] = kn[:, LANES * h:LANES * (h + 1)].T.astype(bf16)
        k_out[0, h, 128:256, :] = kpe_t
        v_out[0, h, :, 0:128] = vv[:, LANES * h:LANES * (h + 1)].astype(bf16)
        v_out[0, h, :, 128:256] = jnp.ones((tm, LANES), bf16)
    ik_out[0] = jnp.where(lo64, pltpu.roll(r, 64, 1), 0.0).astype(bf16)
    iw_out[...] = gw_ref[...] * (IDX_HEADS ** -0.5 * IDX_DIM ** -0.5)


def _dsa_prep(proj, tab, kv_norm_g, idx_norm_g, idx_norm_b, w_uk, w_uv, bsz, t):
    tm = PREP_TM
    nt = t // tm
    zeros64 = jnp.zeros((64,), jnp.float32)
    lng = jnp.concatenate([zeros64, idx_norm_g.astype(jnp.float32)])[None, :]
    lnb = jnp.concatenate([zeros64, idx_norm_b.astype(jnp.float32)])[None, :]
    kvg = kv_norm_g.astype(jnp.float32)[None, :]

    def rows(w, off):
        return pl.BlockSpec((tm, w), lambda b, i: (b * nt + i, off // w))

    def const(shape):
        return pl.BlockSpec(shape, lambda b, i: (0,) * len(shape))

    def heads(w):
        return pl.BlockSpec((1, MLA_HEADS, tm, w), lambda b, i: (b, 0, i, 0))

    bf16 = jnp.bfloat16
    return pl.pallas_call(
        _dsa_prep_kernel,
        grid=(bsz, nt),
        in_specs=[rows(2048, P_QN), rows(1024, P_QPE), rows(1024, P_IQ), rows(512, P_CKV),
                  rows(LANES, P_KPE_IK), rows(LANES, P_GAW),
                  pl.BlockSpec((tm, 768), lambda b, i: (i, 0)),
                  const((1, MLA_KV_RANK)), const((1, LANES)), const((1, LANES)),
                  const((MLA_KV_RANK, MLA_HEADS * MLA_NOPE_DIM)), const((MLA_KV_RANK, MLA_O_WIDTH))],
        out_specs=[heads(QK_PAD), pl.BlockSpec((1, MLA_HEADS, QK_PAD, tm), lambda b, i: (b, 0, 0, i)),
                   heads(V_PAD), heads(LANES),
                   pl.BlockSpec((1, tm, LANES), lambda b, i: (b, i, 0)),
                   pl.BlockSpec((tm, LANES), lambda b, i: (b * nt + i, 0))],
        out_shape=[jax.ShapeDtypeStruct((bsz, MLA_HEADS, t, QK_PAD), bf16),
                   jax.ShapeDtypeStruct((bsz, MLA_HEADS, QK_PAD, t), bf16),
                   jax.ShapeDtypeStruct((bsz, MLA_HEADS, t, V_PAD), bf16),
                   jax.ShapeDtypeStruct((bsz, IDX_HEADS, t, LANES), bf16),
                   jax.ShapeDtypeStruct((bsz, t, LANES), bf16),
                   jax.ShapeDtypeStruct((bsz * t, LANES), jnp.float32)],
        compiler_params=pltpu.CompilerParams(
            dimension_semantics=("arbitrary", "arbitrary"), vmem_limit_bytes=VMEM_LIMIT),
        name="dsa_prep",
    )(proj, proj, proj, proj, proj, proj, tab, kvg, lng, lnb,
      w_uk.astype(bf16), w_uv.astype(bf16))


def _indexer_kernel(iq_ref, ik_ref, iw_ref, bias_ref, key_scr, tie_scr, *, n_keep):
    f32, i32 = jnp.float32, jnp.int32
    tq = iq_ref.shape[2]
    kc = key_scr.shape[2]
    nch_total = key_scr.shape[0]
    qi = pl.program_id(1)
    nch = ((qi + 1) * tq + kc - 1) // kc
    q_chunk = (lax.broadcasted_iota(i32, (tq, kc), 0) + qi * tq) >> 6
    col0 = lax.broadcasted_iota(i32, (tq, kc), 1)
    n_sub = kc // LANES

    def score_chunk(c, carry):
        ik = ik_ref[0, pl.ds(pl.multiple_of(c * kc, kc), kc), :]
        acc = jnp.zeros((tq, kc), f32)
        for h in range(IDX_HEADS):
            s = lax.dot_general(iq_ref[0, h], ik, (((1,), (1,)), ((), ())), preferred_element_type=f32)
            acc = acc + iw_ref[:, 64 + h:65 + h] * jnp.maximum(s, 0.0)
        bits = pltpu.bitcast(acc, i32)
        key = bits ^ ((bits >> 31) & 0x7FFFFFFF)
        adm = ((col0 + c * kc) >> CHUNK_SHIFT) <= q_chunk
        key_scr[c] = jnp.where(adm, key, INT_MIN)
        return carry

    lax.fori_loop(0, nch, score_chunk, 0)

    def lane_fold(x):
        part = x[:, 0:LANES]
        for j in range(1, n_sub):
            part = part + x[:, LANES * j:LANES * (j + 1)]
        return part

    def rep(x):
        return jnp.concatenate([x] * n_sub, axis=1)

    def count(pred_fn):
        def body(c, acc):
            return acc + lane_fold(jnp.where(pred_fn(key_scr[c], c), 1.0, 0.0))
        acc = lax.fori_loop(0, nch, body, jnp.zeros((tq, LANES), f32))
        return jnp.broadcast_to(jnp.sum(acc, axis=1, keepdims=True), (tq, LANES))

    keep = float(n_keep)
    cnt = count(lambda k, c: k >= 0)
    thr = jnp.where(cnt >= keep, 0, INT_MIN).astype(i32)
    cnt_thr = jnp.where(cnt >= keep, cnt, (nch * kc).astype(f32))

    def bit_cond(state):
        i, _, cnt_thr = state
        return jnp.logical_and(i < 31, jnp.max(jnp.abs(cnt_thr - keep)) > 0.0)

    def bit_pass(state):
        i, thr, cnt_thr = state
        cand = thr | jnp.left_shift(jnp.int32(1), 30 - i)
        cand_w = rep(cand)
        cnt = count(lambda k, c: k >= cand_w)
        take = cnt >= keep
        return i + 1, jnp.where(take, cand, thr), jnp.where(take, cnt, cnt_thr)

    _, thr, cnt_ge = lax.while_loop(bit_cond, bit_pass, (jnp.int32(0), thr, cnt_thr))
    thr_w = rep(thr)
    tie_scr[...] = jnp.full((tq, LANES), 2 ** 30, i32)

    @pl.when(jnp.max(cnt_ge) > keep)
    def _():
        cnt_gt = count(lambda k, c: k > thr_w)
        need = keep - cnt_gt

        def idx_pass(i, x):
            cand = x | jnp.left_shift(jnp.int32(1), idx_bits - 1 - i)
            cand_w = rep(cand)
            cnt = count(lambda k, c: (k == thr_w) & ((col0 + c * kc) < cand_w))
            return jnp.where(cnt < need, cand, x)

        idx_bits = (nch_total * kc - 1).bit_length()
        x = lax.fori_loop(0, idx_bits, idx_pass, jnp.zeros((tq, LANES), i32))
        tie_scr[...] = jnp.where(cnt_ge > keep, x, 2 ** 30)

    tie_w = rep(tie_scr[...])

    def write_chunk(c, carry):
        k = key_scr[c]
        sel = (k > thr_w) | ((k == thr_w) & ((col0 + c * kc) <= tie_w))
        sel = sel & (k != INT_MIN)
        bias_ref[0, c] = jnp.where(sel, 0.0, NEG_BIG)
        return carry

    lax.fori_loop(0, nch, write_chunk, 0)

    def fill_chunk(c, carry):
        bias_ref[0, c] = jnp.full((tq, kc), NEG_BIG, f32)
        return carry

    lax.fori_loop(nch, nch_total, fill_chunk, 0)


def _indexer(iq, ik, iw, bsz, t, n_keep):
    tq, kc = IDX_TQ, ATT_TK
    nq, nkc = t // tq, t // kc
    return pl.pallas_call(
        functools.partial(_indexer_kernel, n_keep=n_keep),
        grid=(bsz, nq),
        in_specs=[pl.BlockSpec((1, IDX_HEADS, tq, LANES), lambda b, i: (b, 0, i, 0)),
                  pl.BlockSpec((1, t, LANES), lambda b, i: (b, 0, 0)),
                  pl.BlockSpec((tq, LANES), lambda b, i: (b * nq + i, 0))],
        out_specs=pl.BlockSpec((1, nkc, tq, kc), lambda b, i: (b, 0, i, 0)),
        out_shape=jax.ShapeDtypeStruct((bsz, nkc, t, kc), jnp.float32),
        scratch_shapes=[pltpu.VMEM((nkc, tq, kc), jnp.int32), pltpu.VMEM((tq, LANES), jnp.int32)],
        compiler_params=pltpu.CompilerParams(
            dimension_semantics=("arbitrary", "arbitrary"), vmem_limit_bytes=VMEM_LIMIT),
        name="dsa_indexer",
    )(iq, ik, iw)


def _attn_kernel(q_ref, k_ref, v_ref, b_ref, o_ref, m_scr, acc_scr):
    f32 = jnp.float32
    tq, tk = b_ref.shape[2], b_ref.shape[3]
    qi, ki = pl.program_id(2), pl.program_id(3)
    last = ((qi + 1) * tq - 1) // tk
    n_heads = q_ref.shape[1]

    @pl.when(ki == 0)
    def _():
        m_scr[...] = jnp.full(m_scr.shape, NEG_BIG, f32)
        acc_scr[...] = jnp.zeros(acc_scr.shape, f32)

    @pl.when(ki <= last)
    def _():
        bias = b_ref[0, 0]

        def head_group(gi, carry):
            hs = [gi * ATT_HEAD_UNROLL + u for u in range(ATT_HEAD_UNROLL)]
            s = [jnp.dot(q_ref[0, h], k_ref[0, h], preferred_element_type=f32) + bias for h in hs]
            m_prev = [m_scr[h] for h in hs]
            m_new = [jnp.maximum(mp, jnp.max(x, axis=1, keepdims=True)) for mp, x in zip(m_prev, s)]
            p = [jnp.exp2((x - mn[:, 0:1]).astype(jnp.bfloat16)) for x, mn in zip(s, m_new)]
            pv = [jnp.dot(x, v_ref[0, h], preferred_element_type=f32) for x, h in zip(p, hs)]
            for u, h in enumerate(hs):
                alpha = jnp.exp2(m_prev[u] - m_new[u])
                acc_scr[h, :, 0:128] = alpha * acc_scr[h, :, 0:128] + pv[u][:, 0:128]
                acc_scr[h, :, 128:256] = alpha * acc_scr[h, :, 128:256] + pv[u][:, 128:256]
                m_scr[h] = m_new[u]
            return carry

        lax.fori_loop(0, n_heads // ATT_HEAD_UNROLL, head_group, 0)

    @pl.when(ki == last)
    def _():
        for h in range(n_heads):
            o_ref[0, :, LANES * h:LANES * (h + 1)] = (
                acc_scr[h, :, 0:128] / acc_scr[h, :, 128:256]).astype(o_ref.dtype)


def _masked_attention(q, k, v, bias, bsz, t):
    tq, tk, hg = ATT_TQ, ATT_TK, ATT_HG
    nq, nk = t // tq, t // tk

    def last(i):
        return ((i + 1) * tq - 1) // tk

    return pl.pallas_call(
        _attn_kernel,
        grid=(bsz, MLA_HEADS // hg, nq, nk),
        in_specs=[pl.BlockSpec((1, hg, tq, QK_PAD), lambda b, g, i, j: (b, g, i, 0)),
                  pl.BlockSpec((1, hg, QK_PAD, tk), lambda b, g, i, j: (b, g, 0, jnp.minimum(j, last(i)))),
                  pl.BlockSpec((1, hg, tk, V_PAD), lambda b, g, i, j: (b, g, jnp.minimum(j, last(i)), 0)),
                  pl.BlockSpec((1, 1, tq, tk), lambda b, g, i, j: (b, jnp.minimum(j, last(i)), i, 0))],
        out_specs=pl.BlockSpec((1, tq, hg * MLA_V_DIM), lambda b, g, i, j: (b, i, g)),
        out_shape=jax.ShapeDtypeStruct((bsz, t, MLA_O_WIDTH), jnp.bfloat16),
        scratch_shapes=[pltpu.VMEM((hg, tq, LANES), jnp.float32),
                        pltpu.VMEM((hg, tq, V_PAD), jnp.float32)],
        compiler_params=pltpu.CompilerParams(
            dimension_semantics=("arbitrary", "arbitrary", "arbitrary", "arbitrary"),
            vmem_limit_bytes=VMEM_LIMIT),
        name="dsa_attention",
    )(q, k, v, bias)


def dsa_branch_pallas(proj, kv_norm_g, w_uk, w_uv, idx_norm_g, idx_norm_b, bsz, t):
    n_keep = min(TOPK_MAX, t // 4)
    tab = _rope_tables_lane(t)
    q, k, v, iq, ik, iw = _dsa_prep(proj, tab, kv_norm_g, idx_norm_g, idx_norm_b, w_uk, w_uv, bsz, t)
    bias = _indexer(iq, ik, iw, bsz, t, n_keep)
    o = _masked_attention(q, k, v, bias, bsz, t)
    return o.reshape(bsz * t, MLA_O_WIDTH)


GDN_TB = 256
GDN_HG = 16
GDN_REP = GDN_V_HEADS // GDN_K_HEADS


def _gdn_kernel(q_ref, k_ref, v_ref, z_ref, ab_ref, cwq_ref, cwk_ref, cwv_ref, alog_ref, dtb_ref, ng_ref,
                o_ref, tq_scr, tk_scr, tv_scr, state_scr):
    f32, bf16 = jnp.float32, jnp.bfloat16
    tb = q_ref.shape[0]
    g = pl.program_id(1)
    step = pl.program_id(2)
    n_chunk = tb // CHUNK
    kh_n = GDN_HG // GDN_REP

    @pl.when(step == 0)
    def _():
        tq_scr[...] = jnp.zeros(tq_scr.shape, f32)
        tk_scr[...] = jnp.zeros(tk_scr.shape, f32)
        tv_scr[...] = jnp.zeros(tv_scr.shape, f32)
        state_scr[...] = jnp.zeros(state_scr.shape, f32)

    def conv_silu(x_ref, w_ref, buf):
        x = x_ref[...]
        buf[8:8 + tb, :] = x
        w = w_ref[...]
        y = x * w[3:4, :]
        for s in (1, 2, 3):
            y = y + buf[8 - s:8 - s + tb, :] * w[3 - s:4 - s, :]
        buf[0:8, :] = x[tb - 8:tb]
        return y * (1.0 / (1.0 + jnp.exp(-y)))

    qc = conv_silu(q_ref, cwq_ref, tq_scr)
    kc = conv_silu(k_ref, cwk_ref, tk_scr)
    vc = conv_silu(v_ref, cwv_ref, tv_scr)

    def l2n(x):
        return x * lax.rsqrt(jnp.sum(x * x, axis=-1, keepdims=True) + NORM_EPS)

    qn = [l2n(qc[:, LANES * j:LANES * (j + 1)]) * (GDN_HEAD_DIM ** -0.5) for j in range(kh_n)]
    kn = [l2n(kc[:, LANES * j:LANES * (j + 1)]) for j in range(kh_n)]

    ab = pltpu.roll(ab_ref[...], (LANES - GDN_HG * g) % LANES, 1)
    xg = ab + dtb_ref[0]
    softplus = jnp.maximum(xg, 0.0) + jnp.log1p(jnp.exp(-jnp.abs(xg)))
    g_all = -jnp.exp(alog_ref[0]) * softplus
    beta_all = 1.0 / (1.0 + jnp.exp(-ab))
    rin = lax.broadcasted_iota(jnp.int32, (tb, LANES), 0) & (CHUNK - 1)
    cum = g_all
    for s in (1, 2, 4, 8, 16, 32):
        cum = cum + jnp.where(rin >= s, pltpu.roll(cum, s, 0), 0.0)

    ii = lax.broadcasted_iota(jnp.int32, (tb, tb), 0)
    jj = lax.broadcasted_iota(jnp.int32, (tb, tb), 1)
    same = (ii >> CHUNK_SHIFT) == (jj >> CHUNK_SHIFT)
    incl = same & (ii >= jj)
    strict = same & (ii > jj)
    eye = jnp.where(ii == jj, 1.0, 0.0)
    nt_dims = (((1,), (1,)), ((), ()))
    tn_dims = (((0,), (0,)), ((), ()))
    ng = ng_ref[...]
    heads = range(GDN_HG)

    cum_t = cum.T
    kb = [x.astype(bf16) for x in kn]
    qb = [x.astype(bf16) for x in qn]
    kk = [lax.dot_general(kb[j], kb[j], nt_dims, preferred_element_type=f32) for j in range(kh_n)]
    qk = [lax.dot_general(qb[j], kb[j], nt_dims, preferred_element_type=f32) for j in range(kh_n)]
    gcol = [cum[:, hh:hh + 1] for hh in heads]
    bcol = [beta_all[:, 32 + hh:33 + hh] for hh in heads]
    dec = [jnp.where(incl, jnp.exp(jnp.where(incl, gcol[hh] - cum_t[hh:hh + 1, :], 0.0)), 0.0) for hh in heads]
    pw = [jnp.where(strict, -(bcol[hh] * kk[hh // GDN_REP] * dec[hh]), 0.0) for hh in heads]
    tmat = [eye + pw[hh] for hh in heads]
    pw = [jnp.dot(pw[hh].astype(bf16), pw[hh].astype(bf16), preferred_element_type=f32) for hh in heads]
    for _ in range(4):
        both = [jnp.dot(jnp.concatenate([tmat[hh], pw[hh]], axis=0).astype(bf16), pw[hh].astype(bf16),
                        preferred_element_type=f32) for hh in heads]
        tmat = [tmat[hh] + both[hh][0:tb] for hh in heads]
        pw = [both[hh][tb:2 * tb] for hh in heads]
    tmat = [tmat[hh] + jnp.dot(tmat[hh].astype(bf16), pw[hh].astype(bf16), preferred_element_type=f32)
            for hh in heads]
    eg = [jnp.exp(gcol[hh]) for hh in heads]
    uw = [jnp.dot(tmat[hh].astype(bf16),
                  jnp.concatenate([bcol[hh] * vc[:, LANES * hh:LANES * (hh + 1)],
                                   (bcol[hh] * eg[hh]) * kn[hh // GDN_REP]], axis=1).astype(bf16),
                  preferred_element_type=f32) for hh in heads]
    qkd = [qk[hh // GDN_REP] * dec[hh] for hh in heads]
    qd = [(qn[hh // GDN_REP] * eg[hh]).astype(bf16) for hh in heads]
    states = [state_scr[hh] for hh in heads]

    for c in range(n_chunk):
        r0 = c * CHUNK
        sb = [states[hh].astype(bf16) for hh in heads]
        ws = [jnp.dot(jnp.concatenate([uw[hh][r0:r0 + CHUNK, LANES:2 * LANES].astype(bf16),
                                       qd[hh][r0:r0 + CHUNK]], axis=0), sb[hh],
                      preferred_element_type=f32) for hh in heads]
        vb = [(uw[hh][r0:r0 + CHUNK, 0:LANES] - ws[hh][0:CHUNK]).astype(bf16) for hh in heads]
        g_last = [cum[r0 + CHUNK - 1:r0 + CHUNK, hh:hh + 1] for hh in heads]
        kd = [(kn[hh // GDN_REP][r0:r0 + CHUNK] * jnp.exp(g_last[hh] - gcol[hh][r0:r0 + CHUNK])).astype(bf16)
              for hh in heads]
        o = [ws[hh][CHUNK:2 * CHUNK] + jnp.dot(qkd[hh][r0:r0 + CHUNK, r0:r0 + CHUNK].astype(bf16), vb[hh],
                                               preferred_element_type=f32) for hh in heads]
        states = [states[hh] * jnp.exp(g_last[hh])
                  + lax.dot_general(kd[hh], vb[hh], tn_dims, preferred_element_type=f32) for hh in heads]
        for hh in heads:
            on = o[hh] * lax.rsqrt(jnp.mean(o[hh] * o[hh], axis=-1, keepdims=True) + NORM_EPS) * ng
            zz = z_ref[r0:r0 + CHUNK, LANES * hh:LANES * (hh + 1)]
            o_ref[r0:r0 + CHUNK, LANES * hh:LANES * (hh + 1)] = (
                on * (zz * (1.0 / (1.0 + jnp.exp(-zz))))).astype(o_ref.dtype)

    for hh in range(GDN_HG):
        state_scr[hh] = states[hh]


def _gdn_branch_pallas(proj, conv_w, a_log, dt_bias, norm_g, bsz, t):
    tb = GDN_TB
    nt = t // tb
    n_groups = GDN_V_HEADS // GDN_HG
    qw = GDN_HG // GDN_REP * GDN_HEAD_DIM
    vw = GDN_HG * GDN_HEAD_DIM
    f32 = jnp.float32

    def per_group(x):
        x = x.astype(f32).reshape(n_groups, 1, GDN_HG)
        return jnp.pad(x, ((0, 0), (0, 0), (0, LANES - GDN_HG)))

    def rows(w, off):
        return pl.BlockSpec((tb, w), lambda b, g, s: (b * nt + s, off // w + g))

    def convw(w, off):
        return pl.BlockSpec((4, w), lambda b, g, s: (0, off // w + g))

    grp = pl.BlockSpec((1, 1, LANES), lambda b, g, s: (g, 0, 0))
    conv_w = conv_w.astype(f32)
    return pl.pallas_call(
        _gdn_kernel,
        grid=(bsz, n_groups, nt),
        in_specs=[rows(qw, P_QKV), rows(qw, P_QKV + GDN_QK_WIDTH), rows(vw, P_QKV + 2 * GDN_QK_WIDTH),
                  rows(vw, P_GZ),
                  pl.BlockSpec((tb, LANES), lambda b, g, s: (b * nt + s, P_GAW // LANES)),
                  convw(qw, 0), convw(qw, GDN_QK_WIDTH), convw(vw, 2 * GDN_QK_WIDTH),
                  grp, grp,
                  pl.BlockSpec((1, LANES), lambda b, g, s: (0, 0))],
        out_specs=pl.BlockSpec((tb, vw), lambda b, g, s: (b * nt + s, g)),
        out_shape=jax.ShapeDtypeStruct((bsz * t, GDN_V_WIDTH), jnp.bfloat16),
        scratch_shapes=[pltpu.VMEM((tb + 8, qw), f32), pltpu.VMEM((tb + 8, qw), f32), pltpu.VMEM((tb + 8, vw), f32),
                        pltpu.VMEM((GDN_HG, GDN_HEAD_DIM, GDN_HEAD_DIM), f32)],
        compiler_params=pltpu.CompilerParams(
            dimension_semantics=("arbitrary", "arbitrary", "arbitrary"),
            vmem_limit_bytes=VMEM_LIMIT),
        name="gdn_branch",
    )(proj, proj, proj, proj, proj, conv_w, conv_w, conv_w,
      per_group(a_log), per_group(dt_bias), norm_g.astype(f32)[None, :])


MERGE_TM = 512
MERGE_TN = 512


def _merge_kernel(oa_ref, ob_ref, wa_ref, wb_ref, ga_ref, gb_ref, o_ref):
    f32 = jnp.float32
    ua = jnp.dot(oa_ref[...], wa_ref[...], preferred_element_type=f32)
    ub = jnp.dot(ob_ref[...], wb_ref[...], preferred_element_type=f32)
    sa = 1.0 / (1.0 + jnp.exp(-ga_ref[...]))
    sb = 1.0 / (1.0 + jnp.exp(-gb_ref[...]))
    o_ref[...] = (sa * ua + sb * ub).astype(o_ref.dtype)


def _outproj_kernel(m_ref, w_ref, x_ref, o_ref):
    o_ref[...] = x_ref[...] + jnp.dot(m_ref[...], w_ref[...], preferred_element_type=jnp.float32)


def _merge_and_project(x, proj, o_a, o_b, w_branch_gdn, w_branch_dsa, w_out):
    n, d = x.shape
    tm, tn = MERGE_TM, MERGE_TN
    bf16 = jnp.bfloat16
    params = pltpu.CompilerParams(dimension_semantics=("arbitrary", "arbitrary"), vmem_limit_bytes=VMEM_LIMIT)
    merged = pl.pallas_call(
        _merge_kernel,
        grid=(d // tn, n // tm),
        in_specs=[pl.BlockSpec((tm, o_a.shape[1]), lambda j, i: (i, 0)),
                  pl.BlockSpec((tm, o_b.shape[1]), lambda j, i: (i, 0)),
                  pl.BlockSpec((o_a.shape[1], tn), lambda j, i: (0, j)),
                  pl.BlockSpec((o_b.shape[1], tn), lambda j, i: (0, j)),
                  pl.BlockSpec((tm, tn), lambda j, i: (i, P_GATE_A // tn + j)),
                  pl.BlockSpec((tm, tn), lambda j, i: (i, P_GATE_B // tn + j))],
        out_specs=pl.BlockSpec((tm, tn), lambda j, i: (i, j)),
        out_shape=jax.ShapeDtypeStruct((n, d), bf16),
        compiler_params=params,
        name="branch_merge",
    )(o_a, o_b, w_branch_gdn.astype(bf16), w_branch_dsa.astype(bf16), proj, proj)
    return pl.pallas_call(
        _outproj_kernel,
        grid=(n // tm,),
        in_specs=[pl.BlockSpec((tm, d), lambda i: (i, 0)),
                  pl.BlockSpec((d, d), lambda i: (0, 0)),
                  pl.BlockSpec((tm, d), lambda i: (i, 0))],
        out_specs=pl.BlockSpec((tm, d), lambda i: (i, 0)),
        out_shape=jax.ShapeDtypeStruct((n, d), jnp.float32),
        compiler_params=pltpu.CompilerParams(dimension_semantics=("arbitrary",), vmem_limit_bytes=VMEM_LIMIT),
        name="out_projection",
    )(merged, w_out.astype(bf16), x)


def hybrid_mixer(x, norm_g, w_in, conv_w, a_log, dt_bias, gdn_norm_g, kv_norm_g, w_uk, w_uv,
                 idx_norm_g, idx_norm_b, w_branch_gdn, w_branch_dsa, w_out):
    bsz, t, d = x.shape
    n = bsz * t
    h = rms_norm(x, norm_g).reshape(n, d)
    proj = _matmul(h, _relayout_w_in(w_in), tn=PROJ_TN)
    o_a = _gdn_branch_pallas(proj, conv_w, a_log, dt_bias, gdn_norm_g, bsz, t)
    o_b = dsa_branch_pallas(proj, kv_norm_g, w_uk, w_uv, idx_norm_g, idx_norm_b, bsz, t)
    return _merge_and_project(x.reshape(n, d), proj, o_a, o_b, w_branch_gdn, w_branch_dsa, w_out)


ROUTER_TM = 512
MOE_BLK = 256
MOE_TN = 1024
COMBINE_TM = 128
DISPATCH_TM = 256


def _router_kernel(x_ref, g_ref, rw_ref, rb_ref, meta_ref, cnt_ref, carry_scr):
    f32 = jnp.float32
    tm = x_ref.shape[0]

    @pl.when(pl.program_id(0) == 0)
    def _():
        carry_scr[...] = jnp.zeros(carry_scr.shape, f32)

    x = x_ref[...]
    h = (x * lax.rsqrt(jnp.mean(x * x, axis=-1, keepdims=True) + NORM_EPS) * g_ref[...]).astype(jnp.bfloat16)
    logits = jnp.dot(h, rw_ref[...], preferred_element_type=f32) + rb_ref[...]
    lane = lax.broadcasted_iota(jnp.int32, (tm, LANES), 1).astype(f32)
    vals = logits
    hot_sum = jnp.zeros((tm, LANES), f32)
    tops, ids, hots = [], [], []
    for _ in range(TOP_K):
        m = jnp.max(vals, axis=1, keepdims=True)
        idx = jnp.min(jnp.where(vals == m, lane, float(LANES)), axis=1, keepdims=True)
        hot = lane == idx
        tops.append(m)
        ids.append(idx)
        hots.append(hot)
        vals = jnp.where(hot, -jnp.inf, vals)
        hot_sum = hot_sum + jnp.where(hot, 1.0, 0.0)
    exps = [jnp.exp(v - tops[0]) for v in tops]
    inv = 1.0 / (exps[0] + exps[1] + exps[2] + exps[3])
    ri = lax.broadcasted_iota(jnp.int32, (tm, tm), 0)
    ci = lax.broadcasted_iota(jnp.int32, (tm, tm), 1)
    lower = jnp.where(ri > ci, 1.0, 0.0).astype(jnp.bfloat16)
    base = carry_scr[...] + jnp.dot(lower, hot_sum.astype(jnp.bfloat16), preferred_element_type=f32)
    meta = jnp.zeros((tm, LANES), f32)
    for k in range(TOP_K):
        rank = jnp.sum(jnp.where(hots[k], base, 0.0), axis=1, keepdims=True)
        meta = jnp.where(lane == float(k), ids[k], meta)
        meta = jnp.where(lane == float(TOP_K + k), exps[k] * inv, meta)
        meta = jnp.where(lane == float(2 * TOP_K + k), rank, meta)
    meta_ref[...] = meta
    carry_scr[...] = carry_scr[...] + jnp.sum(hot_sum, axis=0, keepdims=True)
    cnt_ref[...] = carry_scr[...]


def _router(x1, norm_g, router_w, router_b):
    n, d = x1.shape
    tm = ROUTER_TM
    rw = jnp.pad(router_w.astype(jnp.bfloat16), ((0, 0), (0, LANES - N_EXPERTS)))
    rb = jnp.concatenate([router_b.astype(jnp.float32),
                          jnp.full((LANES - N_EXPERTS,), -jnp.inf, jnp.float32)])[None, :]
    return pl.pallas_call(
        _router_kernel,
        grid=(n // tm,),
        in_specs=[pl.BlockSpec((tm, d), lambda i: (i, 0)),
                  pl.BlockSpec((1, d), lambda i: (0, 0)),
                  pl.BlockSpec((d, LANES), lambda i: (0, 0)),
                  pl.BlockSpec((1, LANES), lambda i: (0, 0))],
        out_specs=[pl.BlockSpec((tm, LANES), lambda i: (i, 0)),
                   pl.BlockSpec((1, LANES), lambda i: (0, 0))],
        out_shape=[jax.ShapeDtypeStruct((n, LANES), jnp.float32),
                   jax.ShapeDtypeStruct((1, LANES), jnp.float32)],
        scratch_shapes=[pltpu.VMEM((1, LANES), jnp.float32)],
        compiler_params=pltpu.CompilerParams(dimension_semantics=("arbitrary",), vmem_limit_bytes=VMEM_LIMIT),
        name="moe_router",
    )(x1, norm_g.astype(jnp.float32)[None, :], rw, rb)


def _dispatch_kernel(slot_ref, x_ref, g_ref, xs_in_ref, xs_ref, h_scr, sem):
    del xs_in_ref
    tm = x_ref.shape[0]
    x = x_ref[...]
    h_scr[...] = x * lax.rsqrt(jnp.mean(x * x, axis=-1, keepdims=True) + NORM_EPS) * g_ref[...]

    def copy(r, k):
        return pltpu.make_async_copy(h_scr.at[pl.ds(r, 1)], xs_ref.at[pl.ds(slot_ref[TOP_K * r + k], 1)], sem)

    def start_row(r, carry):
        for k in range(TOP_K):
            copy(r, k).start()
        return carry

    lax.fori_loop(0, tm, start_row, 0)
    for _ in range(TOP_K):
        pltpu.make_async_copy(h_scr, xs_ref.at[pl.ds(0, tm)], sem).wait()


def _dispatch(slot, x1, norm_g, n_rows):
    n, d = x1.shape
    tm = DISPATCH_TM
    xs0 = jnp.zeros((n_rows, d), jnp.float32)
    return pl.pallas_call(
        _dispatch_kernel,
        grid=(n // tm,),
        in_specs=[pl.BlockSpec((tm * TOP_K,), lambda i: (i,), memory_space=pltpu.SMEM),
                  pl.BlockSpec((tm, d), lambda i: (i, 0)),
                  pl.BlockSpec((1, d), lambda i: (0, 0)),
                  pl.BlockSpec(memory_space=pl.ANY)],
        out_specs=pl.BlockSpec(memory_space=pl.ANY),
        out_shape=jax.ShapeDtypeStruct((n_rows, d), jnp.float32),
        scratch_shapes=[pltpu.VMEM((tm, d), jnp.float32), pltpu.SemaphoreType.DMA(())],
        input_output_aliases={3: 0},
        compiler_params=pltpu.CompilerParams(dimension_semantics=("arbitrary",), vmem_limit_bytes=VMEM_LIMIT),
        name="moe_dispatch",
    )(slot, x1, norm_g.astype(jnp.float32)[None, :], xs0)


def _expert_changed(be_ref, j):
    return jnp.logical_or(j == 0, be_ref[j] != be_ref[jnp.maximum(j - 1, 0)])


def _moe_up_kernel(be_ref, nb_ref, x_ref, wg_ref, wu_ref, bg_ref, bu_ref, o_ref, wg_scr, wu_scr):
    j = pl.program_id(1)

    @pl.when(jnp.logical_and(j < nb_ref[0], _expert_changed(be_ref, j)))
    def _():
        wg_scr[...] = wg_ref[0].astype(jnp.bfloat16)
        wu_scr[...] = wu_ref[0].astype(jnp.bfloat16)

    @pl.when(j < nb_ref[0])
    def _():
        x = x_ref[...].astype(jnp.bfloat16)
        gate = jnp.dot(x, wg_scr[...], preferred_element_type=jnp.float32) + bg_ref[0]
        up = jnp.dot(x, wu_scr[...], preferred_element_type=jnp.float32) + bu_ref[0]
        gate = jnp.minimum(gate, SWIGLU_LIMIT)
        up = jnp.clip(up, -SWIGLU_LIMIT, SWIGLU_LIMIT)
        act = (up + 1.0) * (gate * (1.0 / (1.0 + jnp.exp(-SWIGLU_ALPHA * gate))))
        o_ref[...] = act.astype(o_ref.dtype)

    @pl.when(j >= nb_ref[0])
    def _():
        o_ref[...] = jnp.zeros(o_ref.shape, o_ref.dtype)


def _moe_down_kernel(be_ref, nb_ref, a_ref, wd_ref, bd_ref, o_ref, wd_scr):
    j = pl.program_id(1)

    @pl.when(jnp.logical_and(j < nb_ref[0], _expert_changed(be_ref, j)))
    def _():
        wd_scr[...] = wd_ref[0].astype(jnp.bfloat16)

    @pl.when(j < nb_ref[0])
    def _():
        o_ref[...] = jnp.dot(a_ref[...], wd_scr[...], preferred_element_type=jnp.float32) + bd_ref[0]

    @pl.when(j >= nb_ref[0])
    def _():
        o_ref[...] = jnp.zeros(o_ref.shape, o_ref.dtype)


def _moe_experts(xs, block_expert, n_used, w_gate, b_gate, w_up, b_up, w_down, b_down):
    n_rows, d = xs.shape
    f = w_gate.shape[2]
    blk, tn = MOE_BLK, MOE_TN
    n_blocks = n_rows // blk
    bf16 = jnp.bfloat16

    def wspec(k):
        return pl.BlockSpec((1, k, tn), lambda c, j, be, nb: (be[j], 0, c))

    bspec = pl.BlockSpec((1, 1, tn), lambda c, j, be, nb: (be[j], 0, c))
    rows_in = lambda k: pl.BlockSpec((blk, k), lambda c, j, be, nb: (j, 0))
    rows_out = pl.BlockSpec((blk, tn), lambda c, j, be, nb: (j, c))
    params = pltpu.CompilerParams(dimension_semantics=("arbitrary", "arbitrary"), vmem_limit_bytes=VMEM_LIMIT)
    act = pl.pallas_call(
        _moe_up_kernel,
        grid_spec=pltpu.PrefetchScalarGridSpec(
            num_scalar_prefetch=2, grid=(f // tn, n_blocks),
            in_specs=[rows_in(d), wspec(d), wspec(d), bspec, bspec],
            out_specs=rows_out,
            scratch_shapes=[pltpu.VMEM((d, tn), bf16), pltpu.VMEM((d, tn), bf16)]),
        out_shape=jax.ShapeDtypeStruct((n_rows, f), bf16),
        compiler_params=params,
        name="moe_up",
    )(block_expert, n_used, xs, w_gate.astype(jnp.float32), w_up.astype(jnp.float32),
      b_gate.astype(jnp.float32)[:, None, :], b_up.astype(jnp.float32)[:, None, :])
    return pl.pallas_call(
        _moe_down_kernel,
        grid_spec=pltpu.PrefetchScalarGridSpec(
            num_scalar_prefetch=2, grid=(d // tn, n_blocks),
            in_specs=[rows_in(f), wspec(f), bspec],
            out_specs=rows_out,
            scratch_shapes=[pltpu.VMEM((f, tn), bf16)]),
        out_shape=jax.ShapeDtypeStruct((n_rows, d), jnp.float32),
        compiler_params=params,
        name="moe_down",
    )(block_expert, n_used, act, w_down.astype(jnp.float32), b_down.astype(jnp.float32)[:, None, :])


def _combine_kernel(slot_ref, slot_next_ref, x1_ref, meta_ref, g_ref, ys_ref, o_ref, buf, sem):
    i = pl.program_id(0)
    n_steps = pl.num_programs(0)
    tm = x1_ref.shape[0]

    def copy(slots, r, k, half):
        return pltpu.make_async_copy(ys_ref.at[pl.ds(slots[TOP_K * r + k], 1)],
                                     buf.at[half, k, pl.ds(r, 1)], sem.at[half])

    def start_tile(slots, half):
        def body(r, carry):
            for k in range(TOP_K):
                copy(slots, r, k, half).start()
            return carry
        lax.fori_loop(0, tm, body, 0)

    @pl.when(i == 0)
    def _():
        start_tile(slot_ref, 0)

    @pl.when(i + 1 < n_steps)
    def _():
        start_tile(slot_next_ref, (i + 1) % 2)

    half = i % 2

    for k in range(TOP_K):
        pltpu.make_async_copy(ys_ref.at[pl.ds(0, tm)], buf.at[half, k], sem.at[half]).wait()
    acc = x1_ref[...]
    for k in range(TOP_K):
        acc = acc + meta_ref[:, TOP_K + k:TOP_K + k + 1] * buf[half, k]
    o_ref[...] = acc * lax.rsqrt(jnp.mean(acc * acc, axis=-1, keepdims=True) + NORM_EPS) * g_ref[...]


def _combine(slot, x1, meta, ys, norm_g):
    n, d = x1.shape
    tm = COMBINE_TM
    n_steps = n // tm
    return pl.pallas_call(
        _combine_kernel,
        grid=(n_steps,),
        in_specs=[pl.BlockSpec((tm * TOP_K,), lambda i: (i,), memory_space=pltpu.SMEM),
                  pl.BlockSpec((tm * TOP_K,), lambda i: (jnp.minimum(i + 1, n_steps - 1),),
                               memory_space=pltpu.SMEM),
                  pl.BlockSpec((tm, d), lambda i: (i, 0)),
                  pl.BlockSpec((tm, LANES), lambda i: (i, 0)),
                  pl.BlockSpec((1, d), lambda i: (0, 0)),
                  pl.BlockSpec(memory_space=pl.ANY)],
        out_specs=pl.BlockSpec((tm, d), lambda i: (i, 0)),
        out_shape=jax.ShapeDtypeStruct((n, d), jnp.float32),
        scratch_shapes=[pltpu.VMEM((2, TOP_K, tm, d), jnp.float32),
                        pltpu.SemaphoreType.DMA((2,))],
        compiler_params=pltpu.CompilerParams(dimension_semantics=("arbitrary",), vmem_limit_bytes=VMEM_LIMIT),
        name="moe_combine",
    )(slot, slot, x1, meta, norm_g.astype(jnp.float32)[None, :], ys)


def moe_final_pallas(x1, norm_ffn_g, router_w, router_b, w_gate, b_gate, w_up, b_up, w_down, b_down,
                     norm_final_g):
    n, d = x1.shape
    meta, cnt = _router(x1, norm_ffn_g, router_w, router_b)
    counts = cnt[0, :N_EXPERTS].astype(jnp.int32)
    padded = (counts + MOE_BLK - 1) // MOE_BLK * MOE_BLK
    pad_end = jnp.cumsum(padded)
    pad_start = pad_end - padded
    expert = meta[:, 0:TOP_K].astype(jnp.int32)
    rank = meta[:, 2 * TOP_K:3 * TOP_K].astype(jnp.int32)
    seg_start = jnp.sum(jnp.where(expert[..., None] == jnp.arange(N_EXPERTS, dtype=jnp.int32), pad_start, 0),
                        axis=-1)
    slot = (seg_start + rank).reshape(-1)
    n_blocks = n * TOP_K // MOE_BLK + N_EXPERTS
    block_start = jnp.arange(n_blocks, dtype=jnp.int32) * MOE_BLK
    block_expert = jnp.minimum(jnp.sum((pad_end[None, :] <= block_start[:, None]).astype(jnp.int32), axis=1),
                               N_EXPERTS - 1)
    n_used = (pad_end[-1:] // MOE_BLK).astype(jnp.int32)
    xs = _dispatch(slot, x1, norm_ffn_g, n_blocks * MOE_BLK)
    ys = _moe_experts(xs, block_expert, n_used, w_gate, b_gate, w_up, b_up, w_down, b_down)
    return _combine(slot, x1, meta, ys, norm_final_g)


def kernel(x, norm_mix_g, w_in, gdn_conv_w, gdn_a_log, gdn_dt_bias, gdn_norm_g, mla_kv_norm_g,
           mla_w_uk, mla_w_uv, idx_k_norm_g, idx_k_norm_b, w_branch_gdn, w_branch_dsa, w_out,
           norm_ffn_g, router_w, router_b, exp_w_gate, exp_b_gate, exp_w_up, exp_b_up,
           exp_w_down, exp_b_down, norm_final_g):
    l = 0
    bsz, t, d = x.shape
    x1 = hybrid_mixer(x, norm_mix_g[l], w_in[l], gdn_conv_w[l], gdn_a_log[l], gdn_dt_bias[l],
                      gdn_norm_g[l], mla_kv_norm_g[l], mla_w_uk[l], mla_w_uv[l], idx_k_norm_g[l],
                      idx_k_norm_b[l], w_branch_gdn[l], w_branch_dsa[l], w_out[l])
    out = moe_final_pallas(x1, norm_ffn_g[l], router_w[l], router_b[l], exp_w_gate[l],
                           exp_b_gate[l], exp_w_up[l], exp_b_up[l], exp_w_down[l], exp_b_down[l], norm_final_g)
    return out.reshape(bsz, t, d)
```

```python
import functools
import math
import numpy as np
import jax
import jax.numpy as jnp
from jax import lax
from jax.experimental import pallas as pl
from jax.experimental.pallas import tpu as pltpu

D_MODEL = 2048
CHUNK = 64
CHUNK_SHIFT = CHUNK.bit_length() - 1
ROPE_THETA = 10000.0
NORM_EPS = 1e-6
GDN_K_HEADS = 16
GDN_V_HEADS = 32
GDN_HEAD_DIM = 128
GDN_QK_WIDTH = GDN_K_HEADS * GDN_HEAD_DIM
GDN_V_WIDTH = GDN_V_HEADS * GDN_HEAD_DIM
MLA_HEADS = 16
MLA_NOPE_DIM = 128
MLA_ROPE_DIM = 64
MLA_V_DIM = 128
MLA_KV_RANK = 512
MLA_Q_WIDTH = MLA_HEADS * (MLA_NOPE_DIM + MLA_ROPE_DIM)
MLA_O_WIDTH = MLA_HEADS * MLA_V_DIM
IDX_HEADS = 16
IDX_DIM = 64
IDX_ROPE_DIM = 32
TOPK_MAX = 256
Q_BLOCK = 128
N_EXPERTS = 32
TOP_K = 4
D_EXPERT = 2048
SWIGLU_LIMIT = 7.0
SWIGLU_ALPHA = 1.702
EXPERT_BLOCK = 256
LANES = 128
VMEM_LIMIT = 56 * 1024 * 1024
IN_SIZES = (GDN_QK_WIDTH, GDN_QK_WIDTH, GDN_V_WIDTH, GDN_V_WIDTH, GDN_V_HEADS, GDN_V_HEADS,
            MLA_Q_WIDTH, MLA_KV_RANK, MLA_ROPE_DIM, IDX_HEADS * IDX_DIM, IDX_DIM, IDX_HEADS,
            D_MODEL, D_MODEL)


PROJ_TM = 1024
PROJ_TN = 1536


def _inproj_kernel(x_ref, g_ref, w_ref, o_ref, h_scr):
    @pl.when(pl.program_id(1) == 0)
    def _():
        x = x_ref[...]
        h_scr[...] = (x * lax.rsqrt(jnp.mean(x * x, axis=-1, keepdims=True) + NORM_EPS)
                      * g_ref[...]).astype(jnp.bfloat16)

    o_ref[...] = jnp.dot(h_scr[...], w_ref[...], preferred_element_type=jnp.float32)


def _input_projection(x, norm_g, w):
    n, d = x.shape
    width = w.shape[1]
    tm, tn = PROJ_TM, PROJ_TN
    return pl.pallas_call(
        _inproj_kernel,
        grid=(n // tm, width // tn),
        in_specs=[pl.BlockSpec((tm, d), lambda i, j: (i, 0)),
                  pl.BlockSpec((1, d), lambda i, j: (0, 0)),
                  pl.BlockSpec((d, tn), lambda i, j: (0, j))],
        out_specs=pl.BlockSpec((tm, tn), lambda i, j: (i, j)),
        out_shape=jax.ShapeDtypeStruct((n, width), jnp.float32),
        scratch_shapes=[pltpu.VMEM((tm, d), jnp.bfloat16)],
        compiler_params=pltpu.CompilerParams(
            dimension_semantics=("arbitrary", "arbitrary"),
            vmem_limit_bytes=VMEM_LIMIT),
        name="input_projection",
    )(x, norm_g.astype(jnp.float32)[None, :], w)


def rope_tables(n, dim):
    inv = ROPE_THETA ** (-jnp.arange(0, dim, 2, dtype=jnp.float32) / dim)
    ang = jnp.arange(n, dtype=jnp.float32)[:, None] * inv[None, :]
    return jnp.cos(ang), jnp.sin(ang)


_IN_OFF = np.concatenate([[0], np.cumsum(IN_SIZES)]).tolist()
(_O_GQ, _O_GK, _O_GV, _O_GZ, _O_GA, _O_GB, _O_MQ, _O_CKV, _O_KPE, _O_IQ, _O_IK, _O_IW,
 _O_GATE_A, _O_GATE_B) = _IN_OFF[:-1]
P_QKV = 0
P_GZ = 8192
P_QN = 12288
P_GATE_A = 14336
P_GATE_B = 16384
P_QPE = 18432
P_IQ = 19456
P_CKV = 20480
P_KPE_IK = 20992
P_GAW = 21120
P_WIDTH = 21504


def _proj_column_order():
    cols = []
    cols += list(range(_O_GQ, _O_GZ))
    cols += list(range(_O_GZ, _O_GA))
    hd = MLA_NOPE_DIM + MLA_ROPE_DIM
    cols += [_O_MQ + h * hd + i for h in range(MLA_HEADS) for i in range(MLA_NOPE_DIM)]
    cols += list(range(_O_GATE_A, _O_GATE_A + 2 * D_MODEL))
    cols += [_O_MQ + h * hd + MLA_NOPE_DIM + i for h in range(MLA_HEADS) for i in range(MLA_ROPE_DIM)]
    cols += list(range(_O_IQ, _O_IK))
    cols += list(range(_O_CKV, _O_KPE))
    cols += list(range(_O_KPE, _O_IQ)) + list(range(_O_IK, _O_IW))
    cols += list(range(_O_GA, _O_MQ)) + list(range(_O_IW, _O_GATE_A))
    cols += [-1] * (P_WIDTH - len(cols))
    assert len(cols) == P_WIDTH
    return np.asarray(cols, np.int32)


_PROJ_COLS = _proj_column_order()


def _relayout_w_in(w_in):
    d = w_in.shape[0]
    w = w_in.astype(jnp.bfloat16)
    mq = w[:, _O_MQ:_O_CKV].reshape(d, MLA_HEADS, MLA_NOPE_DIM + MLA_ROPE_DIM)
    parts = [w[:, _O_GQ:_O_GA],
             mq[:, :, :MLA_NOPE_DIM].reshape(d, -1),
             w[:, _O_GATE_A:_O_GATE_A + 2 * D_MODEL],
             mq[:, :, MLA_NOPE_DIM:].reshape(d, -1),
             w[:, _O_IQ:_O_IK], w[:, _O_CKV:_O_KPE], w[:, _O_KPE:_O_IQ], w[:, _O_IK:_O_IW],
             w[:, _O_GA:_O_MQ], w[:, _O_IW:_O_GATE_A]]
    used = sum(p.shape[1] for p in parts)
    return jnp.concatenate(parts + [jnp.zeros((d, P_WIDTH - used), jnp.bfloat16)], axis=1)


NEG_BIG = -1e30
INT_MIN = -2 ** 31
PREP_TM = 256
IDX_TQ = 256
ATT_HG = 8
ATT_TQ = 1024
ATT_TK = 512
QK_PAD = 256
ATT_HEAD_UNROLL = 2
V_PAD = 256


def _rope_tables_lane(t):
    cos_m, sin_m = rope_tables(t, MLA_ROPE_DIM)
    cos_i, sin_i = rope_tables(t, IDX_ROPE_DIM)
    one = jnp.ones((t, IDX_DIM - IDX_ROPE_DIM), jnp.float32)
    zero = jnp.zeros_like(one)
    cq = jnp.concatenate([cos_m, cos_m] * 2, axis=1)
    sq = jnp.concatenate([-sin_m, sin_m] * 2, axis=1)
    ci = jnp.concatenate([cos_i, cos_i, one] * 2, axis=1)
    si = jnp.concatenate([-sin_i, sin_i, zero] * 2, axis=1)
    cs = jnp.concatenate([cos_m, cos_m, cos_i, cos_i, one], axis=1)
    ss = jnp.concatenate([-sin_m, sin_m, -sin_i, sin_i, zero], axis=1)
    return jnp.concatenate([cq, sq, ci, si, cs, ss], axis=1)


def _dsa_prep_kernel(qn_ref, qpe_ref, iq_ref, ckv_ref, sm_ref, gw_ref, tab_ref,
                     kvg_ref, lng_ref, lnb_ref, wuk_ref, wuv_ref,
                     q_out, k_out, v_out, iq_out, ik_out, iw_out):
    f32, bf16 = jnp.float32, jnp.bfloat16
    tm = qn_ref.shape[0]
    lane = lax.broadcasted_iota(jnp.int32, (tm, LANES), 1)
    lo64 = lane < 64

    def swap32(x):
        return jnp.where((lane & 32) == 0, pltpu.roll(x, 96, 1), pltpu.roll(x, 32, 1))

    def swap16(x):
        return jnp.where((lane & 16) == 0, pltpu.roll(x, 112, 1), pltpu.roll(x, 16, 1))

    cq, sq = tab_ref[:, 0:128], tab_ref[:, 128:256]
    ci, si = tab_ref[:, 256:384], tab_ref[:, 384:512]
    cs, ss = tab_ref[:, 512:640], tab_ref[:, 640:768]
    scale = (MLA_NOPE_DIM + MLA_ROPE_DIM) ** -0.5 * math.log2(math.e)

    for j in range(MLA_HEADS // 2):
        t = qpe_ref[:, LANES * j:LANES * (j + 1)]
        r = (t * cq + swap32(t) * sq) * scale
        q_out[0, 2 * j, :, 128:256] = jnp.where(lo64, r, 0.0).astype(bf16)
        q_out[0, 2 * j + 1, :, 128:256] = jnp.where(lo64, pltpu.roll(r, 64, 1), 0.0).astype(bf16)
        t = iq_ref[:, LANES * j:LANES * (j + 1)]
        r = t * ci + swap16(t) * si
        iq_out[0, 2 * j] = jnp.where(lo64, r, 0.0).astype(bf16)
        iq_out[0, 2 * j + 1] = jnp.where(lo64, pltpu.roll(r, 64, 1), 0.0).astype(bf16)
    for h in range(MLA_HEADS):
        q_out[0, h, :, 0:128] = (qn_ref[:, LANES * h:LANES * (h + 1)] * scale).astype(bf16)

    c = ckv_ref[...]
    cn = c * lax.rsqrt(jnp.mean(c * c, axis=-1, keepdims=True) + NORM_EPS) * kvg_ref[...]
    cnb = cn.astype(bf16)
    kn = jnp.dot(cnb, wuk_ref[...], preferred_element_type=f32)
    vv = jnp.dot(cnb, wuv_ref[...], preferred_element_type=f32)

    s = sm_ref[...]
    hi64 = jnp.logical_not(lo64)
    mu = jnp.sum(jnp.where(hi64, s, 0.0), axis=-1, keepdims=True) * (1.0 / IDX_DIM)
    xc = jnp.where(hi64, s - mu, 0.0)
    var = jnp.sum(xc * xc, axis=-1, keepdims=True) * (1.0 / IDX_DIM)
    ln = xc * lax.rsqrt(var + NORM_EPS) * lng_ref[...] + lnb_ref[...]
    y = jnp.where(hi64, ln, s)
    r = y * cs + jnp.where(lo64, swap32(y), swap16(y)) * ss
    kpe_t = jnp.where(lo64, r, 0.0).T.astype(bf16)
    for h in range(MLA_HEADS):
        k_out[0, h, 0:128, :] = kn[:, LANES * h:LANES * (h + 1)].T.astype(bf16)
        k_out[0, h, 128:256, :] = kpe_t
        v_out[0, h, :, 0:128] = vv[:, LANES * h:LANES * (h + 1)].astype(bf16)
        v_out[0, h, :, 128:256] = jnp.ones((tm, LANES), bf16)
    ik_out[0] = jnp.where(lo64, pltpu.roll(r, 64, 1), 0.0).astype(bf16)
    iw_out[...] = gw_ref[...] * (IDX_HEADS ** -0.5 * IDX_DIM ** -0.5)


def _dsa_prep(proj, tab, kv_norm_g, idx_norm_g, idx_norm_b, w_uk, w_uv, bsz, t):
    tm = PREP_TM
    nt = t // tm
    zeros64 = jnp.zeros((64,), jnp.float32)
    lng = jnp.concatenate([zeros64, idx_norm_g.astype(jnp.float32)])[None, :]
    lnb = jnp.concatenate([zeros64, idx_norm_b.astype(jnp.float32)])[None, :]
    kvg = kv_norm_g.astype(jnp.float32)[None, :]

    def rows(w, off):
        return pl.BlockSpec((tm, w), lambda b, i: (b * nt + i, off // w))

    def const(shape):
        return pl.BlockSpec(shape, lambda b, i: (0,) * len(shape))

    def heads(w):
        return pl.BlockSpec((1, MLA_HEADS, tm, w), lambda b, i: (b, 0, i, 0))

    bf16 = jnp.bfloat16
    return pl.pallas_call(
        _dsa_prep_kernel,
        grid=(bsz, nt),
        in_specs=[rows(2048, P_QN), rows(1024, P_QPE), rows(1024, P_IQ), rows(512, P_CKV),
                  rows(LANES, P_KPE_IK), rows(LANES, P_GAW),
                  pl.BlockSpec((tm, 768), lambda b, i: (i, 0)),
                  const((1, MLA_KV_RANK)), const((1, LANES)), const((1, LANES)),
                  const((MLA_KV_RANK, MLA_HEADS * MLA_NOPE_DIM)), const((MLA_KV_RANK, MLA_O_WIDTH))],
        out_specs=[heads(QK_PAD), pl.BlockSpec((1, MLA_HEADS, QK_PAD, tm), lambda b, i: (b, 0, 0, i)),
                   heads(V_PAD), heads(LANES),
                   pl.BlockSpec((1, tm, LANES), lambda b, i: (b, i, 0)),
                   pl.BlockSpec((tm, LANES), lambda b, i: (b * nt + i, 0))],
        out_shape=[jax.ShapeDtypeStruct((bsz, MLA_HEADS, t, QK_PAD), bf16),
                   jax.ShapeDtypeStruct((bsz, MLA_HEADS, QK_PAD, t), bf16),
                   jax.ShapeDtypeStruct((bsz, MLA_HEADS, t, V_PAD), bf16),
                   jax.ShapeDtypeStruct((bsz, IDX_HEADS, t, LANES), bf16),
                   jax.ShapeDtypeStruct((bsz, t, LANES), bf16),
                   jax.ShapeDtypeStruct((bsz * t, LANES), jnp.float32)],
        compiler_params=pltpu.CompilerParams(
            dimension_semantics=("arbitrary", "arbitrary"), vmem_limit_bytes=VMEM_LIMIT),
        name="dsa_prep",
    )(proj, proj, proj, proj, proj, proj, tab, kvg, lng, lnb,
      w_uk.astype(bf16), w_uv.astype(bf16))


def _indexer_kernel(iq_ref, ik_ref, iw_ref, bias_ref, key_scr, tie_scr, *, n_keep):
    f32, i32 = jnp.float32, jnp.int32
    tq = iq_ref.shape[2]
    kc = key_scr.shape[2]
    nch_total = key_scr.shape[0]
    qi = pl.program_id(1)
    nch = ((qi + 1) * tq + kc - 1) // kc
    q_chunk = (lax.broadcasted_iota(i32, (tq, kc), 0) + qi * tq) >> 6
    col0 = lax.broadcasted_iota(i32, (tq, kc), 1)
    n_sub = kc // LANES

    def score_chunk(c, carry):
        ik = ik_ref[0, pl.ds(pl.multiple_of(c * kc, kc), kc), :]
        acc = jnp.zeros((tq, kc), f32)
        for h in range(IDX_HEADS):
            s = lax.dot_general(iq_ref[0, h], ik, (((1,), (1,)), ((), ())), preferred_element_type=f32)
            acc = acc + iw_ref[:, 64 + h:65 + h] * jnp.maximum(s, 0.0)
        bits = pltpu.bitcast(acc, i32)
        key = bits ^ ((bits >> 31) & 0x7FFFFFFF)
        adm = ((col0 + c * kc) >> CHUNK_SHIFT) <= q_chunk
        key_scr[c] = jnp.where(adm, key, INT_MIN)
        return carry

    lax.fori_loop(0, nch, score_chunk, 0)

    def lane_fold(x):
        part = x[:, 0:LANES]
        for j in range(1, n_sub):
            part = part + x[:, LANES * j:LANES * (j + 1)]
        return part

    def rep(x):
        return jnp.concatenate([x] * n_sub, axis=1)

    def count(pred_fn):
        def body(c, acc):
            return acc + lane_fold(jnp.where(pred_fn(key_scr[c], c), 1.0, 0.0))
        acc = lax.fori_loop(0, nch, body, jnp.zeros((tq, LANES), f32))
        return jnp.broadcast_to(jnp.sum(acc, axis=1, keepdims=True), (tq, LANES))

    keep = float(n_keep)
    cnt = count(lambda k, c: k >= 0)
    thr = jnp.where(cnt >= keep, 0, INT_MIN).astype(i32)
    cnt_thr = jnp.where(cnt >= keep, cnt, (nch * kc).astype(f32))

    def bit_cond(state):
        i, _, cnt_thr = state
        return jnp.logical_and(i < 31, jnp.max(jnp.abs(cnt_thr - keep)) > 0.0)

    def bit_pass(state):
        i, thr, cnt_thr = state
        cand = thr | jnp.left_shift(jnp.int32(1), 30 - i)
        cand_w = rep(cand)
        cnt = count(lambda k, c: k >= cand_w)
        take = cnt >= keep
        return i + 1, jnp.where(take, cand, thr), jnp.where(take, cnt, cnt_thr)

    _, thr, cnt_ge = lax.while_loop(bit_cond, bit_pass, (jnp.int32(0), thr, cnt_thr))
    thr_w = rep(thr)
    tie_scr[...] = jnp.full((tq, LANES), 2 ** 30, i32)

    @pl.when(jnp.max(cnt_ge) > keep)
    def _():
        cnt_gt = count(lambda k, c: k > thr_w)
        need = keep - cnt_gt

        def idx_pass(i, x):
            cand = x | jnp.left_shift(jnp.int32(1), idx_bits - 1 - i)
            cand_w = rep(cand)
            cnt = count(lambda k, c: (k == thr_w) & ((col0 + c * kc) < cand_w))
            return jnp.where(cnt < need, cand, x)

        idx_bits = (nch_total * kc - 1).bit_length()
        x = lax.fori_loop(0, idx_bits, idx_pass, jnp.zeros((tq, LANES), i32))
        tie_scr[...] = jnp.where(cnt_ge > keep, x, 2 ** 30)

    tie_w = rep(tie_scr[...])

    def write_chunk(c, carry):
        k = key_scr[c]
        sel = (k > thr_w) | ((k == thr_w) & ((col0 + c * kc) <= tie_w))
        sel = sel & (k != INT_MIN)
        bias_ref[0, c] = jnp.where(sel, 0.0, NEG_BIG)
        return carry

    lax.fori_loop(0, nch, write_chunk, 0)

    def fill_chunk(c, carry):
        bias_ref[0, c] = jnp.full((tq, kc), NEG_BIG, f32)
        return carry

    lax.fori_loop(nch, nch_total, fill_chunk, 0)


def _indexer(iq, ik, iw, bsz, t, n_keep):
    tq, kc = IDX_TQ, ATT_TK
    nq, nkc = t // tq, t // kc
    return pl.pallas_call(
        functools.partial(_indexer_kernel, n_keep=n_keep),
        grid=(bsz, nq),
        in_specs=[pl.BlockSpec((1, IDX_HEADS, tq, LANES), lambda b, i: (b, 0, i, 0)),
                  pl.BlockSpec((1, t, LANES), lambda b, i: (b, 0, 0)),
                  pl.BlockSpec((tq, LANES), lambda b, i: (b * nq + i, 0))],
        out_specs=pl.BlockSpec((1, nkc, tq, kc), lambda b, i: (b, 0, i, 0)),
        out_shape=jax.ShapeDtypeStruct((bsz, nkc, t, kc), jnp.float32),
        scratch_shapes=[pltpu.VMEM((nkc, tq, kc), jnp.int32), pltpu.VMEM((tq, LANES), jnp.int32)],
        compiler_params=pltpu.CompilerParams(
            dimension_semantics=("arbitrary", "arbitrary"), vmem_limit_bytes=VMEM_LIMIT),
        name="dsa_indexer",
    )(iq, ik, iw)


def _attn_kernel(q_ref, k_ref, v_ref, b_ref, o_ref, m_scr, acc_scr):
    f32 = jnp.float32
    tq, tk = b_ref.shape[2], b_ref.shape[3]
    qi, ki = pl.program_id(2), pl.program_id(3)
    last = ((qi + 1) * tq - 1) // tk
    n_heads = q_ref.shape[1]

    @pl.when(ki == 0)
    def _():
        m_scr[...] = jnp.full(m_scr.shape, NEG_BIG, f32)
        acc_scr[...] = jnp.zeros(acc_scr.shape, f32)

    @pl.when(ki <= last)
    def _():
        bias = b_ref[0, 0]

        def head_group(gi, carry):
            hs = [gi * ATT_HEAD_UNROLL + u for u in range(ATT_HEAD_UNROLL)]
            s = [jnp.dot(q_ref[0, h], k_ref[0, h], preferred_element_type=f32) + bias for h in hs]
            m_prev = [m_scr[h] for h in hs]
            m_new = [jnp.maximum(mp, jnp.max(x, axis=1, keepdims=True)) for mp, x in zip(m_prev, s)]
            p = [jnp.exp2((x - mn[:, 0:1]).astype(jnp.bfloat16)) for x, mn in zip(s, m_new)]
            pv = [jnp.dot(x, v_ref[0, h], preferred_element_type=f32) for x, h in zip(p, hs)]
            for u, h in enumerate(hs):
                alpha = jnp.exp2(m_prev[u] - m_new[u])
                acc_scr[h, :, 0:128] = alpha * acc_scr[h, :, 0:128] + pv[u][:, 0:128]
                acc_scr[h, :, 128:256] = alpha * acc_scr[h, :, 128:256] + pv[u][:, 128:256]
                m_scr[h] = m_new[u]
            return carry

        lax.fori_loop(0, n_heads // ATT_HEAD_UNROLL, head_group, 0)

    @pl.when(ki == last)
    def _():
        for h in range(n_heads):
            o_ref[0, :, LANES * h:LANES * (h + 1)] = (
                acc_scr[h, :, 0:128] / acc_scr[h, :, 128:256]).astype(o_ref.dtype)


def _masked_attention(q, k, v, bias, bsz, t):
    tq, tk, hg = ATT_TQ, ATT_TK, ATT_HG
    nq, nk = t // tq, t // tk

    def last(i):
        return ((i + 1) * tq - 1) // tk

    return pl.pallas_call(
        _attn_kernel,
        grid=(bsz, MLA_HEADS // hg, nq, nk),
        in_specs=[pl.BlockSpec((1, hg, tq, QK_PAD), lambda b, g, i, j: (b, g, i, 0)),
                  pl.BlockSpec((1, hg, QK_PAD, tk), lambda b, g, i, j: (b, g, 0, jnp.minimum(j, last(i)))),
                  pl.BlockSpec((1, hg, tk, V_PAD), lambda b, g, i, j: (b, g, jnp.minimum(j, last(i)), 0)),
                  pl.BlockSpec((1, 1, tq, tk), lambda b, g, i, j: (b, jnp.minimum(j, last(i)), i, 0))],
        out_specs=pl.BlockSpec((1, tq, hg * MLA_V_DIM), lambda b, g, i, j: (b, i, g)),
        out_shape=jax.ShapeDtypeStruct((bsz, t, MLA_O_WIDTH), jnp.bfloat16),
        scratch_shapes=[pltpu.VMEM((hg, tq, LANES), jnp.float32),
                        pltpu.VMEM((hg, tq, V_PAD), jnp.float32)],
        compiler_params=pltpu.CompilerParams(
            dimension_semantics=("arbitrary", "arbitrary", "arbitrary", "arbitrary"),
            vmem_limit_bytes=VMEM_LIMIT),
        name="dsa_attention",
    )(q, k, v, bias)


def dsa_branch_pallas(proj, kv_norm_g, w_uk, w_uv, idx_norm_g, idx_norm_b, bsz, t):
    n_keep = min(TOPK_MAX, t // 4)
    tab = _rope_tables_lane(t)
    q, k, v, iq, ik, iw = _dsa_prep(proj, tab, kv_norm_g, idx_norm_g, idx_norm_b, w_uk, w_uv, bsz, t)
    bias = _indexer(iq, ik, iw, bsz, t, n_keep)
    o = _masked_attention(q, k, v, bias, bsz, t)
    return o.reshape(bsz * t, MLA_O_WIDTH)


GDN_TB = 256
GDN_HG = 16
GDN_REP = GDN_V_HEADS // GDN_K_HEADS


def _gdn_kernel(q_ref, k_ref, v_ref, z_ref, ab_ref, cwq_ref, cwk_ref, cwv_ref, alog_ref, dtb_ref, ng_ref,
                o_ref, tq_scr, tk_scr, tv_scr, state_scr):
    f32, bf16 = jnp.float32, jnp.bfloat16
    tb = q_ref.shape[0]
    g = pl.program_id(1)
    step = pl.program_id(2)
    n_chunk = tb // CHUNK
    kh_n = GDN_HG // GDN_REP

    @pl.when(step == 0)
    def _():
        tq_scr[...] = jnp.zeros(tq_scr.shape, f32)
        tk_scr[...] = jnp.zeros(tk_scr.shape, f32)
        tv_scr[...] = jnp.zeros(tv_scr.shape, f32)
        state_scr[...] = jnp.zeros(state_scr.shape, f32)

    def conv_silu(x_ref, w_ref, buf):
        x = x_ref[...]
        buf[8:8 + tb, :] = x
        w = w_ref[...]
        y = x * w[3:4, :]
        for s in (1, 2, 3):
            y = y + buf[8 - s:8 - s + tb, :] * w[3 - s:4 - s, :]
        buf[0:8, :] = x[tb - 8:tb]
        return y * (1.0 / (1.0 + jnp.exp(-y)))

    qc = conv_silu(q_ref, cwq_ref, tq_scr)
    kc = conv_silu(k_ref, cwk_ref, tk_scr)
    vc = conv_silu(v_ref, cwv_ref, tv_scr)

    def l2n(x):
        return x * lax.rsqrt(jnp.sum(x * x, axis=-1, keepdims=True) + NORM_EPS)

    qn = [l2n(qc[:, LANES * j:LANES * (j + 1)]) * (GDN_HEAD_DIM ** -0.5) for j in range(kh_n)]
    kn = [l2n(kc[:, LANES * j:LANES * (j + 1)]) for j in range(kh_n)]

    ab = pltpu.roll(ab_ref[...], (LANES - GDN_HG * g) % LANES, 1)
    xg = ab + dtb_ref[0]
    softplus = jnp.maximum(xg, 0.0) + jnp.log1p(jnp.exp(-jnp.abs(xg)))
    g_all = -jnp.exp(alog_ref[0]) * softplus
    beta_all = 1.0 / (1.0 + jnp.exp(-ab))
    rin = lax.broadcasted_iota(jnp.int32, (tb, LANES), 0) & (CHUNK - 1)
    cum = g_all
    for s in (1, 2, 4, 8, 16, 32):
        cum = cum + jnp.where(rin >= s, pltpu.roll(cum, s, 0), 0.0)

    ii = lax.broadcasted_iota(jnp.int32, (tb, tb), 0)
    jj = lax.broadcasted_iota(jnp.int32, (tb, tb), 1)
    same = (ii >> CHUNK_SHIFT) == (jj >> CHUNK_SHIFT)
    incl = same & (ii >= jj)
    strict = same & (ii > jj)
    eye = jnp.where(ii == jj, 1.0, 0.0)
    nt_dims = (((1,), (1,)), ((), ()))
    tn_dims = (((0,), (0,)), ((), ()))
    ng = ng_ref[...]
    heads = range(GDN_HG)

    cum_t = cum.T
    kb = [x.astype(bf16) for x in kn]
    qb = [x.astype(bf16) for x in qn]
    kk = [lax.dot_general(kb[j], kb[j], nt_dims, preferred_element_type=f32) for j in range(kh_n)]
    qk = [lax.dot_general(qb[j], kb[j], nt_dims, preferred_element_type=f32) for j in range(kh_n)]
    gcol = [cum[:, hh:hh + 1] for hh in heads]
    bcol = [beta_all[:, 32 + hh:33 + hh] for hh in heads]
    dec = [jnp.where(incl, jnp.exp(jnp.where(incl, gcol[hh] - cum_t[hh:hh + 1, :], 0.0)), 0.0) for hh in heads]
    pw = [jnp.where(strict, -(bcol[hh] * kk[hh // GDN_REP] * dec[hh]), 0.0) for hh in heads]
    tmat = [eye + pw[hh] for hh in heads]
    pw = [jnp.dot(pw[hh].astype(bf16), pw[hh].astype(bf16), preferred_element_type=f32) for hh in heads]
    for _ in range(4):
        both = [jnp.dot(jnp.concatenate([tmat[hh], pw[hh]], axis=0).astype(bf16), pw[hh].astype(bf16),
                        preferred_element_type=f32) for hh in heads]
        tmat = [tmat[hh] + both[hh][0:tb] for hh in heads]
        pw = [both[hh][tb:2 * tb] for hh in heads]
    tmat = [tmat[hh] + jnp.dot(tmat[hh].astype(bf16), pw[hh].astype(bf16), preferred_element_type=f32)
            for hh in heads]
    eg = [jnp.exp(gcol[hh]) for hh in heads]
    uw = [jnp.dot(tmat[hh].astype(bf16),
                  jnp.concatenate([bcol[hh] * vc[:, LANES * hh:LANES * (hh + 1)],
                                   (bcol[hh] * eg[hh]) * kn[hh // GDN_REP]], axis=1).astype(bf16),
                  preferred_element_type=f32) for hh in heads]
    qkd = [qk[hh // GDN_REP] * dec[hh] for hh in heads]
    qd = [(qn[hh // GDN_REP] * eg[hh]).astype(bf16) for hh in heads]
    states = [state_scr[hh] for hh in heads]

    for c in range(n_chunk):
        r0 = c * CHUNK
        sb = [states[hh].astype(bf16) for hh in heads]
        ws = [jnp.dot(jnp.concatenate([uw[hh][r0:r0 + CHUNK, LANES:2 * LANES].astype(bf16),
                                       qd[hh][r0:r0 + CHUNK]], axis=0), sb[hh],
                      preferred_element_type=f32) for hh in heads]
        vb = [(uw[hh][r0:r0 + CHUNK, 0:LANES] - ws[hh][0:CHUNK]).astype(bf16) for hh in heads]
        g_last = [cum[r0 + CHUNK - 1:r0 + CHUNK, hh:hh + 1] for hh in heads]
        kd = [(kn[hh // GDN_REP][r0:r0 + CHUNK] * jnp.exp(g_last[hh] - gcol[hh][r0:r0 + CHUNK])).astype(bf16)
              for hh in heads]
        o = [ws[hh][CHUNK:2 * CHUNK] + jnp.dot(qkd[hh][r0:r0 + CHUNK, r0:r0 + CHUNK].astype(bf16), vb[hh],
                                               preferred_element_type=f32) for hh in heads]
        states = [states[hh] * jnp.exp(g_last[hh])
                  + lax.dot_general(kd[hh], vb[hh], tn_dims, preferred_element_type=f32) for hh in heads]
        for hh in heads:
            on = o[hh] * lax.rsqrt(jnp.mean(o[hh] * o[hh], axis=-1, keepdims=True) + NORM_EPS) * ng
            zz = z_ref[r0:r0 + CHUNK, LANES * hh:LANES * (hh + 1)]
            o_ref[r0:r0 + CHUNK, LANES * hh:LANES * (hh + 1)] = (
                on * (zz * (1.0 / (1.0 + jnp.exp(-zz))))).astype(o_ref.dtype)

    for hh in range(GDN_HG):
        state_scr[hh] = states[hh]


def _gdn_branch_pallas(proj, conv_w, a_log, dt_bias, norm_g, bsz, t):
    tb = GDN_TB
    nt = t // tb
    n_groups = GDN_V_HEADS // GDN_HG
    qw = GDN_HG // GDN_REP * GDN_HEAD_DIM
    vw = GDN_HG * GDN_HEAD_DIM
    f32 = jnp.float32

    def per_group(x):
        x = x.astype(f32).reshape(n_groups, 1, GDN_HG)
        return jnp.pad(x, ((0, 0), (0, 0), (0, LANES - GDN_HG)))

    def rows(w, off):
        return pl.BlockSpec((tb, w), lambda b, g, s: (b * nt + s, off // w + g))

    def convw(w, off):
        return pl.BlockSpec((4, w), lambda b, g, s: (0, off // w + g))

    grp = pl.BlockSpec((1, 1, LANES), lambda b, g, s: (g, 0, 0))
    conv_w = conv_w.astype(f32)
    return pl.pallas_call(
        _gdn_kernel,
        grid=(bsz, n_groups, nt),
        in_specs=[rows(qw, P_QKV), rows(qw, P_QKV + GDN_QK_WIDTH), rows(vw, P_QKV + 2 * GDN_QK_WIDTH),
                  rows(vw, P_GZ),
                  pl.BlockSpec((tb, LANES), lambda b, g, s: (b * nt + s, P_GAW // LANES)),
                  convw(qw, 0), convw(qw, GDN_QK_WIDTH), convw(vw, 2 * GDN_QK_WIDTH),
                  grp, grp,
                  pl.BlockSpec((1, LANES), lambda b, g, s: (0, 0))],
        out_specs=pl.BlockSpec((tb, vw), lambda b, g, s: (b * nt + s, g)),
        out_shape=jax.ShapeDtypeStruct((bsz * t, GDN_V_WIDTH), jnp.bfloat16),
        scratch_shapes=[pltpu.VMEM((tb + 8, qw), f32), pltpu.VMEM((tb + 8, qw), f32), pltpu.VMEM((tb + 8, vw), f32),
                        pltpu.VMEM((GDN_HG, GDN_HEAD_DIM, GDN_HEAD_DIM), f32)],
        compiler_params=pltpu.CompilerParams(
            dimension_semantics=("arbitrary", "arbitrary", "arbitrary"),
            vmem_limit_bytes=VMEM_LIMIT),
        name="gdn_branch",
    )(proj, proj, proj, proj, proj, conv_w, conv_w, conv_w,
      per_group(a_log), per_group(dt_bias), norm_g.astype(f32)[None, :])


MERGE_TM = 512
MERGE_TN = 512


def _merge_kernel(oa_ref, ob_ref, wa_ref, wb_ref, ga_ref, gb_ref, o_ref):
    f32 = jnp.float32
    ua = jnp.dot(oa_ref[...], wa_ref[...], preferred_element_type=f32)
    ub = jnp.dot(ob_ref[...], wb_ref[...], preferred_element_type=f32)
    sa = 1.0 / (1.0 + jnp.exp(-ga_ref[...]))
    sb = 1.0 / (1.0 + jnp.exp(-gb_ref[...]))
    o_ref[...] = (sa * ua + sb * ub).astype(o_ref.dtype)


def _outproj_kernel(m_ref, w_ref, x_ref, o_ref):
    o_ref[...] = x_ref[...] + jnp.dot(m_ref[...], w_ref[...], preferred_element_type=jnp.float32)


def _merge_and_project(x, proj, o_a, o_b, w_branch_gdn, w_branch_dsa, w_out):
    n, d = x.shape
    tm, tn = MERGE_TM, MERGE_TN
    bf16 = jnp.bfloat16
    params = pltpu.CompilerParams(dimension_semantics=("arbitrary", "arbitrary"), vmem_limit_bytes=VMEM_LIMIT)
    merged = pl.pallas_call(
        _merge_kernel,
        grid=(d // tn, n // tm),
        in_specs=[pl.BlockSpec((tm, o_a.shape[1]), lambda j, i: (i, 0)),
                  pl.BlockSpec((tm, o_b.shape[1]), lambda j, i: (i, 0)),
                  pl.BlockSpec((o_a.shape[1], tn), lambda j, i: (0, j)),
                  pl.BlockSpec((o_b.shape[1], tn), lambda j, i: (0, j)),
                  pl.BlockSpec((tm, tn), lambda j, i: (i, P_GATE_A // tn + j)),
                  pl.BlockSpec((tm, tn), lambda j, i: (i, P_GATE_B // tn + j))],
        out_specs=pl.BlockSpec((tm, tn), lambda j, i: (i, j)),
        out_shape=jax.ShapeDtypeStruct((n, d), bf16),
        compiler_params=params,
        name="branch_merge",
    )(o_a, o_b, w_branch_gdn.astype(bf16), w_branch_dsa.astype(bf16), proj, proj)
    return pl.pallas_call(
        _outproj_kernel,
        grid=(n // tm,),
        in_specs=[pl.BlockSpec((tm, d), lambda i: (i, 0)),
                  pl.BlockSpec((d, d), lambda i: (0, 0)),
                  pl.BlockSpec((tm, d), lambda i: (i, 0))],
        out_specs=pl.BlockSpec((tm, d), lambda i: (i, 0)),
        out_shape=jax.ShapeDtypeStruct((n, d), jnp.float32),
        compiler_params=pltpu.CompilerParams(dimension_semantics=("arbitrary",), vmem_limit_bytes=VMEM_LIMIT),
        name="out_projection",
    )(merged, w_out.astype(bf16), x)


def hybrid_mixer(x, norm_g, w_in, conv_w, a_log, dt_bias, gdn_norm_g, kv_norm_g, w_uk, w_uv,
                 idx_norm_g, idx_norm_b, w_branch_gdn, w_branch_dsa, w_out):
    bsz, t, d = x.shape
    n = bsz * t
    x = x.reshape(n, d)
    proj = _input_projection(x, norm_g, _relayout_w_in(w_in))
    o_a = _gdn_branch_pallas(proj, conv_w, a_log, dt_bias, gdn_norm_g, bsz, t)
    o_b = dsa_branch_pallas(proj, kv_norm_g, w_uk, w_uv, idx_norm_g, idx_norm_b, bsz, t)
    return _merge_and_project(x, proj, o_a, o_b, w_branch_gdn, w_branch_dsa, w_out)


ROUTER_TM = 512
MOE_BLK = 256
MOE_TN = 1024
COMBINE_TM = 128
DISPATCH_TM = 256


def _router_kernel(x_ref, g_ref, rw_ref, rb_ref, meta_ref, cnt_ref, carry_scr):
    f32 = jnp.float32
    tm = x_ref.shape[0]

    @pl.when(pl.program_id(0) == 0)
    def _():
        carry_scr[...] = jnp.zeros(carry_scr.shape, f32)

    x = x_ref[...]
    h = (x * lax.rsqrt(jnp.mean(x * x, axis=-1, keepdims=True) + NORM_EPS) * g_ref[...]).astype(jnp.bfloat16)
    logits = jnp.dot(h, rw_ref[...], preferred_element_type=f32) + rb_ref[...]
    lane = lax.broadcasted_iota(jnp.int32, (tm, LANES), 1).astype(f32)
    vals = logits
    hot_sum = jnp.zeros((tm, LANES), f32)
    tops, ids, hots = [], [], []
    for _ in range(TOP_K):
        m = jnp.max(vals, axis=1, keepdims=True)
        idx = jnp.min(jnp.where(vals == m, lane, float(LANES)), axis=1, keepdims=True)
        hot = lane == idx
        tops.append(m)
        ids.append(idx)
        hots.append(hot)
        vals = jnp.where(hot, -jnp.inf, vals)
        hot_sum = hot_sum + jnp.where(hot, 1.0, 0.0)
    exps = [jnp.exp(v - tops[0]) for v in tops]
    inv = 1.0 / (exps[0] + exps[1] + exps[2] + exps[3])
    ri = lax.broadcasted_iota(jnp.int32, (tm, tm), 0)
    ci = lax.broadcasted_iota(jnp.int32, (tm, tm), 1)
    lower = jnp.where(ri > ci, 1.0, 0.0).astype(jnp.bfloat16)
    base = carry_scr[...] + jnp.dot(lower, hot_sum.astype(jnp.bfloat16), preferred_element_type=f32)
    meta = jnp.zeros((tm, LANES), f32)
    for k in range(TOP_K):
        rank = jnp.sum(jnp.where(hots[k], base, 0.0), axis=1, keepdims=True)
        meta = jnp.where(lane == float(k), ids[k], meta)
        meta = jnp.where(lane == float(TOP_K + k), exps[k] * inv, meta)
        meta = jnp.where(lane == float(2 * TOP_K + k), rank, meta)
    meta_ref[...] = meta
    carry_scr[...] = carry_scr[...] + jnp.sum(hot_sum, axis=0, keepdims=True)
    cnt_ref[...] = carry_scr[...]


def _router(x1, norm_g, router_w, router_b):
    n, d = x1.shape
    tm = ROUTER_TM
    rw = jnp.pad(router_w.astype(jnp.bfloat16), ((0, 0), (0, LANES - N_EXPERTS)))
    rb = jnp.concatenate([router_b.astype(jnp.float32),
                          jnp.full((LANES - N_EXPERTS,), -jnp.inf, jnp.float32)])[None, :]
    return pl.pallas_call(
        _router_kernel,
        grid=(n // tm,),
        in_specs=[pl.BlockSpec((tm, d), lambda i: (i, 0)),
                  pl.BlockSpec((1, d), lambda i: (0, 0)),
                  pl.BlockSpec((d, LANES), lambda i: (0, 0)),
                  pl.BlockSpec((1, LANES), lambda i: (0, 0))],
        out_specs=[pl.BlockSpec((tm, LANES), lambda i: (i, 0)),
                   pl.BlockSpec((1, LANES), lambda i: (0, 0))],
        out_shape=[jax.ShapeDtypeStruct((n, LANES), jnp.float32),
                   jax.ShapeDtypeStruct((1, LANES), jnp.float32)],
        scratch_shapes=[pltpu.VMEM((1, LANES), jnp.float32)],
        compiler_params=pltpu.CompilerParams(dimension_semantics=("arbitrary",), vmem_limit_bytes=VMEM_LIMIT),
        name="moe_router",
    )(x1, norm_g.astype(jnp.float32)[None, :], rw, rb)


def _dispatch_kernel(slot_ref, x_ref, g_ref, xs_in_ref, xs_ref, h_scr, sem):
    del xs_in_ref
    tm = x_ref.shape[0]
    x = x_ref[...]
    h_scr[...] = x * lax.rsqrt(jnp.mean(x * x, axis=-1, keepdims=True) + NORM_EPS) * g_ref[...]

    def copy(r, k):
        return pltpu.make_async_copy(h_scr.at[pl.ds(r, 1)], xs_ref.at[pl.ds(slot_ref[TOP_K * r + k], 1)], sem)

    def start_row(r, carry):
        for k in range(TOP_K):
            copy(r, k).start()
        return carry

    lax.fori_loop(0, tm, start_row, 0)
    for _ in range(TOP_K):
        pltpu.make_async_copy(h_scr, xs_ref.at[pl.ds(0, tm)], sem).wait()


def _dispatch(slot, x1, norm_g, n_rows):
    n, d = x1.shape
    tm = DISPATCH_TM
    xs0 = jnp.zeros((n_rows, d), jnp.float32)
    return pl.pallas_call(
        _dispatch_kernel,
        grid=(n // tm,),
        in_specs=[pl.BlockSpec((tm * TOP_K,), lambda i: (i,), memory_space=pltpu.SMEM),
                  pl.BlockSpec((tm, d), lambda i: (i, 0)),
                  pl.BlockSpec((1, d), lambda i: (0, 0)),
                  pl.BlockSpec(memory_space=pl.ANY)],
        out_specs=pl.BlockSpec(memory_space=pl.ANY),
        out_shape=jax.ShapeDtypeStruct((n_rows, d), jnp.float32),
        scratch_shapes=[pltpu.VMEM((tm, d), jnp.float32), pltpu.SemaphoreType.DMA(())],
        input_output_aliases={3: 0},
        compiler_params=pltpu.CompilerParams(dimension_semantics=("arbitrary",), vmem_limit_bytes=VMEM_LIMIT),
        name="moe_dispatch",
    )(slot, x1, norm_g.astype(jnp.float32)[None, :], xs0)


def _expert_changed(be_ref, j):
    return jnp.logical_or(j == 0, be_ref[j] != be_ref[jnp.maximum(j - 1, 0)])


def _moe_up_kernel(be_ref, nb_ref, x_ref, wg_ref, wu_ref, bg_ref, bu_ref, o_ref, wg_scr, wu_scr):
    j = pl.program_id(1)

    @pl.when(jnp.logical_and(j < nb_ref[0], _expert_changed(be_ref, j)))
    def _():
        wg_scr[...] = wg_ref[0].astype(jnp.bfloat16)
        wu_scr[...] = wu_ref[0].astype(jnp.bfloat16)

    @pl.when(j < nb_ref[0])
    def _():
        x = x_ref[...].astype(jnp.bfloat16)
        gate = jnp.dot(x, wg_scr[...], preferred_element_type=jnp.float32) + bg_ref[0]
        up = jnp.dot(x, wu_scr[...], preferred_element_type=jnp.float32) + bu_ref[0]
        gate = jnp.minimum(gate, SWIGLU_LIMIT)
        up = jnp.clip(up, -SWIGLU_LIMIT, SWIGLU_LIMIT)
        act = (up + 1.0) * (gate * (1.0 / (1.0 + jnp.exp(-SWIGLU_ALPHA * gate))))
        o_ref[...] = act.astype(o_ref.dtype)

    @pl.when(j >= nb_ref[0])
    def _():
        o_ref[...] = jnp.zeros(o_ref.shape, o_ref.dtype)


def _moe_down_kernel(be_ref, nb_ref, a_ref, wd_ref, bd_ref, o_ref, wd_scr):
    j = pl.program_id(1)

    @pl.when(jnp.logical_and(j < nb_ref[0], _expert_changed(be_ref, j)))
    def _():
        wd_scr[...] = wd_ref[0].astype(jnp.bfloat16)

    @pl.when(j < nb_ref[0])
    def _():
        o_ref[...] = jnp.dot(a_ref[...], wd_scr[...], preferred_element_type=jnp.float32) + bd_ref[0]

    @pl.when(j >= nb_ref[0])
    def _():
        o_ref[...] = jnp.zeros(o_ref.shape, o_ref.dtype)


def _moe_experts(xs, block_expert, n_used, w_gate, b_gate, w_up, b_up, w_down, b_down):
    n_rows, d = xs.shape
    f = w_gate.shape[2]
    blk, tn = MOE_BLK, MOE_TN
    n_blocks = n_rows // blk
    bf16 = jnp.bfloat16

    def wspec(k):
        return pl.BlockSpec((1, k, tn), lambda c, j, be, nb: (be[j], 0, c))

    bspec = pl.BlockSpec((1, 1, tn), lambda c, j, be, nb: (be[j], 0, c))
    rows_in = lambda k: pl.BlockSpec((blk, k), lambda c, j, be, nb: (j, 0))
    rows_out = pl.BlockSpec((blk, tn), lambda c, j, be, nb: (j, c))
    params = pltpu.CompilerParams(dimension_semantics=("arbitrary", "arbitrary"), vmem_limit_bytes=VMEM_LIMIT)
    act = pl.pallas_call(
        _moe_up_kernel,
        grid_spec=pltpu.PrefetchScalarGridSpec(
            num_scalar_prefetch=2, grid=(f // tn, n_blocks),
            in_specs=[rows_in(d), wspec(d), wspec(d), bspec, bspec],
            out_specs=rows_out,
            scratch_shapes=[pltpu.VMEM((d, tn), bf16), pltpu.VMEM((d, tn), bf16)]),
        out_shape=jax.ShapeDtypeStruct((n_rows, f), bf16),
        compiler_params=params,
        name="moe_up",
    )(block_expert, n_used, xs, w_gate.astype(jnp.float32), w_up.astype(jnp.float32),
      b_gate.astype(jnp.float32)[:, None, :], b_up.astype(jnp.float32)[:, None, :])
    return pl.pallas_call(
        _moe_down_kernel,
        grid_spec=pltpu.PrefetchScalarGridSpec(
            num_scalar_prefetch=2, grid=(d // tn, n_blocks),
            in_specs=[rows_in(f), wspec(f), bspec],
            out_specs=rows_out,
            scratch_shapes=[pltpu.VMEM((f, tn), bf16)]),
        out_shape=jax.ShapeDtypeStruct((n_rows, d), jnp.float32),
        compiler_params=params,
        name="moe_down",
    )(block_expert, n_used, act, w_down.astype(jnp.float32), b_down.astype(jnp.float32)[:, None, :])


def _combine_kernel(slot_ref, slot_next_ref, x1_ref, meta_ref, g_ref, ys_ref, o_ref, buf, sem):
    i = pl.program_id(0)
    n_steps = pl.num_programs(0)
    tm = x1_ref.shape[0]

    def copy(slots, r, k, half):
        return pltpu.make_async_copy(ys_ref.at[pl.ds(slots[TOP_K * r + k], 1)],
                                     buf.at[half, k, pl.ds(r, 1)], sem.at[half])

    def start_tile(slots, half):
        def body(r, carry):
            for k in range(TOP_K):
                copy(slots, r, k, half).start()
            return carry
        lax.fori_loop(0, tm, body, 0)

    @pl.when(i == 0)
    def _():
        start_tile(slot_ref, 0)

    @pl.when(i + 1 < n_steps)
    def _():
        start_tile(slot_next_ref, (i + 1) % 2)

    half = i % 2

    for k in range(TOP_K):
        pltpu.make_async_copy(ys_ref.at[pl.ds(0, tm)], buf.at[half, k], sem.at[half]).wait()
    acc = x1_ref[...]
    for k in range(TOP_K):
        acc = acc + meta_ref[:, TOP_K + k:TOP_K + k + 1] * buf[half, k]
    o_ref[...] = acc * lax.rsqrt(jnp.mean(acc * acc, axis=-1, keepdims=True) + NORM_EPS) * g_ref[...]


def _combine(slot, x1, meta, ys, norm_g):
    n, d = x1.shape
    tm = COMBINE_TM
    n_steps = n // tm
    return pl.pallas_call(
        _combine_kernel,
        grid=(n_steps,),
        in_specs=[pl.BlockSpec((tm * TOP_K,), lambda i: (i,), memory_space=pltpu.SMEM),
                  pl.BlockSpec((tm * TOP_K,), lambda i: (jnp.minimum(i + 1, n_steps - 1),),
                               memory_space=pltpu.SMEM),
                  pl.BlockSpec((tm, d), lambda i: (i, 0)),
                  pl.BlockSpec((tm, LANES), lambda i: (i, 0)),
                  pl.BlockSpec((1, d), lambda i: (0, 0)),
                  pl.BlockSpec(memory_space=pl.ANY)],
        out_specs=pl.BlockSpec((tm, d), lambda i: (i, 0)),
        out_shape=jax.ShapeDtypeStruct((n, d), jnp.float32),
        scratch_shapes=[pltpu.VMEM((2, TOP_K, tm, d), jnp.float32),
                        pltpu.SemaphoreType.DMA((2,))],
        compiler_params=pltpu.CompilerParams(dimension_semantics=("arbitrary",), vmem_limit_bytes=VMEM_LIMIT),
        name="moe_combine",
    )(slot, slot, x1, meta, norm_g.astype(jnp.float32)[None, :], ys)


def moe_final_pallas(x1, norm_ffn_g, router_w, router_b, w_gate, b_gate, w_up, b_up, w_down, b_down,
                     norm_final_g):
    n, d = x1.shape
    meta, cnt = _router(x1, norm_ffn_g, router_w, router_b)
    counts = cnt[0, :N_EXPERTS].astype(jnp.int32)
    padded = (counts + MOE_BLK - 1) // MOE_BLK * MOE_BLK
    pad_end = jnp.cumsum(padded)
    pad_start = pad_end - padded
    expert = meta[:, 0:TOP_K].astype(jnp.int32)
    rank = meta[:, 2 * TOP_K:3 * TOP_K].astype(jnp.int32)
    seg_start = jnp.sum(jnp.where(expert[..., None] == jnp.arange(N_EXPERTS, dtype=jnp.int32), pad_start, 0),
                        axis=-1)
    slot = (seg_start + rank).reshape(-1)
    n_blocks = n * TOP_K // MOE_BLK + N_EXPERTS
    block_start = jnp.arange(n_blocks, dtype=jnp.int32) * MOE_BLK
    block_expert = jnp.minimum(jnp.sum((pad_end[None, :] <= block_start[:, None]).astype(jnp.int32), axis=1),
                               N_EXPERTS - 1)
    n_used = (pad_end[-1:] // MOE_BLK).astype(jnp.int32)
    xs = _dispatch(slot, x1, norm_ffn_g, n_blocks * MOE_BLK)
    ys = _moe_experts(xs, block_expert, n_used, w_gate, b_gate, w_up, b_up, w_down, b_down)
    return _combine(slot, x1, meta, ys, norm_final_g)


def kernel(x, norm_mix_g, w_in, gdn_conv_w, gdn_a_log, gdn_dt_bias, gdn_norm_g, mla_kv_norm_g,
           mla_w_uk, mla_w_uv, idx_k_norm_g, idx_k_norm_b, w_branch_gdn, w_branch_dsa, w_out,
           norm_ffn_g, router_w, router_b, exp_w_gate, exp_b_gate, exp_w_up, exp_b_up,
           exp_w_down, exp_b_down, norm_final_g):
    l = 0
    bsz, t, d = x.shape
    x1 = hybrid_mixer(x, norm_mix_g[l], w_in[l], gdn_conv_w[l], gdn_a_log[l], gdn_dt_bias[l],
                      gdn_norm_g[l], mla_kv_norm_g[l], mla_w_uk[l], mla_w_uv[l], idx_k_norm_g[l],
                      idx_k_norm_b[l], w_branch_gdn[l], w_branch_dsa[l], w_out[l])
    out = moe_final_pallas(x1, norm_ffn_g[l], router_w[l], router_b[l], exp_w_gate[l],
                           exp_b_gate[l], exp_w_up[l], exp_b_up[l], exp_w_down[l], exp_b_down[l], norm_final_g)
    return out.reshape(bsz, t, d)
```

```python
import functools
import math
import numpy as np
import jax
import jax.numpy as jnp
from jax import lax
from jax.experimental import pallas as pl
from jax.experimental.pallas import tpu as pltpu

D_MODEL = 2048
CHUNK = 64
CHUNK_SHIFT = CHUNK.bit_length() - 1
ROPE_THETA = 10000.0
NORM_EPS = 1e-6
GDN_K_HEADS = 16
GDN_V_HEADS = 32
GDN_HEAD_DIM = 128
GDN_QK_WIDTH = GDN_K_HEADS * GDN_HEAD_DIM
GDN_V_WIDTH = GDN_V_HEADS * GDN_HEAD_DIM
MLA_HEADS = 16
MLA_NOPE_DIM = 128
MLA_ROPE_DIM = 64
MLA_V_DIM = 128
MLA_KV_RANK = 512
MLA_Q_WIDTH = MLA_HEADS * (MLA_NOPE_DIM + MLA_ROPE_DIM)
MLA_O_WIDTH = MLA_HEADS * MLA_V_DIM
IDX_HEADS = 16
IDX_DIM = 64
IDX_ROPE_DIM = 32
TOPK_MAX = 256
Q_BLOCK = 128
N_EXPERTS = 32
TOP_K = 4
D_EXPERT = 2048
SWIGLU_LIMIT = 7.0
SWIGLU_ALPHA = 1.702
EXPERT_BLOCK = 256
LANES = 128
VMEM_LIMIT = 56 * 1024 * 1024
IN_SIZES = (GDN_QK_WIDTH, GDN_QK_WIDTH, GDN_V_WIDTH, GDN_V_WIDTH, GDN_V_HEADS, GDN_V_HEADS,
            MLA_Q_WIDTH, MLA_KV_RANK, MLA_ROPE_DIM, IDX_HEADS * IDX_DIM, IDX_DIM, IDX_HEADS,
            D_MODEL, D_MODEL)


PROJ_TM = 1024
PROJ_TN = 1536


def _inproj_kernel(x_ref, g_ref, w_ref, o_ref, h_scr):
    @pl.when(pl.program_id(1) == 0)
    def _():
        x = x_ref[...]
        h_scr[...] = (x * lax.rsqrt(jnp.mean(x * x, axis=-1, keepdims=True) + NORM_EPS)
                      * g_ref[...]).astype(jnp.bfloat16)

    o_ref[...] = jnp.dot(h_scr[...], w_ref[...], preferred_element_type=jnp.float32)


def _input_projection(x, norm_g, w):
    n, d = x.shape
    width = w.shape[1]
    tm, tn = PROJ_TM, PROJ_TN
    return pl.pallas_call(
        _inproj_kernel,
        grid=(n // tm, width // tn),
        in_specs=[pl.BlockSpec((tm, d), lambda i, j: (i, 0)),
                  pl.BlockSpec((1, d), lambda i, j: (0, 0)),
                  pl.BlockSpec((d, tn), lambda i, j: (0, j))],
        out_specs=pl.BlockSpec((tm, tn), lambda i, j: (i, j)),
        out_shape=jax.ShapeDtypeStruct((n, width), jnp.float32),
        scratch_shapes=[pltpu.VMEM((tm, d), jnp.bfloat16)],
        compiler_params=pltpu.CompilerParams(
            dimension_semantics=("arbitrary", "arbitrary"),
            vmem_limit_bytes=VMEM_LIMIT),
        name="input_projection",
    )(x, norm_g.astype(jnp.float32)[None, :], w)


def rope_tables(n, dim):
    inv = ROPE_THETA ** (-jnp.arange(0, dim, 2, dtype=jnp.float32) / dim)
    ang = jnp.arange(n, dtype=jnp.float32)[:, None] * inv[None, :]
    return jnp.cos(ang), jnp.sin(ang)


_IN_OFF = np.concatenate([[0], np.cumsum(IN_SIZES)]).tolist()
(_O_GQ, _O_GK, _O_GV, _O_GZ, _O_GA, _O_GB, _O_MQ, _O_CKV, _O_KPE, _O_IQ, _O_IK, _O_IW,
 _O_GATE_A, _O_GATE_B) = _IN_OFF[:-1]
P_QKV = 0
P_GZ = 8192
P_QN = 12288
P_GATE_A = 14336
P_GATE_B = 16384
P_QPE = 18432
P_IQ = 19456
P_CKV = 20480
P_KPE_IK = 20992
P_GAW = 21120
P_WIDTH = 21504


def _proj_column_order():
    cols = []
    cols += list(range(_O_GQ, _O_GZ))
    cols += list(range(_O_GZ, _O_GA))
    hd = MLA_NOPE_DIM + MLA_ROPE_DIM
    cols += [_O_MQ + h * hd + i for h in range(MLA_HEADS) for i in range(MLA_NOPE_DIM)]
    cols += list(range(_O_GATE_A, _O_GATE_A + 2 * D_MODEL))
    cols += [_O_MQ + h * hd + MLA_NOPE_DIM + i for h in range(MLA_HEADS) for i in range(MLA_ROPE_DIM)]
    cols += list(range(_O_IQ, _O_IK))
    cols += list(range(_O_CKV, _O_KPE))
    cols += list(range(_O_KPE, _O_IQ)) + list(range(_O_IK, _O_IW))
    cols += list(range(_O_GA, _O_MQ)) + list(range(_O_IW, _O_GATE_A))
    cols += [-1] * (P_WIDTH - len(cols))
    assert len(cols) == P_WIDTH
    return np.asarray(cols, np.int32)


_PROJ_COLS = _proj_column_order()


def _relayout_w_in(w_in):
    d = w_in.shape[0]
    w = w_in.astype(jnp.bfloat16)
    mq = w[:, _O_MQ:_O_CKV].reshape(d, MLA_HEADS, MLA_NOPE_DIM + MLA_ROPE_DIM)
    parts = [w[:, _O_GQ:_O_GA],
             mq[:, :, :MLA_NOPE_DIM].reshape(d, -1),
             w[:, _O_GATE_A:_O_GATE_A + 2 * D_MODEL],
             mq[:, :, MLA_NOPE_DIM:].reshape(d, -1),
             w[:, _O_IQ:_O_IK], w[:, _O_CKV:_O_KPE], w[:, _O_KPE:_O_IQ], w[:, _O_IK:_O_IW],
             w[:, _O_GA:_O_MQ], w[:, _O_IW:_O_GATE_A]]
    used = sum(p.shape[1] for p in parts)
    return jnp.concatenate(parts + [jnp.zeros((d, P_WIDTH - used), jnp.bfloat16)], axis=1)


NEG_BIG = -1e30
INT_MIN = -2 ** 31
PREP_TM = 256
IDX_TQ = 256
IDX_BAND = 128
ATT_HG = 8
ATT_TQ = 1024
ATT_TK = 512
QK_PAD = 256
ATT_HEAD_UNROLL = 2
V_PAD = 256


def _rope_tables_lane(t):
    cos_m, sin_m = rope_tables(t, MLA_ROPE_DIM)
    cos_i, sin_i = rope_tables(t, IDX_ROPE_DIM)
    one = jnp.ones((t, IDX_DIM - IDX_ROPE_DIM), jnp.float32)
    zero = jnp.zeros_like(one)
    cq = jnp.concatenate([cos_m, cos_m] * 2, axis=1)
    sq = jnp.concatenate([-sin_m, sin_m] * 2, axis=1)
    ci = jnp.concatenate([cos_i, cos_i, one] * 2, axis=1)
    si = jnp.concatenate([-sin_i, sin_i, zero] * 2, axis=1)
    cs = jnp.concatenate([cos_m, cos_m, cos_i, cos_i, one], axis=1)
    ss = jnp.concatenate([-sin_m, sin_m, -sin_i, sin_i, zero], axis=1)
    return jnp.concatenate([cq, sq, ci, si, cs, ss], axis=1)


def _dsa_prep_kernel(qn_ref, qpe_ref, iq_ref, ckv_ref, sm_ref, gw_ref, tab_ref,
                     kvg_ref, lng_ref, lnb_ref, wuk_ref, wuv_ref,
                     q_out, k_out, v_out, iq_out, ik_out, iw_out):
    f32, bf16 = jnp.float32, jnp.bfloat16
    tm = qn_ref.shape[0]
    lane = lax.broadcasted_iota(jnp.int32, (tm, LANES), 1)
    lo64 = lane < 64

    def swap32(x):
        return jnp.where((lane & 32) == 0, pltpu.roll(x, 96, 1), pltpu.roll(x, 32, 1))

    def swap16(x):
        return jnp.where((lane & 16) == 0, pltpu.roll(x, 112, 1), pltpu.roll(x, 16, 1))

    cq, sq = tab_ref[:, 0:128], tab_ref[:, 128:256]
    ci, si = tab_ref[:, 256:384], tab_ref[:, 384:512]
    cs, ss = tab_ref[:, 512:640], tab_ref[:, 640:768]
    scale = (MLA_NOPE_DIM + MLA_ROPE_DIM) ** -0.5 * math.log2(math.e)

    for j in range(MLA_HEADS // 2):
        t = qpe_ref[:, LANES * j:LANES * (j + 1)]
        r = (t * cq + swap32(t) * sq) * scale
        q_out[0, 2 * j, :, 128:256] = jnp.where(lo64, r, 0.0).astype(bf16)
        q_out[0, 2 * j + 1, :, 128:256] = jnp.where(lo64, pltpu.roll(r, 64, 1), 0.0).astype(bf16)
        t = iq_ref[:, LANES * j:LANES * (j + 1)]
        r = t * ci + swap16(t) * si
        iq_out[0, 2 * j] = jnp.where(lo64, r, 0.0).astype(bf16)
        iq_out[0, 2 * j + 1] = jnp.where(lo64, pltpu.roll(r, 64, 1), 0.0).astype(bf16)
    for h in range(MLA_HEADS):
        q_out[0, h, :, 0:128] = (qn_ref[:, LANES * h:LANES * (h + 1)] * scale).astype(bf16)

    c = ckv_ref[...]
    cn = c * lax.rsqrt(jnp.mean(c * c, axis=-1, keepdims=True) + NORM_EPS) * kvg_ref[...]
    cnb = cn.astype(bf16)
    kn = jnp.dot(cnb, wuk_ref[...], preferred_element_type=f32)
    vv = jnp.dot(cnb, wuv_ref[...], preferred_element_type=f32)

    s = sm_ref[...]
    hi64 = jnp.logical_not(lo64)
    mu = jnp.sum(jnp.where(hi64, s, 0.0), axis=-1, keepdims=True) * (1.0 / IDX_DIM)
    xc = jnp.where(hi64, s - mu, 0.0)
    var = jnp.sum(xc * xc, axis=-1, keepdims=True) * (1.0 / IDX_DIM)
    ln = xc * lax.rsqrt(var + NORM_EPS) * lng_ref[...] + lnb_ref[...]
    y = jnp.where(hi64, ln, s)
    r = y * cs + jnp.where(lo64, swap32(y), swap16(y)) * ss
    kpe_t = jnp.where(lo64, r, 0.0).T.astype(bf16)
    for h in range(MLA_HEADS):
        k_out[0, h, 0:128, :] = kn[:, LANES * h:LANES * (h + 1)].T.astype(bf16)
        k_out[0, h, 128:256, :] = kpe_t
        v_out[0, h, :, 0:128] = vv[:, LANES * h:LANES * (h + 1)].astype(bf16)
        v_out[0, h, :, 128:256] = jnp.ones((tm, LANES), bf16)
    ik_out[0] = jnp.where(lo64, pltpu.roll(r, 64, 1), 0.0).astype(bf16)
    iw_out[...] = gw_ref[...] * (IDX_HEADS ** -0.5 * IDX_DIM ** -0.5)


def _dsa_prep(proj, tab, kv_norm_g, idx_norm_g, idx_norm_b, w_uk, w_uv, bsz, t):
    tm = PREP_TM
    nt = t // tm
    zeros64 = jnp.zeros((64,), jnp.float32)
    lng = jnp.concatenate([zeros64, idx_norm_g.astype(jnp.float32)])[None, :]
    lnb = jnp.concatenate([zeros64, idx_norm_b.astype(jnp.float32)])[None, :]
    kvg = kv_norm_g.astype(jnp.float32)[None, :]

    def rows(w, off):
        return pl.BlockSpec((tm, w), lambda b, i: (b * nt + i, off // w))

    def const(shape):
        return pl.BlockSpec(shape, lambda b, i: (0,) * len(shape))

    def heads(w):
        return pl.BlockSpec((1, MLA_HEADS, tm, w), lambda b, i: (b, 0, i, 0))

    bf16 = jnp.bfloat16
    return pl.pallas_call(
        _dsa_prep_kernel,
        grid=(bsz, nt),
        in_specs=[rows(2048, P_QN), rows(1024, P_QPE), rows(1024, P_IQ), rows(512, P_CKV),
                  rows(LANES, P_KPE_IK), rows(LANES, P_GAW),
                  pl.BlockSpec((tm, 768), lambda b, i: (i, 0)),
                  const((1, MLA_KV_RANK)), const((1, LANES)), const((1, LANES)),
                  const((MLA_KV_RANK, MLA_HEADS * MLA_NOPE_DIM)), const((MLA_KV_RANK, MLA_O_WIDTH))],
        out_specs=[heads(QK_PAD), pl.BlockSpec((1, MLA_HEADS, QK_PAD, tm), lambda b, i: (b, 0, 0, i)),
                   heads(V_PAD), heads(LANES),
                   pl.BlockSpec((1, tm, LANES), lambda b, i: (b, i, 0)),
                   pl.BlockSpec((tm, LANES), lambda b, i: (b * nt + i, 0))],
        out_shape=[jax.ShapeDtypeStruct((bsz, MLA_HEADS, t, QK_PAD), bf16),
                   jax.ShapeDtypeStruct((bsz, MLA_HEADS, QK_PAD, t), bf16),
                   jax.ShapeDtypeStruct((bsz, MLA_HEADS, t, V_PAD), bf16),
                   jax.ShapeDtypeStruct((bsz, IDX_HEADS, t, LANES), bf16),
                   jax.ShapeDtypeStruct((bsz, t, LANES), bf16),
                   jax.ShapeDtypeStruct((bsz * t, LANES), jnp.float32)],
        compiler_params=pltpu.CompilerParams(
            dimension_semantics=("arbitrary", "arbitrary"), vmem_limit_bytes=VMEM_LIMIT),
        name="dsa_prep",
    )(proj, proj, proj, proj, proj, proj, tab, kvg, lng, lnb,
      w_uk.astype(bf16), w_uv.astype(bf16))


def _indexer_kernel(iq_ref, ik_ref, iw_ref, bias_ref, key_scr, tie_scr, *, n_keep):
    f32, i32 = jnp.float32, jnp.int32
    tq = iq_ref.shape[2]
    kc = key_scr.shape[2]
    nch_total = key_scr.shape[0]
    qi = pl.program_id(1)
    nch = ((qi + 1) * tq + kc - 1) // kc
    q_chunk = (lax.broadcasted_iota(i32, (tq, kc), 0) + qi * tq) >> 6
    col0 = lax.broadcasted_iota(i32, (tq, kc), 1)
    n_sub = kc // LANES

    def score_chunk(c, carry):
        ik = ik_ref[0, pl.ds(pl.multiple_of(c * kc, kc), kc), :]
        acc = jnp.zeros((tq, kc), f32)
        for h in range(IDX_HEADS):
            s = lax.dot_general(iq_ref[0, h], ik, (((1,), (1,)), ((), ())), preferred_element_type=f32)
            acc = acc + iw_ref[:, 64 + h:65 + h] * jnp.maximum(s, 0.0)
        bits = pltpu.bitcast(acc, i32)
        key = bits ^ ((bits >> 31) & 0x7FFFFFFF)
        adm = ((col0 + c * kc) >> CHUNK_SHIFT) <= q_chunk
        key_scr[c] = jnp.where(adm, key, INT_MIN)
        return carry

    lax.fori_loop(0, nch, score_chunk, 0)

    def lane_fold(x):
        part = x[:, 0:LANES]
        for j in range(1, n_sub):
            part = part + x[:, LANES * j:LANES * (j + 1)]
        return part

    def rep(x):
        return jnp.concatenate([x] * n_sub, axis=1)

    def count(pred_fn):
        def body(c, acc):
            return acc + lane_fold(jnp.where(pred_fn(key_scr[c], c), 1.0, 0.0))
        acc = lax.fori_loop(0, nch, body, jnp.zeros((tq, LANES), f32))
        return jnp.broadcast_to(jnp.sum(acc, axis=1, keepdims=True), (tq, LANES))

    def count_ge(cand):
        accs = []
        for r0 in range(0, tq, IDX_BAND):
            cand_r = cand[r0:r0 + IDX_BAND]

            def body(c, acc, r0=r0, cand_r=cand_r):
                k = key_scr[c, r0:r0 + IDX_BAND, :]
                for j in range(n_sub):
                    acc = acc + jnp.where(k[:, LANES * j:LANES * (j + 1)] >= cand_r, 1.0, 0.0)
                return acc

            accs.append(lax.fori_loop(0, nch, body, jnp.zeros((IDX_BAND, LANES), f32)))
        acc = jnp.concatenate(accs, axis=0)
        return jnp.broadcast_to(jnp.sum(acc, axis=1, keepdims=True), (tq, LANES))

    keep = float(n_keep)
    cnt = count(lambda k, c: k >= 0)
    thr = jnp.where(cnt >= keep, 0, INT_MIN).astype(i32)
    cnt_thr = jnp.where(cnt >= keep, cnt, (nch * kc).astype(f32))

    def bit_cond(state):
        i, _, cnt_thr = state
        return jnp.logical_and(i < 31, jnp.max(jnp.abs(cnt_thr - keep)) > 0.0)

    def bit_pass(state):
        i, thr, cnt_thr = state
        cand = thr | jnp.left_shift(jnp.int32(1), 30 - i)
        cnt = count_ge(cand)
        take = cnt >= keep
        return i + 1, jnp.where(take, cand, thr), jnp.where(take, cnt, cnt_thr)

    _, thr, cnt_ge = lax.while_loop(bit_cond, bit_pass, (jnp.int32(0), thr, cnt_thr))
    thr_w = rep(thr)
    tie_scr[...] = jnp.full((tq, LANES), 2 ** 30, i32)

    @pl.when(jnp.max(cnt_ge) > keep)
    def _():
        cnt_gt = count(lambda k, c: k > thr_w)
        need = keep - cnt_gt

        def idx_pass(i, x):
            cand = x | jnp.left_shift(jnp.int32(1), idx_bits - 1 - i)
            cand_w = rep(cand)
            cnt = count(lambda k, c: (k == thr_w) & ((col0 + c * kc) < cand_w))
            return jnp.where(cnt < need, cand, x)

        idx_bits = (nch_total * kc - 1).bit_length()
        x = lax.fori_loop(0, idx_bits, idx_pass, jnp.zeros((tq, LANES), i32))
        tie_scr[...] = jnp.where(cnt_ge > keep, x, 2 ** 30)

    tie_w = rep(tie_scr[...])

    def write_chunk(c, carry):
        k = key_scr[c]
        sel = (k > thr_w) | ((k == thr_w) & ((col0 + c * kc) <= tie_w))
        sel = sel & (k != INT_MIN)
        bias_ref[0, c] = jnp.where(sel, 0.0, NEG_BIG)
        return carry

    lax.fori_loop(0, nch, write_chunk, 0)

    def fill_chunk(c, carry):
        bias_ref[0, c] = jnp.full((tq, kc), NEG_BIG, f32)
        return carry

    lax.fori_loop(nch, nch_total, fill_chunk, 0)


def _indexer(iq, ik, iw, bsz, t, n_keep):
    tq, kc = IDX_TQ, ATT_TK
    nq, nkc = t // tq, t // kc
    return pl.pallas_call(
        functools.partial(_indexer_kernel, n_keep=n_keep),
        grid=(bsz, nq),
        in_specs=[pl.BlockSpec((1, IDX_HEADS, tq, LANES), lambda b, i: (b, 0, i, 0)),
                  pl.BlockSpec((1, t, LANES), lambda b, i: (b, 0, 0)),
                  pl.BlockSpec((tq, LANES), lambda b, i: (b * nq + i, 0))],
        out_specs=pl.BlockSpec((1, nkc, tq, kc), lambda b, i: (b, 0, i, 0)),
        out_shape=jax.ShapeDtypeStruct((bsz, nkc, t, kc), jnp.float32),
        scratch_shapes=[pltpu.VMEM((nkc, tq, kc), jnp.int32), pltpu.VMEM((tq, LANES), jnp.int32)],
        compiler_params=pltpu.CompilerParams(
            dimension_semantics=("arbitrary", "arbitrary"), vmem_limit_bytes=VMEM_LIMIT),
        name="dsa_indexer",
    )(iq, ik, iw)


def _attn_kernel(q_ref, k_ref, v_ref, b_ref, o_ref, m_scr, acc_scr):
    f32 = jnp.float32
    tq, tk = b_ref.shape[2], b_ref.shape[3]
    qi, ki = pl.program_id(2), pl.program_id(3)
    last = ((qi + 1) * tq - 1) // tk
    n_heads = q_ref.shape[1]

    @pl.when(ki == 0)
    def _():
        m_scr[...] = jnp.full(m_scr.shape, NEG_BIG, f32)
        acc_scr[...] = jnp.zeros(acc_scr.shape, f32)

    @pl.when(ki <= last)
    def _():
        bias = b_ref[0, 0]

        def head_group(gi, carry):
            hs = [gi * ATT_HEAD_UNROLL + u for u in range(ATT_HEAD_UNROLL)]
            s = [jnp.dot(q_ref[0, h], k_ref[0, h], preferred_element_type=f32) + bias for h in hs]
            m_prev = [m_scr[h] for h in hs]
            m_new = [jnp.maximum(mp, jnp.max(x, axis=1, keepdims=True)) for mp, x in zip(m_prev, s)]
            p = [jnp.exp2((x - mn[:, 0:1]).astype(jnp.bfloat16)) for x, mn in zip(s, m_new)]
            pv = [jnp.dot(x, v_ref[0, h], preferred_element_type=f32) for x, h in zip(p, hs)]
            for u, h in enumerate(hs):
                alpha = jnp.exp2(m_prev[u] - m_new[u])
                acc_scr[h, :, 0:128] = alpha * acc_scr[h, :, 0:128] + pv[u][:, 0:128]
                acc_scr[h, :, 128:256] = alpha * acc_scr[h, :, 128:256] + pv[u][:, 128:256]
                m_scr[h] = m_new[u]
            return carry

        lax.fori_loop(0, n_heads // ATT_HEAD_UNROLL, head_group, 0)

    @pl.when(ki == last)
    def _():
        for h in range(n_heads):
            o_ref[0, :, LANES * h:LANES * (h + 1)] = (
                acc_scr[h, :, 0:128] / acc_scr[h, :, 128:256]).astype(o_ref.dtype)


def _masked_attention(q, k, v, bias, bsz, t):
    tq, tk, hg = ATT_TQ, ATT_TK, ATT_HG
    nq, nk = t // tq, t // tk

    def last(i):
        return ((i + 1) * tq - 1) // tk

    return pl.pallas_call(
        _attn_kernel,
        grid=(bsz, MLA_HEADS // hg, nq, nk),
        in_specs=[pl.BlockSpec((1, hg, tq, QK_PAD), lambda b, g, i, j: (b, g, i, 0)),
                  pl.BlockSpec((1, hg, QK_PAD, tk), lambda b, g, i, j: (b, g, 0, jnp.minimum(j, last(i)))),
                  pl.BlockSpec((1, hg, tk, V_PAD), lambda b, g, i, j: (b, g, jnp.minimum(j, last(i)), 0)),
                  pl.BlockSpec((1, 1, tq, tk), lambda b, g, i, j: (b, jnp.minimum(j, last(i)), i, 0))],
        out_specs=pl.BlockSpec((1, tq, hg * MLA_V_DIM), lambda b, g, i, j: (b, i, g)),
        out_shape=jax.ShapeDtypeStruct((bsz, t, MLA_O_WIDTH), jnp.bfloat16),
        scratch_shapes=[pltpu.VMEM((hg, tq, LANES), jnp.float32),
                        pltpu.VMEM((hg, tq, V_PAD), jnp.float32)],
        compiler_params=pltpu.CompilerParams(
            dimension_semantics=("arbitrary", "arbitrary", "arbitrary", "arbitrary"),
            vmem_limit_bytes=VMEM_LIMIT),
        name="dsa_attention",
    )(q, k, v, bias)


def dsa_branch_pallas(proj, kv_norm_g, w_uk, w_uv, idx_norm_g, idx_norm_b, bsz, t):
    n_keep = min(TOPK_MAX, t // 4)
    tab = _rope_tables_lane(t)
    q, k, v, iq, ik, iw = _dsa_prep(proj, tab, kv_norm_g, idx_norm_g, idx_norm_b, w_uk, w_uv, bsz, t)
    bias = _indexer(iq, ik, iw, bsz, t, n_keep)
    o = _masked_attention(q, k, v, bias, bsz, t)
    return o.reshape(bsz * t, MLA_O_WIDTH)


GDN_TB = 256
GDN_HG = 16
GDN_REP = GDN_V_HEADS // GDN_K_HEADS


def _gdn_kernel(q_ref, k_ref, v_ref, z_ref, ab_ref, cwq_ref, cwk_ref, cwv_ref, alog_ref, dtb_ref, ng_ref,
                o_ref, tq_scr, tk_scr, tv_scr, state_scr):
    f32, bf16 = jnp.float32, jnp.bfloat16
    tb = q_ref.shape[0]
    g = pl.program_id(1)
    step = pl.program_id(2)
    n_chunk = tb // CHUNK
    kh_n = GDN_HG // GDN_REP

    @pl.when(step == 0)
    def _():
        tq_scr[...] = jnp.zeros(tq_scr.shape, f32)
        tk_scr[...] = jnp.zeros(tk_scr.shape, f32)
        tv_scr[...] = jnp.zeros(tv_scr.shape, f32)
        state_scr[...] = jnp.zeros(state_scr.shape, f32)

    def conv_silu(x_ref, w_ref, buf):
        x = x_ref[...]
        buf[8:8 + tb, :] = x
        w = w_ref[...]
        y = x * w[3:4, :]
        for s in (1, 2, 3):
            y = y + buf[8 - s:8 - s + tb, :] * w[3 - s:4 - s, :]
        buf[0:8, :] = x[tb - 8:tb]
        return y * (1.0 / (1.0 + jnp.exp(-y)))

    qc = conv_silu(q_ref, cwq_ref, tq_scr)
    kc = conv_silu(k_ref, cwk_ref, tk_scr)
    vc = conv_silu(v_ref, cwv_ref, tv_scr)

    def l2n(x):
        return x * lax.rsqrt(jnp.sum(x * x, axis=-1, keepdims=True) + NORM_EPS)

    qn = [l2n(qc[:, LANES * j:LANES * (j + 1)]) * (GDN_HEAD_DIM ** -0.5) for j in range(kh_n)]
    kn = [l2n(kc[:, LANES * j:LANES * (j + 1)]) for j in range(kh_n)]

    ab = pltpu.roll(ab_ref[...], (LANES - GDN_HG * g) % LANES, 1)
    xg = ab + dtb_ref[0]
    softplus = jnp.maximum(xg, 0.0) + jnp.log1p(jnp.exp(-jnp.abs(xg)))
    g_all = -jnp.exp(alog_ref[0]) * softplus
    beta_all = 1.0 / (1.0 + jnp.exp(-ab))
    rin = lax.broadcasted_iota(jnp.int32, (tb, LANES), 0) & (CHUNK - 1)
    cum = g_all
    for s in (1, 2, 4, 8, 16, 32):
        cum = cum + jnp.where(rin >= s, pltpu.roll(cum, s, 0), 0.0)

    ii = lax.broadcasted_iota(jnp.int32, (tb, tb), 0)
    jj = lax.broadcasted_iota(jnp.int32, (tb, tb), 1)
    same = (ii >> CHUNK_SHIFT) == (jj >> CHUNK_SHIFT)
    incl = same & (ii >= jj)
    strict = same & (ii > jj)
    eye = jnp.where(ii == jj, 1.0, 0.0)
    nt_dims = (((1,), (1,)), ((), ()))
    tn_dims = (((0,), (0,)), ((), ()))
    ng = ng_ref[...]
    heads = range(GDN_HG)

    cum_t = cum.T
    kb = [x.astype(bf16) for x in kn]
    qb = [x.astype(bf16) for x in qn]
    kk = [lax.dot_general(kb[j], kb[j], nt_dims, preferred_element_type=f32) for j in range(kh_n)]
    qk = [lax.dot_general(qb[j], kb[j], nt_dims, preferred_element_type=f32) for j in range(kh_n)]
    gcol = [cum[:, hh:hh + 1] for hh in heads]
    bcol = [beta_all[:, 32 + hh:33 + hh] for hh in heads]
    dec = [jnp.where(incl, jnp.exp(jnp.where(incl, gcol[hh] - cum_t[hh:hh + 1, :], 0.0)), 0.0) for hh in heads]
    pw = [jnp.where(strict, -(bcol[hh] * kk[hh // GDN_REP] * dec[hh]), 0.0) for hh in heads]
    tmat = [eye + pw[hh] for hh in heads]
    pw = [jnp.dot(pw[hh].astype(bf16), pw[hh].astype(bf16), preferred_element_type=f32) for hh in heads]
    for _ in range(4):
        both = [jnp.dot(jnp.concatenate([tmat[hh], pw[hh]], axis=0).astype(bf16), pw[hh].astype(bf16),
                        preferred_element_type=f32) for hh in heads]
        tmat = [tmat[hh] + both[hh][0:tb] for hh in heads]
        pw = [both[hh][tb:2 * tb] for hh in heads]
    tmat = [tmat[hh] + jnp.dot(tmat[hh].astype(bf16), pw[hh].astype(bf16), preferred_element_type=f32)
            for hh in heads]
    eg = [jnp.exp(gcol[hh]) for hh in heads]
    uw = [jnp.dot(tmat[hh].astype(bf16),
                  jnp.concatenate([bcol[hh] * vc[:, LANES * hh:LANES * (hh + 1)],
                                   (bcol[hh] * eg[hh]) * kn[hh // GDN_REP]], axis=1).astype(bf16),
                  preferred_element_type=f32) for hh in heads]
    qkd = [qk[hh // GDN_REP] * dec[hh] for hh in heads]
    qd = [(qn[hh // GDN_REP] * eg[hh]).astype(bf16) for hh in heads]
    states = [state_scr[hh] for hh in heads]

    for c in range(n_chunk):
        r0 = c * CHUNK
        sb = [states[hh].astype(bf16) for hh in heads]
        ws = [jnp.dot(jnp.concatenate([uw[hh][r0:r0 + CHUNK, LANES:2 * LANES].astype(bf16),
                                       qd[hh][r0:r0 + CHUNK]], axis=0), sb[hh],
                      preferred_element_type=f32) for hh in heads]
        vb = [(uw[hh][r0:r0 + CHUNK, 0:LANES] - ws[hh][0:CHUNK]).astype(bf16) for hh in heads]
        g_last = [cum[r0 + CHUNK - 1:r0 + CHUNK, hh:hh + 1] for hh in heads]
        kd = [(kn[hh // GDN_REP][r0:r0 + CHUNK] * jnp.exp(g_last[hh] - gcol[hh][r0:r0 + CHUNK])).astype(bf16)
              for hh in heads]
        o = [ws[hh][CHUNK:2 * CHUNK] + jnp.dot(qkd[hh][r0:r0 + CHUNK, r0:r0 + CHUNK].astype(bf16), vb[hh],
                                               preferred_element_type=f32) for hh in heads]
        states = [states[hh] * jnp.exp(g_last[hh])
                  + lax.dot_general(kd[hh], vb[hh], tn_dims, preferred_element_type=f32) for hh in heads]
        for hh in heads:
            on = o[hh] * lax.rsqrt(jnp.mean(o[hh] * o[hh], axis=-1, keepdims=True) + NORM_EPS) * ng
            zz = z_ref[r0:r0 + CHUNK, LANES * hh:LANES * (hh + 1)]
            o_ref[r0:r0 + CHUNK, LANES * hh:LANES * (hh + 1)] = (
                on * (zz * (1.0 / (1.0 + jnp.exp(-zz))))).astype(o_ref.dtype)

    for hh in range(GDN_HG):
        state_scr[hh] = states[hh]


def _gdn_branch_pallas(proj, conv_w, a_log, dt_bias, norm_g, bsz, t):
    tb = GDN_TB
    nt = t // tb
    n_groups = GDN_V_HEADS // GDN_HG
    qw = GDN_HG // GDN_REP * GDN_HEAD_DIM
    vw = GDN_HG * GDN_HEAD_DIM
    f32 = jnp.float32

    def per_group(x):
        x = x.astype(f32).reshape(n_groups, 1, GDN_HG)
        return jnp.pad(x, ((0, 0), (0, 0), (0, LANES - GDN_HG)))

    def rows(w, off):
        return pl.BlockSpec((tb, w), lambda b, g, s: (b * nt + s, off // w + g))

    def convw(w, off):
        return pl.BlockSpec((4, w), lambda b, g, s: (0, off // w + g))

    grp = pl.BlockSpec((1, 1, LANES), lambda b, g, s: (g, 0, 0))
    conv_w = conv_w.astype(f32)
    return pl.pallas_call(
        _gdn_kernel,
        grid=(bsz, n_groups, nt),
        in_specs=[rows(qw, P_QKV), rows(qw, P_QKV + GDN_QK_WIDTH), rows(vw, P_QKV + 2 * GDN_QK_WIDTH),
                  rows(vw, P_GZ),
                  pl.BlockSpec((tb, LANES), lambda b, g, s: (b * nt + s, P_GAW // LANES)),
                  convw(qw, 0), convw(qw, GDN_QK_WIDTH), convw(vw, 2 * GDN_QK_WIDTH),
                  grp, grp,
                  pl.BlockSpec((1, LANES), lambda b, g, s: (0, 0))],
        out_specs=pl.BlockSpec((tb, vw), lambda b, g, s: (b * nt + s, g)),
        out_shape=jax.ShapeDtypeStruct((bsz * t, GDN_V_WIDTH), jnp.bfloat16),
        scratch_shapes=[pltpu.VMEM((tb + 8, qw), f32), pltpu.VMEM((tb + 8, qw), f32), pltpu.VMEM((tb + 8, vw), f32),
                        pltpu.VMEM((GDN_HG, GDN_HEAD_DIM, GDN_HEAD_DIM), f32)],
        compiler_params=pltpu.CompilerParams(
            dimension_semantics=("arbitrary", "arbitrary", "arbitrary"),
            vmem_limit_bytes=VMEM_LIMIT),
        name="gdn_branch",
    )(proj, proj, proj, proj, proj, conv_w, conv_w, conv_w,
      per_group(a_log), per_group(dt_bias), norm_g.astype(f32)[None, :])


MERGE_TM = 512
MERGE_TN = 512


def _merge_kernel(oa_ref, ob_ref, wa_ref, wb_ref, ga_ref, gb_ref, o_ref):
    f32 = jnp.float32
    ua = jnp.dot(oa_ref[...], wa_ref[...], preferred_element_type=f32)
    ub = jnp.dot(ob_ref[...], wb_ref[...], preferred_element_type=f32)
    sa = 1.0 / (1.0 + jnp.exp(-ga_ref[...]))
    sb = 1.0 / (1.0 + jnp.exp(-gb_ref[...]))
    o_ref[...] = (sa * ua + sb * ub).astype(o_ref.dtype)


def _outproj_kernel(m_ref, w_ref, x_ref, o_ref):
    o_ref[...] = x_ref[...] + jnp.dot(m_ref[...], w_ref[...], preferred_element_type=jnp.float32)


def _merge_and_project(x, proj, o_a, o_b, w_branch_gdn, w_branch_dsa, w_out):
    n, d = x.shape
    tm, tn = MERGE_TM, MERGE_TN
    bf16 = jnp.bfloat16
    params = pltpu.CompilerParams(dimension_semantics=("arbitrary", "arbitrary"), vmem_limit_bytes=VMEM_LIMIT)
    merged = pl.pallas_call(
        _merge_kernel,
        grid=(d // tn, n // tm),
        in_specs=[pl.BlockSpec((tm, o_a.shape[1]), lambda j, i: (i, 0)),
                  pl.BlockSpec((tm, o_b.shape[1]), lambda j, i: (i, 0)),
                  pl.BlockSpec((o_a.shape[1], tn), lambda j, i: (0, j)),
                  pl.BlockSpec((o_b.shape[1], tn), lambda j, i: (0, j)),
                  pl.BlockSpec((tm, tn), lambda j, i: (i, P_GATE_A // tn + j)),
                  pl.BlockSpec((tm, tn), lambda j, i: (i, P_GATE_B // tn + j))],
        out_specs=pl.BlockSpec((tm, tn), lambda j, i: (i, j)),
        out_shape=jax.ShapeDtypeStruct((n, d), bf16),
        compiler_params=params,
        name="branch_merge",
    )(o_a, o_b, w_branch_gdn.astype(bf16), w_branch_dsa.astype(bf16), proj, proj)
    return pl.pallas_call(
        _outproj_kernel,
        grid=(n // tm,),
        in_specs=[pl.BlockSpec((tm, d), lambda i: (i, 0)),
                  pl.BlockSpec((d, d), lambda i: (0, 0)),
                  pl.BlockSpec((tm, d), lambda i: (i, 0))],
        out_specs=pl.BlockSpec((tm, d), lambda i: (i, 0)),
        out_shape=jax.ShapeDtypeStruct((n, d), jnp.float32),
        compiler_params=pltpu.CompilerParams(dimension_semantics=("arbitrary",), vmem_limit_bytes=VMEM_LIMIT),
        name="out_projection",
    )(merged, w_out.astype(bf16), x)


def hybrid_mixer(x, norm_g, w_in, conv_w, a_log, dt_bias, gdn_norm_g, kv_norm_g, w_uk, w_uv,
                 idx_norm_g, idx_norm_b, w_branch_gdn, w_branch_dsa, w_out):
    bsz, t, d = x.shape
    n = bsz * t
    x = x.reshape(n, d)
    proj = _input_projection(x, norm_g, _relayout_w_in(w_in))
    o_a = _gdn_branch_pallas(proj, conv_w, a_log, dt_bias, gdn_norm_g, bsz, t)
    o_b = dsa_branch_pallas(proj, kv_norm_g, w_uk, w_uv, idx_norm_g, idx_norm_b, bsz, t)
    return _merge_and_project(x, proj, o_a, o_b, w_branch_gdn, w_branch_dsa, w_out)


ROUTER_TM = 512
MOE_BLK = 256
MOE_TN = 1024
COMBINE_TM = 128
DISPATCH_TM = 256


def _router_kernel(x_ref, g_ref, rw_ref, rb_ref, meta_ref, cnt_ref, carry_scr):
    f32 = jnp.float32
    tm = x_ref.shape[0]

    @pl.when(pl.program_id(0) == 0)
    def _():
        carry_scr[...] = jnp.zeros(carry_scr.shape, f32)

    x = x_ref[...]
    h = (x * lax.rsqrt(jnp.mean(x * x, axis=-1, keepdims=True) + NORM_EPS) * g_ref[...]).astype(jnp.bfloat16)
    logits = jnp.dot(h, rw_ref[...], preferred_element_type=f32) + rb_ref[...]
    lane = lax.broadcasted_iota(jnp.int32, (tm, LANES), 1).astype(f32)
    vals = logits
    hot_sum = jnp.zeros((tm, LANES), f32)
    tops, ids, hots = [], [], []
    for _ in range(TOP_K):
        m = jnp.max(vals, axis=1, keepdims=True)
        idx = jnp.min(jnp.where(vals == m, lane, float(LANES)), axis=1, keepdims=True)
        hot = lane == idx
        tops.append(m)
        ids.append(idx)
        hots.append(hot)
        vals = jnp.where(hot, -jnp.inf, vals)
        hot_sum = hot_sum + jnp.where(hot, 1.0, 0.0)
    exps = [jnp.exp(v - tops[0]) for v in tops]
    inv = 1.0 / (exps[0] + exps[1] + exps[2] + exps[3])
    ri = lax.broadcasted_iota(jnp.int32, (tm, tm), 0)
    ci = lax.broadcasted_iota(jnp.int32, (tm, tm), 1)
    lower = jnp.where(ri > ci, 1.0, 0.0).astype(jnp.bfloat16)
    base = carry_scr[...] + jnp.dot(lower, hot_sum.astype(jnp.bfloat16), preferred_element_type=f32)
    meta = jnp.zeros((tm, LANES), f32)
    for k in range(TOP_K):
        rank = jnp.sum(jnp.where(hots[k], base, 0.0), axis=1, keepdims=True)
        meta = jnp.where(lane == float(k), ids[k], meta)
        meta = jnp.where(lane == float(TOP_K + k), exps[k] * inv, meta)
        meta = jnp.where(lane == float(2 * TOP_K + k), rank, meta)
    meta_ref[...] = meta
    carry_scr[...] = carry_scr[...] + jnp.sum(hot_sum, axis=0, keepdims=True)
    cnt_ref[...] = carry_scr[...]


def _router(x1, norm_g, router_w, router_b):
    n, d = x1.shape
    tm = ROUTER_TM
    rw = jnp.pad(router_w.astype(jnp.bfloat16), ((0, 0), (0, LANES - N_EXPERTS)))
    rb = jnp.concatenate([router_b.astype(jnp.float32),
                          jnp.full((LANES - N_EXPERTS,), -jnp.inf, jnp.float32)])[None, :]
    return pl.pallas_call(
        _router_kernel,
        grid=(n // tm,),
        in_specs=[pl.BlockSpec((tm, d), lambda i: (i, 0)),
                  pl.BlockSpec((1, d), lambda i: (0, 0)),
                  pl.BlockSpec((d, LANES), lambda i: (0, 0)),
                  pl.BlockSpec((1, LANES), lambda i: (0, 0))],
        out_specs=[pl.BlockSpec((tm, LANES), lambda i: (i, 0)),
                   pl.BlockSpec((1, LANES), lambda i: (0, 0))],
        out_shape=[jax.ShapeDtypeStruct((n, LANES), jnp.float32),
                   jax.ShapeDtypeStruct((1, LANES), jnp.float32)],
        scratch_shapes=[pltpu.VMEM((1, LANES), jnp.float32)],
        compiler_params=pltpu.CompilerParams(dimension_semantics=("arbitrary",), vmem_limit_bytes=VMEM_LIMIT),
        name="moe_router",
    )(x1, norm_g.astype(jnp.float32)[None, :], rw, rb)


def _dispatch_kernel(slot_ref, x_ref, g_ref, xs_in_ref, xs_ref, h_scr, sem):
    del xs_in_ref
    tm = x_ref.shape[0]
    x = x_ref[...]
    h_scr[...] = x * lax.rsqrt(jnp.mean(x * x, axis=-1, keepdims=True) + NORM_EPS) * g_ref[...]

    def copy(r, k):
        return pltpu.make_async_copy(h_scr.at[pl.ds(r, 1)], xs_ref.at[pl.ds(slot_ref[TOP_K * r + k], 1)], sem)

    def start_row(r, carry):
        for k in range(TOP_K):
            copy(r, k).start()
        return carry

    lax.fori_loop(0, tm, start_row, 0)
    for _ in range(TOP_K):
        pltpu.make_async_copy(h_scr, xs_ref.at[pl.ds(0, tm)], sem).wait()


def _dispatch(slot, x1, norm_g, n_rows):
    n, d = x1.shape
    tm = DISPATCH_TM
    xs0 = jnp.zeros((n_rows, d), jnp.float32)
    return pl.pallas_call(
        _dispatch_kernel,
        grid=(n // tm,),
        in_specs=[pl.BlockSpec((tm * TOP_K,), lambda i: (i,), memory_space=pltpu.SMEM),
                  pl.BlockSpec((tm, d), lambda i: (i, 0)),
                  pl.BlockSpec((1, d), lambda i: (0, 0)),
                  pl.BlockSpec(memory_space=pl.ANY)],
        out_specs=pl.BlockSpec(memory_space=pl.ANY),
        out_shape=jax.ShapeDtypeStruct((n_rows, d), jnp.float32),
        scratch_shapes=[pltpu.VMEM((tm, d), jnp.float32), pltpu.SemaphoreType.DMA(())],
        input_output_aliases={3: 0},
        compiler_params=pltpu.CompilerParams(dimension_semantics=("arbitrary",), vmem_limit_bytes=VMEM_LIMIT),
        name="moe_dispatch",
    )(slot, x1, norm_g.astype(jnp.float32)[None, :], xs0)


def _expert_changed(be_ref, j):
    return jnp.logical_or(j == 0, be_ref[j] != be_ref[jnp.maximum(j - 1, 0)])


def _moe_up_kernel(be_ref, nb_ref, x_ref, wg_ref, wu_ref, bg_ref, bu_ref, o_ref, wg_scr, wu_scr):
    j = pl.program_id(1)

    @pl.when(jnp.logical_and(j < nb_ref[0], _expert_changed(be_ref, j)))
    def _():
        wg_scr[...] = wg_ref[0].astype(jnp.bfloat16)
        wu_scr[...] = wu_ref[0].astype(jnp.bfloat16)

    @pl.when(j < nb_ref[0])
    def _():
        x = x_ref[...].astype(jnp.bfloat16)
        gate = jnp.dot(x, wg_scr[...], preferred_element_type=jnp.float32) + bg_ref[0]
        up = jnp.dot(x, wu_scr[...], preferred_element_type=jnp.float32) + bu_ref[0]
        gate = jnp.minimum(gate, SWIGLU_LIMIT)
        up = jnp.clip(up, -SWIGLU_LIMIT, SWIGLU_LIMIT)
        act = (up + 1.0) * (gate * (1.0 / (1.0 + jnp.exp(-SWIGLU_ALPHA * gate))))
        o_ref[...] = act.astype(o_ref.dtype)

    @pl.when(j >= nb_ref[0])
    def _():
        o_ref[...] = jnp.zeros(o_ref.shape, o_ref.dtype)


def _moe_down_kernel(be_ref, nb_ref, a_ref, wd_ref, bd_ref, o_ref, wd_scr):
    j = pl.program_id(1)

    @pl.when(jnp.logical_and(j < nb_ref[0], _expert_changed(be_ref, j)))
    def _():
        wd_scr[...] = wd_ref[0].astype(jnp.bfloat16)

    @pl.when(j < nb_ref[0])
    def _():
        o_ref[...] = jnp.dot(a_ref[...], wd_scr[...], preferred_element_type=jnp.float32) + bd_ref[0]

    @pl.when(j >= nb_ref[0])
    def _():
        o_ref[...] = jnp.zeros(o_ref.shape, o_ref.dtype)


def _moe_experts(xs, block_expert, n_used, w_gate, b_gate, w_up, b_up, w_down, b_down):
    n_rows, d = xs.shape
    f = w_gate.shape[2]
    blk, tn = MOE_BLK, MOE_TN
    n_blocks = n_rows // blk
    bf16 = jnp.bfloat16

    def wspec(k):
        return pl.BlockSpec((1, k, tn), lambda c, j, be, nb: (be[j], 0, c))

    bspec = pl.BlockSpec((1, 1, tn), lambda c, j, be, nb: (be[j], 0, c))
    rows_in = lambda k: pl.BlockSpec((blk, k), lambda c, j, be, nb: (j, 0))
    rows_out = pl.BlockSpec((blk, tn), lambda c, j, be, nb: (j, c))
    params = pltpu.CompilerParams(dimension_semantics=("arbitrary", "arbitrary"), vmem_limit_bytes=VMEM_LIMIT)
    act = pl.pallas_call(
        _moe_up_kernel,
        grid_spec=pltpu.PrefetchScalarGridSpec(
            num_scalar_prefetch=2, grid=(f // tn, n_blocks),
            in_specs=[rows_in(d), wspec(d), wspec(d), bspec, bspec],
            out_specs=rows_out,
            scratch_shapes=[pltpu.VMEM((d, tn), bf16), pltpu.VMEM((d, tn), bf16)]),
        out_shape=jax.ShapeDtypeStruct((n_rows, f), bf16),
        compiler_params=params,
        name="moe_up",
    )(block_expert, n_used, xs, w_gate.astype(jnp.float32), w_up.astype(jnp.float32),
      b_gate.astype(jnp.float32)[:, None, :], b_up.astype(jnp.float32)[:, None, :])
    return pl.pallas_call(
        _moe_down_kernel,
        grid_spec=pltpu.PrefetchScalarGridSpec(
            num_scalar_prefetch=2, grid=(d // tn, n_blocks),
            in_specs=[rows_in(f), wspec(f), bspec],
            out_specs=rows_out,
            scratch_shapes=[pltpu.VMEM((f, tn), bf16)]),
        out_shape=jax.ShapeDtypeStruct((n_rows, d), jnp.float32),
        compiler_params=params,
        name="moe_down",
    )(block_expert, n_used, act, w_down.astype(jnp.float32), b_down.astype(jnp.float32)[:, None, :])


def _combine_kernel(slot_ref, slot_next_ref, x1_ref, meta_ref, g_ref, ys_ref, o_ref, buf, sem):
    i = pl.program_id(0)
    n_steps = pl.num_programs(0)
    tm = x1_ref.shape[0]

    def copy(slots, r, k, half):
        return pltpu.make_async_copy(ys_ref.at[pl.ds(slots[TOP_K * r + k], 1)],
                                     buf.at[half, k, pl.ds(r, 1)], sem.at[half])

    def start_tile(slots, half):
        def body(r, carry):
            for k in range(TOP_K):
                copy(slots, r, k, half).start()
            return carry
        lax.fori_loop(0, tm, body, 0)

    @pl.when(i == 0)
    def _():
        start_tile(slot_ref, 0)

    @pl.when(i + 1 < n_steps)
    def _():
        start_tile(slot_next_ref, (i + 1) % 2)

    half = i % 2

    for k in range(TOP_K):
        pltpu.make_async_copy(ys_ref.at[pl.ds(0, tm)], buf.at[half, k], sem.at[half]).wait()
    acc = x1_ref[...]
    for k in range(TOP_K):
        acc = acc + meta_ref[:, TOP_K + k:TOP_K + k + 1] * buf[half, k]
    o_ref[...] = acc * lax.rsqrt(jnp.mean(acc * acc, axis=-1, keepdims=True) + NORM_EPS) * g_ref[...]


def _combine(slot, x1, meta, ys, norm_g):
    n, d = x1.shape
    tm = COMBINE_TM
    n_steps = n // tm
    return pl.pallas_call(
        _combine_kernel,
        grid=(n_steps,),
        in_specs=[pl.BlockSpec((tm * TOP_K,), lambda i: (i,), memory_space=pltpu.SMEM),
                  pl.BlockSpec((tm * TOP_K,), lambda i: (jnp.minimum(i + 1, n_steps - 1),),
                               memory_space=pltpu.SMEM),
                  pl.BlockSpec((tm, d), lambda i: (i, 0)),
                  pl.BlockSpec((tm, LANES), lambda i: (i, 0)),
                  pl.BlockSpec((1, d), lambda i: (0, 0)),
                  pl.BlockSpec(memory_space=pl.ANY)],
        out_specs=pl.BlockSpec((tm, d), lambda i: (i, 0)),
        out_shape=jax.ShapeDtypeStruct((n, d), jnp.float32),
        scratch_shapes=[pltpu.VMEM((2, TOP_K, tm, d), jnp.float32),
                        pltpu.SemaphoreType.DMA((2,))],
        compiler_params=pltpu.CompilerParams(dimension_semantics=("arbitrary",), vmem_limit_bytes=VMEM_LIMIT),
        name="moe_combine",
    )(slot, slot, x1, meta, norm_g.astype(jnp.float32)[None, :], ys)


def moe_final_pallas(x1, norm_ffn_g, router_w, router_b, w_gate, b_gate, w_up, b_up, w_down, b_down,
                     norm_final_g):
    n, d = x1.shape
    meta, cnt = _router(x1, norm_ffn_g, router_w, router_b)
    counts = cnt[0, :N_EXPERTS].astype(jnp.int32)
    padded = (counts + MOE_BLK - 1) // MOE_BLK * MOE_BLK
    pad_end = jnp.cumsum(padded)
    pad_start = pad_end - padded
    expert = meta[:, 0:TOP_K].astype(jnp.int32)
    rank = meta[:, 2 * TOP_K:3 * TOP_K].astype(jnp.int32)
    seg_start = jnp.sum(jnp.where(expert[..., None] == jnp.arange(N_EXPERTS, dtype=jnp.int32), pad_start, 0),
                        axis=-1)
    slot = (seg_start + rank).reshape(-1)
    n_blocks = n * TOP_K // MOE_BLK + N_EXPERTS
    block_start = jnp.arange(n_blocks, dtype=jnp.int32) * MOE_BLK
    block_expert = jnp.minimum(jnp.sum((pad_end[None, :] <= block_start[:, None]).astype(jnp.int32), axis=1),
                               N_EXPERTS - 1)
    n_used = (pad_end[-1:] // MOE_BLK).astype(jnp.int32)
    xs = _dispatch(slot, x1, norm_ffn_g, n_blocks * MOE_BLK)
    ys = _moe_experts(xs, block_expert, n_used, w_gate, b_gate, w_up, b_up, w_down, b_down)
    return _combine(slot, x1, meta, ys, norm_final_g)


def kernel(x, norm_mix_g, w_in, gdn_conv_w, gdn_a_log, gdn_dt_bias, gdn_norm_g, mla_kv_norm_g,
           mla_w_uk, mla_w_uv, idx_k_norm_g, idx_k_norm_b, w_branch_gdn, w_branch_dsa, w_out,
           norm_ffn_g, router_w, router_b, exp_w_gate, exp_b_gate, exp_w_up, exp_b_up,
           exp_w_down, exp_b_down, norm_final_g):
    l = 0
    bsz, t, d = x.shape
    x1 = hybrid_mixer(x, norm_mix_g[l], w_in[l], gdn_conv_w[l], gdn_a_log[l], gdn_dt_bias[l],
                      gdn_norm_g[l], mla_kv_norm_g[l], mla_w_uk[l], mla_w_uv[l], idx_k_norm_g[l],
                      idx_k_norm_b[l], w_branch_gdn[l], w_branch_dsa[l], w_out[l])
    out = moe_final_pallas(x1, norm_ffn_g[l], router_w[l], router_b[l], exp_w_gate[l],
                           exp_b_gate[l], exp_w_up[l], exp_b_up[l], exp_w_down[l], exp_b_down[l], norm_final_g)
    return out.reshape(bsz, t, d)
```

```python
import functools
import math
import numpy as np
import jax
import jax.numpy as jnp
from jax import lax
from jax.experimental import pallas as pl
from jax.experimental.pallas import tpu as pltpu

D_MODEL = 2048
CHUNK = 64
CHUNK_SHIFT = CHUNK.bit_length() - 1
ROPE_THETA = 10000.0
NORM_EPS = 1e-6
GDN_K_HEADS = 16
GDN_V_HEADS = 32
GDN_HEAD_DIM = 128
GDN_QK_WIDTH = GDN_K_HEADS * GDN_HEAD_DIM
GDN_V_WIDTH = GDN_V_HEADS * GDN_HEAD_DIM
MLA_HEADS = 16
MLA_NOPE_DIM = 128
MLA_ROPE_DIM = 64
MLA_V_DIM = 128
MLA_KV_RANK = 512
MLA_Q_WIDTH = MLA_HEADS * (MLA_NOPE_DIM + MLA_ROPE_DIM)
MLA_O_WIDTH = MLA_HEADS * MLA_V_DIM
IDX_HEADS = 16
IDX_DIM = 64
IDX_ROPE_DIM = 32
TOPK_MAX = 256
Q_BLOCK = 128
N_EXPERTS = 32
TOP_K = 4
D_EXPERT = 2048
SWIGLU_LIMIT = 7.0
SWIGLU_ALPHA = 1.702
EXPERT_BLOCK = 256
LANES = 128
VMEM_LIMIT = 56 * 1024 * 1024
IN_SIZES = (GDN_QK_WIDTH, GDN_QK_WIDTH, GDN_V_WIDTH, GDN_V_WIDTH, GDN_V_HEADS, GDN_V_HEADS,
            MLA_Q_WIDTH, MLA_KV_RANK, MLA_ROPE_DIM, IDX_HEADS * IDX_DIM, IDX_DIM, IDX_HEADS,
            D_MODEL, D_MODEL)


PROJ_TM = 1024
PROJ_TN = 1536


def _inproj_kernel(x_ref, g_ref, w_ref, o_ref, h_scr):
    @pl.when(pl.program_id(1) == 0)
    def _():
        x = x_ref[...]
        h_scr[...] = (x * lax.rsqrt(jnp.mean(x * x, axis=-1, keepdims=True) + NORM_EPS)
                      * g_ref[...]).astype(jnp.bfloat16)

    o_ref[...] = jnp.dot(h_scr[...], w_ref[...], preferred_element_type=jnp.float32)


def _input_projection(x, norm_g, w):
    n, d = x.shape
    width = w.shape[1]
    tm, tn = PROJ_TM, PROJ_TN
    return pl.pallas_call(
        _inproj_kernel,
        grid=(n // tm, width // tn),
        in_specs=[pl.BlockSpec((tm, d), lambda i, j: (i, 0)),
                  pl.BlockSpec((1, d), lambda i, j: (0, 0)),
                  pl.BlockSpec((d, tn), lambda i, j: (0, j))],
        out_specs=pl.BlockSpec((tm, tn), lambda i, j: (i, j)),
        out_shape=jax.ShapeDtypeStruct((n, width), jnp.float32),
        scratch_shapes=[pltpu.VMEM((tm, d), jnp.bfloat16)],
        compiler_params=pltpu.CompilerParams(
            dimension_semantics=("arbitrary", "arbitrary"),
            vmem_limit_bytes=VMEM_LIMIT),
        name="input_projection",
    )(x, norm_g.astype(jnp.float32)[None, :], w)


def rope_tables(n, dim):
    inv = ROPE_THETA ** (-jnp.arange(0, dim, 2, dtype=jnp.float32) / dim)
    ang = jnp.arange(n, dtype=jnp.float32)[:, None] * inv[None, :]
    return jnp.cos(ang), jnp.sin(ang)


_IN_OFF = np.concatenate([[0], np.cumsum(IN_SIZES)]).tolist()
(_O_GQ, _O_GK, _O_GV, _O_GZ, _O_GA, _O_GB, _O_MQ, _O_CKV, _O_KPE, _O_IQ, _O_IK, _O_IW,
 _O_GATE_A, _O_GATE_B) = _IN_OFF[:-1]
P_QKV = 0
P_GZ = 8192
P_QN = 12288
P_GATE_A = 14336
P_GATE_B = 16384
P_QPE = 18432
P_IQ = 19456
P_CKV = 20480
P_KPE_IK = 20992
P_GAW = 21120
P_WIDTH = 21504


def _proj_column_order():
    cols = []
    cols += list(range(_O_GQ, _O_GZ))
    cols += list(range(_O_GZ, _O_GA))
    hd = MLA_NOPE_DIM + MLA_ROPE_DIM
    cols += [_O_MQ + h * hd + i for h in range(MLA_HEADS) for i in range(MLA_NOPE_DIM)]
    cols += list(range(_O_GATE_A, _O_GATE_A + 2 * D_MODEL))
    cols += [_O_MQ + h * hd + MLA_NOPE_DIM + i for h in range(MLA_HEADS) for i in range(MLA_ROPE_DIM)]
    cols += list(range(_O_IQ, _O_IK))
    cols += list(range(_O_CKV, _O_KPE))
    cols += list(range(_O_KPE, _O_IQ)) + list(range(_O_IK, _O_IW))
    cols += list(range(_O_GA, _O_MQ)) + list(range(_O_IW, _O_GATE_A))
    cols += [-1] * (P_WIDTH - len(cols))
    assert len(cols) == P_WIDTH
    return np.asarray(cols, np.int32)


_PROJ_COLS = _proj_column_order()


def _relayout_w_in(w_in):
    d = w_in.shape[0]
    w = w_in.astype(jnp.bfloat16)
    mq = w[:, _O_MQ:_O_CKV].reshape(d, MLA_HEADS, MLA_NOPE_DIM + MLA_ROPE_DIM)
    parts = [w[:, _O_GQ:_O_GA],
             mq[:, :, :MLA_NOPE_DIM].reshape(d, -1),
             w[:, _O_GATE_A:_O_GATE_A + 2 * D_MODEL],
             mq[:, :, MLA_NOPE_DIM:].reshape(d, -1),
             w[:, _O_IQ:_O_IK], w[:, _O_CKV:_O_KPE], w[:, _O_KPE:_O_IQ], w[:, _O_IK:_O_IW],
             w[:, _O_GA:_O_MQ], w[:, _O_IW:_O_GATE_A]]
    used = sum(p.shape[1] for p in parts)
    return jnp.concatenate(parts + [jnp.zeros((d, P_WIDTH - used), jnp.bfloat16)], axis=1)


NEG_BIG = -1e30
INT_MIN = -2 ** 31
PREP_TM = 256
IDX_TQ = 256
IDX_BAND = 128
ATT_HG = 8
ATT_TQ = 1024
ATT_TK = 512
QK_PAD = 256
ATT_HEAD_UNROLL = 2
V_PAD = 256


def _rope_tables_lane(t):
    cos_m, sin_m = rope_tables(t, MLA_ROPE_DIM)
    cos_i, sin_i = rope_tables(t, IDX_ROPE_DIM)
    one = jnp.ones((t, IDX_DIM - IDX_ROPE_DIM), jnp.float32)
    zero = jnp.zeros_like(one)
    cq = jnp.concatenate([cos_m, cos_m] * 2, axis=1)
    sq = jnp.concatenate([-sin_m, sin_m] * 2, axis=1)
    ci = jnp.concatenate([cos_i, cos_i, one] * 2, axis=1)
    si = jnp.concatenate([-sin_i, sin_i, zero] * 2, axis=1)
    cs = jnp.concatenate([cos_m, cos_m, cos_i, cos_i, one], axis=1)
    ss = jnp.concatenate([-sin_m, sin_m, -sin_i, sin_i, zero], axis=1)
    return jnp.concatenate([cq, sq, ci, si, cs, ss], axis=1)


def _dsa_prep_kernel(qn_ref, qpe_ref, iq_ref, ckv_ref, sm_ref, gw_ref, tab_ref,
                     kvg_ref, lng_ref, lnb_ref, wuk_ref, wuv_ref,
                     q_out, k_out, v_out, iq_out, ik_out, iw_out):
    f32, bf16 = jnp.float32, jnp.bfloat16
    tm = qn_ref.shape[0]
    lane = lax.broadcasted_iota(jnp.int32, (tm, LANES), 1)
    lo64 = lane < 64

    def swap32(x):
        return jnp.where((lane & 32) == 0, pltpu.roll(x, 96, 1), pltpu.roll(x, 32, 1))

    def swap16(x):
        return jnp.where((lane & 16) == 0, pltpu.roll(x, 112, 1), pltpu.roll(x, 16, 1))

    cq, sq = tab_ref[:, 0:128], tab_ref[:, 128:256]
    ci, si = tab_ref[:, 256:384], tab_ref[:, 384:512]
    cs, ss = tab_ref[:, 512:640], tab_ref[:, 640:768]
    scale = (MLA_NOPE_DIM + MLA_ROPE_DIM) ** -0.5 * math.log2(math.e)

    for j in range(MLA_HEADS // 2):
        t = qpe_ref[:, LANES * j:LANES * (j + 1)]
        r = (t * cq + swap32(t) * sq) * scale
        q_out[0, 2 * j, :, 128:256] = jnp.where(lo64, r, 0.0).astype(bf16)
        q_out[0, 2 * j + 1, :, 128:256] = jnp.where(lo64, pltpu.roll(r, 64, 1), 0.0).astype(bf16)
        t = iq_ref[:, LANES * j:LANES * (j + 1)]
        r = t * ci + swap16(t) * si
        iq_out[0, 2 * j] = jnp.where(lo64, r, 0.0).astype(bf16)
        iq_out[0, 2 * j + 1] = jnp.where(lo64, pltpu.roll(r, 64, 1), 0.0).astype(bf16)
    for h in range(MLA_HEADS):
        q_out[0, h, :, 0:128] = (qn_ref[:, LANES * h:LANES * (h + 1)] * scale).astype(bf16)

    c = ckv_ref[...]
    cn = c * lax.rsqrt(jnp.mean(c * c, axis=-1, keepdims=True) + NORM_EPS) * kvg_ref[...]
    cnb = cn.astype(bf16)
    kn = jnp.dot(cnb, wuk_ref[...], preferred_element_type=f32)
    vv = jnp.dot(cnb, wuv_ref[...], preferred_element_type=f32)

    s = sm_ref[...]
    hi64 = jnp.logical_not(lo64)
    mu = jnp.sum(jnp.where(hi64, s, 0.0), axis=-1, keepdims=True) * (1.0 / IDX_DIM)
    xc = jnp.where(hi64, s - mu, 0.0)
    var = jnp.sum(xc * xc, axis=-1, keepdims=True) * (1.0 / IDX_DIM)
    ln = xc * lax.rsqrt(var + NORM_EPS) * lng_ref[...] + lnb_ref[...]
    y = jnp.where(hi64, ln, s)
    r = y * cs + jnp.where(lo64, swap32(y), swap16(y)) * ss
    kpe_t = jnp.where(lo64, r, 0.0).T.astype(bf16)
    for h in range(MLA_HEADS):
        k_out[0, h, 0:128, :] = kn[:, LANES * h:LANES * (h + 1)].T.astype(bf16)
        k_out[0, h, 128:256, :] = kpe_t
        v_out[0, h, :, 0:128] = vv[:, LANES * h:LANES * (h + 1)].astype(bf16)
        v_out[0, h, :, 128:256] = jnp.ones((tm, LANES), bf16)
    ik_out[0] = jnp.where(lo64, pltpu.roll(r, 64, 1), 0.0).astype(bf16)
    iw_out[...] = gw_ref[...] * (IDX_HEADS ** -0.5 * IDX_DIM ** -0.5)


def _dsa_prep(proj, tab, kv_norm_g, idx_norm_g, idx_norm_b, w_uk, w_uv, bsz, t):
    tm = PREP_TM
    nt = t // tm
    zeros64 = jnp.zeros((64,), jnp.float32)
    lng = jnp.concatenate([zeros64, idx_norm_g.astype(jnp.float32)])[None, :]
    lnb = jnp.concatenate([zeros64, idx_norm_b.astype(jnp.float32)])[None, :]
    kvg = kv_norm_g.astype(jnp.float32)[None, :]

    def rows(w, off):
        return pl.BlockSpec((tm, w), lambda b, i: (b * nt + i, off // w))

    def const(shape):
        return pl.BlockSpec(shape, lambda b, i: (0,) * len(shape))

    def heads(w):
        return pl.BlockSpec((1, MLA_HEADS, tm, w), lambda b, i: (b, 0, i, 0))

    bf16 = jnp.bfloat16
    return pl.pallas_call(
        _dsa_prep_kernel,
        grid=(bsz, nt),
        in_specs=[rows(2048, P_QN), rows(1024, P_QPE), rows(1024, P_IQ), rows(512, P_CKV),
                  rows(LANES, P_KPE_IK), rows(LANES, P_GAW),
                  pl.BlockSpec((tm, 768), lambda b, i: (i, 0)),
                  const((1, MLA_KV_RANK)), const((1, LANES)), const((1, LANES)),
                  const((MLA_KV_RANK, MLA_HEADS * MLA_NOPE_DIM)), const((MLA_KV_RANK, MLA_O_WIDTH))],
        out_specs=[heads(QK_PAD), pl.BlockSpec((1, MLA_HEADS, QK_PAD, tm), lambda b, i: (b, 0, 0, i)),
                   heads(V_PAD), heads(LANES),
                   pl.BlockSpec((1, tm, LANES), lambda b, i: (b, i, 0)),
                   pl.BlockSpec((tm, LANES), lambda b, i: (b * nt + i, 0))],
        out_shape=[jax.ShapeDtypeStruct((bsz, MLA_HEADS, t, QK_PAD), bf16),
                   jax.ShapeDtypeStruct((bsz, MLA_HEADS, QK_PAD, t), bf16),
                   jax.ShapeDtypeStruct((bsz, MLA_HEADS, t, V_PAD), bf16),
                   jax.ShapeDtypeStruct((bsz, IDX_HEADS, t, LANES), bf16),
                   jax.ShapeDtypeStruct((bsz, t, LANES), bf16),
                   jax.ShapeDtypeStruct((bsz * t, LANES), jnp.float32)],
        compiler_params=pltpu.CompilerParams(
            dimension_semantics=("arbitrary", "arbitrary"), vmem_limit_bytes=VMEM_LIMIT),
        name="dsa_prep",
    )(proj, proj, proj, proj, proj, proj, tab, kvg, lng, lnb,
      w_uk.astype(bf16), w_uv.astype(bf16))


def _indexer_kernel(iq_ref, ik_ref, iw_ref, bias_ref, key_scr, tie_scr, *, n_keep):
    f32, i32 = jnp.float32, jnp.int32
    tq = iq_ref.shape[2]
    kc = key_scr.shape[2]
    nch_total = key_scr.shape[0]
    qi = pl.program_id(1)
    nch = ((qi + 1) * tq + kc - 1) // kc
    q_chunk = (lax.broadcasted_iota(i32, (tq, kc), 0) + qi * tq) >> 6
    col0 = lax.broadcasted_iota(i32, (tq, kc), 1)
    n_sub = kc // LANES

    def score_chunk(c, carry):
        ik = ik_ref[0, pl.ds(pl.multiple_of(c * kc, kc), kc), :]
        acc = jnp.zeros((tq, kc), f32)
        for h in range(IDX_HEADS):
            s = lax.dot_general(iq_ref[0, h], ik, (((1,), (1,)), ((), ())), preferred_element_type=f32)
            acc = acc + iw_ref[:, 64 + h:65 + h] * jnp.maximum(s, 0.0)
        bits = pltpu.bitcast(acc, i32)
        key = bits ^ ((bits >> 31) & 0x7FFFFFFF)
        adm = ((col0 + c * kc) >> CHUNK_SHIFT) <= q_chunk
        key_scr[c] = jnp.where(adm, key, INT_MIN)
        return carry

    lax.fori_loop(0, nch, score_chunk, 0)

    def lane_fold(x):
        part = x[:, 0:LANES]
        for j in range(1, n_sub):
            part = part + x[:, LANES * j:LANES * (j + 1)]
        return part

    def rep(x):
        return jnp.concatenate([x] * n_sub, axis=1)

    def count(pred_fn):
        def body(c, acc):
            return acc + lane_fold(jnp.where(pred_fn(key_scr[c], c), 1.0, 0.0))
        acc = lax.fori_loop(0, nch, body, jnp.zeros((tq, LANES), f32))
        return jnp.broadcast_to(jnp.sum(acc, axis=1, keepdims=True), (tq, LANES))

    def count_ge(cand):
        accs = []
        for r0 in range(0, tq, IDX_BAND):
            cand_r = cand[r0:r0 + IDX_BAND]

            def body(c, acc, r0=r0, cand_r=cand_r):
                k = key_scr[c, r0:r0 + IDX_BAND, :]
                for j in range(n_sub):
                    acc = acc + jnp.where(k[:, LANES * j:LANES * (j + 1)] >= cand_r, 1.0, 0.0)
                return acc

            accs.append(lax.fori_loop(0, nch, body, jnp.zeros((IDX_BAND, LANES), f32)))
        acc = jnp.concatenate(accs, axis=0)
        return jnp.broadcast_to(jnp.sum(acc, axis=1, keepdims=True), (tq, LANES))

    keep = float(n_keep)
    cnt = count(lambda k, c: k >= 0)
    thr = jnp.where(cnt >= keep, 0, INT_MIN).astype(i32)
    cnt_thr = jnp.where(cnt >= keep, cnt, (nch * kc).astype(f32))

    def bit_cond(state):
        i, _, cnt_thr = state
        return jnp.logical_and(i < 31, jnp.max(jnp.abs(cnt_thr - keep)) > 0.0)

    def bit_pass(state):
        i, thr, cnt_thr = state
        cand = thr | jnp.left_shift(jnp.int32(1), 30 - i)
        cnt = count_ge(cand)
        take = cnt >= keep
        return i + 1, jnp.where(take, cand, thr), jnp.where(take, cnt, cnt_thr)

    _, thr, cnt_ge = lax.while_loop(bit_cond, bit_pass, (jnp.int32(0), thr, cnt_thr))
    thr_w = rep(thr)
    tie_scr[...] = jnp.full((tq, LANES), 2 ** 30, i32)

    @pl.when(jnp.max(cnt_ge) > keep)
    def _():
        cnt_gt = count(lambda k, c: k > thr_w)
        need = keep - cnt_gt

        def idx_pass(i, x):
            cand = x | jnp.left_shift(jnp.int32(1), idx_bits - 1 - i)
            cand_w = rep(cand)
            cnt = count(lambda k, c: (k == thr_w) & ((col0 + c * kc) < cand_w))
            return jnp.where(cnt < need, cand, x)

        idx_bits = (nch_total * kc - 1).bit_length()
        x = lax.fori_loop(0, idx_bits, idx_pass, jnp.zeros((tq, LANES), i32))
        tie_scr[...] = jnp.where(cnt_ge > keep, x, 2 ** 30)

    tie_w = rep(tie_scr[...])

    def write_chunk(c, carry):
        k = key_scr[c]
        sel = (k > thr_w) | ((k == thr_w) & ((col0 + c * kc) <= tie_w))
        sel = sel & (k != INT_MIN)
        bias_ref[0, c] = jnp.where(sel, 0.0, NEG_BIG)
        return carry

    lax.fori_loop(0, nch, write_chunk, 0)

    def fill_chunk(c, carry):
        bias_ref[0, c] = jnp.full((tq, kc), NEG_BIG, f32)
        return carry

    lax.fori_loop(nch, nch_total, fill_chunk, 0)


def _indexer(iq, ik, iw, bsz, t, n_keep):
    tq, kc = IDX_TQ, ATT_TK
    nq, nkc = t // tq, t // kc
    return pl.pallas_call(
        functools.partial(_indexer_kernel, n_keep=n_keep),
        grid=(bsz, nq),
        in_specs=[pl.BlockSpec((1, IDX_HEADS, tq, LANES), lambda b, i: (b, 0, i, 0)),
                  pl.BlockSpec((1, t, LANES), lambda b, i: (b, 0, 0)),
                  pl.BlockSpec((tq, LANES), lambda b, i: (b * nq + i, 0))],
        out_specs=pl.BlockSpec((1, nkc, tq, kc), lambda b, i: (b, 0, i, 0)),
        out_shape=jax.ShapeDtypeStruct((bsz, nkc, t, kc), jnp.float32),
        scratch_shapes=[pltpu.VMEM((nkc, tq, kc), jnp.int32), pltpu.VMEM((tq, LANES), jnp.int32)],
        compiler_params=pltpu.CompilerParams(
            dimension_semantics=("arbitrary", "arbitrary"), vmem_limit_bytes=VMEM_LIMIT),
        name="dsa_indexer",
    )(iq, ik, iw)


def _attn_kernel(q_ref, k_ref, v_ref, b_ref, o_ref, m_scr, acc_scr):
    f32 = jnp.float32
    tq, tk = b_ref.shape[2], b_ref.shape[3]
    qi, ki = pl.program_id(2), pl.program_id(3)
    last = ((qi + 1) * tq - 1) // tk
    n_heads = q_ref.shape[1]

    @pl.when(ki == 0)
    def _():
        m_scr[...] = jnp.full(m_scr.shape, NEG_BIG, f32)
        acc_scr[...] = jnp.zeros(acc_scr.shape, f32)

    @pl.when(ki <= last)
    def _():
        bias = b_ref[0, 0]

        def head_group(gi, carry):
            hs = [gi * ATT_HEAD_UNROLL + u for u in range(ATT_HEAD_UNROLL)]
            s = [jnp.dot(q_ref[0, h], k_ref[0, h], preferred_element_type=f32) + bias for h in hs]
            m_prev = [m_scr[h] for h in hs]
            m_new = [jnp.maximum(mp, jnp.max(x, axis=1, keepdims=True)) for mp, x in zip(m_prev, s)]
            p = [jnp.exp2((x - mn[:, 0:1]).astype(jnp.bfloat16)) for x, mn in zip(s, m_new)]
            pv = [jnp.dot(x, v_ref[0, h], preferred_element_type=f32) for x, h in zip(p, hs)]
            for u, h in enumerate(hs):
                alpha = jnp.exp2(m_prev[u] - m_new[u])
                acc_scr[h, :, 0:128] = alpha * acc_scr[h, :, 0:128] + pv[u][:, 0:128]
                acc_scr[h, :, 128:256] = alpha * acc_scr[h, :, 128:256] + pv[u][:, 128:256]
                m_scr[h] = m_new[u]
            return carry

        lax.fori_loop(0, n_heads // ATT_HEAD_UNROLL, head_group, 0)

    @pl.when(ki == last)
    def _():
        for h in range(n_heads):
            o_ref[0, :, LANES * h:LANES * (h + 1)] = (
                acc_scr[h, :, 0:128] / acc_scr[h, :, 128:256]).astype(o_ref.dtype)


def _masked_attention(q, k, v, bias, bsz, t):
    tq, tk, hg = ATT_TQ, ATT_TK, ATT_HG
    nq, nk = t // tq, t // tk

    def last(i):
        return ((i + 1) * tq - 1) // tk

    return pl.pallas_call(
        _attn_kernel,
        grid=(bsz, MLA_HEADS // hg, nq, nk),
        in_specs=[pl.BlockSpec((1, hg, tq, QK_PAD), lambda b, g, i, j: (b, g, i, 0)),
                  pl.BlockSpec((1, hg, QK_PAD, tk), lambda b, g, i, j: (b, g, 0, jnp.minimum(j, last(i)))),
                  pl.BlockSpec((1, hg, tk, V_PAD), lambda b, g, i, j: (b, g, jnp.minimum(j, last(i)), 0)),
                  pl.BlockSpec((1, 1, tq, tk), lambda b, g, i, j: (b, jnp.minimum(j, last(i)), i, 0))],
        out_specs=pl.BlockSpec((1, tq, hg * MLA_V_DIM), lambda b, g, i, j: (b, i, g)),
        out_shape=jax.ShapeDtypeStruct((bsz, t, MLA_O_WIDTH), jnp.bfloat16),
        scratch_shapes=[pltpu.VMEM((hg, tq, LANES), jnp.float32),
                        pltpu.VMEM((hg, tq, V_PAD), jnp.float32)],
        compiler_params=pltpu.CompilerParams(
            dimension_semantics=("arbitrary", "arbitrary", "arbitrary", "arbitrary"),
            vmem_limit_bytes=VMEM_LIMIT),
        name="dsa_attention",
    )(q, k, v, bias)


def dsa_branch_pallas(proj, kv_norm_g, w_uk, w_uv, idx_norm_g, idx_norm_b, bsz, t):
    n_keep = min(TOPK_MAX, t // 4)
    tab = _rope_tables_lane(t)
    q, k, v, iq, ik, iw = _dsa_prep(proj, tab, kv_norm_g, idx_norm_g, idx_norm_b, w_uk, w_uv, bsz, t)
    bias = _indexer(iq, ik, iw, bsz, t, n_keep)
    o = _masked_attention(q, k, v, bias, bsz, t)
    return o.reshape(bsz * t, MLA_O_WIDTH)


GDN_TB = 128
GDN_HG = 32
GDN_REP = GDN_V_HEADS // GDN_K_HEADS


def _gdn_kernel(q_ref, k_ref, v_ref, z_ref, ab_ref, cwq_ref, cwk_ref, cwv_ref, alog_ref, dtb_ref, ng_ref,
                o_ref, tq_scr, tk_scr, tv_scr, state_scr):
    f32, bf16 = jnp.float32, jnp.bfloat16
    tb = q_ref.shape[0]
    g = pl.program_id(1)
    step = pl.program_id(2)
    n_chunk = tb // CHUNK
    kh_n = GDN_HG // GDN_REP

    @pl.when(step == 0)
    def _():
        tq_scr[...] = jnp.zeros(tq_scr.shape, f32)
        tk_scr[...] = jnp.zeros(tk_scr.shape, f32)
        tv_scr[...] = jnp.zeros(tv_scr.shape, f32)
        state_scr[...] = jnp.zeros(state_scr.shape, f32)

    def conv_silu(x_ref, w_ref, buf):
        x = x_ref[...]
        buf[8:8 + tb, :] = x
        w = w_ref[...]
        y = x * w[3:4, :]
        for s in (1, 2, 3):
            y = y + buf[8 - s:8 - s + tb, :] * w[3 - s:4 - s, :]
        buf[0:8, :] = x[tb - 8:tb]
        return y * (1.0 / (1.0 + jnp.exp(-y)))

    qc = conv_silu(q_ref, cwq_ref, tq_scr)
    kc = conv_silu(k_ref, cwk_ref, tk_scr)
    vc = conv_silu(v_ref, cwv_ref, tv_scr)

    def l2n(x):
        return x * lax.rsqrt(jnp.sum(x * x, axis=-1, keepdims=True) + NORM_EPS)

    qn = [l2n(qc[:, LANES * j:LANES * (j + 1)]) * (GDN_HEAD_DIM ** -0.5) for j in range(kh_n)]
    kn = [l2n(kc[:, LANES * j:LANES * (j + 1)]) for j in range(kh_n)]

    ab = pltpu.roll(ab_ref[...], (LANES - GDN_HG * g) % LANES, 1)
    xg = ab + dtb_ref[0]
    softplus = jnp.maximum(xg, 0.0) + jnp.log1p(jnp.exp(-jnp.abs(xg)))
    g_all = -jnp.exp(alog_ref[0]) * softplus
    beta_all = 1.0 / (1.0 + jnp.exp(-ab))
    rin = lax.broadcasted_iota(jnp.int32, (tb, LANES), 0) & (CHUNK - 1)
    cum = g_all
    for s in (1, 2, 4, 8, 16, 32):
        cum = cum + jnp.where(rin >= s, pltpu.roll(cum, s, 0), 0.0)

    ii = lax.broadcasted_iota(jnp.int32, (tb, tb), 0)
    jj = lax.broadcasted_iota(jnp.int32, (tb, tb), 1)
    same = (ii >> CHUNK_SHIFT) == (jj >> CHUNK_SHIFT)
    incl = same & (ii >= jj)
    strict = same & (ii > jj)
    eye = jnp.where(ii == jj, 1.0, 0.0)
    nt_dims = (((1,), (1,)), ((), ()))
    tn_dims = (((0,), (0,)), ((), ()))
    ng = ng_ref[...]
    heads = range(GDN_HG)

    cum_t = cum.T
    kb = [x.astype(bf16) for x in kn]
    qb = [x.astype(bf16) for x in qn]
    kk = [lax.dot_general(kb[j], kb[j], nt_dims, preferred_element_type=f32) for j in range(kh_n)]
    qk = [lax.dot_general(qb[j], kb[j], nt_dims, preferred_element_type=f32) for j in range(kh_n)]
    gcol = [cum[:, hh:hh + 1] for hh in heads]
    bcol = [beta_all[:, 32 + hh:33 + hh] for hh in heads]
    dec = [jnp.where(incl, jnp.exp(jnp.where(incl, gcol[hh] - cum_t[hh:hh + 1, :], 0.0)), 0.0) for hh in heads]
    pw = [jnp.where(strict, -(bcol[hh] * kk[hh // GDN_REP] * dec[hh]), 0.0) for hh in heads]
    tmat = [eye + pw[hh] for hh in heads]
    pw = [jnp.dot(pw[hh].astype(bf16), pw[hh].astype(bf16), preferred_element_type=f32) for hh in heads]
    for _ in range(4):
        both = [jnp.dot(jnp.concatenate([tmat[hh], pw[hh]], axis=0).astype(bf16), pw[hh].astype(bf16),
                        preferred_element_type=f32) for hh in heads]
        tmat = [tmat[hh] + both[hh][0:tb] for hh in heads]
        pw = [both[hh][tb:2 * tb] for hh in heads]
    tmat = [tmat[hh] + jnp.dot(tmat[hh].astype(bf16), pw[hh].astype(bf16), preferred_element_type=f32)
            for hh in heads]
    eg = [jnp.exp(gcol[hh]) for hh in heads]
    uw = [jnp.dot(tmat[hh].astype(bf16),
                  jnp.concatenate([bcol[hh] * vc[:, LANES * hh:LANES * (hh + 1)],
                                   (bcol[hh] * eg[hh]) * kn[hh // GDN_REP]], axis=1).astype(bf16),
                  preferred_element_type=f32) for hh in heads]
    qkd = [qk[hh // GDN_REP] * dec[hh] for hh in heads]
    qd = [(qn[hh // GDN_REP] * eg[hh]).astype(bf16) for hh in heads]
    states = [state_scr[hh] for hh in heads]

    for c in range(n_chunk):
        r0 = c * CHUNK
        sb = [states[hh].astype(bf16) for hh in heads]
        ws = [jnp.dot(jnp.concatenate([uw[hh][r0:r0 + CHUNK, LANES:2 * LANES].astype(bf16),
                                       qd[hh][r0:r0 + CHUNK]], axis=0), sb[hh],
                      preferred_element_type=f32) for hh in heads]
        vb = [(uw[hh][r0:r0 + CHUNK, 0:LANES] - ws[hh][0:CHUNK]).astype(bf16) for hh in heads]
        g_last = [cum[r0 + CHUNK - 1:r0 + CHUNK, hh:hh + 1] for hh in heads]
        kd = [(kn[hh // GDN_REP][r0:r0 + CHUNK] * jnp.exp(g_last[hh] - gcol[hh][r0:r0 + CHUNK])).astype(bf16)
              for hh in heads]
        o = [ws[hh][CHUNK:2 * CHUNK] + jnp.dot(qkd[hh][r0:r0 + CHUNK, r0:r0 + CHUNK].astype(bf16), vb[hh],
                                               preferred_element_type=f32) for hh in heads]
        states = [states[hh] * jnp.exp(g_last[hh])
                  + lax.dot_general(kd[hh], vb[hh], tn_dims, preferred_element_type=f32) for hh in heads]
        for hh in heads:
            on = o[hh] * lax.rsqrt(jnp.mean(o[hh] * o[hh], axis=-1, keepdims=True) + NORM_EPS) * ng
            zz = z_ref[r0:r0 + CHUNK, LANES * hh:LANES * (hh + 1)]
            o_ref[r0:r0 + CHUNK, LANES * hh:LANES * (hh + 1)] = (
                on * (zz * (1.0 / (1.0 + jnp.exp(-zz))))).astype(o_ref.dtype)

    for hh in range(GDN_HG):
        state_scr[hh] = states[hh]


def _gdn_branch_pallas(proj, conv_w, a_log, dt_bias, norm_g, bsz, t):
    tb = GDN_TB
    nt = t // tb
    n_groups = GDN_V_HEADS // GDN_HG
    qw = GDN_HG // GDN_REP * GDN_HEAD_DIM
    vw = GDN_HG * GDN_HEAD_DIM
    f32 = jnp.float32

    def per_group(x):
        x = x.astype(f32).reshape(n_groups, 1, GDN_HG)
        return jnp.pad(x, ((0, 0), (0, 0), (0, LANES - GDN_HG)))

    def rows(w, off):
        return pl.BlockSpec((tb, w), lambda b, g, s: (b * nt + s, off // w + g))

    def convw(w, off):
        return pl.BlockSpec((4, w), lambda b, g, s: (0, off // w + g))

    grp = pl.BlockSpec((1, 1, LANES), lambda b, g, s: (g, 0, 0))
    conv_w = conv_w.astype(f32)
    return pl.pallas_call(
        _gdn_kernel,
        grid=(bsz, n_groups, nt),
        in_specs=[rows(qw, P_QKV), rows(qw, P_QKV + GDN_QK_WIDTH), rows(vw, P_QKV + 2 * GDN_QK_WIDTH),
                  rows(vw, P_GZ),
                  pl.BlockSpec((tb, LANES), lambda b, g, s: (b * nt + s, P_GAW // LANES)),
                  convw(qw, 0), convw(qw, GDN_QK_WIDTH), convw(vw, 2 * GDN_QK_WIDTH),
                  grp, grp,
                  pl.BlockSpec((1, LANES), lambda b, g, s: (0, 0))],
        out_specs=pl.BlockSpec((tb, vw), lambda b, g, s: (b * nt + s, g)),
        out_shape=jax.ShapeDtypeStruct((bsz * t, GDN_V_WIDTH), jnp.bfloat16),
        scratch_shapes=[pltpu.VMEM((tb + 8, qw), f32), pltpu.VMEM((tb + 8, qw), f32), pltpu.VMEM((tb + 8, vw), f32),
                        pltpu.VMEM((GDN_HG, GDN_HEAD_DIM, GDN_HEAD_DIM), f32)],
        compiler_params=pltpu.CompilerParams(
            dimension_semantics=("arbitrary", "arbitrary", "arbitrary"),
            vmem_limit_bytes=VMEM_LIMIT),
        name="gdn_branch",
    )(proj, proj, proj, proj, proj, conv_w, conv_w, conv_w,
      per_group(a_log), per_group(dt_bias), norm_g.astype(f32)[None, :])


MERGE_TM = 512
MERGE_TN = 512


def _merge_kernel(oa_ref, ob_ref, wa_ref, wb_ref, ga_ref, gb_ref, o_ref):
    f32 = jnp.float32
    ua = jnp.dot(oa_ref[...], wa_ref[...], preferred_element_type=f32)
    ub = jnp.dot(ob_ref[...], wb_ref[...], preferred_element_type=f32)
    sa = 1.0 / (1.0 + jnp.exp(-ga_ref[...]))
    sb = 1.0 / (1.0 + jnp.exp(-gb_ref[...]))
    o_ref[...] = (sa * ua + sb * ub).astype(o_ref.dtype)


def _outproj_kernel(m_ref, w_ref, x_ref, o_ref):
    o_ref[...] = x_ref[...] + jnp.dot(m_ref[...], w_ref[...], preferred_element_type=jnp.float32)


def _merge_and_project(x, proj, o_a, o_b, w_branch_gdn, w_branch_dsa, w_out):
    n, d = x.shape
    tm, tn = MERGE_TM, MERGE_TN
    bf16 = jnp.bfloat16
    params = pltpu.CompilerParams(dimension_semantics=("arbitrary", "arbitrary"), vmem_limit_bytes=VMEM_LIMIT)
    merged = pl.pallas_call(
        _merge_kernel,
        grid=(d // tn, n // tm),
        in_specs=[pl.BlockSpec((tm, o_a.shape[1]), lambda j, i: (i, 0)),
                  pl.BlockSpec((tm, o_b.shape[1]), lambda j, i: (i, 0)),
                  pl.BlockSpec((o_a.shape[1], tn), lambda j, i: (0, j)),
                  pl.BlockSpec((o_b.shape[1], tn), lambda j, i: (0, j)),
                  pl.BlockSpec((tm, tn), lambda j, i: (i, P_GATE_A // tn + j)),
                  pl.BlockSpec((tm, tn), lambda j, i: (i, P_GATE_B // tn + j))],
        out_specs=pl.BlockSpec((tm, tn), lambda j, i: (i, j)),
        out_shape=jax.ShapeDtypeStruct((n, d), bf16),
        compiler_params=params,
        name="branch_merge",
    )(o_a, o_b, w_branch_gdn.astype(bf16), w_branch_dsa.astype(bf16), proj, proj)
    return pl.pallas_call(
        _outproj_kernel,
        grid=(n // tm,),
        in_specs=[pl.BlockSpec((tm, d), lambda i: (i, 0)),
                  pl.BlockSpec((d, d), lambda i: (0, 0)),
                  pl.BlockSpec((tm, d), lambda i: (i, 0))],
        out_specs=pl.BlockSpec((tm, d), lambda i: (i, 0)),
        out_shape=jax.ShapeDtypeStruct((n, d), jnp.float32),
        compiler_params=pltpu.CompilerParams(dimension_semantics=("arbitrary",), vmem_limit_bytes=VMEM_LIMIT),
        name="out_projection",
    )(merged, w_out.astype(bf16), x)


def hybrid_mixer(x, norm_g, w_in, conv_w, a_log, dt_bias, gdn_norm_g, kv_norm_g, w_uk, w_uv,
                 idx_norm_g, idx_norm_b, w_branch_gdn, w_branch_dsa, w_out):
    bsz, t, d = x.shape
    n = bsz * t
    x = x.reshape(n, d)
    proj = _input_projection(x, norm_g, _relayout_w_in(w_in))
    o_a = _gdn_branch_pallas(proj, conv_w, a_log, dt_bias, gdn_norm_g, bsz, t)
    o_b = dsa_branch_pallas(proj, kv_norm_g, w_uk, w_uv, idx_norm_g, idx_norm_b, bsz, t)
    return _merge_and_project(x, proj, o_a, o_b, w_branch_gdn, w_branch_dsa, w_out)


ROUTER_TM = 512
MOE_BLK = 256
MOE_TN = 1024
COMBINE_TM = 128
DISPATCH_TM = 256


def _router_kernel(x_ref, g_ref, rw_ref, rb_ref, meta_ref, cnt_ref, carry_scr):
    f32 = jnp.float32
    tm = x_ref.shape[0]

    @pl.when(pl.program_id(0) == 0)
    def _():
        carry_scr[...] = jnp.zeros(carry_scr.shape, f32)

    x = x_ref[...]
    h = (x * lax.rsqrt(jnp.mean(x * x, axis=-1, keepdims=True) + NORM_EPS) * g_ref[...]).astype(jnp.bfloat16)
    logits = jnp.dot(h, rw_ref[...], preferred_element_type=f32) + rb_ref[...]
    lane = lax.broadcasted_iota(jnp.int32, (tm, LANES), 1).astype(f32)
    vals = logits
    hot_sum = jnp.zeros((tm, LANES), f32)
    tops, ids, hots = [], [], []
    for _ in range(TOP_K):
        m = jnp.max(vals, axis=1, keepdims=True)
        idx = jnp.min(jnp.where(vals == m, lane, float(LANES)), axis=1, keepdims=True)
        hot = lane == idx
        tops.append(m)
        ids.append(idx)
        hots.append(hot)
        vals = jnp.where(hot, -jnp.inf, vals)
        hot_sum = hot_sum + jnp.where(hot, 1.0, 0.0)
    exps = [jnp.exp(v - tops[0]) for v in tops]
    inv = 1.0 / (exps[0] + exps[1] + exps[2] + exps[3])
    ri = lax.broadcasted_iota(jnp.int32, (tm, tm), 0)
    ci = lax.broadcasted_iota(jnp.int32, (tm, tm), 1)
    lower = jnp.where(ri > ci, 1.0, 0.0).astype(jnp.bfloat16)
    base = carry_scr[...] + jnp.dot(lower, hot_sum.astype(jnp.bfloat16), preferred_element_type=f32)
    meta = jnp.zeros((tm, LANES), f32)
    for k in range(TOP_K):
        rank = jnp.sum(jnp.where(hots[k], base, 0.0), axis=1, keepdims=True)
        meta = jnp.where(lane == float(k), ids[k], meta)
        meta = jnp.where(lane == float(TOP_K + k), exps[k] * inv, meta)
        meta = jnp.where(lane == float(2 * TOP_K + k), rank, meta)
    meta_ref[...] = meta
    carry_scr[...] = carry_scr[...] + jnp.sum(hot_sum, axis=0, keepdims=True)
    cnt_ref[...] = carry_scr[...]


def _router(x1, norm_g, router_w, router_b):
    n, d = x1.shape
    tm = ROUTER_TM
    rw = jnp.pad(router_w.astype(jnp.bfloat16), ((0, 0), (0, LANES - N_EXPERTS)))
    rb = jnp.concatenate([router_b.astype(jnp.float32),
                          jnp.full((LANES - N_EXPERTS,), -jnp.inf, jnp.float32)])[None, :]
    return pl.pallas_call(
        _router_kernel,
        grid=(n // tm,),
        in_specs=[pl.BlockSpec((tm, d), lambda i: (i, 0)),
                  pl.BlockSpec((1, d), lambda i: (0, 0)),
                  pl.BlockSpec((d, LANES), lambda i: (0, 0)),
                  pl.BlockSpec((1, LANES), lambda i: (0, 0))],
        out_specs=[pl.BlockSpec((tm, LANES), lambda i: (i, 0)),
                   pl.BlockSpec((1, LANES), lambda i: (0, 0))],
        out_shape=[jax.ShapeDtypeStruct((n, LANES), jnp.float32),
                   jax.ShapeDtypeStruct((1, LANES), jnp.float32)],
        scratch_shapes=[pltpu.VMEM((1, LANES), jnp.float32)],
        compiler_params=pltpu.CompilerParams(dimension_semantics=("arbitrary",), vmem_limit_bytes=VMEM_LIMIT),
        name="moe_router",
    )(x1, norm_g.astype(jnp.float32)[None, :], rw, rb)


def _dispatch_kernel(slot_ref, x_ref, g_ref, xs_in_ref, xs_ref, h_scr, sem):
    del xs_in_ref
    tm = x_ref.shape[0]
    x = x_ref[...]
    h_scr[...] = x * lax.rsqrt(jnp.mean(x * x, axis=-1, keepdims=True) + NORM_EPS) * g_ref[...]

    def copy(r, k):
        return pltpu.make_async_copy(h_scr.at[pl.ds(r, 1)], xs_ref.at[pl.ds(slot_ref[TOP_K * r + k], 1)], sem)

    def start_row(r, carry):
        for k in range(TOP_K):
            copy(r, k).start()
        return carry

    lax.fori_loop(0, tm, start_row, 0)
    for _ in range(TOP_K):
        pltpu.make_async_copy(h_scr, xs_ref.at[pl.ds(0, tm)], sem).wait()


def _dispatch(slot, x1, norm_g, n_rows):
    n, d = x1.shape
    tm = DISPATCH_TM
    xs0 = jnp.zeros((n_rows, d), jnp.float32)
    return pl.pallas_call(
        _dispatch_kernel,
        grid=(n // tm,),
        in_specs=[pl.BlockSpec((tm * TOP_K,), lambda i: (i,), memory_space=pltpu.SMEM),
                  pl.BlockSpec((tm, d), lambda i: (i, 0)),
                  pl.BlockSpec((1, d), lambda i: (0, 0)),
                  pl.BlockSpec(memory_space=pl.ANY)],
        out_specs=pl.BlockSpec(memory_space=pl.ANY),
        out_shape=jax.ShapeDtypeStruct((n_rows, d), jnp.float32),
        scratch_shapes=[pltpu.VMEM((tm, d), jnp.float32), pltpu.SemaphoreType.DMA(())],
        input_output_aliases={3: 0},
        compiler_params=pltpu.CompilerParams(dimension_semantics=("arbitrary",), vmem_limit_bytes=VMEM_LIMIT),
        name="moe_dispatch",
    )(slot, x1, norm_g.astype(jnp.float32)[None, :], xs0)


def _expert_changed(be_ref, j):
    return jnp.logical_or(j == 0, be_ref[j] != be_ref[jnp.maximum(j - 1, 0)])


def _moe_up_kernel(be_ref, nb_ref, x_ref, wg_ref, wu_ref, bg_ref, bu_ref, o_ref, wg_scr, wu_scr):
    j = pl.program_id(1)

    @pl.when(jnp.logical_and(j < nb_ref[0], _expert_changed(be_ref, j)))
    def _():
        wg_scr[...] = wg_ref[0].astype(jnp.bfloat16)
        wu_scr[...] = wu_ref[0].astype(jnp.bfloat16)

    @pl.when(j < nb_ref[0])
    def _():
        x = x_ref[...].astype(jnp.bfloat16)
        gate = jnp.dot(x, wg_scr[...], preferred_element_type=jnp.float32) + bg_ref[0]
        up = jnp.dot(x, wu_scr[...], preferred_element_type=jnp.float32) + bu_ref[0]
        gate = jnp.minimum(gate, SWIGLU_LIMIT)
        up = jnp.clip(up, -SWIGLU_LIMIT, SWIGLU_LIMIT)
        act = (up + 1.0) * (gate * (1.0 / (1.0 + jnp.exp(-SWIGLU_ALPHA * gate))))
        o_ref[...] = act.astype(o_ref.dtype)

    @pl.when(j >= nb_ref[0])
    def _():
        o_ref[...] = jnp.zeros(o_ref.shape, o_ref.dtype)


def _moe_down_kernel(be_ref, nb_ref, a_ref, wd_ref, bd_ref, o_ref, wd_scr):
    j = pl.program_id(1)

    @pl.when(jnp.logical_and(j < nb_ref[0], _expert_changed(be_ref, j)))
    def _():
        wd_scr[...] = wd_ref[0].astype(jnp.bfloat16)

    @pl.when(j < nb_ref[0])
    def _():
        o_ref[...] = jnp.dot(a_ref[...], wd_scr[...], preferred_element_type=jnp.float32) + bd_ref[0]

    @pl.when(j >= nb_ref[0])
    def _():
        o_ref[...] = jnp.zeros(o_ref.shape, o_ref.dtype)


def _moe_experts(xs, block_expert, n_used, w_gate, b_gate, w_up, b_up, w_down, b_down):
    n_rows, d = xs.shape
    f = w_gate.shape[2]
    blk, tn = MOE_BLK, MOE_TN
    n_blocks = n_rows // blk
    bf16 = jnp.bfloat16

    def wspec(k):
        return pl.BlockSpec((1, k, tn), lambda c, j, be, nb: (be[j], 0, c))

    bspec = pl.BlockSpec((1, 1, tn), lambda c, j, be, nb: (be[j], 0, c))
    rows_in = lambda k: pl.BlockSpec((blk, k), lambda c, j, be, nb: (j, 0))
    rows_out = pl.BlockSpec((blk, tn), lambda c, j, be, nb: (j, c))
    params = pltpu.CompilerParams(dimension_semantics=("arbitrary", "arbitrary"), vmem_limit_bytes=VMEM_LIMIT)
    act = pl.pallas_call(
        _moe_up_kernel,
        grid_spec=pltpu.PrefetchScalarGridSpec(
            num_scalar_prefetch=2, grid=(f // tn, n_blocks),
            in_specs=[rows_in(d), wspec(d), wspec(d), bspec, bspec],
            out_specs=rows_out,
            scratch_shapes=[pltpu.VMEM((d, tn), bf16), pltpu.VMEM((d, tn), bf16)]),
        out_shape=jax.ShapeDtypeStruct((n_rows, f), bf16),
        compiler_params=params,
        name="moe_up",
    )(block_expert, n_used, xs, w_gate.astype(jnp.float32), w_up.astype(jnp.float32),
      b_gate.astype(jnp.float32)[:, None, :], b_up.astype(jnp.float32)[:, None, :])
    return pl.pallas_call(
        _moe_down_kernel,
        grid_spec=pltpu.PrefetchScalarGridSpec(
            num_scalar_prefetch=2, grid=(d // tn, n_blocks),
            in_specs=[rows_in(f), wspec(f), bspec],
            out_specs=rows_out,
            scratch_shapes=[pltpu.VMEM((f, tn), bf16)]),
        out_shape=jax.ShapeDtypeStruct((n_rows, d), jnp.float32),
        compiler_params=params,
        name="moe_down",
    )(block_expert, n_used, act, w_down.astype(jnp.float32), b_down.astype(jnp.float32)[:, None, :])


def _combine_kernel(slot_ref, slot_next_ref, x1_ref, meta_ref, g_ref, ys_ref, o_ref, buf, sem):
    i = pl.program_id(0)
    n_steps = pl.num_programs(0)
    tm = x1_ref.shape[0]

    def copy(slots, r, k, half):
        return pltpu.make_async_copy(ys_ref.at[pl.ds(slots[TOP_K * r + k], 1)],
                                     buf.at[half, k, pl.ds(r, 1)], sem.at[half])

    def start_tile(slots, half):
        def body(r, carry):
            for k in range(TOP_K):
                copy(slots, r, k, half).start()
            return carry
        lax.fori_loop(0, tm, body, 0)

    @pl.when(i == 0)
    def _():
        start_tile(slot_ref, 0)

    @pl.when(i + 1 < n_steps)
    def _():
        start_tile(slot_next_ref, (i + 1) % 2)

    half = i % 2

    for k in range(TOP_K):
        pltpu.make_async_copy(ys_ref.at[pl.ds(0, tm)], buf.at[half, k], sem.at[half]).wait()
    acc = x1_ref[...]
    for k in range(TOP_K):
        acc = acc + meta_ref[:, TOP_K + k:TOP_K + k + 1] * buf[half, k]
    o_ref[...] = acc * lax.rsqrt(jnp.mean(acc * acc, axis=-1, keepdims=True) + NORM_EPS) * g_ref[...]


def _combine(slot, x1, meta, ys, norm_g):
    n, d = x1.shape
    tm = COMBINE_TM
    n_steps = n // tm
    return pl.pallas_call(
        _combine_kernel,
        grid=(n_steps,),
        in_specs=[pl.BlockSpec((tm * TOP_K,), lambda i: (i,), memory_space=pltpu.SMEM),
                  pl.BlockSpec((tm * TOP_K,), lambda i: (jnp.minimum(i + 1, n_steps - 1),),
                               memory_space=pltpu.SMEM),
                  pl.BlockSpec((tm, d), lambda i: (i, 0)),
                  pl.BlockSpec((tm, LANES), lambda i: (i, 0)),
                  pl.BlockSpec((1, d), lambda i: (0, 0)),
                  pl.BlockSpec(memory_space=pl.ANY)],
        out_specs=pl.BlockSpec((tm, d), lambda i: (i, 0)),
        out_shape=jax.ShapeDtypeStruct((n, d), jnp.float32),
        scratch_shapes=[pltpu.VMEM((2, TOP_K, tm, d), jnp.float32),
                        pltpu.SemaphoreType.DMA((2,))],
        compiler_params=pltpu.CompilerParams(dimension_semantics=("arbitrary",), vmem_limit_bytes=VMEM_LIMIT),
        name="moe_combine",
    )(slot, slot, x1, meta, norm_g.astype(jnp.float32)[None, :], ys)


def moe_final_pallas(x1, norm_ffn_g, router_w, router_b, w_gate, b_gate, w_up, b_up, w_down, b_down,
                     norm_final_g):
    n, d = x1.shape
    meta, cnt = _router(x1, norm_ffn_g, router_w, router_b)
    counts = cnt[0, :N_EXPERTS].astype(jnp.int32)
    padded = (counts + MOE_BLK - 1) // MOE_BLK * MOE_BLK
    pad_end = jnp.cumsum(padded)
    pad_start = pad_end - padded
    expert = meta[:, 0:TOP_K].astype(jnp.int32)
    rank = meta[:, 2 * TOP_K:3 * TOP_K].astype(jnp.int32)
    seg_start = jnp.sum(jnp.where(expert[..., None] == jnp.arange(N_EXPERTS, dtype=jnp.int32), pad_start, 0),
                        axis=-1)
    slot = (seg_start + rank).reshape(-1)
    n_blocks = n * TOP_K // MOE_BLK + N_EXPERTS
    block_start = jnp.arange(n_blocks, dtype=jnp.int32) * MOE_BLK
    block_expert = jnp.minimum(jnp.sum((pad_end[None, :] <= block_start[:, None]).astype(jnp.int32), axis=1),
                               N_EXPERTS - 1)
    n_used = (pad_end[-1:] // MOE_BLK).astype(jnp.int32)
    xs = _dispatch(slot, x1, norm_ffn_g, n_blocks * MOE_BLK)
    ys = _moe_experts(xs, block_expert, n_used, w_gate, b_gate, w_up, b_up, w_down, b_down)
    return _combine(slot, x1, meta, ys, norm_final_g)


def kernel(x, norm_mix_g, w_in, gdn_conv_w, gdn_a_log, gdn_dt_bias, gdn_norm_g, mla_kv_norm_g,
           mla_w_uk, mla_w_uv, idx_k_norm_g, idx_k_norm_b, w_branch_gdn, w_branch_dsa, w_out,
           norm_ffn_g, router_w, router_b, exp_w_gate, exp_b_gate, exp_w_up, exp_b_up,
           exp_w_down, exp_b_down, norm_final_g):
    l = 0
    bsz, t, d = x.shape
    x1 = hybrid_mixer(x, norm_mix_g[l], w_in[l], gdn_conv_w[l], gdn_a_log[l], gdn_dt_bias[l],
                      gdn_norm_g[l], mla_kv_norm_g[l], mla_w_uk[l], mla_w_uv[l], idx_k_norm_g[l],
                      idx_k_norm_b[l], w_branch_gdn[l], w_branch_dsa[l], w_out[l])
    out = moe_final_pallas(x1, norm_ffn_g[l], router_w[l], router_b[l], exp_w_gate[l],
                           exp_b_gate[l], exp_w_up[l], exp_b_up[l], exp_w_down[l], exp_b_down[l], norm_final_g)
    return out.reshape(bsz, t, d)
```

```python
import functools
import math
import numpy as np
import jax
import jax.numpy as jnp
from jax import lax
from jax.experimental import pallas as pl
from jax.experimental.pallas import tpu as pltpu

D_MODEL = 2048
CHUNK = 64
CHUNK_SHIFT = CHUNK.bit_length() - 1
ROPE_THETA = 10000.0
NORM_EPS = 1e-6
GDN_K_HEADS = 16
GDN_V_HEADS = 32
GDN_HEAD_DIM = 128
GDN_QK_WIDTH = GDN_K_HEADS * GDN_HEAD_DIM
GDN_V_WIDTH = GDN_V_HEADS * GDN_HEAD_DIM
MLA_HEADS = 16
MLA_NOPE_DIM = 128
MLA_ROPE_DIM = 64
MLA_V_DIM = 128
MLA_KV_RANK = 512
MLA_Q_WIDTH = MLA_HEADS * (MLA_NOPE_DIM + MLA_ROPE_DIM)
MLA_O_WIDTH = MLA_HEADS * MLA_V_DIM
IDX_HEADS = 16
IDX_DIM = 64
IDX_ROPE_DIM = 32
TOPK_MAX = 256
N_EXPERTS = 32
TOP_K = 4
SWIGLU_LIMIT = 7.0
SWIGLU_ALPHA = 1.702
LANES = 128
VMEM_LIMIT = 56 * 1024 * 1024
IN_SIZES = (GDN_QK_WIDTH, GDN_QK_WIDTH, GDN_V_WIDTH, GDN_V_WIDTH, GDN_V_HEADS, GDN_V_HEADS,
            MLA_Q_WIDTH, MLA_KV_RANK, MLA_ROPE_DIM, IDX_HEADS * IDX_DIM, IDX_DIM, IDX_HEADS,
            D_MODEL, D_MODEL)


PROJ_TM = 1024
PROJ_TN = 1536


def _inproj_kernel(x_ref, g_ref, w_ref, o_ref, h_scr):
    @pl.when(pl.program_id(1) == 0)
    def _():
        x = x_ref[...]
        h_scr[...] = (x * lax.rsqrt(jnp.mean(x * x, axis=-1, keepdims=True) + NORM_EPS)
                      * g_ref[...]).astype(jnp.bfloat16)

    o_ref[...] = jnp.dot(h_scr[...], w_ref[...], preferred_element_type=jnp.float32)


def _input_projection(x, norm_g, w):
    n, d = x.shape
    width = w.shape[1]
    tm, tn = PROJ_TM, PROJ_TN
    return pl.pallas_call(
        _inproj_kernel,
        grid=(n // tm, width // tn),
        in_specs=[pl.BlockSpec((tm, d), lambda i, j: (i, 0)),
                  pl.BlockSpec((1, d), lambda i, j: (0, 0)),
                  pl.BlockSpec((d, tn), lambda i, j: (0, j))],
        out_specs=pl.BlockSpec((tm, tn), lambda i, j: (i, j)),
        out_shape=jax.ShapeDtypeStruct((n, width), jnp.float32),
        scratch_shapes=[pltpu.VMEM((tm, d), jnp.bfloat16)],
        compiler_params=pltpu.CompilerParams(
            dimension_semantics=("arbitrary", "arbitrary"),
            vmem_limit_bytes=VMEM_LIMIT),
        name="input_projection",
    )(x, norm_g.astype(jnp.float32)[None, :], w)


def rope_tables(n, dim):
    inv = ROPE_THETA ** (-jnp.arange(0, dim, 2, dtype=jnp.float32) / dim)
    ang = jnp.arange(n, dtype=jnp.float32)[:, None] * inv[None, :]
    return jnp.cos(ang), jnp.sin(ang)


_IN_OFF = np.concatenate([[0], np.cumsum(IN_SIZES)]).tolist()
(_O_GQ, _O_GK, _O_GV, _O_GZ, _O_GA, _O_GB, _O_MQ, _O_CKV, _O_KPE, _O_IQ, _O_IK, _O_IW,
 _O_GATE_A, _O_GATE_B) = _IN_OFF[:-1]
P_QKV = 0
P_GZ = 8192
P_QN = 12288
P_GATE_A = 14336
P_GATE_B = 16384
P_QPE = 18432
P_IQ = 19456
P_CKV = 20480
P_KPE_IK = 20992
P_GAW = 21120
P_WIDTH = 21504


def _proj_column_order():
    cols = []
    cols += list(range(_O_GQ, _O_GZ))
    cols += list(range(_O_GZ, _O_GA))
    hd = MLA_NOPE_DIM + MLA_ROPE_DIM
    cols += [_O_MQ + h * hd + i for h in range(MLA_HEADS) for i in range(MLA_NOPE_DIM)]
    cols += list(range(_O_GATE_A, _O_GATE_A + 2 * D_MODEL))
    cols += [_O_MQ + h * hd + MLA_NOPE_DIM + i for h in range(MLA_HEADS) for i in range(MLA_ROPE_DIM)]
    cols += list(range(_O_IQ, _O_IK))
    cols += list(range(_O_CKV, _O_KPE))
    cols += list(range(_O_KPE, _O_IQ)) + list(range(_O_IK, _O_IW))
    cols += list(range(_O_GA, _O_MQ)) + list(range(_O_IW, _O_GATE_A))
    cols += [-1] * (P_WIDTH - len(cols))
    assert len(cols) == P_WIDTH
    return np.asarray(cols, np.int32)


_PROJ_COLS = _proj_column_order()


def _relayout_w_in(w_in):
    d = w_in.shape[0]
    w = w_in.astype(jnp.bfloat16)
    mq = w[:, _O_MQ:_O_CKV].reshape(d, MLA_HEADS, MLA_NOPE_DIM + MLA_ROPE_DIM)
    parts = [w[:, _O_GQ:_O_GA],
             mq[:, :, :MLA_NOPE_DIM].reshape(d, -1),
             w[:, _O_GATE_A:_O_GATE_A + 2 * D_MODEL],
             mq[:, :, MLA_NOPE_DIM:].reshape(d, -1),
             w[:, _O_IQ:_O_IK], w[:, _O_CKV:_O_KPE], w[:, _O_KPE:_O_IQ], w[:, _O_IK:_O_IW],
             w[:, _O_GA:_O_MQ], w[:, _O_IW:_O_GATE_A]]
    used = sum(p.shape[1] for p in parts)
    return jnp.concatenate(parts + [jnp.zeros((d, P_WIDTH - used), jnp.bfloat16)], axis=1)


NEG_BIG = -1e30
INT_MIN = -2 ** 31
PREP_TM = 256
IDX_TQ = 256
IDX_BAND = 128
ATT_HG = 8
ATT_TQ = 1024
ATT_TK = 512
QK_PAD = 256
ATT_HEAD_UNROLL = 2
V_PAD = 256


def _rope_tables_lane(t):
    cos_m, sin_m = rope_tables(t, MLA_ROPE_DIM)
    cos_i, sin_i = rope_tables(t, IDX_ROPE_DIM)
    one = jnp.ones((t, IDX_DIM - IDX_ROPE_DIM), jnp.float32)
    zero = jnp.zeros_like(one)
    cq = jnp.concatenate([cos_m, cos_m] * 2, axis=1)
    sq = jnp.concatenate([-sin_m, sin_m] * 2, axis=1)
    ci = jnp.concatenate([cos_i, cos_i, one] * 2, axis=1)
    si = jnp.concatenate([-sin_i, sin_i, zero] * 2, axis=1)
    cs = jnp.concatenate([cos_m, cos_m, cos_i, cos_i, one], axis=1)
    ss = jnp.concatenate([-sin_m, sin_m, -sin_i, sin_i, zero], axis=1)
    return jnp.concatenate([cq, sq, ci, si, cs, ss], axis=1)


def _dsa_prep_kernel(qn_ref, qpe_ref, iq_ref, ckv_ref, sm_ref, gw_ref, tab_ref,
                     kvg_ref, lng_ref, lnb_ref, wuk_ref, wuv_ref,
                     q_out, k_out, v_out, iq_out, ik_out, iw_out):
    f32, bf16 = jnp.float32, jnp.bfloat16
    tm = qn_ref.shape[0]
    lane = lax.broadcasted_iota(jnp.int32, (tm, LANES), 1)
    lo64 = lane < 64

    def swap32(x):
        return jnp.where((lane & 32) == 0, pltpu.roll(x, 96, 1), pltpu.roll(x, 32, 1))

    def swap16(x):
        return jnp.where((lane & 16) == 0, pltpu.roll(x, 112, 1), pltpu.roll(x, 16, 1))

    cq, sq = tab_ref[:, 0:128], tab_ref[:, 128:256]
    ci, si = tab_ref[:, 256:384], tab_ref[:, 384:512]
    cs, ss = tab_ref[:, 512:640], tab_ref[:, 640:768]
    scale = (MLA_NOPE_DIM + MLA_ROPE_DIM) ** -0.5 * math.log2(math.e)

    for j in range(MLA_HEADS // 2):
        t = qpe_ref[:, LANES * j:LANES * (j + 1)]
        r = (t * cq + swap32(t) * sq) * scale
        q_out[0, 2 * j, :, 128:256] = jnp.where(lo64, r, 0.0).astype(bf16)
        q_out[0, 2 * j + 1, :, 128:256] = jnp.where(lo64, pltpu.roll(r, 64, 1), 0.0).astype(bf16)
        t = iq_ref[:, LANES * j:LANES * (j + 1)]
        r = t * ci + swap16(t) * si
        iq_out[0, 2 * j] = jnp.where(lo64, r, 0.0).astype(bf16)
        iq_out[0, 2 * j + 1] = jnp.where(lo64, pltpu.roll(r, 64, 1), 0.0).astype(bf16)
    for h in range(MLA_HEADS):
        q_out[0, h, :, 0:128] = (qn_ref[:, LANES * h:LANES * (h + 1)] * scale).astype(bf16)

    c = ckv_ref[...]
    cn = c * lax.rsqrt(jnp.mean(c * c, axis=-1, keepdims=True) + NORM_EPS) * kvg_ref[...]
    cnb = cn.astype(bf16)
    kn = jnp.dot(cnb, wuk_ref[...], preferred_element_type=f32)
    vv = jnp.dot(cnb, wuv_ref[...], preferred_element_type=f32)

    s = sm_ref[...]
    hi64 = jnp.logical_not(lo64)
    mu = jnp.sum(jnp.where(hi64, s, 0.0), axis=-1, keepdims=True) * (1.0 / IDX_DIM)
    xc = jnp.where(hi64, s - mu, 0.0)
    var = jnp.sum(xc * xc, axis=-1, keepdims=True) * (1.0 / IDX_DIM)
    ln = xc * lax.rsqrt(var + NORM_EPS) * lng_ref[...] + lnb_ref[...]
    y = jnp.where(hi64, ln, s)
    r = y * cs + jnp.where(lo64, swap32(y), swap16(y)) * ss
    kpe_t = jnp.where(lo64, r, 0.0).T.astype(bf16)
    for h in range(MLA_HEADS):
        k_out[0, h, 0:128, :] = kn[:, LANES * h:LANES * (h + 1)].T.astype(bf16)
        k_out[0, h, 128:256, :] = kpe_t
        v_out[0, h, :, 0:128] = vv[:, LANES * h:LANES * (h + 1)].astype(bf16)
        v_out[0, h, :, 128:256] = jnp.ones((tm, LANES), bf16)
    ik_out[0] = jnp.where(lo64, pltpu.roll(r, 64, 1), 0.0).astype(bf16)
    iw_out[...] = gw_ref[...] * (IDX_HEADS ** -0.5 * IDX_DIM ** -0.5)


def _dsa_prep(proj, tab, kv_norm_g, idx_norm_g, idx_norm_b, w_uk, w_uv, bsz, t):
    tm = PREP_TM
    nt = t // tm
    zeros64 = jnp.zeros((64,), jnp.float32)
    lng = jnp.concatenate([zeros64, idx_norm_g.astype(jnp.float32)])[None, :]
    lnb = jnp.concatenate([zeros64, idx_norm_b.astype(jnp.float32)])[None, :]
    kvg = kv_norm_g.astype(jnp.float32)[None, :]

    def rows(w, off):
        return pl.BlockSpec((tm, w), lambda b, i: (b * nt + i, off // w))

    def const(shape):
        return pl.BlockSpec(shape, lambda b, i: (0,) * len(shape))

    def heads(w):
        return pl.BlockSpec((1, MLA_HEADS, tm, w), lambda b, i: (b, 0, i, 0))

    bf16 = jnp.bfloat16
    return pl.pallas_call(
        _dsa_prep_kernel,
        grid=(bsz, nt),
        in_specs=[rows(2048, P_QN), rows(1024, P_QPE), rows(1024, P_IQ), rows(512, P_CKV),
                  rows(LANES, P_KPE_IK), rows(LANES, P_GAW),
                  pl.BlockSpec((tm, 768), lambda b, i: (i, 0)),
                  const((1, MLA_KV_RANK)), const((1, LANES)), const((1, LANES)),
                  const((MLA_KV_RANK, MLA_HEADS * MLA_NOPE_DIM)), const((MLA_KV_RANK, MLA_O_WIDTH))],
        out_specs=[heads(QK_PAD), pl.BlockSpec((1, MLA_HEADS, QK_PAD, tm), lambda b, i: (b, 0, 0, i)),
                   heads(V_PAD), heads(LANES),
                   pl.BlockSpec((1, tm, LANES), lambda b, i: (b, i, 0)),
                   pl.BlockSpec((tm, LANES), lambda b, i: (b * nt + i, 0))],
        out_shape=[jax.ShapeDtypeStruct((bsz, MLA_HEADS, t, QK_PAD), bf16),
                   jax.ShapeDtypeStruct((bsz, MLA_HEADS, QK_PAD, t), bf16),
                   jax.ShapeDtypeStruct((bsz, MLA_HEADS, t, V_PAD), bf16),
                   jax.ShapeDtypeStruct((bsz, IDX_HEADS, t, LANES), bf16),
                   jax.ShapeDtypeStruct((bsz, t, LANES), bf16),
                   jax.ShapeDtypeStruct((bsz * t, LANES), jnp.float32)],
        compiler_params=pltpu.CompilerParams(
            dimension_semantics=("arbitrary", "arbitrary"), vmem_limit_bytes=VMEM_LIMIT),
        name="dsa_prep",
    )(proj, proj, proj, proj, proj, proj, tab, kvg, lng, lnb,
      w_uk.astype(bf16), w_uv.astype(bf16))


def _indexer_kernel(iq_ref, ik_ref, iw_ref, bias_ref, key_scr, tie_scr, *, n_keep):
    f32, i32 = jnp.float32, jnp.int32
    tq = iq_ref.shape[2]
    kc = key_scr.shape[2]
    nch_total = key_scr.shape[0]
    qi = pl.program_id(1)
    nch = ((qi + 1) * tq + kc - 1) // kc
    q_chunk = (lax.broadcasted_iota(i32, (tq, kc), 0) + qi * tq) >> CHUNK_SHIFT
    col0 = lax.broadcasted_iota(i32, (tq, kc), 1)
    n_sub = kc // LANES

    def score_chunk(c, carry):
        ik = ik_ref[0, pl.ds(pl.multiple_of(c * kc, kc), kc), :]
        acc = jnp.zeros((tq, kc), f32)
        for h in range(IDX_HEADS):
            s = lax.dot_general(iq_ref[0, h], ik, (((1,), (1,)), ((), ())), preferred_element_type=f32)
            acc = acc + iw_ref[:, 64 + h:65 + h] * jnp.maximum(s, 0.0)
        bits = pltpu.bitcast(acc, i32)
        key = bits ^ ((bits >> 31) & 0x7FFFFFFF)
        adm = ((col0 + c * kc) >> CHUNK_SHIFT) <= q_chunk
        key_scr[c] = jnp.where(adm, key, INT_MIN)
        return carry

    lax.fori_loop(0, nch, score_chunk, 0)

    def lane_fold(x):
        part = x[:, 0:LANES]
        for j in range(1, n_sub):
            part = part + x[:, LANES * j:LANES * (j + 1)]
        return part

    def rep(x):
        return jnp.concatenate([x] * n_sub, axis=1)

    def count(pred_fn):
        def body(c, acc):
            return acc + lane_fold(jnp.where(pred_fn(key_scr[c], c), 1.0, 0.0))
        acc = lax.fori_loop(0, nch, body, jnp.zeros((tq, LANES), f32))
        return jnp.broadcast_to(jnp.sum(acc, axis=1, keepdims=True), (tq, LANES))

    def count_ge(cand):
        accs = []
        for r0 in range(0, tq, IDX_BAND):
            cand_r = cand[r0:r0 + IDX_BAND]

            def body(c, acc, r0=r0, cand_r=cand_r):
                k = key_scr[c, r0:r0 + IDX_BAND, :]
                for j in range(n_sub):
                    acc = acc + jnp.where(k[:, LANES * j:LANES * (j + 1)] >= cand_r, 1.0, 0.0)
                return acc

            accs.append(lax.fori_loop(0, nch, body, jnp.zeros((IDX_BAND, LANES), f32)))
        acc = jnp.concatenate(accs, axis=0)
        return jnp.broadcast_to(jnp.sum(acc, axis=1, keepdims=True), (tq, LANES))

    keep = float(n_keep)
    cnt = count(lambda k, c: k >= 0)
    thr = jnp.where(cnt >= keep, 0, INT_MIN).astype(i32)
    cnt_thr = jnp.where(cnt >= keep, cnt, (nch * kc).astype(f32))

    def bit_cond(state):
        i, _, cnt_thr = state
        return jnp.logical_and(i < 31, jnp.max(jnp.abs(cnt_thr - keep)) > 0.0)

    def bit_pass(state):
        i, thr, cnt_thr = state
        cand = thr | jnp.left_shift(jnp.int32(1), 30 - i)
        cnt = count_ge(cand)
        take = cnt >= keep
        return i + 1, jnp.where(take, cand, thr), jnp.where(take, cnt, cnt_thr)

    _, thr, cnt_ge = lax.while_loop(bit_cond, bit_pass, (jnp.int32(0), thr, cnt_thr))
    thr_w = rep(thr)
    tie_scr[...] = jnp.full((tq, LANES), 2 ** 30, i32)

    @pl.when(jnp.max(cnt_ge) > keep)
    def _():
        cnt_gt = count(lambda k, c: k > thr_w)
        need = keep - cnt_gt

        def idx_pass(i, x):
            cand = x | jnp.left_shift(jnp.int32(1), idx_bits - 1 - i)
            cand_w = rep(cand)
            cnt = count(lambda k, c: (k == thr_w) & ((col0 + c * kc) < cand_w))
            return jnp.where(cnt < need, cand, x)

        idx_bits = (nch_total * kc - 1).bit_length()
        x = lax.fori_loop(0, idx_bits, idx_pass, jnp.zeros((tq, LANES), i32))
        tie_scr[...] = jnp.where(cnt_ge > keep, x, 2 ** 30)

    tie_w = rep(tie_scr[...])

    def write_chunk(c, carry):
        k = key_scr[c]
        sel = (k > thr_w) | ((k == thr_w) & ((col0 + c * kc) <= tie_w))
        sel = sel & (k != INT_MIN)
        bias_ref[0, c] = jnp.where(sel, 0.0, NEG_BIG)
        return carry

    lax.fori_loop(0, nch, write_chunk, 0)

    def fill_chunk(c, carry):
        bias_ref[0, c] = jnp.full((tq, kc), NEG_BIG, f32)
        return carry

    lax.fori_loop(nch, nch_total, fill_chunk, 0)


def _indexer(iq, ik, iw, bsz, t, n_keep):
    tq, kc = IDX_TQ, ATT_TK
    nq, nkc = t // tq, t // kc
    return pl.pallas_call(
        functools.partial(_indexer_kernel, n_keep=n_keep),
        grid=(bsz, nq),
        in_specs=[pl.BlockSpec((1, IDX_HEADS, tq, LANES), lambda b, i: (b, 0, i, 0)),
                  pl.BlockSpec((1, t, LANES), lambda b, i: (b, 0, 0)),
                  pl.BlockSpec((tq, LANES), lambda b, i: (b * nq + i, 0))],
        out_specs=pl.BlockSpec((1, nkc, tq, kc), lambda b, i: (b, 0, i, 0)),
        out_shape=jax.ShapeDtypeStruct((bsz, nkc, t, kc), jnp.float32),
        scratch_shapes=[pltpu.VMEM((nkc, tq, kc), jnp.int32), pltpu.VMEM((tq, LANES), jnp.int32)],
        compiler_params=pltpu.CompilerParams(
            dimension_semantics=("arbitrary", "arbitrary"), vmem_limit_bytes=VMEM_LIMIT),
        name="dsa_indexer",
    )(iq, ik, iw)


def _attn_kernel(q_ref, k_ref, v_ref, b_ref, o_ref, m_scr, acc_scr):
    f32 = jnp.float32
    tq, tk = b_ref.shape[2], b_ref.shape[3]
    qi, ki = pl.program_id(2), pl.program_id(3)
    last = ((qi + 1) * tq - 1) // tk
    n_heads = q_ref.shape[1]

    @pl.when(ki == 0)
    def _():
        m_scr[...] = jnp.full(m_scr.shape, NEG_BIG, f32)
        acc_scr[...] = jnp.zeros(acc_scr.shape, f32)

    @pl.when(ki <= last)
    def _():
        bias = b_ref[0, 0]

        def head_group(gi, carry):
            hs = [gi * ATT_HEAD_UNROLL + u for u in range(ATT_HEAD_UNROLL)]
            s = [jnp.dot(q_ref[0, h], k_ref[0, h], preferred_element_type=f32) + bias for h in hs]
            m_prev = [m_scr[h] for h in hs]
            m_new = [jnp.maximum(mp, jnp.max(x, axis=1, keepdims=True)) for mp, x in zip(m_prev, s)]
            p = [jnp.exp2((x - mn[:, 0:1]).astype(jnp.bfloat16)) for x, mn in zip(s, m_new)]
            pv = [jnp.dot(x, v_ref[0, h], preferred_element_type=f32) for x, h in zip(p, hs)]
            for u, h in enumerate(hs):
                alpha = jnp.exp2(m_prev[u] - m_new[u])
                acc_scr[h, :, 0:128] = alpha * acc_scr[h, :, 0:128] + pv[u][:, 0:128]
                acc_scr[h, :, 128:256] = alpha * acc_scr[h, :, 128:256] + pv[u][:, 128:256]
                m_scr[h] = m_new[u]
            return carry

        lax.fori_loop(0, n_heads // ATT_HEAD_UNROLL, head_group, 0)

    @pl.when(ki == last)
    def _():
        for h in range(n_heads):
            o_ref[0, :, LANES * h:LANES * (h + 1)] = (
                acc_scr[h, :, 0:128] / acc_scr[h, :, 128:256]).astype(o_ref.dtype)


def _masked_attention(q, k, v, bias, bsz, t):
    tq, tk, hg = ATT_TQ, ATT_TK, ATT_HG
    nq, nk = t // tq, t // tk

    def last(i):
        return ((i + 1) * tq - 1) // tk

    return pl.pallas_call(
        _attn_kernel,
        grid=(bsz, MLA_HEADS // hg, nq, nk),
        in_specs=[pl.BlockSpec((1, hg, tq, QK_PAD), lambda b, g, i, j: (b, g, i, 0)),
                  pl.BlockSpec((1, hg, QK_PAD, tk), lambda b, g, i, j: (b, g, 0, jnp.minimum(j, last(i)))),
                  pl.BlockSpec((1, hg, tk, V_PAD), lambda b, g, i, j: (b, g, jnp.minimum(j, last(i)), 0)),
                  pl.BlockSpec((1, 1, tq, tk), lambda b, g, i, j: (b, jnp.minimum(j, last(i)), i, 0))],
        out_specs=pl.BlockSpec((1, tq, hg * MLA_V_DIM), lambda b, g, i, j: (b, i, g)),
        out_shape=jax.ShapeDtypeStruct((bsz, t, MLA_O_WIDTH), jnp.bfloat16),
        scratch_shapes=[pltpu.VMEM((hg, tq, LANES), jnp.float32),
                        pltpu.VMEM((hg, tq, V_PAD), jnp.float32)],
        compiler_params=pltpu.CompilerParams(
            dimension_semantics=("arbitrary", "arbitrary", "arbitrary", "arbitrary"),
            vmem_limit_bytes=VMEM_LIMIT),
        name="dsa_attention",
    )(q, k, v, bias)


def dsa_branch_pallas(proj, kv_norm_g, w_uk, w_uv, idx_norm_g, idx_norm_b, bsz, t):
    n_keep = min(TOPK_MAX, t // 4)
    tab = _rope_tables_lane(t)
    q, k, v, iq, ik, iw = _dsa_prep(proj, tab, kv_norm_g, idx_norm_g, idx_norm_b, w_uk, w_uv, bsz, t)
    bias = _indexer(iq, ik, iw, bsz, t, n_keep)
    o = _masked_attention(q, k, v, bias, bsz, t)
    return o.reshape(bsz * t, MLA_O_WIDTH)


GDN_TB = 128
GDN_HG = 32
GDN_REP = GDN_V_HEADS // GDN_K_HEADS


def _gdn_kernel(q_ref, k_ref, v_ref, z_ref, ab_ref, cwq_ref, cwk_ref, cwv_ref, alog_ref, dtb_ref, ng_ref,
                o_ref, tq_scr, tk_scr, tv_scr, state_scr):
    f32, bf16 = jnp.float32, jnp.bfloat16
    tb = q_ref.shape[0]
    g = pl.program_id(1)
    step = pl.program_id(2)
    n_chunk = tb // CHUNK
    kh_n = GDN_HG // GDN_REP

    @pl.when(step == 0)
    def _():
        tq_scr[...] = jnp.zeros(tq_scr.shape, f32)
        tk_scr[...] = jnp.zeros(tk_scr.shape, f32)
        tv_scr[...] = jnp.zeros(tv_scr.shape, f32)
        state_scr[...] = jnp.zeros(state_scr.shape, f32)

    def conv_silu(x_ref, w_ref, buf):
        x = x_ref[...]
        buf[8:8 + tb, :] = x
        w = w_ref[...]
        y = x * w[3:4, :]
        for s in (1, 2, 3):
            y = y + buf[8 - s:8 - s + tb, :] * w[3 - s:4 - s, :]
        buf[0:8, :] = x[tb - 8:tb]
        return y * (1.0 / (1.0 + jnp.exp(-y)))

    qc = conv_silu(q_ref, cwq_ref, tq_scr)
    kc = conv_silu(k_ref, cwk_ref, tk_scr)
    vc = conv_silu(v_ref, cwv_ref, tv_scr)

    def l2n(x):
        return x * lax.rsqrt(jnp.sum(x * x, axis=-1, keepdims=True) + NORM_EPS)

    qn = [l2n(qc[:, LANES * j:LANES * (j + 1)]) * (GDN_HEAD_DIM ** -0.5) for j in range(kh_n)]
    kn = [l2n(kc[:, LANES * j:LANES * (j + 1)]) for j in range(kh_n)]

    ab = pltpu.roll(ab_ref[...], (LANES - GDN_HG * g) % LANES, 1)
    xg = ab + dtb_ref[0]
    softplus = jnp.maximum(xg, 0.0) + jnp.log1p(jnp.exp(-jnp.abs(xg)))
    g_all = -jnp.exp(alog_ref[0]) * softplus
    beta_all = 1.0 / (1.0 + jnp.exp(-ab))
    rin = lax.broadcasted_iota(jnp.int32, (tb, LANES), 0) & (CHUNK - 1)
    cum = g_all
    for s in (1, 2, 4, 8, 16, 32):
        cum = cum + jnp.where(rin >= s, pltpu.roll(cum, s, 0), 0.0)

    ii = lax.broadcasted_iota(jnp.int32, (tb, tb), 0)
    jj = lax.broadcasted_iota(jnp.int32, (tb, tb), 1)
    same = (ii >> CHUNK_SHIFT) == (jj >> CHUNK_SHIFT)
    incl = same & (ii >= jj)
    strict = same & (ii > jj)
    eye = jnp.where(ii == jj, 1.0, 0.0)
    nt_dims = (((1,), (1,)), ((), ()))
    tn_dims = (((0,), (0,)), ((), ()))
    ng = ng_ref[...]
    heads = range(GDN_HG)

    cum_t = cum.T
    kb = [x.astype(bf16) for x in kn]
    qb = [x.astype(bf16) for x in qn]
    kk = [lax.dot_general(kb[j], kb[j], nt_dims, preferred_element_type=f32) for j in range(kh_n)]
    qk = [lax.dot_general(qb[j], kb[j], nt_dims, preferred_element_type=f32) for j in range(kh_n)]
    gcol = [cum[:, hh:hh + 1] for hh in heads]
    bcol = [beta_all[:, 32 + hh:33 + hh] for hh in heads]
    dec = [jnp.where(incl, jnp.exp(jnp.where(incl, gcol[hh] - cum_t[hh:hh + 1, :], 0.0)), 0.0) for hh in heads]
    pw = [jnp.where(strict, -(bcol[hh] * kk[hh // GDN_REP] * dec[hh]), 0.0) for hh in heads]
    tmat = [eye + pw[hh] for hh in heads]
    pw = [jnp.dot(pw[hh].astype(bf16), pw[hh].astype(bf16), preferred_element_type=f32) for hh in heads]
    for _ in range(4):
        both = [jnp.dot(jnp.concatenate([tmat[hh], pw[hh]], axis=0).astype(bf16), pw[hh].astype(bf16),
                        preferred_element_type=f32) for hh in heads]
        tmat = [tmat[hh] + both[hh][0:tb] for hh in heads]
        pw = [both[hh][tb:2 * tb] for hh in heads]
    tmat = [tmat[hh] + jnp.dot(tmat[hh].astype(bf16), pw[hh].astype(bf16), preferred_element_type=f32)
            for hh in heads]
    eg = [jnp.exp(gcol[hh]) for hh in heads]
    uw = [jnp.dot(tmat[hh].astype(bf16),
                  jnp.concatenate([bcol[hh] * vc[:, LANES * hh:LANES * (hh + 1)],
                                   (bcol[hh] * eg[hh]) * kn[hh // GDN_REP]], axis=1).astype(bf16),
                  preferred_element_type=f32) for hh in heads]
    qkd = [qk[hh // GDN_REP] * dec[hh] for hh in heads]
    qd = [(qn[hh // GDN_REP] * eg[hh]).astype(bf16) for hh in heads]
    states = [state_scr[hh] for hh in heads]

    for c in range(n_chunk):
        r0 = c * CHUNK
        sb = [states[hh].astype(bf16) for hh in heads]
        ws = [jnp.dot(jnp.concatenate([uw[hh][r0:r0 + CHUNK, LANES:2 * LANES].astype(bf16),
                                       qd[hh][r0:r0 + CHUNK]], axis=0), sb[hh],
                      preferred_element_type=f32) for hh in heads]
        vb = [(uw[hh][r0:r0 + CHUNK, 0:LANES] - ws[hh][0:CHUNK]).astype(bf16) for hh in heads]
        g_last = [cum[r0 + CHUNK - 1:r0 + CHUNK, hh:hh + 1] for hh in heads]
        kd = [(kn[hh // GDN_REP][r0:r0 + CHUNK] * jnp.exp(g_last[hh] - gcol[hh][r0:r0 + CHUNK])).astype(bf16)
              for hh in heads]
        o = [ws[hh][CHUNK:2 * CHUNK] + jnp.dot(qkd[hh][r0:r0 + CHUNK, r0:r0 + CHUNK].astype(bf16), vb[hh],
                                               preferred_element_type=f32) for hh in heads]
        states = [states[hh] * jnp.exp(g_last[hh])
                  + lax.dot_general(kd[hh], vb[hh], tn_dims, preferred_element_type=f32) for hh in heads]
        for hh in heads:
            on = o[hh] * lax.rsqrt(jnp.mean(o[hh] * o[hh], axis=-1, keepdims=True) + NORM_EPS) * ng
            zz = z_ref[r0:r0 + CHUNK, LANES * hh:LANES * (hh + 1)]
            o_ref[r0:r0 + CHUNK, LANES * hh:LANES * (hh + 1)] = (
                on * (zz * (1.0 / (1.0 + jnp.exp(-zz))))).astype(o_ref.dtype)

    for hh in range(GDN_HG):
        state_scr[hh] = states[hh]


def _gdn_branch_pallas(proj, conv_w, a_log, dt_bias, norm_g, bsz, t):
    tb = GDN_TB
    nt = t // tb
    n_groups = GDN_V_HEADS // GDN_HG
    qw = GDN_HG // GDN_REP * GDN_HEAD_DIM
    vw = GDN_HG * GDN_HEAD_DIM
    f32 = jnp.float32

    def per_group(x):
        x = x.astype(f32).reshape(n_groups, 1, GDN_HG)
        return jnp.pad(x, ((0, 0), (0, 0), (0, LANES - GDN_HG)))

    def rows(w, off):
        return pl.BlockSpec((tb, w), lambda b, g, s: (b * nt + s, off // w + g))

    def convw(w, off):
        return pl.BlockSpec((4, w), lambda b, g, s: (0, off // w + g))

    grp = pl.BlockSpec((1, 1, LANES), lambda b, g, s: (g, 0, 0))
    conv_w = conv_w.astype(f32)
    return pl.pallas_call(
        _gdn_kernel,
        grid=(bsz, n_groups, nt),
        in_specs=[rows(qw, P_QKV), rows(qw, P_QKV + GDN_QK_WIDTH), rows(vw, P_QKV + 2 * GDN_QK_WIDTH),
                  rows(vw, P_GZ),
                  pl.BlockSpec((tb, LANES), lambda b, g, s: (b * nt + s, P_GAW // LANES)),
                  convw(qw, 0), convw(qw, GDN_QK_WIDTH), convw(vw, 2 * GDN_QK_WIDTH),
                  grp, grp,
                  pl.BlockSpec((1, LANES), lambda b, g, s: (0, 0))],
        out_specs=pl.BlockSpec((tb, vw), lambda b, g, s: (b * nt + s, g)),
        out_shape=jax.ShapeDtypeStruct((bsz * t, GDN_V_WIDTH), jnp.bfloat16),
        scratch_shapes=[pltpu.VMEM((tb + 8, qw), f32), pltpu.VMEM((tb + 8, qw), f32), pltpu.VMEM((tb + 8, vw), f32),
                        pltpu.VMEM((GDN_HG, GDN_HEAD_DIM, GDN_HEAD_DIM), f32)],
        compiler_params=pltpu.CompilerParams(
            dimension_semantics=("arbitrary", "arbitrary", "arbitrary"),
            vmem_limit_bytes=VMEM_LIMIT),
        name="gdn_branch",
    )(proj, proj, proj, proj, proj, conv_w, conv_w, conv_w,
      per_group(a_log), per_group(dt_bias), norm_g.astype(f32)[None, :])


MERGE_TM = 512
MERGE_TN = 512


def _merge_kernel(oa_ref, ob_ref, wa_ref, wb_ref, ga_ref, gb_ref, o_ref):
    f32 = jnp.float32
    ua = jnp.dot(oa_ref[...], wa_ref[...], preferred_element_type=f32)
    ub = jnp.dot(ob_ref[...], wb_ref[...], preferred_element_type=f32)
    sa = 1.0 / (1.0 + jnp.exp(-ga_ref[...]))
    sb = 1.0 / (1.0 + jnp.exp(-gb_ref[...]))
    o_ref[...] = (sa * ua + sb * ub).astype(o_ref.dtype)


def _outproj_kernel(m_ref, w_ref, x_ref, o_ref):
    o_ref[...] = x_ref[...] + jnp.dot(m_ref[...], w_ref[...], preferred_element_type=jnp.float32)


def _merge_and_project(x, proj, o_a, o_b, w_branch_gdn, w_branch_dsa, w_out):
    n, d = x.shape
    tm, tn = MERGE_TM, MERGE_TN
    bf16 = jnp.bfloat16
    params = pltpu.CompilerParams(dimension_semantics=("arbitrary", "arbitrary"), vmem_limit_bytes=VMEM_LIMIT)
    merged = pl.pallas_call(
        _merge_kernel,
        grid=(d // tn, n // tm),
        in_specs=[pl.BlockSpec((tm, o_a.shape[1]), lambda j, i: (i, 0)),
                  pl.BlockSpec((tm, o_b.shape[1]), lambda j, i: (i, 0)),
                  pl.BlockSpec((o_a.shape[1], tn), lambda j, i: (0, j)),
                  pl.BlockSpec((o_b.shape[1], tn), lambda j, i: (0, j)),
                  pl.BlockSpec((tm, tn), lambda j, i: (i, P_GATE_A // tn + j)),
                  pl.BlockSpec((tm, tn), lambda j, i: (i, P_GATE_B // tn + j))],
        out_specs=pl.BlockSpec((tm, tn), lambda j, i: (i, j)),
        out_shape=jax.ShapeDtypeStruct((n, d), bf16),
        compiler_params=params,
        name="branch_merge",
    )(o_a, o_b, w_branch_gdn.astype(bf16), w_branch_dsa.astype(bf16), proj, proj)
    return pl.pallas_call(
        _outproj_kernel,
        grid=(n // tm,),
        in_specs=[pl.BlockSpec((tm, d), lambda i: (i, 0)),
                  pl.BlockSpec((d, d), lambda i: (0, 0)),
                  pl.BlockSpec((tm, d), lambda i: (i, 0))],
        out_specs=pl.BlockSpec((tm, d), lambda i: (i, 0)),
        out_shape=jax.ShapeDtypeStruct((n, d), jnp.float32),
        compiler_params=pltpu.CompilerParams(dimension_semantics=("arbitrary",), vmem_limit_bytes=VMEM_LIMIT),
        name="out_projection",
    )(merged, w_out.astype(bf16), x)


def hybrid_mixer(x, norm_g, w_in, conv_w, a_log, dt_bias, gdn_norm_g, kv_norm_g, w_uk, w_uv,
                 idx_norm_g, idx_norm_b, w_branch_gdn, w_branch_dsa, w_out):
    bsz, t, d = x.shape
    n = bsz * t
    x = x.reshape(n, d)
    proj = _input_projection(x, norm_g, _relayout_w_in(w_in))
    o_a = _gdn_branch_pallas(proj, conv_w, a_log, dt_bias, gdn_norm_g, bsz, t)
    o_b = dsa_branch_pallas(proj, kv_norm_g, w_uk, w_uv, idx_norm_g, idx_norm_b, bsz, t)
    return _merge_and_project(x, proj, o_a, o_b, w_branch_gdn, w_branch_dsa, w_out)


ROUTER_TM = 512
MOE_BLK = 256
MOE_TN = 1024
COMBINE_TM = 128
DISPATCH_TM = 256


def _router_kernel(x_ref, g_ref, rw_ref, rb_ref, meta_ref, cnt_ref, carry_scr):
    f32 = jnp.float32
    tm = x_ref.shape[0]

    @pl.when(pl.program_id(0) == 0)
    def _():
        carry_scr[...] = jnp.zeros(carry_scr.shape, f32)

    x = x_ref[...]
    h = (x * lax.rsqrt(jnp.mean(x * x, axis=-1, keepdims=True) + NORM_EPS) * g_ref[...]).astype(jnp.bfloat16)
    logits = jnp.dot(h, rw_ref[...], preferred_element_type=f32) + rb_ref[...]
    lane = lax.broadcasted_iota(jnp.int32, (tm, LANES), 1).astype(f32)
    vals = logits
    hot_sum = jnp.zeros((tm, LANES), f32)
    tops, ids, hots = [], [], []
    for _ in range(TOP_K):
        m = jnp.max(vals, axis=1, keepdims=True)
        idx = jnp.min(jnp.where(vals == m, lane, float(LANES)), axis=1, keepdims=True)
        hot = lane == idx
        tops.append(m)
        ids.append(idx)
        hots.append(hot)
        vals = jnp.where(hot, -jnp.inf, vals)
        hot_sum = hot_sum + jnp.where(hot, 1.0, 0.0)
    exps = [jnp.exp(v - tops[0]) for v in tops]
    inv = 1.0 / (exps[0] + exps[1] + exps[2] + exps[3])
    ri = lax.broadcasted_iota(jnp.int32, (tm, tm), 0)
    ci = lax.broadcasted_iota(jnp.int32, (tm, tm), 1)
    lower = jnp.where(ri > ci, 1.0, 0.0).astype(jnp.bfloat16)
    base = carry_scr[...] + jnp.dot(lower, hot_sum.astype(jnp.bfloat16), preferred_element_type=f32)
    meta = jnp.zeros((tm, LANES), f32)
    for k in range(TOP_K):
        rank = jnp.sum(jnp.where(hots[k], base, 0.0), axis=1, keepdims=True)
        meta = jnp.where(lane == float(k), ids[k], meta)
        meta = jnp.where(lane == float(TOP_K + k), exps[k] * inv, meta)
        meta = jnp.where(lane == float(2 * TOP_K + k), rank, meta)
    meta_ref[...] = meta
    carry_scr[...] = carry_scr[...] + jnp.sum(hot_sum, axis=0, keepdims=True)
    cnt_ref[...] = carry_scr[...]


def _router(x1, norm_g, router_w, router_b):
    n, d = x1.shape
    tm = ROUTER_TM
    rw = jnp.pad(router_w.astype(jnp.bfloat16), ((0, 0), (0, LANES - N_EXPERTS)))
    rb = jnp.concatenate([router_b.astype(jnp.float32),
                          jnp.full((LANES - N_EXPERTS,), -jnp.inf, jnp.float32)])[None, :]
    return pl.pallas_call(
        _router_kernel,
        grid=(n // tm,),
        in_specs=[pl.BlockSpec((tm, d), lambda i: (i, 0)),
                  pl.BlockSpec((1, d), lambda i: (0, 0)),
                  pl.BlockSpec((d, LANES), lambda i: (0, 0)),
                  pl.BlockSpec((1, LANES), lambda i: (0, 0))],
        out_specs=[pl.BlockSpec((tm, LANES), lambda i: (i, 0)),
                   pl.BlockSpec((1, LANES), lambda i: (0, 0))],
        out_shape=[jax.ShapeDtypeStruct((n, LANES), jnp.float32),
                   jax.ShapeDtypeStruct((1, LANES), jnp.float32)],
        scratch_shapes=[pltpu.VMEM((1, LANES), jnp.float32)],
        compiler_params=pltpu.CompilerParams(dimension_semantics=("arbitrary",), vmem_limit_bytes=VMEM_LIMIT),
        name="moe_router",
    )(x1, norm_g.astype(jnp.float32)[None, :], rw, rb)


def _dispatch_kernel(slot_ref, x_ref, g_ref, xs_in_ref, xs_ref, h_scr, sem):
    del xs_in_ref
    tm = x_ref.shape[0]
    x = x_ref[...]
    h_scr[...] = x * lax.rsqrt(jnp.mean(x * x, axis=-1, keepdims=True) + NORM_EPS) * g_ref[...]

    def copy(r, k):
        return pltpu.make_async_copy(h_scr.at[pl.ds(r, 1)], xs_ref.at[pl.ds(slot_ref[TOP_K * r + k], 1)], sem)

    def start_row(r, carry):
        for k in range(TOP_K):
            copy(r, k).start()
        return carry

    lax.fori_loop(0, tm, start_row, 0)
    for _ in range(TOP_K):
        pltpu.make_async_copy(h_scr, xs_ref.at[pl.ds(0, tm)], sem).wait()


def _dispatch(slot, x1, norm_g, n_rows):
    n, d = x1.shape
    tm = DISPATCH_TM
    xs0 = jnp.zeros((n_rows, d), jnp.float32)
    return pl.pallas_call(
        _dispatch_kernel,
        grid=(n // tm,),
        in_specs=[pl.BlockSpec((tm * TOP_K,), lambda i: (i,), memory_space=pltpu.SMEM),
                  pl.BlockSpec((tm, d), lambda i: (i, 0)),
                  pl.BlockSpec((1, d), lambda i: (0, 0)),
                  pl.BlockSpec(memory_space=pl.ANY)],
        out_specs=pl.BlockSpec(memory_space=pl.ANY),
        out_shape=jax.ShapeDtypeStruct((n_rows, d), jnp.float32),
        scratch_shapes=[pltpu.VMEM((tm, d), jnp.float32), pltpu.SemaphoreType.DMA(())],
        input_output_aliases={3: 0},
        compiler_params=pltpu.CompilerParams(dimension_semantics=("arbitrary",), vmem_limit_bytes=VMEM_LIMIT),
        name="moe_dispatch",
    )(slot, x1, norm_g.astype(jnp.float32)[None, :], xs0)


def _expert_changed(be_ref, j):
    return jnp.logical_or(j == 0, be_ref[j] != be_ref[jnp.maximum(j - 1, 0)])


def _moe_up_kernel(be_ref, nb_ref, x_ref, wg_ref, wu_ref, bg_ref, bu_ref, o_ref, wg_scr, wu_scr):
    j = pl.program_id(1)

    @pl.when(jnp.logical_and(j < nb_ref[0], _expert_changed(be_ref, j)))
    def _():
        wg_scr[...] = wg_ref[0].astype(jnp.bfloat16)
        wu_scr[...] = wu_ref[0].astype(jnp.bfloat16)

    @pl.when(j < nb_ref[0])
    def _():
        x = x_ref[...].astype(jnp.bfloat16)
        gate = jnp.dot(x, wg_scr[...], preferred_element_type=jnp.float32) + bg_ref[0]
        up = jnp.dot(x, wu_scr[...], preferred_element_type=jnp.float32) + bu_ref[0]
        gate = jnp.minimum(gate, SWIGLU_LIMIT)
        up = jnp.clip(up, -SWIGLU_LIMIT, SWIGLU_LIMIT)
        act = (up + 1.0) * (gate * (1.0 / (1.0 + jnp.exp(-SWIGLU_ALPHA * gate))))
        o_ref[...] = act.astype(o_ref.dtype)

    @pl.when(j >= nb_ref[0])
    def _():
        o_ref[...] = jnp.zeros(o_ref.shape, o_ref.dtype)


def _moe_down_kernel(be_ref, nb_ref, a_ref, wd_ref, bd_ref, o_ref, wd_scr):
    j = pl.program_id(1)

    @pl.when(jnp.logical_and(j < nb_ref[0], _expert_changed(be_ref, j)))
    def _():
        wd_scr[...] = wd_ref[0].astype(jnp.bfloat16)

    @pl.when(j < nb_ref[0])
    def _():
        o_ref[...] = jnp.dot(a_ref[...], wd_scr[...], preferred_element_type=jnp.float32) + bd_ref[0]

    @pl.when(j >= nb_ref[0])
    def _():
        o_ref[...] = jnp.zeros(o_ref.shape, o_ref.dtype)


def _moe_experts(xs, block_expert, n_used, w_gate, b_gate, w_up, b_up, w_down, b_down):
    n_rows, d = xs.shape
    f = w_gate.shape[2]
    blk, tn = MOE_BLK, MOE_TN
    n_blocks = n_rows // blk
    bf16 = jnp.bfloat16

    def wspec(k):
        return pl.BlockSpec((1, k, tn), lambda c, j, be, nb: (be[j], 0, c))

    bspec = pl.BlockSpec((1, 1, tn), lambda c, j, be, nb: (be[j], 0, c))
    rows_in = lambda k: pl.BlockSpec((blk, k), lambda c, j, be, nb: (j, 0))
    rows_out = pl.BlockSpec((blk, tn), lambda c, j, be, nb: (j, c))
    params = pltpu.CompilerParams(dimension_semantics=("arbitrary", "arbitrary"), vmem_limit_bytes=VMEM_LIMIT)
    act = pl.pallas_call(
        _moe_up_kernel,
        grid_spec=pltpu.PrefetchScalarGridSpec(
            num_scalar_prefetch=2, grid=(f // tn, n_blocks),
            in_specs=[rows_in(d), wspec(d), wspec(d), bspec, bspec],
            out_specs=rows_out,
            scratch_shapes=[pltpu.VMEM((d, tn), bf16), pltpu.VMEM((d, tn), bf16)]),
        out_shape=jax.ShapeDtypeStruct((n_rows, f), bf16),
        compiler_params=params,
        name="moe_up",
    )(block_expert, n_used, xs, w_gate.astype(jnp.float32), w_up.astype(jnp.float32),
      b_gate.astype(jnp.float32)[:, None, :], b_up.astype(jnp.float32)[:, None, :])
    return pl.pallas_call(
        _moe_down_kernel,
        grid_spec=pltpu.PrefetchScalarGridSpec(
            num_scalar_prefetch=2, grid=(d // tn, n_blocks),
            in_specs=[rows_in(f), wspec(f), bspec],
            out_specs=rows_out,
            scratch_shapes=[pltpu.VMEM((f, tn), bf16)]),
        out_shape=jax.ShapeDtypeStruct((n_rows, d), jnp.float32),
        compiler_params=params,
        name="moe_down",
    )(block_expert, n_used, act, w_down.astype(jnp.float32), b_down.astype(jnp.float32)[:, None, :])


def _combine_kernel(slot_ref, slot_next_ref, x1_ref, meta_ref, g_ref, ys_ref, o_ref, buf, sem):
    i = pl.program_id(0)
    n_steps = pl.num_programs(0)
    tm = x1_ref.shape[0]

    def copy(slots, r, k, half):
        return pltpu.make_async_copy(ys_ref.at[pl.ds(slots[TOP_K * r + k], 1)],
                                     buf.at[half, k, pl.ds(r, 1)], sem.at[half])

    def start_tile(slots, half):
        def body(r, carry):
            for k in range(TOP_K):
                copy(slots, r, k, half).start()
            return carry
        lax.fori_loop(0, tm, body, 0)

    @pl.when(i == 0)
    def _():
        start_tile(slot_ref, 0)

    @pl.when(i + 1 < n_steps)
    def _():
        start_tile(slot_next_ref, (i + 1) % 2)

    half = i % 2

    for k in range(TOP_K):
        pltpu.make_async_copy(ys_ref.at[pl.ds(0, tm)], buf.at[half, k], sem.at[half]).wait()
    acc = x1_ref[...]
    for k in range(TOP_K):
        acc = acc + meta_ref[:, TOP_K + k:TOP_K + k + 1] * buf[half, k]
    o_ref[...] = acc * lax.rsqrt(jnp.mean(acc * acc, axis=-1, keepdims=True) + NORM_EPS) * g_ref[...]


def _combine(slot, x1, meta, ys, norm_g):
    n, d = x1.shape
    tm = COMBINE_TM
    n_steps = n // tm
    return pl.pallas_call(
        _combine_kernel,
        grid=(n_steps,),
        in_specs=[pl.BlockSpec((tm * TOP_K,), lambda i: (i,), memory_space=pltpu.SMEM),
                  pl.BlockSpec((tm * TOP_K,), lambda i: (jnp.minimum(i + 1, n_steps - 1),),
                               memory_space=pltpu.SMEM),
                  pl.BlockSpec((tm, d), lambda i: (i, 0)),
                  pl.BlockSpec((tm, LANES), lambda i: (i, 0)),
                  pl.BlockSpec((1, d), lambda i: (0, 0)),
                  pl.BlockSpec(memory_space=pl.ANY)],
        out_specs=pl.BlockSpec((tm, d), lambda i: (i, 0)),
        out_shape=jax.ShapeDtypeStruct((n, d), jnp.float32),
        scratch_shapes=[pltpu.VMEM((2, TOP_K, tm, d), jnp.float32),
                        pltpu.SemaphoreType.DMA((2,))],
        compiler_params=pltpu.CompilerParams(dimension_semantics=("arbitrary",), vmem_limit_bytes=VMEM_LIMIT),
        name="moe_combine",
    )(slot, slot, x1, meta, norm_g.astype(jnp.float32)[None, :], ys)


def moe_final_pallas(x1, norm_ffn_g, router_w, router_b, w_gate, b_gate, w_up, b_up, w_down, b_down,
                     norm_final_g):
    n, d = x1.shape
    meta, cnt = _router(x1, norm_ffn_g, router_w, router_b)
    counts = cnt[0, :N_EXPERTS].astype(jnp.int32)
    padded = (counts + MOE_BLK - 1) // MOE_BLK * MOE_BLK
    pad_end = jnp.cumsum(padded)
    pad_start = pad_end - padded
    expert = meta[:, 0:TOP_K].astype(jnp.int32)
    rank = meta[:, 2 * TOP_K:3 * TOP_K].astype(jnp.int32)
    seg_start = jnp.sum(jnp.where(expert[..., None] == jnp.arange(N_EXPERTS, dtype=jnp.int32), pad_start, 0),
                        axis=-1)
    slot = (seg_start + rank).reshape(-1)
    n_blocks = n * TOP_K // MOE_BLK + N_EXPERTS
    block_start = jnp.arange(n_blocks, dtype=jnp.int32) * MOE_BLK
    block_expert = jnp.minimum(jnp.sum((pad_end[None, :] <= block_start[:, None]).astype(jnp.int32), axis=1),
                               N_EXPERTS - 1)
    n_used = (pad_end[-1:] // MOE_BLK).astype(jnp.int32)
    xs = _dispatch(slot, x1, norm_ffn_g, n_blocks * MOE_BLK)
    ys = _moe_experts(xs, block_expert, n_used, w_gate, b_gate, w_up, b_up, w_down, b_down)
    return _combine(slot, x1, meta, ys, norm_final_g)


def kernel(x, norm_mix_g, w_in, gdn_conv_w, gdn_a_log, gdn_dt_bias, gdn_norm_g, mla_kv_norm_g,
           mla_w_uk, mla_w_uv, idx_k_norm_g, idx_k_norm_b, w_branch_gdn, w_branch_dsa, w_out,
           norm_ffn_g, router_w, router_b, exp_w_gate, exp_b_gate, exp_w_up, exp_b_up,
           exp_w_down, exp_b_down, norm_final_g):
    l = 0
    bsz, t, d = x.shape
    x1 = hybrid_mixer(x, norm_mix_g[l], w_in[l], gdn_conv_w[l], gdn_a_log[l], gdn_dt_bias[l],
                      gdn_norm_g[l], mla_kv_norm_g[l], mla_w_uk[l], mla_w_uv[l], idx_k_norm_g[l],
                      idx_k_norm_b[l], w_branch_gdn[l], w_branch_dsa[l], w_out[l])
    out = moe_final_pallas(x1, norm_ffn_g[l], router_w[l], router_b[l], exp_w_gate[l],
                           exp_b_gate[l], exp_w_up[l], exp_b_up[l], exp_w_down[l], exp_b_down[l], norm_final_g)
    return out.reshape(bsz, t, d)
```
